```python
import math
import jax, jax.numpy as jnp
from jax import lax
import numpy as np

D_MODEL = 4096
BATCH = 1
SEQ = 8192
DEPTH = 4

GRID_W = 64
CTX_LEN = 256
HEAD_DIM = 128
MIX_WIDTH = D_MODEL
EPS = 1e-6
ROPE_THETA = 10000.0
HY_WIDTH = MIX_WIDTH // 2
HY_ORDER = 2
HY_SHORT = 3
HY_EMB_BANDS = 16
HY_EMB_DIM = 1 + 2 * HY_EMB_BANDS
HY_FILTER_HIDDEN = 64
HY_FAST_DECAY = 0.3
HY_SLOW_DECAY = 1.5
HY_DECAY_TARGET = 1e-2
HY_FILTER_OUT_SCALE = 0.005
GQA_WIDTH = MIX_WIDTH - HY_WIDTH
GQA_Q_HEADS = GQA_WIDTH // HEAD_DIM
GQA_KV_HEADS = GQA_Q_HEADS // 4
GQA_KV_WIDTH = GQA_KV_HEADS * HEAD_DIM
Q_BLOCK = 128
OFF_HY_GATE = (HY_ORDER + 1) * HY_WIDTH
OFF_Q = OFF_HY_GATE + HY_WIDTH
OFF_K = OFF_Q + GQA_WIDTH
OFF_V = OFF_K + GQA_KV_WIDTH
OFF_AT_GATE = OFF_V + GQA_KV_WIDTH
EVEN_IN = OFF_AT_GATE + GQA_WIDTH
NA_WIDTH = MIX_WIDTH
NA_HEADS = NA_WIDTH // HEAD_DIM
NA_KH_MAX = 8
NA_KW = 16
ODD_IN = 4 * NA_WIDTH
N_EVEN = (DEPTH + 1) // 2
N_ODD = DEPTH // 2

kernel_name = "hyena_gqa_natten_hybrid_dit"


def rms_norm(x, g):
    xf = x.astype(jnp.float32)
    y = xf * lax.rsqrt(jnp.mean(xf * xf, axis=-1, keepdims=True) + EPS)
    return (y * g.astype(jnp.float32)).astype(x.dtype)


def ada_mod(cvec, w, b, n):
    d = w.shape[0]
    m = jax.nn.silu(cvec) @ w[:, : n * d] + b[: n * d]
    return jnp.split(m, n, axis=-1)


def modulate(h, shift, scale):
    return h * (1.0 + scale[:, None, :]) + shift[:, None, :]


def short_conv(u, w):
    L = u.shape[1]
    pad = HY_SHORT // 2
    up = jnp.pad(u, ((0, 0), (pad, HY_SHORT - 1 - pad), (0, 0)))
    out = up[:, 0:L] * w[0]
    for j in range(1, HY_SHORT):
        out = out + up[:, j:j + L] * w[j]
    return out


def hyena_filters(L, w1, b1, w2, b2, w3, freq, decay):
    f32 = jnp.float32
    t = jnp.linspace(0.0, 1.0, L, dtype=f32)[:, None]
    w = (2.0 * math.pi / L) * jnp.arange(L, dtype=f32)[:, None]
    bands = jnp.linspace(1e-4, HY_EMB_BANDS - 1, HY_EMB_BANDS, dtype=f32)[None, :]
    z = jnp.concatenate([t, jnp.cos(bands * w), -jnp.sin(bands * w)], axis=-1)
    fr = freq.astype(f32)
    hdn = jnp.sin(fr * (z @ w1.astype(f32) + b1.astype(f32)))
    hdn = jnp.sin(fr * (hdn @ w2.astype(f32) + b2.astype(f32)))
    filt = (hdn @ w3.astype(f32)).reshape(L, HY_ORDER, 2, -1)
    window = jnp.exp(-t[:, :, None] * jnp.abs(decay.astype(f32))[None])
    filt = filt * window[:, :, None, :]
    fwd, bwd = filt[:, :, 0], filt[:, :, 1]
    two_sided = jnp.concatenate([fwd, jnp.zeros_like(fwd[:1]), bwd[:0:-1]], axis=0)
    return jnp.fft.rfft(two_sided, axis=0)


def long_conv(z, hf, d):
    L = z.shape[1]
    zf = jnp.fft.rfft(z, n=2 * L, axis=1)
    return jnp.fft.irfft(zf * hf[None], n=2 * L, axis=1)[:, :L] + z * d


def hyena(u, conv_w, filt_f, dskip):
    u = short_conv(u, conv_w).astype(jnp.float32)
    pieces = jnp.split(u, HY_ORDER + 1, axis=-1)
    z = pieces[0]
    ds = dskip.astype(jnp.float32)
    for o in range(HY_ORDER):
        z = pieces[o + 1] * long_conv(z, filt_f[:, o], ds[o])
    return z


def axial_rope(x, rows, cols):
    half = HEAD_DIM // 2
    quarter = half // 2
    inv = ROPE_THETA ** (-jnp.arange(quarter, dtype=jnp.float32) / quarter)

    def rot(xa, pos):
        ang = pos.astype(jnp.float32)[:, None] * inv[None]
        cos = jnp.cos(ang)[None, :, None, :]
        sin = jnp.sin(ang)[None, :, None, :]
        x1, x2 = xa[..., :quarter], xa[..., quarter:]
        return jnp.concatenate([x1 * cos - x2 * sin, x1 * sin + x2 * cos], axis=-1)

    xf = x.astype(jnp.float32)
    out = jnp.concatenate([rot(xf[..., :half], rows), rot(xf[..., half:], cols)], axis=-1)
    return out.astype(x.dtype)


def dense_attention(q, k, v):
    B, Lq, HQ, hd = q.shape
    HKV = k.shape[2]
    qg = q.reshape(B, Lq, HKV, HQ // HKV, hd)
    s = jnp.einsum('bqkgd,bskd->bkgqs', qg, k, preferred_element_type=jnp.float32) * (hd ** -0.5)
    p = jax.nn.softmax(s, axis=-1).astype(v.dtype)
    return jnp.einsum('bkgqs,bskd->bqkgd', p, v).reshape(B, Lq, HQ * hd)


def block_attention(q, k, v):
    B, L, HQ, hd = q.shape
    nb = L // Q_BLOCK
    qb = q.reshape(B, nb, Q_BLOCK, HQ, hd).transpose(1, 0, 2, 3, 4)
    o = lax.map(lambda qq: dense_attention(qq, k, v), qb)
    return o.transpose(1, 0, 2, 3).reshape(B, L, HQ * hd)


def neighbourhood_attention(q, k, v, kc, vc, rel_bias):
    B, L, H, hd = q.shape
    rows = L // GRID_W
    kh = min(NA_KH_MAX, rows)
    kw = NA_KW
    scale = hd ** -0.5
    qg = q.reshape(B, rows, GRID_W, H, hd)
    kg = k.reshape(B, rows, GRID_W, H, hd)
    vg = v.reshape(B, rows, GRID_W, H, hd)
    cols = jnp.arange(GRID_W)
    col_start = jnp.clip(cols - kw // 2, 0, GRID_W - kw)
    col_idx = col_start[:, None] + jnp.arange(kw)[None]
    dcol = col_idx - cols[:, None] + (kw - 1)
    bias_cols = rel_bias[:, :, dcol]

    def one_row(r):
        rs = jnp.clip(r - kh // 2, 0, rows - kh)
        kb = lax.dynamic_slice_in_dim(kg, rs, kh, axis=1)
        vb = lax.dynamic_slice_in_dim(vg, rs, kh, axis=1)
        kwin = kb[:, :, col_idx]
        vwin = vb[:, :, col_idx]
        qr = lax.dynamic_index_in_dim(qg, r, axis=1, keepdims=False)
        drow = rs + jnp.arange(kh) - r + (NA_KH_MAX - 1)
        bias = jnp.take(bias_cols, drow, axis=1).transpose(0, 2, 1, 3)
        s_win = jnp.einsum('bchd,bicjhd->bhcij', qr, kwin, preferred_element_type=jnp.float32) * scale
        s_win = s_win + bias[None].astype(jnp.float32)
        s_ctx = jnp.einsum('bchd,bshd->bhcs', qr, kc, preferred_element_type=jnp.float32) * scale
        s = jnp.concatenate([s_win.reshape(B, H, GRID_W, kh * kw), s_ctx], axis=-1)
        p = jax.nn.softmax(s, axis=-1).astype(v.dtype)
        p_win = p[..., : kh * kw].reshape(B, H, GRID_W, kh, kw)
        p_ctx = p[..., kh * kw:]
        return (jnp.einsum('bhcij,bicjhd->bchd', p_win, vwin)
                + jnp.einsum('bhcs,bshd->bchd', p_ctx, vc))

    o = lax.map(one_row, jnp.arange(rows))
    return o.transpose(1, 0, 2, 3, 4).reshape(B, L, H * hd)


def even_mixer(h, hc, need_ctx_out, rows_pos, cols_pos, w_in, w_out, hy_conv, hy_w1, hy_b1,
               hy_w2, hy_b2, hy_w3, hy_freq, hy_decay, hy_dskip, q_norm, k_norm):
    B, L, _ = h.shape
    Lc = hc.shape[1]
    u = h @ w_in
    q = rms_norm(u[..., OFF_Q:OFF_K].reshape(B, L, GQA_Q_HEADS, HEAD_DIM), q_norm)
    k = rms_norm(u[..., OFF_K:OFF_V].reshape(B, L, GQA_KV_HEADS, HEAD_DIM), k_norm)
    v = u[..., OFF_V:OFF_AT_GATE].reshape(B, L, GQA_KV_HEADS, HEAD_DIM)
    q = axial_rope(q, rows_pos, cols_pos)
    k = axial_rope(k, rows_pos, cols_pos)
    if need_ctx_out:
        uc = hc @ w_in
        kvc = uc[..., OFF_K:OFF_AT_GATE]
    else:
        kvc = hc @ w_in[:, OFF_K:OFF_AT_GATE]
    kc = rms_norm(kvc[..., :GQA_KV_WIDTH].reshape(B, Lc, GQA_KV_HEADS, HEAD_DIM), k_norm)
    vc = kvc[..., GQA_KV_WIDTH:].reshape(B, Lc, GQA_KV_HEADS, HEAD_DIM)
    att = block_attention(q, jnp.concatenate([kc, k], axis=1), jnp.concatenate([vc, v], axis=1))
    filt = (hy_w1, hy_b1, hy_w2, hy_b2, hy_w3, hy_freq, hy_decay)
    hy = hyena(u[..., :OFF_HY_GATE], hy_conv, hyena_filters(L, *filt), hy_dskip).astype(h.dtype)
    y = jnp.concatenate([hy * jax.nn.silu(u[..., OFF_HY_GATE:OFF_Q]),
                         att * jax.nn.silu(u[..., OFF_AT_GATE:])], axis=-1) @ w_out
    yc = None
    if need_ctx_out:
        qc = rms_norm(uc[..., OFF_Q:OFF_K].reshape(B, Lc, GQA_Q_HEADS, HEAD_DIM), q_norm)
        att_c = dense_attention(qc, kc, vc)
        hy_c = hyena(uc[..., :OFF_HY_GATE], hy_conv, hyena_filters(Lc, *filt), hy_dskip).astype(hc.dtype)
        yc = jnp.concatenate([hy_c * jax.nn.silu(uc[..., OFF_HY_GATE:OFF_Q]),
                              att_c * jax.nn.silu(uc[..., OFF_AT_GATE:])], axis=-1) @ w_out
    return y, yc


def odd_mixer(h, hc, need_ctx_out, w_in, w_out, q_norm, k_norm, rel_bias):
    B, L, _ = h.shape
    Lc = hc.shape[1]
    W = NA_WIDTH
    u = h @ w_in
    q = rms_norm(u[..., :W].reshape(B, L, NA_HEADS, HEAD_DIM), q_norm)
    k = rms_norm(u[..., W:2 * W].reshape(B, L, NA_HEADS, HEAD_DIM), k_norm)
    v = u[..., 2 * W:3 * W].reshape(B, L, NA_HEADS, HEAD_DIM)
    if need_ctx_out:
        uc = hc @ w_in
        kvc = uc[..., W:3 * W]
    else:
        kvc = hc @ w_in[:, W:3 * W]
    kc = rms_norm(kvc[..., :W].reshape(B, Lc, NA_HEADS, HEAD_DIM), k_norm)
    vc = kvc[..., W:].reshape(B, Lc, NA_HEADS, HEAD_DIM)
    att = neighbourhood_attention(q, k, v, kc, vc, rel_bias)
    y = (att * jax.nn.silu(u[..., 3 * W:])) @ w_out
    yc = None
    if need_ctx_out:
        qc = rms_norm(uc[..., :W].reshape(B, Lc, NA_HEADS, HEAD_DIM), q_norm)
        att_c = dense_attention(qc, kc, vc)
        yc = (att_c * jax.nn.silu(uc[..., 3 * W:])) @ w_out
    return y, yc


def setup_inputs(seed: int = 0) -> dict:
    key = jax.random.key(seed)
    ks = jax.random.split(key, 26)
    f32 = jnp.float32

    def nrm(k, shape, s):
        return jax.random.normal(k, shape, f32) * s

    decay_lo = abs(math.log(HY_DECAY_TARGET)) / HY_SLOW_DECAY
    decay_hi = abs(math.log(HY_DECAY_TARGET)) / HY_FAST_DECAY
    base_decay = jnp.linspace(decay_lo, decay_hi, HY_WIDTH, dtype=f32)
    return {
        "x": nrm(ks[0], (BATCH, SEQ, D_MODEL), 1.0),
        "c": nrm(ks[1], (BATCH, D_MODEL), 1.0),
        "ctx": nrm(ks[2], (BATCH, CTX_LEN, D_MODEL), 1.0),
        "c_ctx": nrm(ks[3], (D_MODEL,), 1.0),
        "norm_g": 1.0 + nrm(ks[4], (DEPTH, D_MODEL), 0.02),
        "ada_w": nrm(ks[5], (DEPTH, D_MODEL, 3 * D_MODEL), 0.5 * D_MODEL ** -0.5),
        "ada_b": nrm(ks[6], (DEPTH, 3 * D_MODEL), 0.02),
        "e_w_in": nrm(ks[7], (N_EVEN, D_MODEL, EVEN_IN), D_MODEL ** -0.5),
        "e_w_out": nrm(ks[8], (N_EVEN, MIX_WIDTH, D_MODEL), MIX_WIDTH ** -0.5),
        "hy_conv": nrm(ks[9], (N_EVEN, HY_SHORT, (HY_ORDER + 1) * HY_WIDTH), HY_SHORT ** -0.5),
        "hy_w1": nrm(ks[10], (N_EVEN, HY_EMB_DIM, HY_FILTER_HIDDEN), HY_EMB_DIM ** -0.5),
        "hy_b1": nrm(ks[11], (N_EVEN, HY_FILTER_HIDDEN), 0.1),
        "hy_w2": nrm(ks[12], (N_EVEN, HY_FILTER_HIDDEN, HY_FILTER_HIDDEN), HY_FILTER_HIDDEN ** -0.5),
        "hy_b2": nrm(ks[13], (N_EVEN, HY_FILTER_HIDDEN), 0.1),
        "hy_w3": nrm(ks[14], (N_EVEN, HY_FILTER_HIDDEN, HY_ORDER * 2 * HY_WIDTH), HY_FILTER_OUT_SCALE),
        "hy_freq": 1.0 + nrm(ks[15], (N_EVEN, HY_FILTER_HIDDEN), 0.1),
        "hy_decay": base_decay * (1.0 + nrm(ks[16], (N_EVEN, HY_ORDER, HY_WIDTH), 0.05)),
        "hy_dskip": nrm(ks[17], (N_EVEN, HY_ORDER, HY_WIDTH), 0.5),
        "gqa_q_norm": 1.0 + nrm(ks[18], (N_EVEN, HEAD_DIM), 0.02),
        "gqa_k_norm": 1.0 + nrm(ks[19], (N_EVEN, HEAD_DIM), 0.02),
        "o_w_in": nrm(ks[20], (N_ODD, D_MODEL, ODD_IN), D_MODEL ** -0.5),
        "o_w_out": nrm(ks[21], (N_ODD, NA_WIDTH, D_MODEL), NA_WIDTH ** -0.5),
        "na_q_norm": 1.0 + nrm(ks[22], (N_ODD, HEAD_DIM), 0.02),
        "na_k_norm": 1.0 + nrm(ks[23], (N_ODD, HEAD_DIM), 0.02),
        "na_rel_bias": nrm(ks[24], (N_ODD, NA_HEADS, 2 * NA_KH_MAX - 1, 2 * NA_KW - 1), 0.1),
    }


def reference(x, c, ctx, c_ctx, norm_g, ada_w, ada_b, e_w_in, e_w_out, hy_conv, hy_w1, hy_b1,
              hy_w2, hy_b2, hy_w3, hy_freq, hy_decay, hy_dskip, gqa_q_norm, gqa_k_norm,
              o_w_in, o_w_out, na_q_norm, na_k_norm, na_rel_bias):
    L = x.shape[1]
    t = jnp.arange(L)
    rows_pos = t // GRID_W
    cols_pos = t % GRID_W
    xc = ctx
    for layer in range(DEPTH):
        need_ctx_out = layer < DEPTH - 1
        shift, scale, gate = ada_mod(c, ada_w[layer], ada_b[layer], 3)
        mods_c = ada_mod(c_ctx[None], ada_w[layer], ada_b[layer], 3 if need_ctx_out else 2)
        h = modulate(rms_norm(x, norm_g[layer]), shift, scale)
        hc = modulate(rms_norm(xc, norm_g[layer]), mods_c[0], mods_c[1])
        if layer % 2 == 0:
            i = layer // 2
            y, yc = even_mixer(h, hc, need_ctx_out, rows_pos, cols_pos, e_w_in[i], e_w_out[i],
                               hy_conv[i], hy_w1[i], hy_b1[i], hy_w2[i], hy_b2[i], hy_w3[i],
                               hy_freq[i], hy_decay[i], hy_dskip[i], gqa_q_norm[i], gqa_k_norm[i])
        else:
            i = layer // 2
            y, yc = odd_mixer(h, hc, need_ctx_out, o_w_in[i], o_w_out[i], na_q_norm[i],
                              na_k_norm[i], na_rel_bias[i])
        x = x + gate[:, None, :] * y
        if need_ctx_out:
            xc = xc + mods_c[2][:, None, :] * yc
    return x
```

```python
import functools
import math

import numpy as np
import jax
import jax.numpy as jnp
from jax import lax
from jax.experimental import pallas as pl
from jax.experimental.pallas import tpu as pltpu

F32 = jnp.float32
BF16 = jnp.bfloat16

HEAD_DIM = 128
GRID_W = 64
EPS = 1e-6
ROPE_THETA = 10000.0
NA_KH = 8
NA_KW = 16
NA_Q_ROWS = 4
NA_WIN_ROWS = 12
HY_SHORT = 3
FFT_N2 = 128
NEG_INF = -1e30

V7X_VMEM_BYTES = 64 * 1024 * 1024
LANE = 128


def _params(semantics, vmem_mb):
    assert vmem_mb * 1024 * 1024 < V7X_VMEM_BYTES
    return pltpu.CompilerParams(dimension_semantics=semantics,
                                vmem_limit_bytes=vmem_mb * 1024 * 1024)


def _largest_tile(n, target, quantum):
    best = None
    t = quantum
    while t <= min(n, target):
        if n % t == 0:
            best = t
        t += quantum
    assert best is not None, (n, target, quantum)
    return best


def _silu(x):
    return x * jax.nn.sigmoid(x)


def _mods_kernel(c_ref, w_ref, b_ref, o_ref):
    s = _silu(c_ref[...]).astype(BF16)
    o_ref[...] = jnp.dot(s, w_ref[...].astype(BF16), preferred_element_type=F32) + b_ref[...]


def _ada_mods(cvecs, ada_w, ada_b):
    depth, d, n = ada_w.shape
    tn = _largest_tile(n, 512, LANE)
    return pl.pallas_call(
        _mods_kernel,
        grid=(depth, n // tn),
        in_specs=[pl.BlockSpec((8, d), lambda l, j: (0, 0)),
                  pl.BlockSpec((None, d, tn), lambda l, j: (l, 0, j)),
                  pl.BlockSpec((None, 1, tn), lambda l, j: (l, 0, j))],
        out_specs=pl.BlockSpec((None, 8, tn), lambda l, j: (l, 0, j)),
        out_shape=jax.ShapeDtypeStruct((depth, 8, n), F32),
        compiler_params=_params(("arbitrary", "arbitrary"), 40),
        name="ada_mods",
    )(cvecs, ada_w, ada_b.reshape(depth, 1, n))


def _norm_mod_kernel(x_ref, g_ref, sh_ref, sc_ref, o_ref, *, n_lat_tiles):
    x = x_ref[...]
    y = x * lax.rsqrt(jnp.mean(x * x, axis=-1, keepdims=True) + EPS) * g_ref[...]
    is_ctx = pl.program_id(0) >= n_lat_tiles
    sh = jnp.where(is_ctx, sh_ref[1:2, :], sh_ref[0:1, :])
    sc = jnp.where(is_ctx, sc_ref[1:2, :], sc_ref[0:1, :])
    o_ref[...] = (y * (1.0 + sc) + sh).astype(o_ref.dtype)


def _norm_mod(x_all, g, mods_l, n_lat):
    t, d = x_all.shape
    tr = 256
    assert t % tr == 0 and n_lat % tr == 0
    return pl.pallas_call(
        functools.partial(_norm_mod_kernel, n_lat_tiles=n_lat // tr),
        grid=(t // tr,),
        in_specs=[pl.BlockSpec((tr, d), lambda i: (i, 0)),
                  pl.BlockSpec((1, d), lambda i: (0, 0)),
                  pl.BlockSpec((8, d), lambda i: (0, 0)),
                  pl.BlockSpec((8, d), lambda i: (0, 1))],
        out_specs=pl.BlockSpec((tr, d), lambda i: (i, 0)),
        out_shape=jax.ShapeDtypeStruct((t, d), BF16),
        compiler_params=_params(("arbitrary",), 40),
        name="norm_mod",
    )(x_all, g.reshape(1, d), mods_l, mods_l)


def _matmul_kernel(a_ref, b_ref, o_ref):
    o_ref[...] = jnp.dot(a_ref[...], b_ref[...], preferred_element_type=F32).astype(o_ref.dtype)


def _in_proj(h, w):
    m, k = h.shape
    n = w.shape[1]
    tm = _largest_tile(m, 1024, 256)
    tn = _largest_tile(n, 1024, 256)
    return pl.pallas_call(
        _matmul_kernel,
        grid=(m // tm, n // tn),
        in_specs=[pl.BlockSpec((tm, k), lambda i, j: (i, 0)),
                  pl.BlockSpec((k, tn), lambda i, j: (0, j))],
        out_specs=pl.BlockSpec((tm, tn), lambda i, j: (i, j)),
        out_shape=jax.ShapeDtypeStruct((m, n), BF16),
        compiler_params=_params(("arbitrary", "arbitrary"), 52),
        name="in_proj",
    )(h, w)


def _out_proj_kernel(a1_ref, a2_ref, w1_ref, w2_ref, x_ref, gate_ref, o_ref, *, tm, n_lat):
    y = jnp.dot(a1_ref[...], w1_ref[...], preferred_element_type=F32)
    y = y + jnp.dot(a2_ref[...], w2_ref[...], preferred_element_type=F32)
    rows = pl.program_id(0) * tm + lax.broadcasted_iota(jnp.int32, (tm, 1), 0)
    gate = jnp.where(rows >= n_lat, gate_ref[1:2, :], gate_ref[0:1, :])
    o_ref[...] = x_ref[...] + gate * y


def _out_proj(a1, a1_blk, a2, a2_blk, w, x_all, mods_l, n_lat):
    t, d = x_all.shape
    kh = w.shape[0] // 2
    tm = _largest_tile(t, 1024, 256)
    tn = _largest_tile(d, 1024, 256)
    gate_blk0 = 2 * d // tn
    return pl.pallas_call(
        functools.partial(_out_proj_kernel, tm=tm, n_lat=n_lat),
        grid=(t // tm, d // tn),
        in_specs=[pl.BlockSpec((tm, kh), lambda i, j: (i, a1_blk)),
                  pl.BlockSpec((tm, kh), lambda i, j: (i, a2_blk)),
                  pl.BlockSpec((kh, tn), lambda i, j: (0, j)),
                  pl.BlockSpec((kh, tn), lambda i, j: (1, j)),
                  pl.BlockSpec((tm, tn), lambda i, j: (i, j)),
                  pl.BlockSpec((8, tn), lambda i, j: (0, gate_blk0 + j))],
        out_specs=pl.BlockSpec((tm, tn), lambda i, j: (i, j)),
        out_shape=jax.ShapeDtypeStruct((t, d), F32),
        compiler_params=_params(("arbitrary", "arbitrary"), 52),
        name="out_proj",
    )(a1, a2, w, w, x_all, mods_l)


def _qk_prep_kernel(u_ref, g_ref, cos_ref, sin_ref, o_ref, *, n_heads, rope):
    for h in range(n_heads):
        sl = slice(h * HEAD_DIM, (h + 1) * HEAD_DIM)
        x = u_ref[:, sl].astype(F32)
        y = x * lax.rsqrt(jnp.mean(x * x, axis=-1, keepdims=True) + EPS) * g_ref[:, sl]
        if rope:
            lane = lax.broadcasted_iota(jnp.int32, y.shape, 1)
            partner = jnp.where((lane % 64) < 32,
                                pltpu.roll(y, HEAD_DIM - 32, 1), pltpu.roll(y, 32, 1))
            y = y * cos_ref[...] + partner * sin_ref[...]
        o_ref[:, sl] = y.astype(o_ref.dtype)


def _qk_prep(u, col0, gvec, cos_t, sin_t, rope):
    t = u.shape[0]
    width = gvec.shape[1]
    bw = 512 if width % 512 == 0 and col0 % 512 == 0 else 256
    assert width % bw == 0 and col0 % bw == 0
    tr = _largest_tile(t, 1024, 256)
    blk0 = col0 // bw
    return pl.pallas_call(
        functools.partial(_qk_prep_kernel, n_heads=bw // HEAD_DIM, rope=rope),
        grid=(t // tr, width // bw),
        in_specs=[pl.BlockSpec((tr, bw), lambda i, j: (i, blk0 + j)),
                  pl.BlockSpec((1, bw), lambda i, j: (0, j)),
                  pl.BlockSpec((tr, HEAD_DIM), lambda i, j: (i, 0)),
                  pl.BlockSpec((tr, HEAD_DIM), lambda i, j: (i, 0))],
        out_specs=pl.BlockSpec((tr, bw), lambda i, j: (i, j)),
        out_shape=jax.ShapeDtypeStruct((t, width), BF16),
        compiler_params=_params(("arbitrary", "arbitrary"), 32),
        name="qk_prep",
    )(u, gvec, cos_t, sin_t)


def _rope_tables(n_lat, n_ctx):
    quarter = HEAD_DIM // 4
    inv = ROPE_THETA ** (-jnp.arange(quarter, dtype=F32) / quarter)
    tok = jnp.arange(n_lat)
    rows = (tok // GRID_W).astype(F32)[:, None] * inv[None]
    cols = (tok % GRID_W).astype(F32)[:, None] * inv[None]
    cos = jnp.concatenate([jnp.cos(rows), jnp.cos(rows), jnp.cos(cols), jnp.cos(cols)], axis=-1)
    sin = jnp.concatenate([-jnp.sin(rows), jnp.sin(rows), -jnp.sin(cols), jnp.sin(cols)], axis=-1)
    cos = jnp.concatenate([cos, jnp.ones((n_ctx, HEAD_DIM), F32)], axis=0)
    sin = jnp.concatenate([sin, jnp.zeros((n_ctx, HEAD_DIM), F32)], axis=0)
    return cos, sin


def _gqa_kernel(q_ref, k_ref, v_ref, gate_ref, o_ref, qst, m_s, l_s, acc_s,
                *, group, tq, tk, n_lat):
    qi = pl.program_id(1)
    kv = pl.program_id(2)
    n_kv = pl.num_programs(2)

    @pl.when(kv == 0)
    def _():
        for j in range(group):
            qst[j * tq:(j + 1) * tq, :] = q_ref[:, j * HEAD_DIM:(j + 1) * HEAD_DIM]
        m_s[...] = jnp.full(m_s.shape, NEG_INF, F32)
        l_s[...] = jnp.zeros(l_s.shape, F32)
        acc_s[...] = jnp.zeros(acc_s.shape, F32)

    def step(mask_latent_keys):
        s = lax.dot_general(qst[...], k_ref[...], (((1,), (1,)), ((), ())),
                            preferred_element_type=F32)
        if mask_latent_keys:
            col = kv * tk + lax.broadcasted_iota(jnp.int32, (1, tk), 1)
            s = jnp.where(col >= n_lat, s, NEG_INF)
        m_prev = m_s[...]
        m_new = jnp.maximum(m_prev, jnp.max(s, axis=-1, keepdims=True))
        alpha = jnp.exp(m_prev - m_new)
        p = jnp.exp(s - m_new)
        l_s[...] = alpha * l_s[...] + jnp.sum(p, axis=-1, keepdims=True)
        acc_s[...] = alpha * acc_s[...] + jnp.dot(p.astype(BF16), v_ref[...],
                                                  preferred_element_type=F32)
        m_s[...] = m_new

    is_ctx = qi * tq >= n_lat

    @pl.when(jnp.logical_not(is_ctx))
    def _():
        step(False)

    @pl.when(jnp.logical_and(is_ctx, (kv + 1) * tk > n_lat))
    def _():
        step(True)

    @pl.when(kv == n_kv - 1)
    def _():
        o = acc_s[...] / l_s[...]
        for j in range(group):
            sl = slice(j * HEAD_DIM, (j + 1) * HEAD_DIM)
            g = gate_ref[:, sl].astype(F32)
            o_ref[:, sl] = (o[j * tq:(j + 1) * tq, :] * _silu(g)).astype(o_ref.dtype)


def _gqa_attention(qk, u, n_q_heads, n_kv_heads, off_v, off_gate, n_lat):
    t = u.shape[0]
    group = n_q_heads // n_kv_heads
    gw = group * HEAD_DIM
    qw = n_q_heads * HEAD_DIM
    tq = 256
    tk = _largest_tile(t, 1024, 256)
    assert t % tq == 0 and n_lat % tq == 0 and off_gate % gw == 0 and off_v % HEAD_DIM == 0
    k_blk0 = qw // HEAD_DIM
    v_blk0 = off_v // HEAD_DIM
    g_blk0 = off_gate // gw
    return pl.pallas_call(
        functools.partial(_gqa_kernel, group=group, tq=tq, tk=tk, n_lat=n_lat),
        grid=(n_kv_heads, t // tq, t // tk),
        in_specs=[pl.BlockSpec((tq, gw), lambda g, i, k: (i, g)),
                  pl.BlockSpec((tk, HEAD_DIM), lambda g, i, k: (k, k_blk0 + g)),
                  pl.BlockSpec((tk, HEAD_DIM), lambda g, i, k: (k, v_blk0 + g)),
                  pl.BlockSpec((tq, gw), lambda g, i, k: (i, g_blk0 + g))],
        out_specs=pl.BlockSpec((tq, gw), lambda g, i, k: (i, g)),
        out_shape=jax.ShapeDtypeStruct((t, qw), BF16),
        scratch_shapes=[pltpu.VMEM((group * tq, HEAD_DIM), BF16),
                        pltpu.VMEM((group * tq, 1), F32),
                        pltpu.VMEM((group * tq, 1), F32),
                        pltpu.VMEM((group * tq, HEAD_DIM), F32)],
        compiler_params=_params(("arbitrary", "arbitrary", "arbitrary"), 48),
        name="gqa_attention",
    )(qk, qk, u, u)


def _na_kernel(q_ref, k0_ref, k1_ref, k2_ref, v0_ref, v1_ref, v2_ref, kc_ref, vc_ref,
               bias_ref, gate_ref, o_ref, *, n_lat_blocks):
    qb = pl.program_id(1)
    q = q_ref[...]
    dn = (((1,), (1,)), ((), ()))
    s_ctx = lax.dot_general(q, kc_ref[...], dn, preferred_element_type=F32)
    gate = _silu(gate_ref[...].astype(F32))

    @pl.when(qb < n_lat_blocks)
    def _():
        k = jnp.concatenate([k0_ref[...], k1_ref[...], k2_ref[...]], axis=0)
        v = jnp.concatenate([v0_ref[...], v1_ref[...], v2_ref[...]], axis=0)
        s_win = lax.dot_general(q, k, dn, preferred_element_type=F32) + bias_ref[...]
        m = jnp.maximum(jnp.max(s_win, axis=-1, keepdims=True),
                        jnp.max(s_ctx, axis=-1, keepdims=True))
        p_win = jnp.exp(s_win - m)
        p_ctx = jnp.exp(s_ctx - m)
        den = jnp.sum(p_win, axis=-1, keepdims=True) + jnp.sum(p_ctx, axis=-1, keepdims=True)
        o = jnp.dot(p_win.astype(BF16), v, preferred_element_type=F32)
        o = o + jnp.dot(p_ctx.astype(BF16), vc_ref[...], preferred_element_type=F32)
        o_ref[...] = (o / den * gate).astype(o_ref.dtype)

    @pl.when(qb >= n_lat_blocks)
    def _():
        m = jnp.max(s_ctx, axis=-1, keepdims=True)
        p = jnp.exp(s_ctx - m)
        den = jnp.sum(p, axis=-1, keepdims=True)
        o = jnp.dot(p.astype(BF16), vc_ref[...], preferred_element_type=F32)
        o_ref[...] = (o / den * gate).astype(o_ref.dtype)


def _na_bias_table(rel_bias, n_rows):
    qn = NA_Q_ROWS * GRID_W
    kn = NA_WIN_ROWS * GRID_W
    qa, qc = np.divmod(np.arange(qn), GRID_W)
    ki, kj = np.divmod(np.arange(kn), GRID_W)
    tables = []
    for r0, ws in ((0, 0), (NA_Q_ROWS, 0), (n_rows - NA_Q_ROWS, n_rows - NA_WIN_ROWS)):
        r = (r0 + qa)[:, None]
        c = qc[:, None]
        kr = (ws + ki)[None, :]
        kc = kj[None, :]
        rs = np.clip(r - NA_KH // 2, 0, n_rows - NA_KH)
        cs = np.clip(c - NA_KW // 2, 0, GRID_W - NA_KW)
        valid = (kr >= rs) & (kr < rs + NA_KH) & (kc >= cs) & (kc < cs + NA_KW)
        drow = np.clip(kr - r + NA_KH - 1, 0, 2 * NA_KH - 2)
        dcol = np.clip(kc - c + NA_KW - 1, 0, 2 * NA_KW - 2)
        tbl = rel_bias[:, drow, dcol].astype(F32)
        tables.append(jnp.where(jnp.asarray(valid)[None], tbl, NEG_INF))
    return jnp.stack(tables, axis=0)


def _na_attention(qk, u, n_heads, off_v, off_gate, bias_tbl, n_lat):
    t = u.shape[0]
    w = n_heads * HEAD_DIM
    tb = NA_Q_ROWS * GRID_W
    n_lat_blocks = n_lat // tb
    n_blocks = t // tb
    assert t - n_lat == tb and n_lat_blocks >= 3
    k_blk0 = w // HEAD_DIM
    v_blk0 = off_v // HEAD_DIM
    g_blk0 = off_gate // HEAD_DIM
    ctx_blk = n_blocks - 1

    def win(j):
        return lambda h, i: (jnp.clip(i - 1, 0, n_lat_blocks - 3) + j, 0)

    def k_spec(j):
        f = win(j)
        return pl.BlockSpec((tb, HEAD_DIM), lambda h, i: (f(h, i)[0], k_blk0 + h))

    def v_spec(j):
        f = win(j)
        return pl.BlockSpec((tb, HEAD_DIM), lambda h, i: (f(h, i)[0], v_blk0 + h))

    def bias_cls(h, i):
        return (jnp.where(i == 0, 0, jnp.where(i >= n_lat_blocks - 1, 2, 1)), h, 0, 0)

    return pl.pallas_call(
        functools.partial(_na_kernel, n_lat_blocks=n_lat_blocks),
        grid=(n_heads, n_blocks),
        in_specs=[pl.BlockSpec((tb, HEAD_DIM), lambda h, i: (i, h)),
                  k_spec(0), k_spec(1), k_spec(2), v_spec(0), v_spec(1), v_spec(2),
                  pl.BlockSpec((tb, HEAD_DIM), lambda h, i: (ctx_blk, k_blk0 + h)),
                  pl.BlockSpec((tb, HEAD_DIM), lambda h, i: (ctx_blk, v_blk0 + h)),
                  pl.BlockSpec((None, None, tb, NA_WIN_ROWS * GRID_W), bias_cls),
                  pl.BlockSpec((tb, HEAD_DIM), lambda h, i: (i, g_blk0 + h))],
        out_specs=pl.BlockSpec((tb, HEAD_DIM), lambda h, i: (i, h)),
        out_shape=jax.ShapeDtypeStruct((t, w), BF16),
        compiler_params=_params(("arbitrary", "arbitrary"), 32),
        name="na_attention",
    )(qk, qk, qk, qk, u, u, u, qk, u, bias_tbl, u)


def _short_conv_kernel(x_ref, prev_ref, next_ref, w_ref, o_ref, *, tr):
    i = pl.program_id(0)
    n = pl.num_programs(0)
    x = x_ref[...].astype(F32)
    halo = prev_ref.shape[0]
    prev_row = jnp.where(i == 0, 0.0, prev_ref[halo - 1:halo, :].astype(F32))
    next_row = jnp.where(i == n - 1, 0.0, next_ref[0:1, :].astype(F32))
    row = lax.broadcasted_iota(jnp.int32, x.shape, 0)
    up = jnp.where(row == 0, prev_row, pltpu.roll(x, 1, 0))
    down = jnp.where(row == tr - 1, next_row, pltpu.roll(x, tr - 1, 0))
    o_ref[...] = (up * w_ref[0:1, :] + x * w_ref[1:2, :] + down * w_ref[2:3, :]).astype(o_ref.dtype)


def _short_conv(u, width, conv_w, n_lat):
    tr = _largest_tile(n_lat, 512, 256)
    tc = _largest_tile(width, 1024, 256)
    halo = 16
    hb = tr // halo
    n_tiles = n_lat // tr
    wpad = jnp.zeros((8, width), F32).at[:HY_SHORT].set(conv_w.astype(F32))
    return pl.pallas_call(
        functools.partial(_short_conv_kernel, tr=tr),
        grid=(n_tiles, width // tc),
        in_specs=[pl.BlockSpec((tr, tc), lambda i, j: (i, j)),
                  pl.BlockSpec((halo, tc), lambda i, j: (jnp.maximum(i * hb - 1, 0), j)),
                  pl.BlockSpec((halo, tc), lambda i, j: (jnp.minimum((i + 1) * hb, n_tiles * hb - 1), j)),
                  pl.BlockSpec((8, tc), lambda i, j: (0, j))],
        out_specs=pl.BlockSpec((tr, tc), lambda i, j: (i, j)),
        out_shape=jax.ShapeDtypeStruct((n_lat, width), BF16),
        compiler_params=_params(("arbitrary", "arbitrary"), 32),
        name="short_conv",
    )(u, u, u, wpad)


def _hy_gate1_kernel(v_ref, x1_ref, c_ref, d_ref, o_ref):
    v = v_ref[...].astype(F32)
    o_ref[...] = (x1_ref[...].astype(F32) * (c_ref[...].astype(F32) + v * d_ref[0:1, :])).astype(o_ref.dtype)


def _hy_gate1(vx, c1, dskip8, hy):
    n_lat = vx.shape[0]
    tr = _largest_tile(n_lat, 1024, 256)
    tc = _largest_tile(hy, 512, 256)
    nb = hy // tc
    return pl.pallas_call(
        _hy_gate1_kernel,
        grid=(n_lat // tr, nb),
        in_specs=[pl.BlockSpec((tr, tc), lambda i, j: (i, j)),
                  pl.BlockSpec((tr, tc), lambda i, j: (i, nb + j)),
                  pl.BlockSpec((tr, tc), lambda i, j: (i, j)),
                  pl.BlockSpec((8, tc), lambda i, j: (0, j))],
        out_specs=pl.BlockSpec((tr, tc), lambda i, j: (i, j)),
        out_shape=jax.ShapeDtypeStruct((n_lat, hy), BF16),
        compiler_params=_params(("arbitrary", "arbitrary"), 32),
        name="hy_gate1",
    )(vx, vx, c1, dskip8)


def _hy_gate2_kernel(x2_ref, c_ref, y1_ref, d_ref, g_ref, ctx_ref, o_ref, *, n_lat_tiles):
    i = pl.program_id(0)

    @pl.when(i < n_lat_tiles)
    def _():
        y1 = y1_ref[...].astype(F32)
        z = x2_ref[...].astype(F32) * (c_ref[...].astype(F32) + y1 * d_ref[1:2, :])
        o_ref[...] = (z * _silu(g_ref[...].astype(F32))).astype(o_ref.dtype)

    @pl.when(i >= n_lat_tiles)
    def _():
        o_ref[...] = ctx_ref[...]


def _hy_gate2(vx, c2, y1, dskip8, u, off_gate, hy_ctx, hy):
    t = u.shape[0]
    n_lat = vx.shape[0]
    tr = t - n_lat
    tc = _largest_tile(hy, 512, 256)
    nb = hy // tc
    n_lat_tiles = n_lat // tr
    assert n_lat % tr == 0 and off_gate % tc == 0
    g_blk0 = off_gate // tc
    lat = lambda i: jnp.minimum(i, n_lat_tiles - 1)
    return pl.pallas_call(
        functools.partial(_hy_gate2_kernel, n_lat_tiles=n_lat_tiles),
        grid=(t // tr, nb),
        in_specs=[pl.BlockSpec((tr, tc), lambda i, j: (lat(i), 2 * nb + j)),
                  pl.BlockSpec((tr, tc), lambda i, j: (lat(i), j)),
                  pl.BlockSpec((tr, tc), lambda i, j: (lat(i), j)),
                  pl.BlockSpec((8, tc), lambda i, j: (0, j)),
                  pl.BlockSpec((tr, tc), lambda i, j: (i, g_blk0 + j)),
                  pl.BlockSpec((tr, tc), lambda i, j: (0, j))],
        out_specs=pl.BlockSpec((tr, tc), lambda i, j: (i, j)),
        out_shape=jax.ShapeDtypeStruct((t, hy), BF16),
        compiler_params=_params(("arbitrary", "arbitrary"), 32),
        name="hy_gate2",
    )(vx, c2, y1, dskip8, u, hy_ctx)


def _filter_hidden_kernel(z_ref, w1_ref, b1_ref, w2_ref, b2_ref, fr_ref, o_ref):
    hp = lax.Precision.HIGHEST
    fr = fr_ref[...]
    h = jnp.sin(fr * (jnp.dot(z_ref[...], w1_ref[...], precision=hp,
                              preferred_element_type=F32) + b1_ref[...]))
    o_ref[...] = jnp.sin(fr * (jnp.dot(h, w2_ref[...], precision=hp,
                                       preferred_element_type=F32) + b2_ref[...]))


def _pad2(a, rows, cols):
    return jnp.zeros((rows, cols), F32).at[:a.shape[0], :a.shape[1]].set(a.astype(F32))


def _filter_hidden(feat, w1, b1, w2, b2, freq):
    r = feat.shape[0]
    tr = _largest_tile(r, 1024, 256)
    full = lambda i: (0, 0)
    return pl.pallas_call(
        _filter_hidden_kernel,
        grid=(r // tr,),
        in_specs=[pl.BlockSpec((tr, LANE), lambda i: (i, 0)),
                  pl.BlockSpec((LANE, LANE), full), pl.BlockSpec((1, LANE), full),
                  pl.BlockSpec((LANE, LANE), full), pl.BlockSpec((1, LANE), full),
                  pl.BlockSpec((1, LANE), full)],
        out_specs=pl.BlockSpec((tr, LANE), lambda i: (i, 0)),
        out_shape=jax.ShapeDtypeStruct((r, LANE), F32),
        compiler_params=_params(("arbitrary",), 32),
        name="filter_hidden",
    )(feat, _pad2(w1, LANE, LANE), _pad2(b1[None], 1, LANE), _pad2(w2, LANE, LANE),
      _pad2(b2[None], 1, LANE), _pad2(freq[None], 1, LANE))


def _two_sided_positions(n):
    idx = np.arange(2 * n)
    return np.where(idx < n, idx, 2 * n - idx) % n


def _filter_features(n, n_bands):
    pos = _two_sided_positions(n)
    t = jnp.linspace(0.0, 1.0, n, dtype=F32)[:, None]
    w = (2.0 * math.pi / n) * jnp.arange(n, dtype=F32)[:, None]
    bands = jnp.linspace(1e-4, n_bands - 1, n_bands, dtype=F32)[None, :]
    z = jnp.concatenate([t, jnp.cos(bands * w), -jnp.sin(bands * w)], axis=-1)
    z = z[pos]
    return _pad2(z, 2 * n, LANE), t[pos]


def _filter_gen_kernel(h_ref, w_ref, t_ref, dec_ref, o_ref, *, tr, n):
    i = pl.program_id(0)
    f = jnp.dot(h_ref[...].astype(BF16), w_ref[...], preferred_element_type=F32)
    f = f * jnp.exp(-t_ref[...] * jnp.abs(dec_ref[0:1, :]))
    row = i * tr + lax.broadcasted_iota(jnp.int32, (tr, 1), 0)
    o_ref[...] = jnp.where(row == n, 0.0, f).astype(o_ref.dtype)


def _filter_gen(hidden, row0, w3p, tcol, decay8, n, order, hy):
    tr = _largest_tile(n, 512, 256)
    tc = _largest_tile(hy, 1024, 256)
    nb = hy // tc
    n_half = n // tr
    hb0 = row0 // tr
    assert row0 % tr == 0
    return pl.pallas_call(
        functools.partial(_filter_gen_kernel, tr=tr, n=n),
        grid=(2 * n // tr, order * nb),
        in_specs=[pl.BlockSpec((tr, LANE), lambda i, j: (hb0 + i, 0)),
                  pl.BlockSpec((LANE, tc), lambda i, j: (0, (j // nb) * 2 * nb + (i // n_half) * nb + j % nb)),
                  pl.BlockSpec((tr, 1), lambda i, j: (i, 0)),
                  pl.BlockSpec((None, 8, tc), lambda i, j: (j // nb, 0, j % nb))],
        out_specs=pl.BlockSpec((tr, tc), lambda i, j: (i, j)),
        out_shape=jax.ShapeDtypeStruct((2 * n, order * hy), BF16),
        compiler_params=_params(("arbitrary", "arbitrary"), 32),
        name="filter_gen",
    )(hidden, w3p, tcol, decay8)


def _dft_tables(n1, nz):
    n2 = FFT_N2
    n = n1 * n2
    k1 = jnp.arange(n1)
    two_pi = 2.0 * math.pi

    def cs(idx, period):
        ang = (idx % period).astype(F32) * (two_pi / period)
        return jnp.cos(ang), -jnp.sin(ang)

    eye8 = jnp.eye(8, dtype=F32)
    fr, fi = cs(k1[:, None] * jnp.arange(nz)[None, :], n1)
    kf = jnp.stack([fr, fi], axis=1)
    kf = jnp.einsum('kpn,jl->kpjnl', kf, eye8).reshape(n1 * 16, nz * 8)
    half = n1 // 2
    gr, gi = cs(jnp.arange(half)[:, None] * k1[None, :], n1)
    ki = jnp.stack([gr, gi], axis=2) / n
    ki = jnp.einsum('nkp,jl->njkpl', ki, eye8).reshape(half * 8, n1 * 16)
    r = jnp.arange(2 * n2)
    part = (r % 16) // 8
    n2_of_r = 8 * (r // 16) + r % 8
    k2 = jnp.arange(n2)
    idx = k2[None, :, None] * n2_of_r[None, None, :] * n1 + k1[:, None, None] * n2_of_r[None, None, :]
    cr, ci = cs(idx, n)
    top = jnp.where(part[None, None, :] == 0, cr, -ci)
    bot = jnp.where(part[None, None, :] == 0, ci, cr)
    m = jnp.concatenate([top, bot], axis=1)
    return kf.astype(BF16), ki.astype(BF16), m.astype(BF16)


def _cross_slab_forward(z_ref, kf_ref, a_scr, nz, n1, ct):
    def body(b, carry):
        z = z_ref[:, pl.ds(pl.multiple_of(16 * b, 16), 16), :].astype(F32)
        for half in range(2):
            g = z[:, 8 * half:8 * half + 8, :].reshape(nz * 8, ct).astype(BF16)
            p = jnp.dot(kf_ref[...], g, preferred_element_type=F32)
            start = pl.multiple_of(32 * b + 16 * half, 16)
            a_scr[:, pl.ds(start, 16), :] = p.astype(BF16).reshape(n1, 16, ct)
        return carry

    lax.fori_loop(0, FFT_N2 // 16, body, 0)


def _fft_fwd_kernel(z_ref, kf_ref, m_ref, o_ref, a_scr, *, nz, n1, kb, ct):
    kblk = pl.program_id(1)

    @pl.when(kblk == 0)
    def _():
        _cross_slab_forward(z_ref, kf_ref, a_scr, nz, n1, ct)

    def slab(k, carry):
        x = jnp.dot(m_ref[k], a_scr[kblk * kb + k], preferred_element_type=F32)
        o_ref[k] = x.astype(o_ref.dtype)
        return carry

    lax.fori_loop(0, kb, slab, 0)


def _fft_forward(z3, kf, m, ct=256):
    nz, _, c = z3.shape
    n1 = m.shape[0]
    kb = min(16, n1)
    return pl.pallas_call(
        functools.partial(_fft_fwd_kernel, nz=nz, n1=n1, kb=kb, ct=ct),
        grid=(c // ct, n1 // kb),
        in_specs=[pl.BlockSpec((nz, FFT_N2, ct), lambda j, k: (0, 0, j)),
                  pl.BlockSpec(kf.shape, lambda j, k: (0, 0)),
                  pl.BlockSpec((kb, 2 * FFT_N2, 2 * FFT_N2), lambda j, k: (k, 0, 0))],
        out_specs=pl.BlockSpec((kb, 2 * FFT_N2, ct), lambda j, k: (k, 0, j)),
        out_shape=jax.ShapeDtypeStruct((n1, 2 * FFT_N2, c), BF16),
        scratch_shapes=[pltpu.VMEM((n1, 2 * FFT_N2, ct), BF16)],
        compiler_params=_params(("arbitrary", "arbitrary"), 56),
        name="fft_forward",
    )(z3, kf, m)


def _fft_conv_kernel(z_ref, kf_ref, ki_ref, m_ref, h_ref, o_ref, a_scr, *, nz, n1, kb, ct):
    kblk = pl.program_id(1)
    n2 = FFT_N2

    @pl.when(kblk == 0)
    def _():
        _cross_slab_forward(z_ref, kf_ref, a_scr, nz, n1, ct)

    def slab(k, carry):
        mk = m_ref[k]
        x = jnp.dot(mk, a_scr[kblk * kb + k], preferred_element_type=F32)
        h = h_ref[k].astype(F32)
        xr, xi, hr, hi = x[:n2], x[n2:], h[:n2], h[n2:]
        y = jnp.concatenate([xr * hr - xi * hi, xr * hi + xi * hr], axis=0).astype(BF16)
        b = lax.dot_general(mk, y, (((0,), (0,)), ((), ())), preferred_element_type=F32)
        a_scr[kblk * kb + k] = b.astype(BF16)
        return carry

    lax.fori_loop(0, kb, slab, 0)

    @pl.when(kblk == pl.num_programs(1) - 1)
    def _():
        def body(b, carry):
            halves = []
            for half in range(2):
                start = pl.multiple_of(32 * b + 16 * half, 16)
                r = a_scr[:, pl.ds(start, 16), :].reshape(n1 * 16, ct)
                p = jnp.dot(ki_ref[...], r, preferred_element_type=F32)
                halves.append(p.reshape(nz, 8, ct))
            out = jnp.concatenate(halves, axis=1)
            o_ref[:, pl.ds(pl.multiple_of(16 * b, 16), 16), :] = out.astype(o_ref.dtype)
            return carry

        lax.fori_loop(0, n2 // 16, body, 0)


def _fft_conv(z3, z_blk0, spec, spec_blk0, kf, ki, m, c, ct=256):
    nz = z3.shape[0]
    n1 = m.shape[0]
    kb = min(16, n1)
    assert n1 == 2 * nz
    return pl.pallas_call(
        functools.partial(_fft_conv_kernel, nz=nz, n1=n1, kb=kb, ct=ct),
        grid=(c // ct, n1 // kb),
        in_specs=[pl.BlockSpec((nz, FFT_N2, ct), lambda j, k: (0, 0, z_blk0 + j)),
                  pl.BlockSpec(kf.shape, lambda j, k: (0, 0)),
                  pl.BlockSpec(ki.shape, lambda j, k: (0, 0)),
                  pl.BlockSpec((kb, 2 * FFT_N2, 2 * FFT_N2), lambda j, k: (k, 0, 0)),
                  pl.BlockSpec((kb, 2 * FFT_N2, ct), lambda j, k: (k, 0, spec_blk0 + j))],
        out_specs=pl.BlockSpec((nz, FFT_N2, ct), lambda j, k: (0, 0, j)),
        out_shape=jax.ShapeDtypeStruct((nz, FFT_N2, c), BF16),
        scratch_shapes=[pltpu.VMEM((n1, 2 * FFT_N2, ct), BF16)],
        compiler_params=_params(("arbitrary", "arbitrary"), 56),
        name="fft_conv",
    )(z3, kf, ki, m, spec)


def _ctx_dft_tables(n):
    nn = 2 * n
    k = jnp.arange(nn)
    ang = ((k[:, None] * k[None, :]) % nn).astype(F32) * (2.0 * math.pi / nn)
    cr, ci = jnp.cos(ang), -jnp.sin(ang)
    full = jnp.concatenate([cr, ci], axis=0)
    inv = jnp.concatenate([cr[:n], ci[:n]], axis=1) / nn
    return full[:, :n].astype(BF16), full.astype(BF16), inv.astype(BF16)


def _ctx_hyena_kernel(v_ref, x1_ref, x2_ref, g_ref, cw0_ref, cw1_ref, cw2_ref, hid_ref,
                      w10_ref, w11_ref, w20_ref, w21_ref, t_ref, dec_ref, dsk_ref,
                      ff_ref, ffull_ref, finv_ref, o_ref, *, n):
    row = lax.broadcasted_iota(jnp.int32, (n, 1), 0)

    def sconv(x_ref, w_ref):
        x = x_ref[...].astype(F32)
        up = jnp.where(row == 0, 0.0, pltpu.roll(x, 1, 0))
        down = jnp.where(row == n - 1, 0.0, pltpu.roll(x, n - 1, 0))
        y = up * w_ref[0:1, :] + x * w_ref[1:2, :] + down * w_ref[2:3, :]
        return y.astype(BF16).astype(F32)

    hid = hid_ref[...].astype(BF16)
    tcol = t_ref[...]
    row2 = lax.broadcasted_iota(jnp.int32, (2 * n, 1), 0)

    def spectrum(wf_ref, wb_ref, o):
        fwd = jnp.dot(hid[:n], wf_ref[...], preferred_element_type=F32)
        bwd = jnp.dot(hid[n:], wb_ref[...], preferred_element_type=F32)
        f = jnp.concatenate([fwd, bwd], axis=0) * jnp.exp(-tcol * jnp.abs(dec_ref[o:o + 1, :]))
        f = jnp.where(row2 == n, 0.0, f).astype(BF16)
        return jnp.dot(ffull_ref[...], f, preferred_element_type=F32)

    def conv(z, h):
        x = jnp.dot(ff_ref[...], z.astype(BF16), preferred_element_type=F32)
        h = h.astype(BF16).astype(F32)
        nn = 2 * n
        xr, xi, hr, hi = x[:nn], x[nn:], h[:nn], h[nn:]
        y = jnp.concatenate([xr * hr - xi * hi, xr * hi + xi * hr], axis=0).astype(BF16)
        return jnp.dot(finv_ref[...], y, preferred_element_type=F32)

    v = sconv(v_ref, cw0_ref)
    x1 = sconv(x1_ref, cw1_ref)
    x2 = sconv(x2_ref, cw2_ref)
    c1 = conv(v, spectrum(w10_ref, w11_ref, 0)).astype(BF16).astype(F32)
    y1 = (x1 * (c1 + v * dsk_ref[0:1, :])).astype(BF16).astype(F32)
    c2 = conv(y1, spectrum(w20_ref, w21_ref, 1)).astype(BF16).astype(F32)
    z = x2 * (c2 + y1 * dsk_ref[1:2, :])
    o_ref[...] = (z * _silu(g_ref[...].astype(F32))).astype(o_ref.dtype)


def _ctx_hyena(u, n_lat, hy, order, off_gate, conv8, hidden, hid_row0, w3p, tcol, decay8, dskip8, tables):
    t = u.shape[0]
    n = t - n_lat
    assert order == 2 and n_lat % n == 0 and hid_row0 % (2 * n) == 0
    ct = 256
    nb = hy // ct
    rb = n_lat // n
    ff, ffull, finv = tables
    const = lambda j: (0, 0)
    ublk = lambda c0: pl.BlockSpec((n, ct), lambda j: (rb, c0 + j))
    wblk = lambda c0: pl.BlockSpec((LANE, ct), lambda j: (0, c0 + j))
    cwblk = lambda c0: pl.BlockSpec((8, ct), lambda j: (0, c0 + j))
    return pl.pallas_call(
        functools.partial(_ctx_hyena_kernel, n=n),
        grid=(nb,),
        in_specs=[ublk(0), ublk(nb), ublk(2 * nb), ublk(off_gate // ct),
                  cwblk(0), cwblk(nb), cwblk(2 * nb),
                  pl.BlockSpec((2 * n, LANE), lambda j: (hid_row0 // (2 * n), 0)),
                  wblk(0), wblk(nb), wblk(2 * nb), wblk(3 * nb),
                  pl.BlockSpec((2 * n, 1), const),
                  pl.BlockSpec((8, ct), lambda j: (0, j)),
                  pl.BlockSpec((8, ct), lambda j: (0, j)),
                  pl.BlockSpec(ff.shape, const), pl.BlockSpec(ffull.shape, const),
                  pl.BlockSpec(finv.shape, const)],
        out_specs=pl.BlockSpec((n, ct), lambda j: (0, j)),
        out_shape=jax.ShapeDtypeStruct((n, hy), BF16),
        compiler_params=_params(("arbitrary",), 40),
        name="ctx_hyena",
    )(u, u, u, u, conv8, conv8, conv8, hidden, w3p, w3p, w3p, w3p, tcol, decay8, dskip8,
      ff, ffull, finv)


def _even_layer(x_all, mods_l, g, n_lat, w_in, w_out, conv_w, w1, b1, w2, b2, w3, freq, decay, dskip,
                q_norm, k_norm, rope, dft, ctx_dft):
    t, d = x_all.shape
    n_ctx = t - n_lat
    order, hy = decay.shape
    gqa_w = d - hy
    n_q = gqa_w // HEAD_DIM
    n_kv = n_q // 4
    kv_w = n_kv * HEAD_DIM
    off_hy_gate = (order + 1) * hy
    off_q = off_hy_gate + hy
    off_v = off_q + gqa_w + kv_w
    off_at_gate = off_v + kv_w
    scale = HEAD_DIM ** -0.5

    h = _norm_mod(x_all, g, mods_l, n_lat)
    u = _in_proj(h, w_in)

    gvec = jnp.concatenate([jnp.tile(q_norm.astype(F32) * scale, n_q),
                            jnp.tile(k_norm.astype(F32), n_kv)])[None]
    qk = _qk_prep(u, off_q, gvec, rope[0], rope[1], True)
    att = _gqa_attention(qk, u, n_q, n_kv, off_v, off_at_gate, n_lat)

    n_bands = (w1.shape[0] - 1) // 2
    feat_l, t_l = _filter_features(n_lat, n_bands)
    feat_c, t_c = _filter_features(n_ctx, n_bands)
    hidden = _filter_hidden(jnp.concatenate([feat_l, feat_c], axis=0), w1, b1, w2, b2, freq)
    w3p = jnp.zeros((LANE, w3.shape[1]), BF16).at[:w3.shape[0]].set(w3.astype(BF16))
    decay8 = jnp.zeros((order, 8, hy), F32).at[:, 0].set(decay.astype(F32))
    dskip8 = jnp.zeros((8, hy), F32).at[:order].set(dskip.astype(F32))
    conv8 = jnp.zeros((8, conv_w.shape[1]), F32).at[:HY_SHORT].set(conv_w.astype(F32))
    kf_pad, kf_full, ki, m = dft
    n1 = m.shape[0]
    filt = _filter_gen(hidden, 0, w3p, t_l, decay8, n_lat, order, hy)
    spec = _fft_forward(filt.reshape(n1, FFT_N2, order * hy), kf_full, m)

    vx = _short_conv(u, (order + 1) * hy, conv_w, n_lat)
    vx3 = vx.reshape(n1 // 2, FFT_N2, (order + 1) * hy)
    ct = 256
    c1 = _fft_conv(vx3, 0, spec, 0, kf_pad, ki, m, hy).reshape(n_lat, hy)
    y1 = _hy_gate1(vx, c1, dskip8, hy)
    c2 = _fft_conv(y1.reshape(n1 // 2, FFT_N2, hy), 0, spec, hy // ct, kf_pad, ki, m, hy).reshape(n_lat, hy)

    decay_c8 = jnp.zeros((8, hy), F32).at[:order].set(decay.astype(F32))
    hy_ctx = _ctx_hyena(u, n_lat, hy, order, off_hy_gate, conv8, hidden, 2 * n_lat, w3p, t_c,
                        decay_c8, dskip8, ctx_dft)
    hy_mix = _hy_gate2(vx, c2, y1, dskip8, u, off_hy_gate, hy_ctx, hy)

    return _out_proj(hy_mix, 0, att, 0, w_out, x_all, mods_l, n_lat)


def _odd_layer(x_all, mods_l, g, n_lat, w_in, w_out, q_norm, k_norm, rel_bias):
    t, d = x_all.shape
    n_heads = rel_bias.shape[0]
    w = n_heads * HEAD_DIM
    scale = HEAD_DIM ** -0.5
    h = _norm_mod(x_all, g, mods_l, n_lat)
    u = _in_proj(h, w_in)
    gvec = jnp.concatenate([jnp.tile(q_norm.astype(F32) * scale, n_heads),
                            jnp.tile(k_norm.astype(F32), n_heads)])[None]
    dummy = jnp.zeros((t, HEAD_DIM), F32)
    qk = _qk_prep(u, 0, gvec, dummy, dummy, False)
    bias_tbl = _na_bias_table(rel_bias.astype(F32), n_lat // GRID_W)
    att = _na_attention(qk, u, n_heads, 2 * w, 3 * w, bias_tbl, n_lat)
    return _out_proj(att, 0, att, 1, w_out, x_all, mods_l, n_lat)


def kernel(x, c, ctx, c_ctx, norm_g, ada_w, ada_b, e_w_in, e_w_out, hy_conv, hy_w1, hy_b1, hy_w2, hy_b2,
           hy_w3, hy_freq, hy_decay, hy_dskip, gqa_q_norm, gqa_k_norm, o_w_in, o_w_out, na_q_norm,
           na_k_norm, na_rel_bias):
    batch, n_lat, d = x.shape
    n_ctx = ctx.shape[1]
    depth = norm_g.shape[0]
    assert batch == 1 and n_lat % (2 * FFT_N2) == 0 and n_ctx == NA_Q_ROWS * GRID_W

    x_all = jnp.concatenate([x[0], ctx[0]], axis=0).astype(F32)
    cvecs = jnp.zeros((8, d), F32).at[0].set(c[0]).at[1].set(c_ctx)
    mods = _ada_mods(cvecs, ada_w, ada_b)

    rope = _rope_tables(n_lat, n_ctx)
    n1 = 2 * n_lat // FFT_N2
    kf_pad, ki, m = _dft_tables(n1, n1 // 2)
    kf_full, _, _ = _dft_tables(n1, n1)
    dft = (kf_pad, kf_full, ki, m)
    ctx_dft = _ctx_dft_tables(n_ctx)

    for layer in range(depth):
        i = layer // 2
        if layer % 2 == 0:
            x_all = _even_layer(x_all, mods[layer], norm_g[layer], n_lat,
                                e_w_in[i].astype(BF16), e_w_out[i].astype(BF16), hy_conv[i],
                                hy_w1[i], hy_b1[i], hy_w2[i], hy_b2[i], hy_w3[i], hy_freq[i],
                                hy_decay[i], hy_dskip[i], gqa_q_norm[i], gqa_k_norm[i],
                                rope, dft, ctx_dft)
        else:
            x_all = _odd_layer(x_all, mods[layer], norm_g[layer], n_lat,
                               o_w_in[i].astype(BF16), o_w_out[i].astype(BF16),
                               na_q_norm[i], na_k_norm[i], na_rel_bias[i])
    return x_all[:n_lat][None]
```

```python
import functools
import math

import numpy as np
import jax
import jax.numpy as jnp
from jax import lax
from jax.experimental import pallas as pl
from jax.experimental.pallas import tpu as pltpu

F32 = jnp.float32
BF16 = jnp.bfloat16

HEAD_DIM = 128
GRID_W = 64
EPS = 1e-6
ROPE_THETA = 10000.0
NA_KH = 8
NA_KW = 16
NA_Q_ROWS = 4
NA_WIN_ROWS = 12
HY_SHORT = 3
FFT_N2 = 128
NEG_INF = -1e30

V7X_VMEM_BYTES = 64 * 1024 * 1024
LANE = 128


def _params(semantics, vmem_mb):
    assert vmem_mb * 1024 * 1024 < V7X_VMEM_BYTES
    return pltpu.CompilerParams(dimension_semantics=semantics,
                                vmem_limit_bytes=vmem_mb * 1024 * 1024)


def _largest_tile(n, target, quantum):
    best = None
    t = quantum
    while t <= min(n, target):
        if n % t == 0:
            best = t
        t += quantum
    assert best is not None, (n, target, quantum)
    return best


def _silu(x):
    return x * jax.nn.sigmoid(x)


def _mods_kernel(c_ref, w_ref, b_ref, o_ref):
    s = _silu(c_ref[...]).astype(BF16)
    o_ref[...] = jnp.dot(s, w_ref[...].astype(BF16), preferred_element_type=F32) + b_ref[...]


def _ada_mods(cvecs, ada_w, ada_b):
    depth, d, n = ada_w.shape
    tn = _largest_tile(n, 512, LANE)
    return pl.pallas_call(
        _mods_kernel,
        grid=(depth, n // tn),
        in_specs=[pl.BlockSpec((8, d), lambda l, j: (0, 0)),
                  pl.BlockSpec((None, d, tn), lambda l, j: (l, 0, j)),
                  pl.BlockSpec((None, 1, tn), lambda l, j: (l, 0, j))],
        out_specs=pl.BlockSpec((None, 8, tn), lambda l, j: (l, 0, j)),
        out_shape=jax.ShapeDtypeStruct((depth, 8, n), F32),
        compiler_params=_params(("arbitrary", "arbitrary"), 40),
        name="ada_mods",
    )(cvecs, ada_w, ada_b.reshape(depth, 1, n))


def _norm_mod_kernel(x_ref, g_ref, sh_ref, sc_ref, o_ref, *, n_lat_tiles):
    x = x_ref[...]
    y = x * lax.rsqrt(jnp.mean(x * x, axis=-1, keepdims=True) + EPS) * g_ref[...]
    is_ctx = pl.program_id(0) >= n_lat_tiles
    sh = jnp.where(is_ctx, sh_ref[1:2, :], sh_ref[0:1, :])
    sc = jnp.where(is_ctx, sc_ref[1:2, :], sc_ref[0:1, :])
    o_ref[...] = (y * (1.0 + sc) + sh).astype(o_ref.dtype)


def _norm_mod(x_all, g, mods_l, n_lat):
    t, d = x_all.shape
    tr = 256
    assert t % tr == 0 and n_lat % tr == 0
    return pl.pallas_call(
        functools.partial(_norm_mod_kernel, n_lat_tiles=n_lat // tr),
        grid=(t // tr,),
        in_specs=[pl.BlockSpec((tr, d), lambda i: (i, 0)),
                  pl.BlockSpec((1, d), lambda i: (0, 0)),
                  pl.BlockSpec((8, d), lambda i: (0, 0)),
                  pl.BlockSpec((8, d), lambda i: (0, 1))],
        out_specs=pl.BlockSpec((tr, d), lambda i: (i, 0)),
        out_shape=jax.ShapeDtypeStruct((t, d), BF16),
        compiler_params=_params(("arbitrary",), 40),
        name="norm_mod",
    )(x_all, g.reshape(1, d), mods_l, mods_l)


def _matmul_kernel(a_ref, b_ref, o_ref):
    o_ref[...] = jnp.dot(a_ref[...], b_ref[...], preferred_element_type=F32).astype(o_ref.dtype)


def _in_proj(h, w):
    m, k = h.shape
    n = w.shape[1]
    tm = _largest_tile(m, 1024, 256)
    tn = _largest_tile(n, 1024, 256)
    return pl.pallas_call(
        _matmul_kernel,
        grid=(m // tm, n // tn),
        in_specs=[pl.BlockSpec((tm, k), lambda i, j: (i, 0)),
                  pl.BlockSpec((k, tn), lambda i, j: (0, j))],
        out_specs=pl.BlockSpec((tm, tn), lambda i, j: (i, j)),
        out_shape=jax.ShapeDtypeStruct((m, n), BF16),
        compiler_params=_params(("arbitrary", "arbitrary"), 52),
        name="in_proj",
    )(h, w)


def _out_proj_kernel(a1_ref, a2_ref, w1_ref, w2_ref, x_ref, gate_ref, o_ref, *, tm, n_lat):
    y = jnp.dot(a1_ref[...], w1_ref[...], preferred_element_type=F32)
    y = y + jnp.dot(a2_ref[...], w2_ref[...], preferred_element_type=F32)
    rows = pl.program_id(0) * tm + lax.broadcasted_iota(jnp.int32, (tm, 1), 0)
    gate = jnp.where(rows >= n_lat, gate_ref[1:2, :], gate_ref[0:1, :])
    o_ref[...] = x_ref[...] + gate * y


def _out_proj(a1, a1_blk, a2, a2_blk, w, x_all, mods_l, n_lat):
    t, d = x_all.shape
    kh = w.shape[0] // 2
    tm = _largest_tile(t, 1024, 256)
    tn = _largest_tile(d, 1024, 256)
    gate_blk0 = 2 * d // tn
    return pl.pallas_call(
        functools.partial(_out_proj_kernel, tm=tm, n_lat=n_lat),
        grid=(t // tm, d // tn),
        in_specs=[pl.BlockSpec((tm, kh), lambda i, j: (i, a1_blk)),
                  pl.BlockSpec((tm, kh), lambda i, j: (i, a2_blk)),
                  pl.BlockSpec((kh, tn), lambda i, j: (0, j)),
                  pl.BlockSpec((kh, tn), lambda i, j: (1, j)),
                  pl.BlockSpec((tm, tn), lambda i, j: (i, j)),
                  pl.BlockSpec((8, tn), lambda i, j: (0, gate_blk0 + j))],
        out_specs=pl.BlockSpec((tm, tn), lambda i, j: (i, j)),
        out_shape=jax.ShapeDtypeStruct((t, d), F32),
        compiler_params=_params(("arbitrary", "arbitrary"), 52),
        name="out_proj",
    )(a1, a2, w, w, x_all, mods_l)


def _qk_prep_kernel(u_ref, g_ref, cos_ref, sin_ref, o_ref, *, n_heads, rope):
    for h in range(n_heads):
        sl = slice(h * HEAD_DIM, (h + 1) * HEAD_DIM)
        x = u_ref[:, sl].astype(F32)
        y = x * lax.rsqrt(jnp.mean(x * x, axis=-1, keepdims=True) + EPS) * g_ref[:, sl]
        if rope:
            lane = lax.broadcasted_iota(jnp.int32, y.shape, 1)
            partner = jnp.where((lane % 64) < 32,
                                pltpu.roll(y, HEAD_DIM - 32, 1), pltpu.roll(y, 32, 1))
            y = y * cos_ref[...] + partner * sin_ref[...]
        o_ref[:, sl] = y.astype(o_ref.dtype)


def _qk_prep(u, col0, gvec, cos_t, sin_t, rope):
    t = u.shape[0]
    width = gvec.shape[1]
    bw = 512 if width % 512 == 0 and col0 % 512 == 0 else 256
    assert width % bw == 0 and col0 % bw == 0
    tr = _largest_tile(t, 1024, 256)
    blk0 = col0 // bw
    return pl.pallas_call(
        functools.partial(_qk_prep_kernel, n_heads=bw // HEAD_DIM, rope=rope),
        grid=(t // tr, width // bw),
        in_specs=[pl.BlockSpec((tr, bw), lambda i, j: (i, blk0 + j)),
                  pl.BlockSpec((1, bw), lambda i, j: (0, j)),
                  pl.BlockSpec((tr, HEAD_DIM), lambda i, j: (i, 0)),
                  pl.BlockSpec((tr, HEAD_DIM), lambda i, j: (i, 0))],
        out_specs=pl.BlockSpec((tr, bw), lambda i, j: (i, j)),
        out_shape=jax.ShapeDtypeStruct((t, width), BF16),
        compiler_params=_params(("arbitrary", "arbitrary"), 32),
        name="qk_prep",
    )(u, gvec, cos_t, sin_t)


def _rope_tables(n_lat, n_ctx):
    quarter = HEAD_DIM // 4
    inv = ROPE_THETA ** (-jnp.arange(quarter, dtype=F32) / quarter)
    tok = jnp.arange(n_lat)
    rows = (tok // GRID_W).astype(F32)[:, None] * inv[None]
    cols = (tok % GRID_W).astype(F32)[:, None] * inv[None]
    cos = jnp.concatenate([jnp.cos(rows), jnp.cos(rows), jnp.cos(cols), jnp.cos(cols)], axis=-1)
    sin = jnp.concatenate([-jnp.sin(rows), jnp.sin(rows), -jnp.sin(cols), jnp.sin(cols)], axis=-1)
    cos = jnp.concatenate([cos, jnp.ones((n_ctx, HEAD_DIM), F32)], axis=0)
    sin = jnp.concatenate([sin, jnp.zeros((n_ctx, HEAD_DIM), F32)], axis=0)
    return cos, sin


GQA_ONES_ROWS = 16


def _gqa_prep_kernel(q_ref, k_ref, v_ref, g_ref, cos_ref, sin_ref, qt_ref, ko_ref, vt_ref,
                     *, n_q, n_kv):
    cos = cos_ref[...]
    sin = sin_ref[...]
    lane = lax.broadcasted_iota(jnp.int32, cos.shape, 1)
    first_quarter = (lane % 64) < 32

    def norm_rope(x, g):
        y = x * lax.rsqrt(jnp.mean(x * x, axis=-1, keepdims=True) + EPS) * g
        partner = jnp.where(first_quarter, pltpu.roll(y, HEAD_DIM - 32, 1), pltpu.roll(y, 32, 1))
        return y * cos + partner * sin

    for h in range(n_q):
        sl = slice(h * HEAD_DIM, (h + 1) * HEAD_DIM)
        qt_ref[sl, :] = norm_rope(q_ref[:, sl].astype(F32), g_ref[:, sl]).T.astype(qt_ref.dtype)
    for h in range(n_kv):
        sl = slice(h * HEAD_DIM, (h + 1) * HEAD_DIM)
        gk = g_ref[:, (n_q + h) * HEAD_DIM:(n_q + h + 1) * HEAD_DIM]
        ko_ref[:, sl] = norm_rope(k_ref[:, sl].astype(F32), gk).astype(ko_ref.dtype)
        r0 = h * (HEAD_DIM + GQA_ONES_ROWS)
        vt_ref[r0:r0 + HEAD_DIM, :] = v_ref[:, sl].astype(F32).T.astype(vt_ref.dtype)
        vt_ref[r0 + HEAD_DIM:r0 + HEAD_DIM + GQA_ONES_ROWS, :] = jnp.ones(
            (GQA_ONES_ROWS, vt_ref.shape[1]), vt_ref.dtype)


def _gqa_prep(u, off_q, n_q, n_kv, gvec, cos_t, sin_t):
    t = u.shape[0]
    qw, kw = n_q * HEAD_DIM, n_kv * HEAD_DIM
    tr = 256
    vrows = n_kv * (HEAD_DIM + GQA_ONES_ROWS)
    assert off_q % qw == 0 and (off_q + qw) % kw == 0 and t % tr == 0
    q_blk, k_blk = off_q // qw, (off_q + qw) // kw
    return pl.pallas_call(
        functools.partial(_gqa_prep_kernel, n_q=n_q, n_kv=n_kv),
        grid=(t // tr,),
        in_specs=[pl.BlockSpec((tr, qw), lambda i: (i, q_blk)),
                  pl.BlockSpec((tr, kw), lambda i: (i, k_blk)),
                  pl.BlockSpec((tr, kw), lambda i: (i, k_blk + 1)),
                  pl.BlockSpec((1, qw + kw), lambda i: (0, 0)),
                  pl.BlockSpec((tr, HEAD_DIM), lambda i: (i, 0)),
                  pl.BlockSpec((tr, HEAD_DIM), lambda i: (i, 0))],
        out_specs=[pl.BlockSpec((qw, tr), lambda i: (0, i)),
                   pl.BlockSpec((tr, kw), lambda i: (i, 0)),
                   pl.BlockSpec((vrows, tr), lambda i: (0, i))],
        out_shape=[jax.ShapeDtypeStruct((qw, t), BF16),
                   jax.ShapeDtypeStruct((t, kw), BF16),
                   jax.ShapeDtypeStruct((vrows, t), BF16)],
        compiler_params=_params(("arbitrary",), 32),
        name="gqa_prep",
    )(u, u, u, gvec, cos_t, sin_t)


def _gqa_kernel(bounded_ref, qt_ref, k_ref, vt_ref, gate_ref, o_ref, m_s, acc_s,
                *, group, tq, tk, n_lat, n_ctx):
    acc_s[...] = jnp.zeros(acc_s.shape, F32)

    def chunk_bounded(kc, vtc):
        for j in range(group):
            qt = qt_ref[j * HEAD_DIM:(j + 1) * HEAD_DIM, :]
            s = jnp.dot(kc, qt, preferred_element_type=F32)
            acc_s[j] += jnp.dot(vtc, jnp.exp2(s).astype(BF16), preferred_element_type=F32)

    def chunk_online(kc, vtc):
        for j in range(group):
            qt = qt_ref[j * HEAD_DIM:(j + 1) * HEAD_DIM, :]
            s = jnp.dot(kc, qt, preferred_element_type=F32)
            m = m_s[j]
            m_new = jnp.maximum(m, jnp.max(s, axis=0, keepdims=True))
            p = jnp.exp2(s - m_new).astype(BF16)
            acc_s[j] = jnp.exp2(m - m_new) * acc_s[j] + jnp.dot(vtc, p, preferred_element_type=F32)
            m_s[j] = m_new

    def sweep(chunk):
        @pl.when(pl.program_id(1) * tq < n_lat)
        def _():
            def body(c, carry):
                r0 = pl.multiple_of(c * tk, tk)
                chunk(k_ref[pl.ds(r0, tk), :], vt_ref[:, pl.ds(r0, tk)])
                return carry

            lax.fori_loop(0, n_lat // tk, body, 0)

        chunk(k_ref[n_lat:n_lat + n_ctx, :], vt_ref[:, n_lat:n_lat + n_ctx])

    @pl.when(bounded_ref[0] != 0)
    def _():
        sweep(chunk_bounded)

    @pl.when(bounded_ref[0] == 0)
    def _():
        m_s[...] = jnp.full(m_s.shape, NEG_INF, F32)
        sweep(chunk_online)

    for j in range(group):
        sl = slice(j * HEAD_DIM, (j + 1) * HEAD_DIM)
        acc = acc_s[j]
        o = (acc[:HEAD_DIM] / acc[HEAD_DIM:HEAD_DIM + 1]).T
        o_ref[:, sl] = (o * _silu(gate_ref[:, sl].astype(F32))).astype(o_ref.dtype)


SCORE_BOUND_LOG2 = 60.0


def _scores_bounded(q_gain, k_gain, extra=0.0):
    bound = HEAD_DIM * jnp.max(jnp.abs(q_gain)) * jnp.max(jnp.abs(k_gain)) * 1.02 + extra
    return (bound <= SCORE_BOUND_LOG2).astype(jnp.int32).reshape(1)


def _gqa_attention(bounded, qt, k, vt, u, n_q_heads, n_kv_heads, off_gate, n_lat):
    t = u.shape[0]
    group = n_q_heads // n_kv_heads
    gw = group * HEAD_DIM
    tq = 256
    tk = _largest_tile(n_lat, 1024, 256)
    vrows = HEAD_DIM + GQA_ONES_ROWS
    assert t % tq == 0 and n_lat % tq == 0 and off_gate % gw == 0
    g_blk0 = off_gate // gw
    grid_spec = pltpu.PrefetchScalarGridSpec(
        num_scalar_prefetch=1,
        grid=(n_kv_heads, t // tq),
        in_specs=[pl.BlockSpec((gw, tq), lambda g, i, b: (g, i)),
                  pl.BlockSpec((t, HEAD_DIM), lambda g, i, b: (0, g)),
                  pl.BlockSpec((vrows, t), lambda g, i, b: (g, 0)),
                  pl.BlockSpec((tq, gw), lambda g, i, b: (i, g_blk0 + g))],
        out_specs=pl.BlockSpec((tq, gw), lambda g, i, b: (i, g)),
        scratch_shapes=[pltpu.VMEM((group, 1, tq), F32),
                        pltpu.VMEM((group, vrows, tq), F32)])
    return pl.pallas_call(
        functools.partial(_gqa_kernel, group=group, tq=tq, tk=tk, n_lat=n_lat, n_ctx=t - n_lat),
        grid_spec=grid_spec,
        out_shape=jax.ShapeDtypeStruct((t, n_q_heads * HEAD_DIM), BF16),
        compiler_params=_params(("arbitrary", "arbitrary"), 48),
        name="gqa_attention",
    )(bounded, qt, k, vt, u)


def _na_kernel(q_ref, k0_ref, k1_ref, k2_ref, v0_ref, v1_ref, v2_ref, kc_ref, vc_ref,
               bias_ref, gate_ref, o_ref, *, n_lat_blocks):
    qb = pl.program_id(1)
    q = q_ref[...]
    dn = (((1,), (1,)), ((), ()))
    s_ctx = lax.dot_general(q, kc_ref[...], dn, preferred_element_type=F32)
    gate = _silu(gate_ref[...].astype(F32))

    @pl.when(qb < n_lat_blocks)
    def _():
        k = jnp.concatenate([k0_ref[...], k1_ref[...], k2_ref[...]], axis=0)
        v = jnp.concatenate([v0_ref[...], v1_ref[...], v2_ref[...]], axis=0)
        s_win = lax.dot_general(q, k, dn, preferred_element_type=F32) + bias_ref[...]
        m = jnp.maximum(jnp.max(s_win, axis=-1, keepdims=True),
                        jnp.max(s_ctx, axis=-1, keepdims=True))
        p_win = jnp.exp(s_win - m)
        p_ctx = jnp.exp(s_ctx - m)
        den = jnp.sum(p_win, axis=-1, keepdims=True) + jnp.sum(p_ctx, axis=-1, keepdims=True)
        o = jnp.dot(p_win.astype(BF16), v, preferred_element_type=F32)
        o = o + jnp.dot(p_ctx.astype(BF16), vc_ref[...], preferred_element_type=F32)
        o_ref[...] = (o / den * gate).astype(o_ref.dtype)

    @pl.when(qb >= n_lat_blocks)
    def _():
        m = jnp.max(s_ctx, axis=-1, keepdims=True)
        p = jnp.exp(s_ctx - m)
        den = jnp.sum(p, axis=-1, keepdims=True)
        o = jnp.dot(p.astype(BF16), vc_ref[...], preferred_element_type=F32)
        o_ref[...] = (o / den * gate).astype(o_ref.dtype)


def _na_bias_table(rel_bias, n_rows):
    n_heads = rel_bias.shape[0]
    c = np.arange(GRID_W)[:, None]
    kj = np.arange(GRID_W)[None, :]
    cs = np.clip(c - NA_KW // 2, 0, GRID_W - NA_KW)
    col_ok = (kj >= cs) & (kj < cs + NA_KW)
    sel_c = (np.arange(2 * NA_KW - 1)[:, None, None] == (kj - c + NA_KW - 1)[None]) & col_ok[None]
    a = np.arange(NA_Q_ROWS)[:, None]
    i = np.arange(NA_WIN_ROWS)[None, :]
    sel_r, row_ok = [], []
    for r0, ws in ((0, 0), (NA_Q_ROWS, 0), (n_rows - NA_Q_ROWS, n_rows - NA_WIN_ROWS)):
        r = r0 + a
        kr = ws + i
        rs = np.clip(r - NA_KH // 2, 0, n_rows - NA_KH)
        ok = (kr >= rs) & (kr < rs + NA_KH)
        row_ok.append(ok)
        sel_r.append((np.arange(2 * NA_KH - 1)[None, None, :] == (kr - r + NA_KH - 1)[:, :, None]) & ok[:, :, None])
    sel_r = jnp.asarray(np.stack(sel_r), F32)
    row_ok = np.stack(row_ok)
    hp = lax.Precision.HIGHEST
    cols = jnp.einsum('hde,eck->hdck', rel_bias, jnp.asarray(sel_c, F32), precision=hp)
    tbl = jnp.einsum('zaid,hdck->zhacik', sel_r, cols, precision=hp)
    valid = row_ok[:, None, :, None, :, None] & col_ok[None, None, None, :, None, :]
    tbl = jnp.where(jnp.asarray(valid), tbl, NEG_INF)
    return tbl.reshape(3, n_heads, NA_Q_ROWS * GRID_W, NA_WIN_ROWS * GRID_W)


def _na_attention(qk, u, n_heads, off_v, off_gate, bias_tbl, n_lat):
    t = u.shape[0]
    w = n_heads * HEAD_DIM
    tb = NA_Q_ROWS * GRID_W
    n_lat_blocks = n_lat // tb
    n_blocks = t // tb
    assert t - n_lat == tb and n_lat_blocks >= 3
    k_blk0 = w // HEAD_DIM
    v_blk0 = off_v // HEAD_DIM
    g_blk0 = off_gate // HEAD_DIM
    ctx_blk = n_blocks - 1

    def win(j):
        return lambda h, i: (jnp.clip(i - 1, 0, n_lat_blocks - 3) + j, 0)

    def k_spec(j):
        f = win(j)
        return pl.BlockSpec((tb, HEAD_DIM), lambda h, i: (f(h, i)[0], k_blk0 + h))

    def v_spec(j):
        f = win(j)
        return pl.BlockSpec((tb, HEAD_DIM), lambda h, i: (f(h, i)[0], v_blk0 + h))

    def bias_cls(h, i):
        return (jnp.where(i == 0, 0, jnp.where(i >= n_lat_blocks - 1, 2, 1)), h, 0, 0)

    return pl.pallas_call(
        functools.partial(_na_kernel, n_lat_blocks=n_lat_blocks),
        grid=(n_heads, n_blocks),
        in_specs=[pl.BlockSpec((tb, HEAD_DIM), lambda h, i: (i, h)),
                  k_spec(0), k_spec(1), k_spec(2), v_spec(0), v_spec(1), v_spec(2),
                  pl.BlockSpec((tb, HEAD_DIM), lambda h, i: (ctx_blk, k_blk0 + h)),
                  pl.BlockSpec((tb, HEAD_DIM), lambda h, i: (ctx_blk, v_blk0 + h)),
                  pl.BlockSpec((None, None, tb, NA_WIN_ROWS * GRID_W), bias_cls),
                  pl.BlockSpec((tb, HEAD_DIM), lambda h, i: (i, g_blk0 + h))],
        out_specs=pl.BlockSpec((tb, HEAD_DIM), lambda h, i: (i, h)),
        out_shape=jax.ShapeDtypeStruct((t, w), BF16),
        compiler_params=_params(("arbitrary", "arbitrary"), 32),
        name="na_attention",
    )(qk, qk, qk, qk, u, u, u, qk, u, bias_tbl, u)


def _short_conv_kernel(x_ref, prev_ref, next_ref, w_ref, o_ref, *, tr):
    i = pl.program_id(0)
    n = pl.num_programs(0)
    x = x_ref[...].astype(F32)
    halo = prev_ref.shape[0]
    prev_row = jnp.where(i == 0, 0.0, prev_ref[halo - 1:halo, :].astype(F32))
    next_row = jnp.where(i == n - 1, 0.0, next_ref[0:1, :].astype(F32))
    row = lax.broadcasted_iota(jnp.int32, x.shape, 0)
    up = jnp.where(row == 0, prev_row, pltpu.roll(x, 1, 0))
    down = jnp.where(row == tr - 1, next_row, pltpu.roll(x, tr - 1, 0))
    o_ref[...] = (up * w_ref[0:1, :] + x * w_ref[1:2, :] + down * w_ref[2:3, :]).astype(o_ref.dtype)


def _short_conv(u, width, conv_w, n_lat):
    tr = _largest_tile(n_lat, 512, 256)
    tc = _largest_tile(width, 1024, 256)
    halo = 16
    hb = tr // halo
    n_tiles = n_lat // tr
    wpad = jnp.zeros((8, width), F32).at[:HY_SHORT].set(conv_w.astype(F32))
    return pl.pallas_call(
        functools.partial(_short_conv_kernel, tr=tr),
        grid=(n_tiles, width // tc),
        in_specs=[pl.BlockSpec((tr, tc), lambda i, j: (i, j)),
                  pl.BlockSpec((halo, tc), lambda i, j: (jnp.maximum(i * hb - 1, 0), j)),
                  pl.BlockSpec((halo, tc), lambda i, j: (jnp.minimum((i + 1) * hb, n_tiles * hb - 1), j)),
                  pl.BlockSpec((8, tc), lambda i, j: (0, j))],
        out_specs=pl.BlockSpec((tr, tc), lambda i, j: (i, j)),
        out_shape=jax.ShapeDtypeStruct((n_lat, width), BF16),
        compiler_params=_params(("arbitrary", "arbitrary"), 32),
        name="short_conv",
    )(u, u, u, wpad)


def _hy_gate1_kernel(v_ref, x1_ref, c_ref, d_ref, o_ref):
    v = v_ref[...].astype(F32)
    o_ref[...] = (x1_ref[...].astype(F32) * (c_ref[...].astype(F32) + v * d_ref[0:1, :])).astype(o_ref.dtype)


def _hy_gate1(vx, c1, dskip8, hy):
    n_lat = vx.shape[0]
    tr = _largest_tile(n_lat, 1024, 256)
    tc = _largest_tile(hy, 512, 256)
    nb = hy // tc
    return pl.pallas_call(
        _hy_gate1_kernel,
        grid=(n_lat // tr, nb),
        in_specs=[pl.BlockSpec((tr, tc), lambda i, j: (i, j)),
                  pl.BlockSpec((tr, tc), lambda i, j: (i, nb + j)),
                  pl.BlockSpec((tr, tc), lambda i, j: (i, j)),
                  pl.BlockSpec((8, tc), lambda i, j: (0, j))],
        out_specs=pl.BlockSpec((tr, tc), lambda i, j: (i, j)),
        out_shape=jax.ShapeDtypeStruct((n_lat, hy), BF16),
        compiler_params=_params(("arbitrary", "arbitrary"), 32),
        name="hy_gate1",
    )(vx, vx, c1, dskip8)


def _hy_gate2_kernel(x2_ref, c_ref, y1_ref, d_ref, g_ref, ctx_ref, o_ref, *, n_lat_tiles):
    i = pl.program_id(0)

    @pl.when(i < n_lat_tiles)
    def _():
        y1 = y1_ref[...].astype(F32)
        z = x2_ref[...].astype(F32) * (c_ref[...].astype(F32) + y1 * d_ref[1:2, :])
        o_ref[...] = (z * _silu(g_ref[...].astype(F32))).astype(o_ref.dtype)

    @pl.when(i >= n_lat_tiles)
    def _():
        o_ref[...] = ctx_ref[...]


def _hy_gate2(vx, c2, y1, dskip8, u, off_gate, hy_ctx, hy):
    t = u.shape[0]
    n_lat = vx.shape[0]
    tr = t - n_lat
    tc = _largest_tile(hy, 512, 256)
    nb = hy // tc
    n_lat_tiles = n_lat // tr
    assert n_lat % tr == 0 and off_gate % tc == 0
    g_blk0 = off_gate // tc
    lat = lambda i: jnp.minimum(i, n_lat_tiles - 1)
    return pl.pallas_call(
        functools.partial(_hy_gate2_kernel, n_lat_tiles=n_lat_tiles),
        grid=(t // tr, nb),
        in_specs=[pl.BlockSpec((tr, tc), lambda i, j: (lat(i), 2 * nb + j)),
                  pl.BlockSpec((tr, tc), lambda i, j: (lat(i), j)),
                  pl.BlockSpec((tr, tc), lambda i, j: (lat(i), j)),
                  pl.BlockSpec((8, tc), lambda i, j: (0, j)),
                  pl.BlockSpec((tr, tc), lambda i, j: (i, g_blk0 + j)),
                  pl.BlockSpec((tr, tc), lambda i, j: (0, j))],
        out_specs=pl.BlockSpec((tr, tc), lambda i, j: (i, j)),
        out_shape=jax.ShapeDtypeStruct((t, hy), BF16),
        compiler_params=_params(("arbitrary", "arbitrary"), 32),
        name="hy_gate2",
    )(vx, c2, y1, dskip8, u, hy_ctx)


def _filter_hidden_kernel(z_ref, w1_ref, b1_ref, w2_ref, b2_ref, fr_ref, o_ref):
    hp = lax.Precision.HIGHEST
    fr = fr_ref[...]
    h = jnp.sin(fr * (jnp.dot(z_ref[...], w1_ref[...], precision=hp,
                              preferred_element_type=F32) + b1_ref[...]))
    o_ref[...] = jnp.sin(fr * (jnp.dot(h, w2_ref[...], precision=hp,
                                       preferred_element_type=F32) + b2_ref[...]))


def _pad2(a, rows, cols):
    return jnp.zeros((rows, cols), F32).at[:a.shape[0], :a.shape[1]].set(a.astype(F32))


def _filter_hidden(feat, w1, b1, w2, b2, freq):
    r = feat.shape[0]
    tr = _largest_tile(r, 1024, 256)
    full = lambda i: (0, 0)
    return pl.pallas_call(
        _filter_hidden_kernel,
        grid=(r // tr,),
        in_specs=[pl.BlockSpec((tr, LANE), lambda i: (i, 0)),
                  pl.BlockSpec((LANE, LANE), full), pl.BlockSpec((1, LANE), full),
                  pl.BlockSpec((LANE, LANE), full), pl.BlockSpec((1, LANE), full),
                  pl.BlockSpec((1, LANE), full)],
        out_specs=pl.BlockSpec((tr, LANE), lambda i: (i, 0)),
        out_shape=jax.ShapeDtypeStruct((r, LANE), F32),
        compiler_params=_params(("arbitrary",), 32),
        name="filter_hidden",
    )(feat, _pad2(w1, LANE, LANE), _pad2(b1[None], 1, LANE), _pad2(w2, LANE, LANE),
      _pad2(b2[None], 1, LANE), _pad2(freq[None], 1, LANE))


def _two_sided_positions(n):
    idx = np.arange(2 * n)
    return np.where(idx < n, idx, 2 * n - idx) % n


def _filter_features(n, n_bands):
    pos = _two_sided_positions(n)
    t = jnp.linspace(0.0, 1.0, n, dtype=F32)[:, None]
    w = (2.0 * math.pi / n) * jnp.arange(n, dtype=F32)[:, None]
    bands = jnp.linspace(1e-4, n_bands - 1, n_bands, dtype=F32)[None, :]
    z = jnp.concatenate([t, jnp.cos(bands * w), -jnp.sin(bands * w)], axis=-1)
    z = z[pos]
    return _pad2(z, 2 * n, LANE), t[pos]


def _filter_gen_kernel(h_ref, w_ref, t_ref, dec_ref, o_ref, *, tr, n):
    i = pl.program_id(0)
    f = jnp.dot(h_ref[...].astype(BF16), w_ref[...], preferred_element_type=F32)
    f = f * jnp.exp(-t_ref[...] * jnp.abs(dec_ref[0:1, :]))
    row = i * tr + lax.broadcasted_iota(jnp.int32, (tr, 1), 0)
    o_ref[...] = jnp.where(row == n, 0.0, f).astype(o_ref.dtype)


def _filter_gen(hidden, row0, w3p, tcol, decay8, n, order, hy):
    tr = _largest_tile(n, 512, 256)
    tc = _largest_tile(hy, 1024, 256)
    nb = hy // tc
    n_half = n // tr
    hb0 = row0 // tr
    assert row0 % tr == 0
    return pl.pallas_call(
        functools.partial(_filter_gen_kernel, tr=tr, n=n),
        grid=(2 * n // tr, order * nb),
        in_specs=[pl.BlockSpec((tr, LANE), lambda i, j: (hb0 + i, 0)),
                  pl.BlockSpec((LANE, tc), lambda i, j: (0, (j // nb) * 2 * nb + (i // n_half) * nb + j % nb)),
                  pl.BlockSpec((tr, 1), lambda i, j: (i, 0)),
                  pl.BlockSpec((None, 8, tc), lambda i, j: (j // nb, 0, j % nb))],
        out_specs=pl.BlockSpec((tr, tc), lambda i, j: (i, j)),
        out_shape=jax.ShapeDtypeStruct((2 * n, order * hy), BF16),
        compiler_params=_params(("arbitrary", "arbitrary"), 32),
        name="filter_gen",
    )(hidden, w3p, tcol, decay8)


def _dft_tables(n1, nz):
    n2 = FFT_N2
    n = n1 * n2
    half = n1 // 2
    k1 = jnp.arange(half + 1)
    two_pi = 2.0 * math.pi

    def cs(idx, period):
        ang = (idx % period).astype(F32) * (two_pi / period)
        return jnp.cos(ang), -jnp.sin(ang)

    eye8 = jnp.eye(8, dtype=F32)
    fr, fi = cs(k1[:, None] * jnp.arange(nz)[None, :], n1)
    kf = jnp.stack([fr, fi], axis=1)
    kf = jnp.einsum('kpn,jl->kpjnl', kf, eye8).reshape((half + 1) * 16, nz * 8)
    gr, gi = cs(jnp.arange(half)[:, None] * k1[None, :], n1)
    weight = jnp.where((k1 == 0) | (k1 == half), 1.0, 2.0) / n
    ki = jnp.stack([gr, gi], axis=2) * weight[None, :, None]
    ki = jnp.einsum('nkp,jl->njkpl', ki, eye8).reshape(half * 8, (half + 1) * 16)
    r = jnp.arange(2 * n2)
    part = (r % 16) // 8
    n2_of_r = 8 * (r // 16) + r % 8
    k2 = jnp.arange(n2)
    idx = k2[None, :, None] * n2_of_r[None, None, :] * n1 + k1[:, None, None] * n2_of_r[None, None, :]
    cr, ci = cs(idx, n)
    top = jnp.where(part[None, None, :] == 0, cr, -ci)
    bot = jnp.where(part[None, None, :] == 0, ci, cr)
    m = jnp.concatenate([top, bot], axis=1)
    return kf.astype(BF16), ki.astype(BF16), m.astype(BF16)


def _cross_slab_forward(z_ref, kf_ref, a_scr, nz, nh, ct):
    def body(b, carry):
        z = z_ref[:, pl.ds(pl.multiple_of(16 * b, 16), 16), :].astype(F32)
        for half in range(2):
            g = z[:, 8 * half:8 * half + 8, :].reshape(nz * 8, ct).astype(BF16)
            p = jnp.dot(kf_ref[...], g, preferred_element_type=F32)
            start = pl.multiple_of(32 * b + 16 * half, 16)
            a_scr[:, pl.ds(start, 16), :] = p.astype(BF16).reshape(nh, 16, ct)
        return carry

    lax.fori_loop(0, FFT_N2 // 16, body, 0)


def _slab_block(nh):
    return max(k for k in range(1, 17) if nh % k == 0)


def _fft_fwd_kernel(z_ref, kf_ref, m_ref, o_ref, a_scr, *, nz, nh, kb, ct):
    kblk = pl.program_id(1)

    @pl.when(kblk == 0)
    def _():
        _cross_slab_forward(z_ref, kf_ref, a_scr, nz, nh, ct)

    for k in range(kb):
        x = jnp.dot(m_ref[k], a_scr[kblk * kb + k], preferred_element_type=F32)
        o_ref[k] = x.astype(o_ref.dtype)


def _fft_forward(z3, kf, m, ct=256):
    nz, _, c = z3.shape
    nh = m.shape[0]
    kb = _slab_block(nh)
    return pl.pallas_call(
        functools.partial(_fft_fwd_kernel, nz=nz, nh=nh, kb=kb, ct=ct),
        grid=(c // ct, nh // kb),
        in_specs=[pl.BlockSpec((nz, FFT_N2, ct), lambda j, k: (0, 0, j)),
                  pl.BlockSpec(kf.shape, lambda j, k: (0, 0)),
                  pl.BlockSpec((kb, 2 * FFT_N2, 2 * FFT_N2), lambda j, k: (k, 0, 0))],
        out_specs=pl.BlockSpec((kb, 2 * FFT_N2, ct), lambda j, k: (k, 0, j)),
        out_shape=jax.ShapeDtypeStruct((nh, 2 * FFT_N2, c), BF16),
        scratch_shapes=[pltpu.VMEM((nh, 2 * FFT_N2, ct), BF16)],
        compiler_params=_params(("arbitrary", "arbitrary"), 56),
        name="fft_forward",
    )(z3, kf, m)


def _fft_conv_kernel(z_ref, kf_ref, ki_ref, m_ref, h_ref, o_ref, a_scr, *, nz, nh, kb, ct):
    kblk = pl.program_id(1)
    n2 = FFT_N2

    @pl.when(kblk == 0)
    def _():
        _cross_slab_forward(z_ref, kf_ref, a_scr, nz, nh, ct)

    for k in range(kb):
        mk = m_ref[k]
        x = jnp.dot(mk, a_scr[kblk * kb + k], preferred_element_type=F32)
        h = h_ref[k].astype(F32)
        xr, xi, hr, hi = x[:n2], x[n2:], h[:n2], h[n2:]
        y = jnp.concatenate([xr * hr - xi * hi, xr * hi + xi * hr], axis=0).astype(BF16)
        b = lax.dot_general(mk, y, (((0,), (0,)), ((), ())), preferred_element_type=F32)
        a_scr[kblk * kb + k] = b.astype(BF16)

    @pl.when(kblk == pl.num_programs(1) - 1)
    def _():
        def body(b, carry):
            halves = []
            for half in range(2):
                start = pl.multiple_of(32 * b + 16 * half, 16)
                r = a_scr[:, pl.ds(start, 16), :].reshape(nh * 16, ct)
                p = jnp.dot(ki_ref[...], r, preferred_element_type=F32)
                halves.append(p.reshape(nz, 8, ct))
            out = jnp.concatenate(halves, axis=1)
            o_ref[:, pl.ds(pl.multiple_of(16 * b, 16), 16), :] = out.astype(o_ref.dtype)
            return carry

        lax.fori_loop(0, n2 // 16, body, 0)


def _fft_conv(z3, z_blk0, spec, spec_blk0, kf, ki, m, c, ct=256):
    nz = z3.shape[0]
    nh = m.shape[0]
    kb = _slab_block(nh)
    assert nh == nz + 1
    return pl.pallas_call(
        functools.partial(_fft_conv_kernel, nz=nz, nh=nh, kb=kb, ct=ct),
        grid=(c // ct, nh // kb),
        in_specs=[pl.BlockSpec((nz, FFT_N2, ct), lambda j, k: (0, 0, z_blk0 + j)),
                  pl.BlockSpec(kf.shape, lambda j, k: (0, 0)),
                  pl.BlockSpec(ki.shape, lambda j, k: (0, 0)),
                  pl.BlockSpec((kb, 2 * FFT_N2, 2 * FFT_N2), lambda j, k: (k, 0, 0)),
                  pl.BlockSpec((kb, 2 * FFT_N2, ct), lambda j, k: (k, 0, spec_blk0 + j))],
        out_specs=pl.BlockSpec((nz, FFT_N2, ct), lambda j, k: (0, 0, j)),
        out_shape=jax.ShapeDtypeStruct((nz, FFT_N2, c), BF16),
        scratch_shapes=[pltpu.VMEM((nh, 2 * FFT_N2, ct), BF16)],
        compiler_params=_params(("arbitrary", "arbitrary"), 56),
        name="fft_conv",
    )(z3, kf, ki, m, spec)


def _ctx_dft_tables(n):
    nn = 2 * n
    k = jnp.arange(nn)
    ang = ((k[:, None] * k[None, :]) % nn).astype(F32) * (2.0 * math.pi / nn)
    cr, ci = jnp.cos(ang), -jnp.sin(ang)
    full = jnp.concatenate([cr, ci], axis=0)
    inv = jnp.concatenate([cr[:n], ci[:n]], axis=1) / nn
    return full[:, :n].astype(BF16), full.astype(BF16), inv.astype(BF16)


def _ctx_hyena_kernel(v_ref, x1_ref, x2_ref, g_ref, cw0_ref, cw1_ref, cw2_ref, hid_ref,
                      w10_ref, w11_ref, w20_ref, w21_ref, t_ref, dec_ref, dsk_ref,
                      ff_ref, ffull_ref, finv_ref, o_ref, *, n):
    row = lax.broadcasted_iota(jnp.int32, (n, 1), 0)

    def sconv(x_ref, w_ref):
        x = x_ref[...].astype(F32)
        up = jnp.where(row == 0, 0.0, pltpu.roll(x, 1, 0))
        down = jnp.where(row == n - 1, 0.0, pltpu.roll(x, n - 1, 0))
        y = up * w_ref[0:1, :] + x * w_ref[1:2, :] + down * w_ref[2:3, :]
        return y.astype(BF16).astype(F32)

    hid = hid_ref[...].astype(BF16)
    tcol = t_ref[...]
    row2 = lax.broadcasted_iota(jnp.int32, (2 * n, 1), 0)

    def spectrum(wf_ref, wb_ref, o):
        fwd = jnp.dot(hid[:n], wf_ref[...], preferred_element_type=F32)
        bwd = jnp.dot(hid[n:], wb_ref[...], preferred_element_type=F32)
        f = jnp.concatenate([fwd, bwd], axis=0) * jnp.exp(-tcol * jnp.abs(dec_ref[o:o + 1, :]))
        f = jnp.where(row2 == n, 0.0, f).astype(BF16)
        return jnp.dot(ffull_ref[...], f, preferred_element_type=F32)

    def conv(z, h):
        x = jnp.dot(ff_ref[...], z.astype(BF16), preferred_element_type=F32)
        h = h.astype(BF16).astype(F32)
        nn = 2 * n
        xr, xi, hr, hi = x[:nn], x[nn:], h[:nn], h[nn:]
        y = jnp.concatenate([xr * hr - xi * hi, xr * hi + xi * hr], axis=0).astype(BF16)
        return jnp.dot(finv_ref[...], y, preferred_element_type=F32)

    v = sconv(v_ref, cw0_ref)
    x1 = sconv(x1_ref, cw1_ref)
    x2 = sconv(x2_ref, cw2_ref)
    c1 = conv(v, spectrum(w10_ref, w11_ref, 0)).astype(BF16).astype(F32)
    y1 = (x1 * (c1 + v * dsk_ref[0:1, :])).astype(BF16).astype(F32)
    c2 = conv(y1, spectrum(w20_ref, w21_ref, 1)).astype(BF16).astype(F32)
    z = x2 * (c2 + y1 * dsk_ref[1:2, :])
    o_ref[...] = (z * _silu(g_ref[...].astype(F32))).astype(o_ref.dtype)


def _ctx_hyena(u, n_lat, hy, order, off_gate, conv8, hidden, hid_row0, w3p, tcol, decay8, dskip8, tables):
    t = u.shape[0]
    n = t - n_lat
    assert order == 2 and n_lat % n == 0 and hid_row0 % (2 * n) == 0
    ct = 256
    nb = hy // ct
    rb = n_lat // n
    ff, ffull, finv = tables
    const = lambda j: (0, 0)
    ublk = lambda c0: pl.BlockSpec((n, ct), lambda j: (rb, c0 + j))
    wblk = lambda c0: pl.BlockSpec((LANE, ct), lambda j: (0, c0 + j))
    cwblk = lambda c0: pl.BlockSpec((8, ct), lambda j: (0, c0 + j))
    return pl.pallas_call(
        functools.partial(_ctx_hyena_kernel, n=n),
        grid=(nb,),
        in_specs=[ublk(0), ublk(nb), ublk(2 * nb), ublk(off_gate // ct),
                  cwblk(0), cwblk(nb), cwblk(2 * nb),
                  pl.BlockSpec((2 * n, LANE), lambda j: (hid_row0 // (2 * n), 0)),
                  wblk(0), wblk(nb), wblk(2 * nb), wblk(3 * nb),
                  pl.BlockSpec((2 * n, 1), const),
                  pl.BlockSpec((8, ct), lambda j: (0, j)),
                  pl.BlockSpec((8, ct), lambda j: (0, j)),
                  pl.BlockSpec(ff.shape, const), pl.BlockSpec(ffull.shape, const),
                  pl.BlockSpec(finv.shape, const)],
        out_specs=pl.BlockSpec((n, ct), lambda j: (0, j)),
        out_shape=jax.ShapeDtypeStruct((n, hy), BF16),
        compiler_params=_params(("arbitrary",), 40),
        name="ctx_hyena",
    )(u, u, u, u, conv8, conv8, conv8, hidden, w3p, w3p, w3p, w3p, tcol, decay8, dskip8,
      ff, ffull, finv)


def _even_layer(x_all, mods_l, g, n_lat, w_in, w_out, conv_w, w1, b1, w2, b2, w3, freq, decay, dskip,
                q_norm, k_norm, rope, dft, ctx_dft):
    t, d = x_all.shape
    n_ctx = t - n_lat
    order, hy = decay.shape
    gqa_w = d - hy
    n_q = gqa_w // HEAD_DIM
    n_kv = n_q // 4
    kv_w = n_kv * HEAD_DIM
    off_hy_gate = (order + 1) * hy
    off_q = off_hy_gate + hy
    off_v = off_q + gqa_w + kv_w
    off_at_gate = off_v + kv_w
    scale = HEAD_DIM ** -0.5

    h = _norm_mod(x_all, g, mods_l, n_lat)
    u = _in_proj(h, w_in)

    q_gain = q_norm.astype(F32) * (scale * math.log2(math.e))
    k_gain = k_norm.astype(F32)
    gvec = jnp.concatenate([jnp.tile(q_gain, n_q), jnp.tile(k_gain, n_kv)])[None]
    qt, kn, vt = _gqa_prep(u, off_q, n_q, n_kv, gvec, rope[0], rope[1])
    att = _gqa_attention(_scores_bounded(q_gain, k_gain), qt, kn, vt, u, n_q, n_kv, off_at_gate, n_lat)

    n_bands = (w1.shape[0] - 1) // 2
    feat_l, t_l = _filter_features(n_lat, n_bands)
    feat_c, t_c = _filter_features(n_ctx, n_bands)
    hidden = _filter_hidden(jnp.concatenate([feat_l, feat_c], axis=0), w1, b1, w2, b2, freq)
    w3p = jnp.zeros((LANE, w3.shape[1]), BF16).at[:w3.shape[0]].set(w3.astype(BF16))
    decay8 = jnp.zeros((order, 8, hy), F32).at[:, 0].set(decay.astype(F32))
    dskip8 = jnp.zeros((8, hy), F32).at[:order].set(dskip.astype(F32))
    conv8 = jnp.zeros((8, conv_w.shape[1]), F32).at[:HY_SHORT].set(conv_w.astype(F32))
    kf_pad, kf_full, ki, m = dft
    n1 = 2 * n_lat // FFT_N2
    filt = _filter_gen(hidden, 0, w3p, t_l, decay8, n_lat, order, hy)
    spec = _fft_forward(filt.reshape(n1, FFT_N2, order * hy), kf_full, m)

    vx = _short_conv(u, (order + 1) * hy, conv_w, n_lat)
    vx3 = vx.reshape(n1 // 2, FFT_N2, (order + 1) * hy)
    ct = 256
    c1 = _fft_conv(vx3, 0, spec, 0, kf_pad, ki, m, hy).reshape(n_lat, hy)
    y1 = _hy_gate1(vx, c1, dskip8, hy)
    c2 = _fft_conv(y1.reshape(n1 // 2, FFT_N2, hy), 0, spec, hy // ct, kf_pad, ki, m, hy).reshape(n_lat, hy)

    decay_c8 = jnp.zeros((8, hy), F32).at[:order].set(decay.astype(F32))
    hy_ctx = _ctx_hyena(u, n_lat, hy, order, off_hy_gate, conv8, hidden, 2 * n_lat, w3p, t_c,
                        decay_c8, dskip8, ctx_dft)
    hy_mix = _hy_gate2(vx, c2, y1, dskip8, u, off_hy_gate, hy_ctx, hy)

    return _out_proj(hy_mix, 0, att, 0, w_out, x_all, mods_l, n_lat)


def _odd_layer(x_all, mods_l, g, n_lat, w_in, w_out, q_norm, k_norm, rel_bias):
    t, d = x_all.shape
    n_heads = rel_bias.shape[0]
    w = n_heads * HEAD_DIM
    scale = HEAD_DIM ** -0.5
    h = _norm_mod(x_all, g, mods_l, n_lat)
    u = _in_proj(h, w_in)
    gvec = jnp.concatenate([jnp.tile(q_norm.astype(F32) * scale, n_heads),
                            jnp.tile(k_norm.astype(F32), n_heads)])[None]
    dummy = jnp.zeros((t, HEAD_DIM), F32)
    qk = _qk_prep(u, 0, gvec, dummy, dummy, False)
    bias_tbl = _na_bias_table(rel_bias.astype(F32), n_lat // GRID_W)
    att = _na_attention(qk, u, n_heads, 2 * w, 3 * w, bias_tbl, n_lat)
    return _out_proj(att, 0, att, 1, w_out, x_all, mods_l, n_lat)


def kernel(x, c, ctx, c_ctx, norm_g, ada_w, ada_b, e_w_in, e_w_out, hy_conv, hy_w1, hy_b1, hy_w2, hy_b2,
           hy_w3, hy_freq, hy_decay, hy_dskip, gqa_q_norm, gqa_k_norm, o_w_in, o_w_out, na_q_norm,
           na_k_norm, na_rel_bias):
    batch, n_lat, d = x.shape
    n_ctx = ctx.shape[1]
    depth = norm_g.shape[0]
    assert batch == 1 and n_lat % (2 * FFT_N2) == 0 and n_ctx == NA_Q_ROWS * GRID_W

    x_all = jnp.concatenate([x[0], ctx[0]], axis=0).astype(F32)
    cvecs = jnp.zeros((8, d), F32).at[0].set(c[0]).at[1].set(c_ctx)
    mods = _ada_mods(cvecs, ada_w, ada_b)

    rope = _rope_tables(n_lat, n_ctx)
    n1 = 2 * n_lat // FFT_N2
    kf_pad, ki, m = _dft_tables(n1, n1 // 2)
    kf_full, _, _ = _dft_tables(n1, n1)
    dft = (kf_pad, kf_full, ki, m)
    ctx_dft = _ctx_dft_tables(n_ctx)

    for layer in range(depth):
        i = layer // 2
        if layer % 2 == 0:
            x_all = _even_layer(x_all, mods[layer], norm_g[layer], n_lat,
                                e_w_in[i].astype(BF16), e_w_out[i].astype(BF16), hy_conv[i],
                                hy_w1[i], hy_b1[i], hy_w2[i], hy_b2[i], hy_w3[i], hy_freq[i],
                                hy_decay[i], hy_dskip[i], gqa_q_norm[i], gqa_k_norm[i],
                                rope, dft, ctx_dft)
        else:
            x_all = _odd_layer(x_all, mods[layer], norm_g[layer], n_lat,
                               o_w_in[i].astype(BF16), o_w_out[i].astype(BF16),
                               na_q_norm[i], na_k_norm[i], na_rel_bias[i])
    return x_all[:n_lat][None]
```

```python
import functools
import math

import numpy as np
import jax
import jax.numpy as jnp
from jax import lax
from jax.experimental import pallas as pl
from jax.experimental.pallas import tpu as pltpu

F32 = jnp.float32
BF16 = jnp.bfloat16

HEAD_DIM = 128
GRID_W = 64
EPS = 1e-6
ROPE_THETA = 10000.0
NA_KH = 8
NA_KW = 16
NA_Q_ROWS = 4
NA_WIN_ROWS = 12
NA_HEADS_PER_STEP = 4
HY_SHORT = 3
FFT_N2 = 128
NEG_INF = -1e30

V7X_VMEM_BYTES = 64 * 1024 * 1024
LANE = 128


def _params(semantics, vmem_mb):
    assert vmem_mb * 1024 * 1024 < V7X_VMEM_BYTES
    return pltpu.CompilerParams(dimension_semantics=semantics,
                                vmem_limit_bytes=vmem_mb * 1024 * 1024)


def _largest_tile(n, target, quantum):
    best = None
    t = quantum
    while t <= min(n, target):
        if n % t == 0:
            best = t
        t += quantum
    assert best is not None, (n, target, quantum)
    return best


def _silu(x):
    return x * jax.nn.sigmoid(x)


def _mods_kernel(c_ref, w_ref, b_ref, o_ref):
    s = _silu(c_ref[...]).astype(BF16)
    o_ref[...] = jnp.dot(s, w_ref[...].astype(BF16), preferred_element_type=F32) + b_ref[...]


def _ada_mods(cvecs, ada_w, ada_b):
    depth, d, n = ada_w.shape
    tn = _largest_tile(n, 512, LANE)
    return pl.pallas_call(
        _mods_kernel,
        grid=(depth, n // tn),
        in_specs=[pl.BlockSpec((8, d), lambda l, j: (0, 0)),
                  pl.BlockSpec((None, d, tn), lambda l, j: (l, 0, j)),
                  pl.BlockSpec((None, 1, tn), lambda l, j: (l, 0, j))],
        out_specs=pl.BlockSpec((None, 8, tn), lambda l, j: (l, 0, j)),
        out_shape=jax.ShapeDtypeStruct((depth, 8, n), F32),
        compiler_params=_params(("arbitrary", "arbitrary"), 40),
        name="ada_mods",
    )(cvecs, ada_w, ada_b.reshape(depth, 1, n))


def _norm_mod_kernel(x_ref, g_ref, sh_ref, sc_ref, o_ref, *, n_lat_tiles):
    x = x_ref[...]
    y = x * lax.rsqrt(jnp.mean(x * x, axis=-1, keepdims=True) + EPS) * g_ref[...]
    is_ctx = pl.program_id(0) >= n_lat_tiles
    sh = jnp.where(is_ctx, sh_ref[1:2, :], sh_ref[0:1, :])
    sc = jnp.where(is_ctx, sc_ref[1:2, :], sc_ref[0:1, :])
    o_ref[...] = (y * (1.0 + sc) + sh).astype(o_ref.dtype)


def _norm_mod(x_all, g, mods_l, n_lat):
    t, d = x_all.shape
    tr = 256
    assert t % tr == 0 and n_lat % tr == 0
    return pl.pallas_call(
        functools.partial(_norm_mod_kernel, n_lat_tiles=n_lat // tr),
        grid=(t // tr,),
        in_specs=[pl.BlockSpec((tr, d), lambda i: (i, 0)),
                  pl.BlockSpec((1, d), lambda i: (0, 0)),
                  pl.BlockSpec((8, d), lambda i: (0, 0)),
                  pl.BlockSpec((8, d), lambda i: (0, 1))],
        out_specs=pl.BlockSpec((tr, d), lambda i: (i, 0)),
        out_shape=jax.ShapeDtypeStruct((t, d), BF16),
        compiler_params=_params(("arbitrary",), 40),
        name="norm_mod",
    )(x_all, g.reshape(1, d), mods_l, mods_l)


def _matmul_kernel(a_ref, b_ref, o_ref):
    o_ref[...] = jnp.dot(a_ref[...], b_ref[...], preferred_element_type=F32).astype(o_ref.dtype)


def _in_proj(h, w):
    m, k = h.shape
    n = w.shape[1]
    tm = _largest_tile(m, 1024, 256)
    tn = _largest_tile(n, 1024, 256)
    return pl.pallas_call(
        _matmul_kernel,
        grid=(m // tm, n // tn),
        in_specs=[pl.BlockSpec((tm, k), lambda i, j: (i, 0)),
                  pl.BlockSpec((k, tn), lambda i, j: (0, j))],
        out_specs=pl.BlockSpec((tm, tn), lambda i, j: (i, j)),
        out_shape=jax.ShapeDtypeStruct((m, n), BF16),
        compiler_params=_params(("arbitrary", "arbitrary"), 52),
        name="in_proj",
    )(h, w)


def _out_proj_kernel(a1_ref, a2_ref, w1_ref, w2_ref, x_ref, gate_ref, o_ref, *, tm, n_lat):
    y = jnp.dot(a1_ref[...], w1_ref[...], preferred_element_type=F32)
    y = y + jnp.dot(a2_ref[...], w2_ref[...], preferred_element_type=F32)
    rows = pl.program_id(0) * tm + lax.broadcasted_iota(jnp.int32, (tm, 1), 0)
    gate = jnp.where(rows >= n_lat, gate_ref[1:2, :], gate_ref[0:1, :])
    o_ref[...] = x_ref[...] + gate * y


def _out_proj(a1, a1_blk, a2, a2_blk, w, x_all, mods_l, n_lat, latent_only):
    d = x_all.shape[1]
    t = n_lat if latent_only else x_all.shape[0]
    kh = w.shape[0] // 2
    tm = _largest_tile(t, 1024, 256)
    tn = _largest_tile(d, 512, 256)
    gate_blk0 = 2 * d // tn
    return pl.pallas_call(
        functools.partial(_out_proj_kernel, tm=tm, n_lat=n_lat),
        grid=(t // tm, d // tn),
        in_specs=[pl.BlockSpec((tm, kh), lambda i, j: (i, a1_blk)),
                  pl.BlockSpec((tm, kh), lambda i, j: (i, a2_blk)),
                  pl.BlockSpec((kh, tn), lambda i, j: (0, j)),
                  pl.BlockSpec((kh, tn), lambda i, j: (1, j)),
                  pl.BlockSpec((tm, tn), lambda i, j: (i, j)),
                  pl.BlockSpec((8, tn), lambda i, j: (0, gate_blk0 + j))],
        out_specs=pl.BlockSpec((tm, tn), lambda i, j: (i, j)),
        out_shape=jax.ShapeDtypeStruct((t, d), F32),
        compiler_params=_params(("arbitrary", "arbitrary"), 52),
        name="out_proj",
    )(a1, a2, w, w, x_all, mods_l)


def _rope_tables(n_lat, n_ctx):
    quarter = HEAD_DIM // 4
    inv = ROPE_THETA ** (-jnp.arange(quarter, dtype=F32) / quarter)
    tok = jnp.arange(n_lat)
    rows = (tok // GRID_W).astype(F32)[:, None] * inv[None]
    cols = (tok % GRID_W).astype(F32)[:, None] * inv[None]
    cos = jnp.concatenate([jnp.cos(rows), jnp.cos(rows), jnp.cos(cols), jnp.cos(cols)], axis=-1)
    sin = jnp.concatenate([-jnp.sin(rows), jnp.sin(rows), -jnp.sin(cols), jnp.sin(cols)], axis=-1)
    cos = jnp.concatenate([cos, jnp.ones((n_ctx, HEAD_DIM), F32)], axis=0)
    sin = jnp.concatenate([sin, jnp.zeros((n_ctx, HEAD_DIM), F32)], axis=0)
    return cos, sin


GQA_ONES_ROWS = 16


def _attn_prep_kernel(q_ref, k_ref, v_ref, g_ref, cos_ref, sin_ref, qt_ref, ko_ref, vt_ref,
                      *, n_q, n_kv, rope):
    if rope:
        cos = cos_ref[...]
        sin = sin_ref[...]
        lane = lax.broadcasted_iota(jnp.int32, cos.shape, 1)
        first_quarter = (lane % 64) < 32

    def norm_rope(x, g):
        y = x * lax.rsqrt(jnp.mean(x * x, axis=-1, keepdims=True) + EPS) * g
        if not rope:
            return y
        partner = jnp.where(first_quarter, pltpu.roll(y, HEAD_DIM - 32, 1), pltpu.roll(y, 32, 1))
        return y * cos + partner * sin

    for h in range(n_q):
        sl = slice(h * HEAD_DIM, (h + 1) * HEAD_DIM)
        qt_ref[sl, :] = norm_rope(q_ref[:, sl].astype(F32), g_ref[:, sl]).T.astype(qt_ref.dtype)
    for h in range(n_kv):
        sl = slice(h * HEAD_DIM, (h + 1) * HEAD_DIM)
        gk = g_ref[:, (n_q + h) * HEAD_DIM:(n_q + h + 1) * HEAD_DIM]
        ko_ref[:, sl] = norm_rope(k_ref[:, sl].astype(F32), gk).astype(ko_ref.dtype)
        r0 = h * (HEAD_DIM + GQA_ONES_ROWS)
        vt_ref[r0:r0 + HEAD_DIM, :] = v_ref[:, sl].astype(F32).T.astype(vt_ref.dtype)
        vt_ref[r0 + HEAD_DIM:r0 + HEAD_DIM + GQA_ONES_ROWS, :] = jnp.ones(
            (GQA_ONES_ROWS, vt_ref.shape[1]), vt_ref.dtype)


def _attn_prep(u, off_q, off_k, off_v, n_q, n_kv, n_groups, q_gain, k_gain, cos_t, sin_t, rope):
    t = u.shape[0]
    qw, kw = n_q * HEAD_DIM, n_kv * HEAD_DIM
    tr = 256
    vrows = n_kv * (HEAD_DIM + GQA_ONES_ROWS)
    assert off_q % qw == 0 and off_k % kw == 0 and off_v % kw == 0 and t % tr == 0
    q_blk, k_blk, v_blk = off_q // qw, off_k // kw, off_v // kw
    gvec = jnp.concatenate([jnp.tile(q_gain, n_q), jnp.tile(k_gain, n_kv)])[None]
    return pl.pallas_call(
        functools.partial(_attn_prep_kernel, n_q=n_q, n_kv=n_kv, rope=rope),
        grid=(t // tr, n_groups),
        in_specs=[pl.BlockSpec((tr, qw), lambda i, j: (i, q_blk + j)),
                  pl.BlockSpec((tr, kw), lambda i, j: (i, k_blk + j)),
                  pl.BlockSpec((tr, kw), lambda i, j: (i, v_blk + j)),
                  pl.BlockSpec((1, qw + kw), lambda i, j: (0, 0)),
                  pl.BlockSpec((tr, HEAD_DIM), lambda i, j: (i, 0)),
                  pl.BlockSpec((tr, HEAD_DIM), lambda i, j: (i, 0))],
        out_specs=[pl.BlockSpec((qw, tr), lambda i, j: (j, i)),
                   pl.BlockSpec((tr, kw), lambda i, j: (i, j)),
                   pl.BlockSpec((vrows, tr), lambda i, j: (j, i))],
        out_shape=[jax.ShapeDtypeStruct((n_groups * qw, t), BF16),
                   jax.ShapeDtypeStruct((t, n_groups * kw), BF16),
                   jax.ShapeDtypeStruct((n_groups * vrows, t), BF16)],
        compiler_params=_params(("arbitrary", "arbitrary"), 32),
        name="attn_prep",
    )(u, u, u, gvec, cos_t, sin_t)


def _gqa_kernel(bounded_ref, qt_ref, k_ref, vt_ref, gate_ref, o_ref, m_s, acc_s,
                *, group, tq, tk, n_lat, n_ctx):
    acc_s[...] = jnp.zeros(acc_s.shape, F32)

    def chunk_bounded(kc, vtc):
        ps = []
        for j in range(group):
            qt = qt_ref[j * HEAD_DIM:(j + 1) * HEAD_DIM, :]
            s = jnp.dot(kc, qt, preferred_element_type=F32)
            ps.append(jnp.exp2(s).astype(BF16))
        for j in range(group):
            acc_s[j] += jnp.dot(vtc, ps[j], preferred_element_type=F32)

    def chunk_online(kc, vtc):
        for j in range(group):
            qt = qt_ref[j * HEAD_DIM:(j + 1) * HEAD_DIM, :]
            s = jnp.dot(kc, qt, preferred_element_type=F32)
            m = m_s[j]
            m_new = jnp.maximum(m, jnp.max(s, axis=0, keepdims=True))
            p = jnp.exp2(s - m_new).astype(BF16)
            acc_s[j] = jnp.exp2(m - m_new) * acc_s[j] + jnp.dot(vtc, p, preferred_element_type=F32)
            m_s[j] = m_new

    def sweep(chunk):
        @pl.when(pl.program_id(1) * tq < n_lat)
        def _():
            def body(c, carry):
                r0 = pl.multiple_of(c * tk, tk)
                chunk(k_ref[pl.ds(r0, tk), :], vt_ref[:, pl.ds(r0, tk)])
                return carry

            lax.fori_loop(0, n_lat // tk, body, 0)

        chunk(k_ref[n_lat:n_lat + n_ctx, :], vt_ref[:, n_lat:n_lat + n_ctx])

    @pl.when(bounded_ref[0] != 0)
    def _():
        sweep(chunk_bounded)

    @pl.when(bounded_ref[0] == 0)
    def _():
        m_s[...] = jnp.full(m_s.shape, NEG_INF, F32)
        sweep(chunk_online)

    for j in range(group):
        sl = slice(j * HEAD_DIM, (j + 1) * HEAD_DIM)
        acc = acc_s[j]
        o = (acc[:HEAD_DIM] / acc[HEAD_DIM:HEAD_DIM + 1]).T
        o_ref[:, sl] = (o * _silu(gate_ref[:, sl].astype(F32))).astype(o_ref.dtype)


SCORE_BOUND_LOG2 = 60.0


def _scores_bounded(q_gain, k_gain, extra=0.0):
    bound = HEAD_DIM * jnp.max(jnp.abs(q_gain)) * jnp.max(jnp.abs(k_gain)) * 1.02 + extra
    return (bound <= SCORE_BOUND_LOG2).astype(jnp.int32).reshape(1)


def _gqa_attention(bounded, qt, k, vt, u, n_q_heads, n_kv_heads, off_gate, n_lat):
    t = u.shape[0]
    group = n_q_heads // n_kv_heads
    gw = group * HEAD_DIM
    tq = 256
    tk = _largest_tile(n_lat, 1024, 256)
    vrows = HEAD_DIM + GQA_ONES_ROWS
    assert t % tq == 0 and n_lat % tq == 0 and off_gate % gw == 0
    g_blk0 = off_gate // gw
    grid_spec = pltpu.PrefetchScalarGridSpec(
        num_scalar_prefetch=1,
        grid=(n_kv_heads, t // tq),
        in_specs=[pl.BlockSpec((gw, tq), lambda g, i, b: (g, i)),
                  pl.BlockSpec((t, HEAD_DIM), lambda g, i, b: (0, g)),
                  pl.BlockSpec((vrows, t), lambda g, i, b: (g, 0)),
                  pl.BlockSpec((tq, gw), lambda g, i, b: (i, g_blk0 + g))],
        out_specs=pl.BlockSpec((tq, gw), lambda g, i, b: (i, g)),
        scratch_shapes=[pltpu.VMEM((group, 1, tq), F32),
                        pltpu.VMEM((group, vrows, tq), F32)])
    return pl.pallas_call(
        functools.partial(_gqa_kernel, group=group, tq=tq, tk=tk, n_lat=n_lat, n_ctx=t - n_lat),
        grid_spec=grid_spec,
        out_shape=jax.ShapeDtypeStruct((t, n_q_heads * HEAD_DIM), BF16),
        compiler_params=_params(("arbitrary", "arbitrary"), 48),
        name="gqa_attention",
    )(bounded, qt, k, vt, u)


def _na_kernel(bounded_ref, qt_ref, k0_ref, k1_ref, k2_ref, kc_ref, v0_ref, v1_ref, v2_ref, vc_ref,
               bias_ref, gate_ref, o_ref, *, n_lat_blocks):
    is_latent = pl.program_id(1) < n_lat_blocks
    is_bounded = bounded_ref[0] != 0
    vrows = HEAD_DIM + GQA_ONES_ROWS

    def probabilities(j, latent, bounded):
        hs = slice(j * HEAD_DIM, (j + 1) * HEAD_DIM)
        qt = qt_ref[hs, :]
        scores = [jnp.dot(kc_ref[:, hs], qt, preferred_element_type=F32)]
        if latent:
            k = jnp.concatenate([k0_ref[:, hs], k1_ref[:, hs], k2_ref[:, hs]], axis=0)
            scores.append(jnp.dot(k, qt, preferred_element_type=F32) + bias_ref[j].astype(F32))
        if not bounded:
            m = functools.reduce(jnp.maximum, [jnp.max(s, axis=0, keepdims=True) for s in scores])
            scores = [s - m for s in scores]
        return [jnp.exp2(s).astype(BF16) for s in scores]

    def output(j, probs, latent):
        hs = slice(j * HEAD_DIM, (j + 1) * HEAD_DIM)
        vs = slice(j * vrows, (j + 1) * vrows)
        acc = jnp.dot(vc_ref[vs, :], probs[0], preferred_element_type=F32)
        if latent:
            vt = jnp.concatenate([v0_ref[vs, :], v1_ref[vs, :], v2_ref[vs, :]], axis=1)
            acc = acc + jnp.dot(vt, probs[1], preferred_element_type=F32)
        o = (acc[:HEAD_DIM] / acc[HEAD_DIM:HEAD_DIM + 1]).T
        o_ref[:, hs] = (o * _silu(gate_ref[:, hs].astype(F32))).astype(o_ref.dtype)

    heads = range(qt_ref.shape[0] // HEAD_DIM)
    for latent in (True, False):
        for bounded in (True, False):
            @pl.when(jnp.logical_and(is_latent == latent, is_bounded == bounded))
            def _():
                probs = [probabilities(j, latent, bounded) for j in heads]
                for j in heads:
                    output(j, probs[j], latent)


def _na_bias_table(rel_bias, n_rows):
    c = np.arange(GRID_W)[None, :]
    kj = np.arange(GRID_W)[:, None]
    cs = np.clip(c - NA_KW // 2, 0, GRID_W - NA_KW)
    col_ok = (kj >= cs) & (kj < cs + NA_KW)
    sel_c = (np.arange(2 * NA_KW - 1)[:, None, None] == (kj - c + NA_KW - 1)[None]) & col_ok[None]
    cols = jnp.einsum('hde,ekc->hdkc', rel_bias * math.log2(math.e), jnp.asarray(sel_c, F32),
                      precision=lax.Precision.HIGHEST)
    cols = jnp.where(jnp.asarray(col_ok)[None, None], cols, NEG_INF).astype(BF16)
    neg = jnp.full((rel_bias.shape[0], GRID_W, GRID_W), NEG_INF, BF16)
    tables = []
    for r0, ws in ((0, 0), (NA_Q_ROWS, 0), (n_rows - NA_Q_ROWS, n_rows - NA_WIN_ROWS)):
        key_rows = []
        for i in range(NA_WIN_ROWS):
            blocks = []
            for a in range(NA_Q_ROWS):
                r, kr = r0 + a, ws + i
                rs = min(max(r - NA_KH // 2, 0), n_rows - NA_KH)
                blocks.append(cols[:, kr - r + NA_KH - 1] if rs <= kr < rs + NA_KH else neg)
            key_rows.append(jnp.concatenate(blocks, axis=-1))
        tables.append(jnp.concatenate(key_rows, axis=-2))
    return jnp.stack(tables, axis=0)


def _na_attention(bounded, qt, kn, vt, u, n_heads, off_gate, bias_tbl, n_lat):
    t = u.shape[0]
    tb = NA_Q_ROWS * GRID_W
    n_lat_blocks = n_lat // tb
    ctx_blk = t // tb - 1
    hp = NA_HEADS_PER_STEP
    hw = hp * HEAD_DIM
    vrows = hp * (HEAD_DIM + GQA_ONES_ROWS)
    assert t - n_lat == tb and n_lat_blocks >= 3 and off_gate % hw == 0 and n_heads % hp == 0
    g_blk0 = off_gate // hw

    def win(i, j):
        return jnp.clip(i - 1, 0, n_lat_blocks - 3) + j

    def k_spec(j):
        return pl.BlockSpec((tb, hw), lambda h, i, b: (win(i, j), h))

    def v_spec(j):
        return pl.BlockSpec((vrows, tb), lambda h, i, b: (h, win(i, j)))

    def bias_cls(h, i, b):
        return (jnp.where(i == 0, 0, jnp.where(i >= n_lat_blocks - 1, 2, 1)), h, 0, 0)

    grid_spec = pltpu.PrefetchScalarGridSpec(
        num_scalar_prefetch=1,
        grid=(n_heads // hp, t // tb),
        in_specs=[pl.BlockSpec((hw, tb), lambda h, i, b: (h, i)),
                  k_spec(0), k_spec(1), k_spec(2),
                  pl.BlockSpec((tb, hw), lambda h, i, b: (ctx_blk, h)),
                  v_spec(0), v_spec(1), v_spec(2),
                  pl.BlockSpec((vrows, tb), lambda h, i, b: (h, ctx_blk)),
                  pl.BlockSpec((None, hp, NA_WIN_ROWS * GRID_W, tb), bias_cls),
                  pl.BlockSpec((tb, hw), lambda h, i, b: (i, g_blk0 + h))],
        out_specs=pl.BlockSpec((tb, hw), lambda h, i, b: (i, h)))
    return pl.pallas_call(
        functools.partial(_na_kernel, n_lat_blocks=n_lat_blocks),
        grid_spec=grid_spec,
        out_shape=jax.ShapeDtypeStruct((t, n_heads * HEAD_DIM), BF16),
        compiler_params=_params(("arbitrary", "arbitrary"), 32),
        name="na_attention",
    )(bounded, qt, kn, kn, kn, kn, vt, vt, vt, vt, bias_tbl, u)


def _short_conv_kernel(x_ref, prev_ref, next_ref, w_ref, o_ref, *, tr):
    i = pl.program_id(0)
    n = pl.num_programs(0)
    x = x_ref[...].astype(F32)
    halo = prev_ref.shape[0]
    prev_row = jnp.where(i == 0, 0.0, prev_ref[halo - 1:halo, :].astype(F32))
    next_row = jnp.where(i == n - 1, 0.0, next_ref[0:1, :].astype(F32))
    row = lax.broadcasted_iota(jnp.int32, x.shape, 0)
    up = jnp.where(row == 0, prev_row, pltpu.roll(x, 1, 0))
    down = jnp.where(row == tr - 1, next_row, pltpu.roll(x, tr - 1, 0))
    o_ref[...] = (up * w_ref[0:1, :] + x * w_ref[1:2, :] + down * w_ref[2:3, :]).astype(o_ref.dtype)


def _short_conv(u, width, conv_w, n_lat):
    tr = _largest_tile(n_lat, 512, 256)
    tc = _largest_tile(width, 1024, 256)
    halo = 16
    hb = tr // halo
    n_tiles = n_lat // tr
    wpad = jnp.zeros((8, width), F32).at[:HY_SHORT].set(conv_w.astype(F32))
    return pl.pallas_call(
        functools.partial(_short_conv_kernel, tr=tr),
        grid=(n_tiles, width // tc),
        in_specs=[pl.BlockSpec((tr, tc), lambda i, j: (i, j)),
                  pl.BlockSpec((halo, tc), lambda i, j: (jnp.maximum(i * hb - 1, 0), j)),
                  pl.BlockSpec((halo, tc), lambda i, j: (jnp.minimum((i + 1) * hb, n_tiles * hb - 1), j)),
                  pl.BlockSpec((8, tc), lambda i, j: (0, j))],
        out_specs=pl.BlockSpec((tr, tc), lambda i, j: (i, j)),
        out_shape=jax.ShapeDtypeStruct((n_lat, width), BF16),
        compiler_params=_params(("arbitrary", "arbitrary"), 32),
        name="short_conv",
    )(u, u, u, wpad)


def _hy_gate1_kernel(v_ref, x1_ref, c_ref, d_ref, o_ref):
    v = v_ref[...].astype(F32)
    o_ref[...] = (x1_ref[...].astype(F32) * (c_ref[...].astype(F32) + v * d_ref[0:1, :])).astype(o_ref.dtype)


def _hy_gate1(vx, c1, dskip8, hy):
    n_lat = vx.shape[0]
    tr = _largest_tile(n_lat, 1024, 256)
    tc = _largest_tile(hy, 512, 256)
    nb = hy // tc
    return pl.pallas_call(
        _hy_gate1_kernel,
        grid=(n_lat // tr, nb),
        in_specs=[pl.BlockSpec((tr, tc), lambda i, j: (i, j)),
                  pl.BlockSpec((tr, tc), lambda i, j: (i, nb + j)),
                  pl.BlockSpec((tr, tc), lambda i, j: (i, j)),
                  pl.BlockSpec((8, tc), lambda i, j: (0, j))],
        out_specs=pl.BlockSpec((tr, tc), lambda i, j: (i, j)),
        out_shape=jax.ShapeDtypeStruct((n_lat, hy), BF16),
        compiler_params=_params(("arbitrary", "arbitrary"), 32),
        name="hy_gate1",
    )(vx, vx, c1, dskip8)


def _hy_gate2_kernel(x2_ref, c_ref, y1_ref, d_ref, g_ref, ctx_ref, o_ref, *, n_lat_tiles):
    i = pl.program_id(0)

    @pl.when(i < n_lat_tiles)
    def _():
        y1 = y1_ref[...].astype(F32)
        z = x2_ref[...].astype(F32) * (c_ref[...].astype(F32) + y1 * d_ref[1:2, :])
        o_ref[...] = (z * _silu(g_ref[...].astype(F32))).astype(o_ref.dtype)

    @pl.when(i >= n_lat_tiles)
    def _():
        o_ref[...] = ctx_ref[...]


def _hy_gate2(vx, c2, y1, dskip8, u, off_gate, hy_ctx, hy):
    t = u.shape[0]
    n_lat = vx.shape[0]
    tr = t - n_lat
    tc = _largest_tile(hy, 512, 256)
    nb = hy // tc
    n_lat_tiles = n_lat // tr
    assert n_lat % tr == 0 and off_gate % tc == 0
    g_blk0 = off_gate // tc
    lat = lambda i: jnp.minimum(i, n_lat_tiles - 1)
    return pl.pallas_call(
        functools.partial(_hy_gate2_kernel, n_lat_tiles=n_lat_tiles),
        grid=(t // tr, nb),
        in_specs=[pl.BlockSpec((tr, tc), lambda i, j: (lat(i), 2 * nb + j)),
                  pl.BlockSpec((tr, tc), lambda i, j: (lat(i), j)),
                  pl.BlockSpec((tr, tc), lambda i, j: (lat(i), j)),
                  pl.BlockSpec((8, tc), lambda i, j: (0, j)),
                  pl.BlockSpec((tr, tc), lambda i, j: (i, g_blk0 + j)),
                  pl.BlockSpec((tr, tc), lambda i, j: (0, j))],
        out_specs=pl.BlockSpec((tr, tc), lambda i, j: (i, j)),
        out_shape=jax.ShapeDtypeStruct((t, hy), BF16),
        compiler_params=_params(("arbitrary", "arbitrary"), 32),
        name="hy_gate2",
    )(vx, c2, y1, dskip8, u, hy_ctx)


def _filter_hidden_kernel(z_ref, w1_ref, b1_ref, w2_ref, b2_ref, fr_ref, o_ref):
    hp = lax.Precision.HIGHEST
    fr = fr_ref[...]
    h = jnp.sin(fr * (jnp.dot(z_ref[...], w1_ref[...], precision=hp,
                              preferred_element_type=F32) + b1_ref[...]))
    o_ref[...] = jnp.sin(fr * (jnp.dot(h, w2_ref[...], precision=hp,
                                       preferred_element_type=F32) + b2_ref[...]))


def _pad2(a, rows, cols):
    return jnp.zeros((rows, cols), F32).at[:a.shape[0], :a.shape[1]].set(a.astype(F32))


def _filter_hidden(feat, w1, b1, w2, b2, freq):
    r = feat.shape[0]
    tr = _largest_tile(r, 1024, 256)
    full = lambda i: (0, 0)
    return pl.pallas_call(
        _filter_hidden_kernel,
        grid=(r // tr,),
        in_specs=[pl.BlockSpec((tr, LANE), lambda i: (i, 0)),
                  pl.BlockSpec((LANE, LANE), full), pl.BlockSpec((1, LANE), full),
                  pl.BlockSpec((LANE, LANE), full), pl.BlockSpec((1, LANE), full),
                  pl.BlockSpec((1, LANE), full)],
        out_specs=pl.BlockSpec((tr, LANE), lambda i: (i, 0)),
        out_shape=jax.ShapeDtypeStruct((r, LANE), F32),
        compiler_params=_params(("arbitrary",), 32),
        name="filter_hidden",
    )(feat, _pad2(w1, LANE, LANE), _pad2(b1[None], 1, LANE), _pad2(w2, LANE, LANE),
      _pad2(b2[None], 1, LANE), _pad2(freq[None], 1, LANE))


def _two_sided_positions(n):
    idx = np.arange(2 * n)
    return np.where(idx < n, idx, 2 * n - idx) % n


def _filter_features(n, n_bands):
    pos = _two_sided_positions(n)
    t = jnp.linspace(0.0, 1.0, n, dtype=F32)[:, None]
    w = (2.0 * math.pi / n) * jnp.arange(n, dtype=F32)[:, None]
    bands = jnp.linspace(1e-4, n_bands - 1, n_bands, dtype=F32)[None, :]
    z = jnp.concatenate([t, jnp.cos(bands * w), -jnp.sin(bands * w)], axis=-1)
    z = z[pos]
    return _pad2(z, 2 * n, LANE), t[pos]


def _filter_gen_kernel(h_ref, w_ref, t_ref, dec_ref, o_ref, *, tr, n):
    i = pl.program_id(0)
    f = jnp.dot(h_ref[...].astype(BF16), w_ref[...], preferred_element_type=F32)
    f = f * jnp.exp(-t_ref[...] * jnp.abs(dec_ref[0:1, :]))
    row = i * tr + lax.broadcasted_iota(jnp.int32, (tr, 1), 0)
    o_ref[...] = jnp.where(row == n, 0.0, f).astype(o_ref.dtype)


def _filter_gen(hidden, row0, w3p, tcol, decay8, n, order, hy):
    tr = _largest_tile(n, 512, 256)
    tc = _largest_tile(hy, 1024, 256)
    nb = hy // tc
    n_half = n // tr
    hb0 = row0 // tr
    assert row0 % tr == 0
    return pl.pallas_call(
        functools.partial(_filter_gen_kernel, tr=tr, n=n),
        grid=(2 * n // tr, order * nb),
        in_specs=[pl.BlockSpec((tr, LANE), lambda i, j: (hb0 + i, 0)),
                  pl.BlockSpec((LANE, tc), lambda i, j: (0, (j // nb) * 2 * nb + (i // n_half) * nb + j % nb)),
                  pl.BlockSpec((tr, 1), lambda i, j: (i, 0)),
                  pl.BlockSpec((None, 8, tc), lambda i, j: (j // nb, 0, j % nb))],
        out_specs=pl.BlockSpec((tr, tc), lambda i, j: (i, j)),
        out_shape=jax.ShapeDtypeStruct((2 * n, order * hy), BF16),
        compiler_params=_params(("arbitrary", "arbitrary"), 32),
        name="filter_gen",
    )(hidden, w3p, tcol, decay8)


def _dft_tables(n1, nz):
    n2 = FFT_N2
    n = n1 * n2
    half = n1 // 2
    k1 = jnp.arange(half + 1)
    two_pi = 2.0 * math.pi

    def cs(idx, period):
        ang = (idx % period).astype(F32) * (two_pi / period)
        return jnp.cos(ang), -jnp.sin(ang)

    eye8 = jnp.eye(8, dtype=F32)
    fr, fi = cs(k1[:, None] * jnp.arange(nz)[None, :], n1)
    kf = jnp.stack([fr, fi], axis=1)
    kf = jnp.einsum('kpn,jl->kpjnl', kf, eye8).reshape((half + 1) * 16, nz * 8)
    gr, gi = cs(jnp.arange(half)[:, None] * k1[None, :], n1)
    weight = jnp.where((k1 == 0) | (k1 == half), 1.0, 2.0) / n
    ki = jnp.stack([gr, gi], axis=2) * weight[None, :, None]
    ki = jnp.einsum('nkp,jl->njkpl', ki, eye8).reshape(half * 8, (half + 1) * 16)
    r = jnp.arange(2 * n2)
    part = (r % 16) // 8
    n2_of_r = 8 * (r // 16) + r % 8
    k2 = jnp.arange(n2)
    idx = k2[None, :, None] * n2_of_r[None, None, :] * n1 + k1[:, None, None] * n2_of_r[None, None, :]
    cr, ci = cs(idx, n)
    top = jnp.where(part[None, None, :] == 0, cr, -ci)
    bot = jnp.where(part[None, None, :] == 0, ci, cr)
    m = jnp.concatenate([top, bot], axis=1)
    return kf.astype(BF16), ki.astype(BF16), m.astype(BF16)


def _cross_slab_forward(z_ref, kf_ref, a_scr, nz, nh, ct):
    def body(b, carry):
        z = z_ref[:, pl.ds(pl.multiple_of(16 * b, 16), 16), :].astype(F32)
        for half in range(2):
            g = z[:, 8 * half:8 * half + 8, :].reshape(nz * 8, ct).astype(BF16)
            p = jnp.dot(kf_ref[...], g, preferred_element_type=F32)
            start = pl.multiple_of(32 * b + 16 * half, 16)
            a_scr[:, pl.ds(start, 16), :] = p.astype(BF16).reshape(nh, 16, ct)
        return carry

    lax.fori_loop(0, FFT_N2 // 16, body, 0)


def _slab_block(nh):
    return max(k for k in range(1, 17) if nh % k == 0)


def _fft_fwd_kernel(z_ref, kf_ref, m_ref, o_ref, a_scr, *, nz, nh, kb, ct):
    kblk = pl.program_id(1)

    @pl.when(kblk == 0)
    def _():
        _cross_slab_forward(z_ref, kf_ref, a_scr, nz, nh, ct)

    for k in range(kb):
        x = jnp.dot(m_ref[k], a_scr[kblk * kb + k], preferred_element_type=F32)
        o_ref[k] = x.astype(o_ref.dtype)


def _fft_forward(z3, kf, m, ct=256):
    nz, _, c = z3.shape
    nh = m.shape[0]
    kb = _slab_block(nh)
    return pl.pallas_call(
        functools.partial(_fft_fwd_kernel, nz=nz, nh=nh, kb=kb, ct=ct),
        grid=(c // ct, nh // kb),
        in_specs=[pl.BlockSpec((nz, FFT_N2, ct), lambda j, k: (0, 0, j)),
                  pl.BlockSpec(kf.shape, lambda j, k: (0, 0)),
                  pl.BlockSpec((kb, 2 * FFT_N2, 2 * FFT_N2), lambda j, k: (k, 0, 0))],
        out_specs=pl.BlockSpec((kb, 2 * FFT_N2, ct), lambda j, k: (k, 0, j)),
        out_shape=jax.ShapeDtypeStruct((nh, 2 * FFT_N2, c), BF16),
        scratch_shapes=[pltpu.VMEM((nh, 2 * FFT_N2, ct), BF16)],
        compiler_params=_params(("arbitrary", "arbitrary"), 56),
        name="fft_forward",
    )(z3, kf, m)


def _fft_conv_kernel(z_ref, kf_ref, ki_ref, m_ref, h_ref, o_ref, a_scr, *, nz, nh, kb, ct):
    kblk = pl.program_id(1)
    n2 = FFT_N2

    @pl.when(kblk == 0)
    def _():
        _cross_slab_forward(z_ref, kf_ref, a_scr, nz, nh, ct)

    for k in range(kb):
        mk = m_ref[k]
        x = jnp.dot(mk, a_scr[kblk * kb + k], preferred_element_type=F32)
        h = h_ref[k].astype(F32)
        xr, xi, hr, hi = x[:n2], x[n2:], h[:n2], h[n2:]
        y = jnp.concatenate([xr * hr - xi * hi, xr * hi + xi * hr], axis=0).astype(BF16)
        b = lax.dot_general(mk, y, (((0,), (0,)), ((), ())), preferred_element_type=F32)
        a_scr[kblk * kb + k] = b.astype(BF16)

    @pl.when(kblk == pl.num_programs(1) - 1)
    def _():
        def body(b, carry):
            halves = []
            for half in range(2):
                start = pl.multiple_of(32 * b + 16 * half, 16)
                r = a_scr[:, pl.ds(start, 16), :].reshape(nh * 16, ct)
                p = jnp.dot(ki_ref[...], r, preferred_element_type=F32)
                halves.append(p.reshape(nz, 8, ct))
            out = jnp.concatenate(halves, axis=1)
            o_ref[:, pl.ds(pl.multiple_of(16 * b, 16), 16), :] = out.astype(o_ref.dtype)
            return carry

        lax.fori_loop(0, n2 // 16, body, 0)


def _fft_conv(z3, z_blk0, spec, spec_blk0, kf, ki, m, c, ct=256):
    nz = z3.shape[0]
    nh = m.shape[0]
    kb = _slab_block(nh)
    assert nh == nz + 1
    return pl.pallas_call(
        functools.partial(_fft_conv_kernel, nz=nz, nh=nh, kb=kb, ct=ct),
        grid=(c // ct, nh // kb),
        in_specs=[pl.BlockSpec((nz, FFT_N2, ct), lambda j, k: (0, 0, z_blk0 + j)),
                  pl.BlockSpec(kf.shape, lambda j, k: (0, 0)),
                  pl.BlockSpec(ki.shape, lambda j, k: (0, 0)),
                  pl.BlockSpec((kb, 2 * FFT_N2, 2 * FFT_N2), lambda j, k: (k, 0, 0)),
                  pl.BlockSpec((kb, 2 * FFT_N2, ct), lambda j, k: (k, 0, spec_blk0 + j))],
        out_specs=pl.BlockSpec((nz, FFT_N2, ct), lambda j, k: (0, 0, j)),
        out_shape=jax.ShapeDtypeStruct((nz, FFT_N2, c), BF16),
        scratch_shapes=[pltpu.VMEM((nh, 2 * FFT_N2, ct), BF16)],
        compiler_params=_params(("arbitrary", "arbitrary"), 56),
        name="fft_conv",
    )(z3, kf, ki, m, spec)


def _ctx_dft_tables(n):
    nn = 2 * n
    k = jnp.arange(nn)
    ang = ((k[:, None] * k[None, :]) % nn).astype(F32) * (2.0 * math.pi / nn)
    cr, ci = jnp.cos(ang), -jnp.sin(ang)
    full = jnp.concatenate([cr, ci], axis=0)
    inv = jnp.concatenate([cr[:n], ci[:n]], axis=1) / nn
    return full[:, :n].astype(BF16), full.astype(BF16), inv.astype(BF16)


def _ctx_hyena_kernel(v_ref, x1_ref, x2_ref, g_ref, cw0_ref, cw1_ref, cw2_ref, hid_ref,
                      w10_ref, w11_ref, w20_ref, w21_ref, t_ref, dec_ref, dsk_ref,
                      ff_ref, ffull_ref, finv_ref, o_ref, *, n):
    row = lax.broadcasted_iota(jnp.int32, (n, 1), 0)

    def sconv(x_ref, w_ref):
        x = x_ref[...].astype(F32)
        up = jnp.where(row == 0, 0.0, pltpu.roll(x, 1, 0))
        down = jnp.where(row == n - 1, 0.0, pltpu.roll(x, n - 1, 0))
        y = up * w_ref[0:1, :] + x * w_ref[1:2, :] + down * w_ref[2:3, :]
        return y.astype(BF16).astype(F32)

    hid = hid_ref[...].astype(BF16)
    tcol = t_ref[...]
    row2 = lax.broadcasted_iota(jnp.int32, (2 * n, 1), 0)

    def spectrum(wf_ref, wb_ref, o):
        fwd = jnp.dot(hid[:n], wf_ref[...], preferred_element_type=F32)
        bwd = jnp.dot(hid[n:], wb_ref[...], preferred_element_type=F32)
        f = jnp.concatenate([fwd, bwd], axis=0) * jnp.exp(-tcol * jnp.abs(dec_ref[o:o + 1, :]))
        f = jnp.where(row2 == n, 0.0, f).astype(BF16)
        return jnp.dot(ffull_ref[...], f, preferred_element_type=F32)

    def conv(z, h):
        x = jnp.dot(ff_ref[...], z.astype(BF16), preferred_element_type=F32)
        h = h.astype(BF16).astype(F32)
        nn = 2 * n
        xr, xi, hr, hi = x[:nn], x[nn:], h[:nn], h[nn:]
        y = jnp.concatenate([xr * hr - xi * hi, xr * hi + xi * hr], axis=0).astype(BF16)
        return jnp.dot(finv_ref[...], y, preferred_element_type=F32)

    v = sconv(v_ref, cw0_ref)
    x1 = sconv(x1_ref, cw1_ref)
    x2 = sconv(x2_ref, cw2_ref)
    c1 = conv(v, spectrum(w10_ref, w11_ref, 0)).astype(BF16).astype(F32)
    y1 = (x1 * (c1 + v * dsk_ref[0:1, :])).astype(BF16).astype(F32)
    c2 = conv(y1, spectrum(w20_ref, w21_ref, 1)).astype(BF16).astype(F32)
    z = x2 * (c2 + y1 * dsk_ref[1:2, :])
    o_ref[...] = (z * _silu(g_ref[...].astype(F32))).astype(o_ref.dtype)


def _ctx_hyena(u, n_lat, hy, order, off_gate, conv8, hidden, hid_row0, w3p, tcol, decay8, dskip8, tables):
    t = u.shape[0]
    n = t - n_lat
    assert order == 2 and n_lat % n == 0 and hid_row0 % (2 * n) == 0
    ct = 256
    nb = hy // ct
    rb = n_lat // n
    ff, ffull, finv = tables
    const = lambda j: (0, 0)
    ublk = lambda c0: pl.BlockSpec((n, ct), lambda j: (rb, c0 + j))
    wblk = lambda c0: pl.BlockSpec((LANE, ct), lambda j: (0, c0 + j))
    cwblk = lambda c0: pl.BlockSpec((8, ct), lambda j: (0, c0 + j))
    return pl.pallas_call(
        functools.partial(_ctx_hyena_kernel, n=n),
        grid=(nb,),
        in_specs=[ublk(0), ublk(nb), ublk(2 * nb), ublk(off_gate // ct),
                  cwblk(0), cwblk(nb), cwblk(2 * nb),
                  pl.BlockSpec((2 * n, LANE), lambda j: (hid_row0 // (2 * n), 0)),
                  wblk(0), wblk(nb), wblk(2 * nb), wblk(3 * nb),
                  pl.BlockSpec((2 * n, 1), const),
                  pl.BlockSpec((8, ct), lambda j: (0, j)),
                  pl.BlockSpec((8, ct), lambda j: (0, j)),
                  pl.BlockSpec(ff.shape, const), pl.BlockSpec(ffull.shape, const),
                  pl.BlockSpec(finv.shape, const)],
        out_specs=pl.BlockSpec((n, ct), lambda j: (0, j)),
        out_shape=jax.ShapeDtypeStruct((n, hy), BF16),
        compiler_params=_params(("arbitrary",), 40),
        name="ctx_hyena",
    )(u, u, u, u, conv8, conv8, conv8, hidden, w3p, w3p, w3p, w3p, tcol, decay8, dskip8,
      ff, ffull, finv)


def _even_layer(x_all, mods_l, g, n_lat, w_in, w_out, conv_w, w1, b1, w2, b2, w3, freq, decay, dskip,
                q_norm, k_norm, rope, dft, ctx_dft, last):
    t, d = x_all.shape
    n_ctx = t - n_lat
    order, hy = decay.shape
    gqa_w = d - hy
    n_q = gqa_w // HEAD_DIM
    n_kv = n_q // 4
    kv_w = n_kv * HEAD_DIM
    off_hy_gate = (order + 1) * hy
    off_q = off_hy_gate + hy
    off_v = off_q + gqa_w + kv_w
    off_at_gate = off_v + kv_w
    scale = HEAD_DIM ** -0.5

    h = _norm_mod(x_all, g, mods_l, n_lat)
    u = _in_proj(h, w_in)

    q_gain = q_norm.astype(F32) * (scale * math.log2(math.e))
    k_gain = k_norm.astype(F32)
    qt, kn, vt = _attn_prep(u, off_q, off_q + gqa_w, off_v, n_q, n_kv, 1, q_gain, k_gain,
                            rope[0], rope[1], True)
    att = _gqa_attention(_scores_bounded(q_gain, k_gain), qt, kn, vt, u, n_q, n_kv, off_at_gate, n_lat)

    n_bands = (w1.shape[0] - 1) // 2
    feat_l, t_l = _filter_features(n_lat, n_bands)
    feat_c, t_c = _filter_features(n_ctx, n_bands)
    hidden = _filter_hidden(jnp.concatenate([feat_l, feat_c], axis=0), w1, b1, w2, b2, freq)
    w3p = jnp.zeros((LANE, w3.shape[1]), BF16).at[:w3.shape[0]].set(w3.astype(BF16))
    decay8 = jnp.zeros((order, 8, hy), F32).at[:, 0].set(decay.astype(F32))
    dskip8 = jnp.zeros((8, hy), F32).at[:order].set(dskip.astype(F32))
    conv8 = jnp.zeros((8, conv_w.shape[1]), F32).at[:HY_SHORT].set(conv_w.astype(F32))
    kf_pad, kf_full, ki, m = dft
    n1 = 2 * n_lat // FFT_N2
    filt = _filter_gen(hidden, 0, w3p, t_l, decay8, n_lat, order, hy)
    spec = _fft_forward(filt.reshape(n1, FFT_N2, order * hy), kf_full, m)

    vx = _short_conv(u, (order + 1) * hy, conv_w, n_lat)
    vx3 = vx.reshape(n1 // 2, FFT_N2, (order + 1) * hy)
    ct = 256
    c1 = _fft_conv(vx3, 0, spec, 0, kf_pad, ki, m, hy).reshape(n_lat, hy)
    y1 = _hy_gate1(vx, c1, dskip8, hy)
    c2 = _fft_conv(y1.reshape(n1 // 2, FFT_N2, hy), 0, spec, hy // ct, kf_pad, ki, m, hy).reshape(n_lat, hy)

    decay_c8 = jnp.zeros((8, hy), F32).at[:order].set(decay.astype(F32))
    hy_ctx = _ctx_hyena(u, n_lat, hy, order, off_hy_gate, conv8, hidden, 2 * n_lat, w3p, t_c,
                        decay_c8, dskip8, ctx_dft)
    hy_mix = _hy_gate2(vx, c2, y1, dskip8, u, off_hy_gate, hy_ctx, hy)

    return _out_proj(hy_mix, 0, att, 0, w_out, x_all, mods_l, n_lat, last)


def _odd_layer(x_all, mods_l, g, n_lat, w_in, w_out, q_norm, k_norm, rel_bias, last):
    t, d = x_all.shape
    n_heads = rel_bias.shape[0]
    w = n_heads * HEAD_DIM
    scale = HEAD_DIM ** -0.5
    h = _norm_mod(x_all, g, mods_l, n_lat)
    u = _in_proj(h, w_in)
    log2e = math.log2(math.e)
    q_gain = q_norm.astype(F32) * (scale * log2e)
    k_gain = k_norm.astype(F32)
    heads_per_step = NA_HEADS_PER_STEP
    unused = jnp.zeros((t, HEAD_DIM), F32)
    qt, kn, vt = _attn_prep(u, 0, w, 2 * w, heads_per_step, heads_per_step, n_heads // heads_per_step,
                            q_gain, k_gain, unused, unused, False)
    bias_tbl = _na_bias_table(rel_bias.astype(F32), n_lat // GRID_W)
    bounded = _scores_bounded(q_gain, k_gain, jnp.max(jnp.abs(rel_bias)).astype(F32) * (log2e * 1.02))
    att = _na_attention(bounded, qt, kn, vt, u, n_heads, 3 * w, bias_tbl, n_lat)
    return _out_proj(att, 0, att, 1, w_out, x_all, mods_l, n_lat, last)


def kernel(x, c, ctx, c_ctx, norm_g, ada_w, ada_b, e_w_in, e_w_out, hy_conv, hy_w1, hy_b1, hy_w2, hy_b2,
           hy_w3, hy_freq, hy_decay, hy_dskip, gqa_q_norm, gqa_k_norm, o_w_in, o_w_out, na_q_norm,
           na_k_norm, na_rel_bias):
    batch, n_lat, d = x.shape
    n_ctx = ctx.shape[1]
    depth = norm_g.shape[0]
    assert batch == 1 and n_lat % (2 * FFT_N2) == 0 and n_ctx == NA_Q_ROWS * GRID_W

    x_all = jnp.concatenate([x[0], ctx[0]], axis=0).astype(F32)
    cvecs = jnp.zeros((8, d), F32).at[0].set(c[0]).at[1].set(c_ctx)
    mods = _ada_mods(cvecs, ada_w, ada_b)

    rope = _rope_tables(n_lat, n_ctx)
    n1 = 2 * n_lat // FFT_N2
    kf_pad, ki, m = _dft_tables(n1, n1 // 2)
    kf_full, _, _ = _dft_tables(n1, n1)
    dft = (kf_pad, kf_full, ki, m)
    ctx_dft = _ctx_dft_tables(n_ctx)

    for layer in range(depth):
        i = layer // 2
        last = layer == depth - 1
        if layer % 2 == 0:
            x_all = _even_layer(x_all, mods[layer], norm_g[layer], n_lat,
                                e_w_in[i].astype(BF16), e_w_out[i].astype(BF16), hy_conv[i],
                                hy_w1[i], hy_b1[i], hy_w2[i], hy_b2[i], hy_w3[i], hy_freq[i],
                                hy_decay[i], hy_dskip[i], gqa_q_norm[i], gqa_k_norm[i],
                                rope, dft, ctx_dft, last)
        else:
            x_all = _odd_layer(x_all, mods[layer], norm_g[layer], n_lat,
                               o_w_in[i].astype(BF16), o_w_out[i].astype(BF16),
                               na_q_norm[i], na_k_norm[i], na_rel_bias[i], last)
    return x_all[None]
```

```python
import functools
import math

import numpy as np
import jax
import jax.numpy as jnp
from jax import lax
from jax.experimental import pallas as pl
from jax.experimental.pallas import tpu as pltpu

F32 = jnp.float32
BF16 = jnp.bfloat16

HEAD_DIM = 128
GRID_W = 64
EPS = 1e-6
ROPE_THETA = 10000.0
NA_KH = 8
NA_KW = 16
NA_Q_ROWS = 4
NA_WIN_ROWS = 12
NA_HEADS_PER_STEP = 4
HY_SHORT = 3
FFT_N2 = 128
NEG_INF = -1e30

V7X_VMEM_BYTES = 64 * 1024 * 1024
LANE = 128


def _params(semantics, vmem_mb):
    assert vmem_mb * 1024 * 1024 < V7X_VMEM_BYTES
    return pltpu.CompilerParams(dimension_semantics=semantics,
                                vmem_limit_bytes=vmem_mb * 1024 * 1024)


def _largest_tile(n, target, quantum):
    best = None
    t = quantum
    while t <= min(n, target):
        if n % t == 0:
            best = t
        t += quantum
    assert best is not None, (n, target, quantum)
    return best


def _silu(x):
    return x * jax.nn.sigmoid(x)


def _mods_kernel(c_ref, w_ref, b_ref, o_ref):
    s = _silu(c_ref[...]).astype(BF16)
    o_ref[...] = jnp.dot(s, w_ref[...].astype(BF16), preferred_element_type=F32) + b_ref[...]


def _ada_mods(cvecs, ada_w, ada_b):
    depth, d, n = ada_w.shape
    tn = _largest_tile(n, 512, LANE)
    return pl.pallas_call(
        _mods_kernel,
        grid=(depth, n // tn),
        in_specs=[pl.BlockSpec((8, d), lambda l, j: (0, 0)),
                  pl.BlockSpec((None, d, tn), lambda l, j: (l, 0, j)),
                  pl.BlockSpec((None, 1, tn), lambda l, j: (l, 0, j))],
        out_specs=pl.BlockSpec((None, 8, tn), lambda l, j: (l, 0, j)),
        out_shape=jax.ShapeDtypeStruct((depth, 8, n), F32),
        compiler_params=_params(("arbitrary", "arbitrary"), 40),
        name="ada_mods",
    )(cvecs, ada_w, ada_b.reshape(depth, 1, n))


def _norm_mod_kernel(x_ref, g_ref, sh_ref, sc_ref, o_ref, *, n_lat_tiles):
    x = x_ref[...]
    y = x * lax.rsqrt(jnp.mean(x * x, axis=-1, keepdims=True) + EPS) * g_ref[...]
    is_ctx = pl.program_id(0) >= n_lat_tiles
    sh = jnp.where(is_ctx, sh_ref[1:2, :], sh_ref[0:1, :])
    sc = jnp.where(is_ctx, sc_ref[1:2, :], sc_ref[0:1, :])
    o_ref[...] = (y * (1.0 + sc) + sh).astype(o_ref.dtype)


def _norm_mod(x_all, g, mods_l, n_lat):
    t, d = x_all.shape
    tr = 256
    assert t % tr == 0 and n_lat % tr == 0
    return pl.pallas_call(
        functools.partial(_norm_mod_kernel, n_lat_tiles=n_lat // tr),
        grid=(t // tr,),
        in_specs=[pl.BlockSpec((tr, d), lambda i: (i, 0)),
                  pl.BlockSpec((1, d), lambda i: (0, 0)),
                  pl.BlockSpec((8, d), lambda i: (0, 0)),
                  pl.BlockSpec((8, d), lambda i: (0, 1))],
        out_specs=pl.BlockSpec((tr, d), lambda i: (i, 0)),
        out_shape=jax.ShapeDtypeStruct((t, d), BF16),
        compiler_params=_params(("arbitrary",), 40),
        name="norm_mod",
    )(x_all, g.reshape(1, d), mods_l, mods_l)


def _in_proj_kernel(a_ref, w_ref, o_ref, wb):
    @pl.when(pl.program_id(1) == 0)
    def _():
        wb[...] = w_ref[...].astype(BF16)

    o_ref[...] = jnp.dot(a_ref[...], wb[...], preferred_element_type=F32).astype(o_ref.dtype)


def _in_proj(h, w_all, layer_idx):
    m, k = h.shape
    n = w_all.shape[2]
    tm = _largest_tile(m, 1024, 256)
    tn = _largest_tile(n, 512, 256)
    return pl.pallas_call(
        _in_proj_kernel,
        grid=(n // tn, m // tm),
        in_specs=[pl.BlockSpec((tm, k), lambda j, i: (i, 0)),
                  pl.BlockSpec((None, k, tn), lambda j, i: (layer_idx, 0, j))],
        out_specs=pl.BlockSpec((tm, tn), lambda j, i: (i, j)),
        out_shape=jax.ShapeDtypeStruct((m, n), BF16),
        scratch_shapes=[pltpu.VMEM((k, tn), BF16)],
        compiler_params=_params(("arbitrary", "arbitrary"), 48),
        name="in_proj",
    )(h, w_all)


def _out_proj_kernel(a1_ref, a2_ref, w1_ref, w2_ref, x_ref, gate_ref, o_ref, wb1, wb2, *, tm, n_lat):
    i = pl.program_id(1)

    @pl.when(i == 0)
    def _():
        wb1[...] = w1_ref[...].astype(BF16)
        wb2[...] = w2_ref[...].astype(BF16)

    y = jnp.dot(a1_ref[...], wb1[...], preferred_element_type=F32)
    y = y + jnp.dot(a2_ref[...], wb2[...], preferred_element_type=F32)
    rows = i * tm + lax.broadcasted_iota(jnp.int32, (tm, 1), 0)
    gate = jnp.where(rows >= n_lat, gate_ref[1:2, :], gate_ref[0:1, :])
    o_ref[...] = x_ref[...] + gate * y


def _out_proj(a1, a1_blk, a2, a2_blk, w_all, layer_idx, x_all, mods_l, n_lat, latent_only):
    d = x_all.shape[1]
    t = n_lat if latent_only else x_all.shape[0]
    kh = w_all.shape[1] // 2
    tm = _largest_tile(t, 1024, 256)
    tn = _largest_tile(d, 512, 256)
    gate_blk0 = 2 * d // tn
    return pl.pallas_call(
        functools.partial(_out_proj_kernel, tm=tm, n_lat=n_lat),
        grid=(d // tn, t // tm),
        in_specs=[pl.BlockSpec((tm, kh), lambda j, i: (i, a1_blk)),
                  pl.BlockSpec((tm, kh), lambda j, i: (i, a2_blk)),
                  pl.BlockSpec((None, kh, tn), lambda j, i: (layer_idx, 0, j)),
                  pl.BlockSpec((None, kh, tn), lambda j, i: (layer_idx, 1, j)),
                  pl.BlockSpec((tm, tn), lambda j, i: (i, j)),
                  pl.BlockSpec((8, tn), lambda j, i: (0, gate_blk0 + j))],
        out_specs=pl.BlockSpec((tm, tn), lambda j, i: (i, j)),
        out_shape=jax.ShapeDtypeStruct((t, d), F32),
        scratch_shapes=[pltpu.VMEM((kh, tn), BF16), pltpu.VMEM((kh, tn), BF16)],
        compiler_params=_params(("arbitrary", "arbitrary"), 48),
        name="out_proj",
    )(a1, a2, w_all, w_all, x_all, mods_l)


def _rope_tables(n_lat, n_ctx):
    quarter = HEAD_DIM // 4
    inv = ROPE_THETA ** (-jnp.arange(quarter, dtype=F32) / quarter)
    tok = jnp.arange(n_lat)
    rows = (tok // GRID_W).astype(F32)[:, None] * inv[None]
    cols = (tok % GRID_W).astype(F32)[:, None] * inv[None]
    cos = jnp.concatenate([jnp.cos(rows), jnp.cos(rows), jnp.cos(cols), jnp.cos(cols)], axis=-1)
    sin = jnp.concatenate([-jnp.sin(rows), jnp.sin(rows), -jnp.sin(cols), jnp.sin(cols)], axis=-1)
    cos = jnp.concatenate([cos, jnp.ones((n_ctx, HEAD_DIM), F32)], axis=0)
    sin = jnp.concatenate([sin, jnp.zeros((n_ctx, HEAD_DIM), F32)], axis=0)
    return cos, sin


GQA_ONES_ROWS = 16


def _attn_prep_kernel(q_ref, k_ref, v_ref, g_ref, cos_ref, sin_ref, qt_ref, ko_ref, vt_ref,
                      *, n_q, n_kv, rope):
    if rope:
        cos = cos_ref[...]
        sin = sin_ref[...]
        lane = lax.broadcasted_iota(jnp.int32, cos.shape, 1)
        first_quarter = (lane % 64) < 32

    def norm_rope(x, g):
        y = x * lax.rsqrt(jnp.mean(x * x, axis=-1, keepdims=True) + EPS) * g
        if not rope:
            return y
        partner = jnp.where(first_quarter, pltpu.roll(y, HEAD_DIM - 32, 1), pltpu.roll(y, 32, 1))
        return y * cos + partner * sin

    for h in range(n_q):
        sl = slice(h * HEAD_DIM, (h + 1) * HEAD_DIM)
        qt_ref[sl, :] = norm_rope(q_ref[:, sl].astype(F32), g_ref[:, sl]).T.astype(qt_ref.dtype)
    for h in range(n_kv):
        sl = slice(h * HEAD_DIM, (h + 1) * HEAD_DIM)
        gk = g_ref[:, (n_q + h) * HEAD_DIM:(n_q + h + 1) * HEAD_DIM]
        ko_ref[:, sl] = norm_rope(k_ref[:, sl].astype(F32), gk).astype(ko_ref.dtype)
        r0 = h * (HEAD_DIM + GQA_ONES_ROWS)
        vt_ref[r0:r0 + HEAD_DIM, :] = v_ref[:, sl].astype(F32).T.astype(vt_ref.dtype)
        vt_ref[r0 + HEAD_DIM:r0 + HEAD_DIM + GQA_ONES_ROWS, :] = jnp.ones(
            (GQA_ONES_ROWS, vt_ref.shape[1]), vt_ref.dtype)


def _attn_prep(u, off_q, off_k, off_v, n_q, n_kv, n_groups, q_gain, k_gain, cos_t, sin_t, rope):
    t = u.shape[0]
    qw, kw = n_q * HEAD_DIM, n_kv * HEAD_DIM
    tr = 256
    vrows = n_kv * (HEAD_DIM + GQA_ONES_ROWS)
    assert off_q % qw == 0 and off_k % kw == 0 and off_v % kw == 0 and t % tr == 0
    q_blk, k_blk, v_blk = off_q // qw, off_k // kw, off_v // kw
    gvec = jnp.concatenate([jnp.tile(q_gain, n_q), jnp.tile(k_gain, n_kv)])[None]
    return pl.pallas_call(
        functools.partial(_attn_prep_kernel, n_q=n_q, n_kv=n_kv, rope=rope),
        grid=(t // tr, n_groups),
        in_specs=[pl.BlockSpec((tr, qw), lambda i, j: (i, q_blk + j)),
                  pl.BlockSpec((tr, kw), lambda i, j: (i, k_blk + j)),
                  pl.BlockSpec((tr, kw), lambda i, j: (i, v_blk + j)),
                  pl.BlockSpec((1, qw + kw), lambda i, j: (0, 0)),
                  pl.BlockSpec((tr, HEAD_DIM), lambda i, j: (i, 0)),
                  pl.BlockSpec((tr, HEAD_DIM), lambda i, j: (i, 0))],
        out_specs=[pl.BlockSpec((qw, tr), lambda i, j: (j, i)),
                   pl.BlockSpec((tr, kw), lambda i, j: (i, j)),
                   pl.BlockSpec((vrows, tr), lambda i, j: (j, i))],
        out_shape=[jax.ShapeDtypeStruct((n_groups * qw, t), BF16),
                   jax.ShapeDtypeStruct((t, n_groups * kw), BF16),
                   jax.ShapeDtypeStruct((n_groups * vrows, t), BF16)],
        compiler_params=_params(("arbitrary", "arbitrary"), 32),
        name="attn_prep",
    )(u, u, u, gvec, cos_t, sin_t)


def _gqa_kernel(bounded_ref, qt_ref, k_ref, vt_ref, gate_ref, o_ref, m_s, acc_s,
                *, group, tq, tk, n_lat, n_ctx):
    acc_s[...] = jnp.zeros(acc_s.shape, F32)

    def chunk_bounded(kc, vtc):
        ps = []
        for j in range(group):
            qt = qt_ref[j * HEAD_DIM:(j + 1) * HEAD_DIM, :]
            s = jnp.dot(kc, qt, preferred_element_type=F32)
            ps.append(jnp.exp2(s).astype(BF16))
        for j in range(group):
            acc_s[j] += jnp.dot(vtc, ps[j], preferred_element_type=F32)

    def chunk_online(kc, vtc):
        for j in range(group):
            qt = qt_ref[j * HEAD_DIM:(j + 1) * HEAD_DIM, :]
            s = jnp.dot(kc, qt, preferred_element_type=F32)
            m = m_s[j]
            m_new = jnp.maximum(m, jnp.max(s, axis=0, keepdims=True))
            p = jnp.exp2(s - m_new).astype(BF16)
            acc_s[j] = jnp.exp2(m - m_new) * acc_s[j] + jnp.dot(vtc, p, preferred_element_type=F32)
            m_s[j] = m_new

    def sweep(chunk):
        @pl.when(pl.program_id(1) * tq < n_lat)
        def _():
            def body(c, carry):
                r0 = pl.multiple_of(c * tk, tk)
                chunk(k_ref[pl.ds(r0, tk), :], vt_ref[:, pl.ds(r0, tk)])
                return carry

            lax.fori_loop(0, n_lat // tk, body, 0)

        chunk(k_ref[n_lat:n_lat + n_ctx, :], vt_ref[:, n_lat:n_lat + n_ctx])

    @pl.when(bounded_ref[0] != 0)
    def _():
        sweep(chunk_bounded)

    @pl.when(bounded_ref[0] == 0)
    def _():
        m_s[...] = jnp.full(m_s.shape, NEG_INF, F32)
        sweep(chunk_online)

    for j in range(group):
        sl = slice(j * HEAD_DIM, (j + 1) * HEAD_DIM)
        acc = acc_s[j]
        o = (acc[:HEAD_DIM] / acc[HEAD_DIM:HEAD_DIM + 1]).T
        o_ref[:, sl] = (o * _silu(gate_ref[:, sl].astype(F32))).astype(o_ref.dtype)


SCORE_BOUND_LOG2 = 60.0


def _scores_bounded(q_gain, k_gain, extra=0.0):
    bound = HEAD_DIM * jnp.max(jnp.abs(q_gain)) * jnp.max(jnp.abs(k_gain)) * 1.02 + extra
    return (bound <= SCORE_BOUND_LOG2).astype(jnp.int32).reshape(1)


def _gqa_attention(bounded, qt, k, vt, u, n_q_heads, n_kv_heads, off_gate, n_lat):
    t = u.shape[0]
    group = n_q_heads // n_kv_heads
    gw = group * HEAD_DIM
    tq = 256
    tk = _largest_tile(n_lat, 1024, 256)
    vrows = HEAD_DIM + GQA_ONES_ROWS
    assert t % tq == 0 and n_lat % tq == 0 and off_gate % gw == 0
    g_blk0 = off_gate // gw
    grid_spec = pltpu.PrefetchScalarGridSpec(
        num_scalar_prefetch=1,
        grid=(n_kv_heads, t // tq),
        in_specs=[pl.BlockSpec((gw, tq), lambda g, i, b: (g, i)),
                  pl.BlockSpec((t, HEAD_DIM), lambda g, i, b: (0, g)),
                  pl.BlockSpec((vrows, t), lambda g, i, b: (g, 0)),
                  pl.BlockSpec((tq, gw), lambda g, i, b: (i, g_blk0 + g))],
        out_specs=pl.BlockSpec((tq, gw), lambda g, i, b: (i, g)),
        scratch_shapes=[pltpu.VMEM((group, 1, tq), F32),
                        pltpu.VMEM((group, vrows, tq), F32)])
    return pl.pallas_call(
        functools.partial(_gqa_kernel, group=group, tq=tq, tk=tk, n_lat=n_lat, n_ctx=t - n_lat),
        grid_spec=grid_spec,
        out_shape=jax.ShapeDtypeStruct((t, n_q_heads * HEAD_DIM), BF16),
        compiler_params=_params(("arbitrary", "arbitrary"), 48),
        name="gqa_attention",
    )(bounded, qt, k, vt, u)


def _na_kernel(bounded_ref, qt_ref, k0_ref, k1_ref, k2_ref, kc_ref, v0_ref, v1_ref, v2_ref, vc_ref,
               bias_ref, gate_ref, o_ref, *, n_lat_blocks):
    is_latent = pl.program_id(1) < n_lat_blocks
    is_bounded = bounded_ref[0] != 0
    vrows = HEAD_DIM + GQA_ONES_ROWS

    def probabilities(j, latent, bounded):
        hs = slice(j * HEAD_DIM, (j + 1) * HEAD_DIM)
        qt = qt_ref[hs, :]
        scores = [jnp.dot(kc_ref[:, hs], qt, preferred_element_type=F32)]
        if latent:
            k = jnp.concatenate([k0_ref[:, hs], k1_ref[:, hs], k2_ref[:, hs]], axis=0)
            scores.append(jnp.dot(k, qt, preferred_element_type=F32) + bias_ref[j].astype(F32))
        if not bounded:
            m = functools.reduce(jnp.maximum, [jnp.max(s, axis=0, keepdims=True) for s in scores])
            scores = [s - m for s in scores]
        return [jnp.exp2(s).astype(BF16) for s in scores]

    def output(j, probs, latent):
        hs = slice(j * HEAD_DIM, (j + 1) * HEAD_DIM)
        vs = slice(j * vrows, (j + 1) * vrows)
        acc = jnp.dot(vc_ref[vs, :], probs[0], preferred_element_type=F32)
        if latent:
            vt = jnp.concatenate([v0_ref[vs, :], v1_ref[vs, :], v2_ref[vs, :]], axis=1)
            acc = acc + jnp.dot(vt, probs[1], preferred_element_type=F32)
        o = (acc[:HEAD_DIM] / acc[HEAD_DIM:HEAD_DIM + 1]).T
        o_ref[:, hs] = (o * _silu(gate_ref[:, hs].astype(F32))).astype(o_ref.dtype)

    heads = range(qt_ref.shape[0] // HEAD_DIM)
    for latent in (True, False):
        for bounded in (True, False):
            @pl.when(jnp.logical_and(is_latent == latent, is_bounded == bounded))
            def _():
                probs = [probabilities(j, latent, bounded) for j in heads]
                for j in heads:
                    output(j, probs[j], latent)


def _na_bias_table(rel_bias, n_rows):
    c = np.arange(GRID_W)[None, :]
    kj = np.arange(GRID_W)[:, None]
    cs = np.clip(c - NA_KW // 2, 0, GRID_W - NA_KW)
    col_ok = (kj >= cs) & (kj < cs + NA_KW)
    sel_c = (np.arange(2 * NA_KW - 1)[:, None, None] == (kj - c + NA_KW - 1)[None]) & col_ok[None]
    cols = jnp.einsum('hde,ekc->hdkc', rel_bias * math.log2(math.e), jnp.asarray(sel_c, F32),
                      precision=lax.Precision.HIGHEST)
    cols = jnp.where(jnp.asarray(col_ok)[None, None], cols, NEG_INF).astype(BF16)
    neg = jnp.full((rel_bias.shape[0], GRID_W, GRID_W), NEG_INF, BF16)
    tables = []
    for r0, ws in ((0, 0), (NA_Q_ROWS, 0), (n_rows - NA_Q_ROWS, n_rows - NA_WIN_ROWS)):
        key_rows = []
        for i in range(NA_WIN_ROWS):
            blocks = []
            for a in range(NA_Q_ROWS):
                r, kr = r0 + a, ws + i
                rs = min(max(r - NA_KH // 2, 0), n_rows - NA_KH)
                blocks.append(cols[:, kr - r + NA_KH - 1] if rs <= kr < rs + NA_KH else neg)
            key_rows.append(jnp.concatenate(blocks, axis=-1))
        tables.append(jnp.concatenate(key_rows, axis=-2))
    return jnp.stack(tables, axis=0)


def _na_attention(bounded, qt, kn, vt, u, n_heads, off_gate, bias_tbl, n_lat):
    t = u.shape[0]
    tb = NA_Q_ROWS * GRID_W
    n_lat_blocks = n_lat // tb
    ctx_blk = t // tb - 1
    hp = NA_HEADS_PER_STEP
    hw = hp * HEAD_DIM
    vrows = hp * (HEAD_DIM + GQA_ONES_ROWS)
    assert t - n_lat == tb and n_lat_blocks >= 3 and off_gate % hw == 0 and n_heads % hp == 0
    g_blk0 = off_gate // hw

    def win(i, j):
        return jnp.clip(i - 1, 0, n_lat_blocks - 3) + j

    def k_spec(j):
        return pl.BlockSpec((tb, hw), lambda h, i, b: (win(i, j), h))

    def v_spec(j):
        return pl.BlockSpec((vrows, tb), lambda h, i, b: (h, win(i, j)))

    def bias_cls(h, i, b):
        return (jnp.where(i == 0, 0, jnp.where(i >= n_lat_blocks - 1, 2, 1)), h, 0, 0)

    grid_spec = pltpu.PrefetchScalarGridSpec(
        num_scalar_prefetch=1,
        grid=(n_heads // hp, t // tb),
        in_specs=[pl.BlockSpec((hw, tb), lambda h, i, b: (h, i)),
                  k_spec(0), k_spec(1), k_spec(2),
                  pl.BlockSpec((tb, hw), lambda h, i, b: (ctx_blk, h)),
                  v_spec(0), v_spec(1), v_spec(2),
                  pl.BlockSpec((vrows, tb), lambda h, i, b: (h, ctx_blk)),
                  pl.BlockSpec((None, hp, NA_WIN_ROWS * GRID_W, tb), bias_cls),
                  pl.BlockSpec((tb, hw), lambda h, i, b: (i, g_blk0 + h))],
        out_specs=pl.BlockSpec((tb, hw), lambda h, i, b: (i, h)))
    return pl.pallas_call(
        functools.partial(_na_kernel, n_lat_blocks=n_lat_blocks),
        grid_spec=grid_spec,
        out_shape=jax.ShapeDtypeStruct((t, n_heads * HEAD_DIM), BF16),
        compiler_params=_params(("arbitrary", "arbitrary"), 32),
        name="na_attention",
    )(bounded, qt, kn, kn, kn, kn, vt, vt, vt, vt, bias_tbl, u)


def _short_conv_kernel(x_ref, prev_ref, next_ref, w_ref, o_ref, *, tr):
    i = pl.program_id(0)
    n = pl.num_programs(0)
    x = x_ref[...].astype(F32)
    halo = prev_ref.shape[0]
    prev_row = jnp.where(i == 0, 0.0, prev_ref[halo - 1:halo, :].astype(F32))
    next_row = jnp.where(i == n - 1, 0.0, next_ref[0:1, :].astype(F32))
    row = lax.broadcasted_iota(jnp.int32, x.shape, 0)
    up = jnp.where(row == 0, prev_row, pltpu.roll(x, 1, 0))
    down = jnp.where(row == tr - 1, next_row, pltpu.roll(x, tr - 1, 0))
    o_ref[...] = (up * w_ref[0:1, :] + x * w_ref[1:2, :] + down * w_ref[2:3, :]).astype(o_ref.dtype)


def _short_conv(u, width, conv_w, n_lat):
    tr = _largest_tile(n_lat, 512, 256)
    tc = _largest_tile(width, 1024, 256)
    halo = 16
    hb = tr // halo
    n_tiles = n_lat // tr
    wpad = jnp.zeros((8, width), F32).at[:HY_SHORT].set(conv_w.astype(F32))
    return pl.pallas_call(
        functools.partial(_short_conv_kernel, tr=tr),
        grid=(n_tiles, width // tc),
        in_specs=[pl.BlockSpec((tr, tc), lambda i, j: (i, j)),
                  pl.BlockSpec((halo, tc), lambda i, j: (jnp.maximum(i * hb - 1, 0), j)),
                  pl.BlockSpec((halo, tc), lambda i, j: (jnp.minimum((i + 1) * hb, n_tiles * hb - 1), j)),
                  pl.BlockSpec((8, tc), lambda i, j: (0, j))],
        out_specs=pl.BlockSpec((tr, tc), lambda i, j: (i, j)),
        out_shape=jax.ShapeDtypeStruct((n_lat, width), BF16),
        compiler_params=_params(("arbitrary", "arbitrary"), 32),
        name="short_conv",
    )(u, u, u, wpad)


def _hy_gate1_kernel(v_ref, x1_ref, c_ref, d_ref, o_ref):
    v = v_ref[...].astype(F32)
    o_ref[...] = (x1_ref[...].astype(F32) * (c_ref[...].astype(F32) + v * d_ref[0:1, :])).astype(o_ref.dtype)


def _hy_gate1(vx, c1, dskip8, hy):
    n_lat = vx.shape[0]
    tr = _largest_tile(n_lat, 1024, 256)
    tc = _largest_tile(hy, 512, 256)
    nb = hy // tc
    return pl.pallas_call(
        _hy_gate1_kernel,
        grid=(n_lat // tr, nb),
        in_specs=[pl.BlockSpec((tr, tc), lambda i, j: (i, j)),
                  pl.BlockSpec((tr, tc), lambda i, j: (i, nb + j)),
                  pl.BlockSpec((tr, tc), lambda i, j: (i, j)),
                  pl.BlockSpec((8, tc), lambda i, j: (0, j))],
        out_specs=pl.BlockSpec((tr, tc), lambda i, j: (i, j)),
        out_shape=jax.ShapeDtypeStruct((n_lat, hy), BF16),
        compiler_params=_params(("arbitrary", "arbitrary"), 32),
        name="hy_gate1",
    )(vx, vx, c1, dskip8)


def _hy_gate2_kernel(x2_ref, c_ref, y1_ref, d_ref, g_ref, ctx_ref, o_ref, *, n_lat_tiles):
    i = pl.program_id(0)

    @pl.when(i < n_lat_tiles)
    def _():
        y1 = y1_ref[...].astype(F32)
        z = x2_ref[...].astype(F32) * (c_ref[...].astype(F32) + y1 * d_ref[1:2, :])
        o_ref[...] = (z * _silu(g_ref[...].astype(F32))).astype(o_ref.dtype)

    @pl.when(i >= n_lat_tiles)
    def _():
        o_ref[...] = ctx_ref[...]


def _hy_gate2(vx, c2, y1, dskip8, u, off_gate, hy_ctx, hy):
    t = u.shape[0]
    n_lat = vx.shape[0]
    tr = t - n_lat
    tc = _largest_tile(hy, 512, 256)
    nb = hy // tc
    n_lat_tiles = n_lat // tr
    assert n_lat % tr == 0 and off_gate % tc == 0
    g_blk0 = off_gate // tc
    lat = lambda i: jnp.minimum(i, n_lat_tiles - 1)
    return pl.pallas_call(
        functools.partial(_hy_gate2_kernel, n_lat_tiles=n_lat_tiles),
        grid=(t // tr, nb),
        in_specs=[pl.BlockSpec((tr, tc), lambda i, j: (lat(i), 2 * nb + j)),
                  pl.BlockSpec((tr, tc), lambda i, j: (lat(i), j)),
                  pl.BlockSpec((tr, tc), lambda i, j: (lat(i), j)),
                  pl.BlockSpec((8, tc), lambda i, j: (0, j)),
                  pl.BlockSpec((tr, tc), lambda i, j: (i, g_blk0 + j)),
                  pl.BlockSpec((tr, tc), lambda i, j: (0, j))],
        out_specs=pl.BlockSpec((tr, tc), lambda i, j: (i, j)),
        out_shape=jax.ShapeDtypeStruct((t, hy), BF16),
        compiler_params=_params(("arbitrary", "arbitrary"), 32),
        name="hy_gate2",
    )(vx, c2, y1, dskip8, u, hy_ctx)


def _filter_hidden_kernel(z_ref, w1_ref, b1_ref, w2_ref, b2_ref, fr_ref, o_ref):
    hp = lax.Precision.HIGHEST
    fr = fr_ref[...]
    h = jnp.sin(fr * (jnp.dot(z_ref[...], w1_ref[...], precision=hp,
                              preferred_element_type=F32) + b1_ref[...]))
    o_ref[...] = jnp.sin(fr * (jnp.dot(h, w2_ref[...], precision=hp,
                                       preferred_element_type=F32) + b2_ref[...]))


def _pad2(a, rows, cols):
    return jnp.zeros((rows, cols), F32).at[:a.shape[0], :a.shape[1]].set(a.astype(F32))


def _filter_hidden(feat, w1, b1, w2, b2, freq):
    r = feat.shape[0]
    tr = _largest_tile(r, 1024, 256)
    full = lambda i: (0, 0)
    return pl.pallas_call(
        _filter_hidden_kernel,
        grid=(r // tr,),
        in_specs=[pl.BlockSpec((tr, LANE), lambda i: (i, 0)),
                  pl.BlockSpec((LANE, LANE), full), pl.BlockSpec((1, LANE), full),
                  pl.BlockSpec((LANE, LANE), full), pl.BlockSpec((1, LANE), full),
                  pl.BlockSpec((1, LANE), full)],
        out_specs=pl.BlockSpec((tr, LANE), lambda i: (i, 0)),
        out_shape=jax.ShapeDtypeStruct((r, LANE), F32),
        compiler_params=_params(("arbitrary",), 32),
        name="filter_hidden",
    )(feat, _pad2(w1, LANE, LANE), _pad2(b1[None], 1, LANE), _pad2(w2, LANE, LANE),
      _pad2(b2[None], 1, LANE), _pad2(freq[None], 1, LANE))


def _two_sided_positions(n):
    idx = jnp.arange(2 * n)
    return jnp.where(idx < n, idx, 2 * n - idx) % n


def _filter_features(n, n_bands):
    pos = _two_sided_positions(n).astype(F32)[:, None]
    t = pos / (n - 1)
    w = (2.0 * math.pi / n) * pos
    bands = jnp.linspace(1e-4, n_bands - 1, n_bands, dtype=F32)[None, :]
    z = jnp.concatenate([t, jnp.cos(bands * w), -jnp.sin(bands * w)], axis=-1)
    return _pad2(z, 2 * n, LANE), t


def _filter_gen_kernel(h_ref, w_ref, t_ref, dec_ref, o_ref, *, tr, n):
    i = pl.program_id(0)
    f = jnp.dot(h_ref[...].astype(BF16), w_ref[...], preferred_element_type=F32)
    f = f * jnp.exp(-t_ref[...] * jnp.abs(dec_ref[0:1, :]))
    row = i * tr + lax.broadcasted_iota(jnp.int32, (tr, 1), 0)
    o_ref[...] = jnp.where(row == n, 0.0, f).astype(o_ref.dtype)


def _filter_gen(hidden, row0, w3p, tcol, decay8, n, order, hy):
    tr = _largest_tile(n, 512, 256)
    tc = _largest_tile(hy, 1024, 256)
    nb = hy // tc
    n_half = n // tr
    hb0 = row0 // tr
    assert row0 % tr == 0
    return pl.pallas_call(
        functools.partial(_filter_gen_kernel, tr=tr, n=n),
        grid=(2 * n // tr, order * nb),
        in_specs=[pl.BlockSpec((tr, LANE), lambda i, j: (hb0 + i, 0)),
                  pl.BlockSpec((LANE, tc), lambda i, j: (0, (j // nb) * 2 * nb + (i // n_half) * nb + j % nb)),
                  pl.BlockSpec((tr, 1), lambda i, j: (i, 0)),
                  pl.BlockSpec((None, 8, tc), lambda i, j: (j // nb, 0, j % nb))],
        out_specs=pl.BlockSpec((tr, tc), lambda i, j: (i, j)),
        out_shape=jax.ShapeDtypeStruct((2 * n, order * hy), BF16),
        compiler_params=_params(("arbitrary", "arbitrary"), 32),
        name="filter_gen",
    )(hidden, w3p, tcol, decay8)


def _dft_tables(n1, nz):
    n2 = FFT_N2
    n = n1 * n2
    half = n1 // 2
    k1 = jnp.arange(half + 1)
    two_pi = 2.0 * math.pi

    def cs(idx, period):
        ang = (idx % period).astype(F32) * (two_pi / period)
        return jnp.cos(ang), -jnp.sin(ang)

    eye8 = jnp.eye(8, dtype=F32)
    fr, fi = cs(k1[:, None] * jnp.arange(nz)[None, :], n1)
    kf = jnp.stack([fr, fi], axis=1)
    kf = jnp.einsum('kpn,jl->kpjnl', kf, eye8).reshape((half + 1) * 16, nz * 8)
    gr, gi = cs(jnp.arange(half)[:, None] * k1[None, :], n1)
    weight = jnp.where((k1 == 0) | (k1 == half), 1.0, 2.0) / n
    ki = jnp.stack([gr, gi], axis=2) * weight[None, :, None]
    ki = jnp.einsum('nkp,jl->njkpl', ki, eye8).reshape(half * 8, (half + 1) * 16)
    r = jnp.arange(2 * n2)
    part = (r % 16) // 8
    n2_of_r = 8 * (r // 16) + r % 8
    k2 = jnp.arange(n2)
    idx = k2[None, :, None] * n2_of_r[None, None, :] * n1 + k1[:, None, None] * n2_of_r[None, None, :]
    cr, ci = cs(idx, n)
    top = jnp.where(part[None, None, :] == 0, cr, -ci)
    bot = jnp.where(part[None, None, :] == 0, ci, cr)
    m = jnp.concatenate([top, bot], axis=1)
    return kf.astype(BF16), ki.astype(BF16), m.astype(BF16)


def _cross_slab_forward(z_ref, kf_ref, a_scr, nz, nh, ct):
    def body(b, carry):
        z = z_ref[:, pl.ds(pl.multiple_of(16 * b, 16), 16), :].astype(F32)
        for half in range(2):
            g = z[:, 8 * half:8 * half + 8, :].reshape(nz * 8, ct).astype(BF16)
            p = jnp.dot(kf_ref[...], g, preferred_element_type=F32)
            start = pl.multiple_of(32 * b + 16 * half, 16)
            a_scr[:, pl.ds(start, 16), :] = p.astype(BF16).reshape(nh, 16, ct)
        return carry

    lax.fori_loop(0, FFT_N2 // 16, body, 0)


def _slab_block(nh):
    return max(k for k in range(1, 17) if nh % k == 0)


def _fft_fwd_kernel(z_ref, kf_ref, m_ref, o_ref, a_scr, *, nz, nh, kb, ct):
    kblk = pl.program_id(1)

    @pl.when(kblk == 0)
    def _():
        _cross_slab_forward(z_ref, kf_ref, a_scr, nz, nh, ct)

    for k in range(kb):
        x = jnp.dot(m_ref[k], a_scr[kblk * kb + k], preferred_element_type=F32)
        o_ref[k] = x.astype(o_ref.dtype)


def _fft_forward(z3, kf, m, ct=256):
    nz, _, c = z3.shape
    nh = m.shape[0]
    kb = _slab_block(nh)
    return pl.pallas_call(
        functools.partial(_fft_fwd_kernel, nz=nz, nh=nh, kb=kb, ct=ct),
        grid=(c // ct, nh // kb),
        in_specs=[pl.BlockSpec((nz, FFT_N2, ct), lambda j, k: (0, 0, j)),
                  pl.BlockSpec(kf.shape, lambda j, k: (0, 0)),
                  pl.BlockSpec((kb, 2 * FFT_N2, 2 * FFT_N2), lambda j, k: (k, 0, 0))],
        out_specs=pl.BlockSpec((kb, 2 * FFT_N2, ct), lambda j, k: (k, 0, j)),
        out_shape=jax.ShapeDtypeStruct((nh, 2 * FFT_N2, c), BF16),
        scratch_shapes=[pltpu.VMEM((nh, 2 * FFT_N2, ct), BF16)],
        compiler_params=_params(("arbitrary", "arbitrary"), 56),
        name="fft_forward",
    )(z3, kf, m)


def _fft_conv_kernel(z_ref, kf_ref, ki_ref, m_ref, h_ref, o_ref, a_scr, *, nz, nh, kb, ct):
    kblk = pl.program_id(1)
    n2 = FFT_N2

    @pl.when(kblk == 0)
    def _():
        _cross_slab_forward(z_ref, kf_ref, a_scr, nz, nh, ct)

    for k in range(kb):
        mk = m_ref[k]
        x = jnp.dot(mk, a_scr[kblk * kb + k], preferred_element_type=F32)
        h = h_ref[k].astype(F32)
        xr, xi, hr, hi = x[:n2], x[n2:], h[:n2], h[n2:]
        y = jnp.concatenate([xr * hr - xi * hi, xr * hi + xi * hr], axis=0).astype(BF16)
        b = lax.dot_general(mk, y, (((0,), (0,)), ((), ())), preferred_element_type=F32)
        a_scr[kblk * kb + k] = b.astype(BF16)

    @pl.when(kblk == pl.num_programs(1) - 1)
    def _():
        def body(b, carry):
            halves = []
            for half in range(2):
                start = pl.multiple_of(32 * b + 16 * half, 16)
                r = a_scr[:, pl.ds(start, 16), :].reshape(nh * 16, ct)
                p = jnp.dot(ki_ref[...], r, preferred_element_type=F32)
                halves.append(p.reshape(nz, 8, ct))
            out = jnp.concatenate(halves, axis=1)
            o_ref[:, pl.ds(pl.multiple_of(16 * b, 16), 16), :] = out.astype(o_ref.dtype)
            return carry

        lax.fori_loop(0, n2 // 16, body, 0)


def _fft_conv(z3, z_blk0, spec, spec_blk0, kf, ki, m, c, ct=256):
    nz = z3.shape[0]
    nh = m.shape[0]
    kb = _slab_block(nh)
    assert nh == nz + 1
    return pl.pallas_call(
        functools.partial(_fft_conv_kernel, nz=nz, nh=nh, kb=kb, ct=ct),
        grid=(c // ct, nh // kb),
        in_specs=[pl.BlockSpec((nz, FFT_N2, ct), lambda j, k: (0, 0, z_blk0 + j)),
                  pl.BlockSpec(kf.shape, lambda j, k: (0, 0)),
                  pl.BlockSpec(ki.shape, lambda j, k: (0, 0)),
                  pl.BlockSpec((kb, 2 * FFT_N2, 2 * FFT_N2), lambda j, k: (k, 0, 0)),
                  pl.BlockSpec((kb, 2 * FFT_N2, ct), lambda j, k: (k, 0, spec_blk0 + j))],
        out_specs=pl.BlockSpec((nz, FFT_N2, ct), lambda j, k: (0, 0, j)),
        out_shape=jax.ShapeDtypeStruct((nz, FFT_N2, c), BF16),
        scratch_shapes=[pltpu.VMEM((nh, 2 * FFT_N2, ct), BF16)],
        compiler_params=_params(("arbitrary", "arbitrary"), 56),
        name="fft_conv",
    )(z3, kf, ki, m, spec)


def _ctx_dft_tables(n):
    nn = 2 * n
    k = jnp.arange(nn)
    ang = ((k[:, None] * k[None, :]) % nn).astype(F32) * (2.0 * math.pi / nn)
    cr, ci = jnp.cos(ang), -jnp.sin(ang)
    full = jnp.concatenate([cr, ci], axis=0)
    inv = jnp.concatenate([cr[:n], ci[:n]], axis=1) / nn
    return full[:, :n].astype(BF16), full.astype(BF16), inv.astype(BF16)


def _ctx_hyena_kernel(v_ref, x1_ref, x2_ref, g_ref, cw0_ref, cw1_ref, cw2_ref, hid_ref,
                      w10_ref, w11_ref, w20_ref, w21_ref, t_ref, dec_ref, dsk_ref,
                      ff_ref, ffull_ref, finv_ref, o_ref, *, n):
    row = lax.broadcasted_iota(jnp.int32, (n, 1), 0)

    def sconv(x_ref, w_ref):
        x = x_ref[...].astype(F32)
        up = jnp.where(row == 0, 0.0, pltpu.roll(x, 1, 0))
        down = jnp.where(row == n - 1, 0.0, pltpu.roll(x, n - 1, 0))
        y = up * w_ref[0:1, :] + x * w_ref[1:2, :] + down * w_ref[2:3, :]
        return y.astype(BF16).astype(F32)

    hid = hid_ref[...].astype(BF16)
    tcol = t_ref[...]
    row2 = lax.broadcasted_iota(jnp.int32, (2 * n, 1), 0)

    def spectrum(wf_ref, wb_ref, o):
        fwd = jnp.dot(hid[:n], wf_ref[...], preferred_element_type=F32)
        bwd = jnp.dot(hid[n:], wb_ref[...], preferred_element_type=F32)
        f = jnp.concatenate([fwd, bwd], axis=0) * jnp.exp(-tcol * jnp.abs(dec_ref[o:o + 1, :]))
        f = jnp.where(row2 == n, 0.0, f).astype(BF16)
        return jnp.dot(ffull_ref[...], f, preferred_element_type=F32)

    def conv(z, h):
        x = jnp.dot(ff_ref[...], z.astype(BF16), preferred_element_type=F32)
        h = h.astype(BF16).astype(F32)
        nn = 2 * n
        xr, xi, hr, hi = x[:nn], x[nn:], h[:nn], h[nn:]
        y = jnp.concatenate([xr * hr - xi * hi, xr * hi + xi * hr], axis=0).astype(BF16)
        return jnp.dot(finv_ref[...], y, preferred_element_type=F32)

    v = sconv(v_ref, cw0_ref)
    x1 = sconv(x1_ref, cw1_ref)
    x2 = sconv(x2_ref, cw2_ref)
    c1 = conv(v, spectrum(w10_ref, w11_ref, 0)).astype(BF16).astype(F32)
    y1 = (x1 * (c1 + v * dsk_ref[0:1, :])).astype(BF16).astype(F32)
    c2 = conv(y1, spectrum(w20_ref, w21_ref, 1)).astype(BF16).astype(F32)
    z = x2 * (c2 + y1 * dsk_ref[1:2, :])
    o_ref[...] = (z * _silu(g_ref[...].astype(F32))).astype(o_ref.dtype)


def _ctx_hyena(u, n_lat, hy, order, off_gate, conv8, hidden, hid_row0, w3p, tcol, decay8, dskip8, tables):
    t = u.shape[0]
    n = t - n_lat
    assert order == 2 and n_lat % n == 0 and hid_row0 % (2 * n) == 0
    ct = 256
    nb = hy // ct
    rb = n_lat // n
    ff, ffull, finv = tables
    const = lambda j: (0, 0)
    ublk = lambda c0: pl.BlockSpec((n, ct), lambda j: (rb, c0 + j))
    wblk = lambda c0: pl.BlockSpec((LANE, ct), lambda j: (0, c0 + j))
    cwblk = lambda c0: pl.BlockSpec((8, ct), lambda j: (0, c0 + j))
    return pl.pallas_call(
        functools.partial(_ctx_hyena_kernel, n=n),
        grid=(nb,),
        in_specs=[ublk(0), ublk(nb), ublk(2 * nb), ublk(off_gate // ct),
                  cwblk(0), cwblk(nb), cwblk(2 * nb),
                  pl.BlockSpec((2 * n, LANE), lambda j: (hid_row0 // (2 * n), 0)),
                  wblk(0), wblk(nb), wblk(2 * nb), wblk(3 * nb),
                  pl.BlockSpec((2 * n, 1), const),
                  pl.BlockSpec((8, ct), lambda j: (0, j)),
                  pl.BlockSpec((8, ct), lambda j: (0, j)),
                  pl.BlockSpec(ff.shape, const), pl.BlockSpec(ffull.shape, const),
                  pl.BlockSpec(finv.shape, const)],
        out_specs=pl.BlockSpec((n, ct), lambda j: (0, j)),
        out_shape=jax.ShapeDtypeStruct((n, hy), BF16),
        compiler_params=_params(("arbitrary",), 40),
        name="ctx_hyena",
    )(u, u, u, u, conv8, conv8, conv8, hidden, w3p, w3p, w3p, w3p, tcol, decay8, dskip8,
      ff, ffull, finv)


def _even_layer(x_all, mods_l, g, n_lat, w_in, w_out, conv_w, w1, b1, w2, b2, w3, freq, decay, dskip,
                q_norm, k_norm, rope, dft, ctx_dft, last):
    t, d = x_all.shape
    n_ctx = t - n_lat
    order, hy = decay.shape
    gqa_w = d - hy
    n_q = gqa_w // HEAD_DIM
    n_kv = n_q // 4
    kv_w = n_kv * HEAD_DIM
    off_hy_gate = (order + 1) * hy
    off_q = off_hy_gate + hy
    off_v = off_q + gqa_w + kv_w
    off_at_gate = off_v + kv_w
    scale = HEAD_DIM ** -0.5

    h = _norm_mod(x_all, g, mods_l, n_lat)
    u = _in_proj(h, *w_in)

    q_gain = q_norm.astype(F32) * (scale * math.log2(math.e))
    k_gain = k_norm.astype(F32)
    qt, kn, vt = _attn_prep(u, off_q, off_q + gqa_w, off_v, n_q, n_kv, 1, q_gain, k_gain,
                            rope[0], rope[1], True)
    att = _gqa_attention(_scores_bounded(q_gain, k_gain), qt, kn, vt, u, n_q, n_kv, off_at_gate, n_lat)

    n_bands = (w1.shape[0] - 1) // 2
    feat_l, t_l = _filter_features(n_lat, n_bands)
    feat_c, t_c = _filter_features(n_ctx, n_bands)
    hidden = _filter_hidden(jnp.concatenate([feat_l, feat_c], axis=0), w1, b1, w2, b2, freq)
    w3p = jnp.zeros((LANE, w3.shape[1]), BF16).at[:w3.shape[0]].set(w3.astype(BF16))
    decay8 = jnp.zeros((order, 8, hy), F32).at[:, 0].set(decay.astype(F32))
    dskip8 = jnp.zeros((8, hy), F32).at[:order].set(dskip.astype(F32))
    conv8 = jnp.zeros((8, conv_w.shape[1]), F32).at[:HY_SHORT].set(conv_w.astype(F32))
    kf_pad, kf_full, ki, m = dft
    n1 = 2 * n_lat // FFT_N2
    filt = _filter_gen(hidden, 0, w3p, t_l, decay8, n_lat, order, hy)
    spec = _fft_forward(filt.reshape(n1, FFT_N2, order * hy), kf_full, m)

    vx = _short_conv(u, (order + 1) * hy, conv_w, n_lat)
    vx3 = vx.reshape(n1 // 2, FFT_N2, (order + 1) * hy)
    ct = 256
    c1 = _fft_conv(vx3, 0, spec, 0, kf_pad, ki, m, hy).reshape(n_lat, hy)
    y1 = _hy_gate1(vx, c1, dskip8, hy)
    c2 = _fft_conv(y1.reshape(n1 // 2, FFT_N2, hy), 0, spec, hy // ct, kf_pad, ki, m, hy).reshape(n_lat, hy)

    decay_c8 = jnp.zeros((8, hy), F32).at[:order].set(decay.astype(F32))
    hy_ctx = _ctx_hyena(u, n_lat, hy, order, off_hy_gate, conv8, hidden, 2 * n_lat, w3p, t_c,
                        decay_c8, dskip8, ctx_dft)
    hy_mix = _hy_gate2(vx, c2, y1, dskip8, u, off_hy_gate, hy_ctx, hy)

    return _out_proj(hy_mix, 0, att, 0, *w_out, x_all, mods_l, n_lat, last)


def _odd_layer(x_all, mods_l, g, n_lat, w_in, w_out, q_norm, k_norm, rel_bias, last):
    t, d = x_all.shape
    n_heads = rel_bias.shape[0]
    w = n_heads * HEAD_DIM
    scale = HEAD_DIM ** -0.5
    h = _norm_mod(x_all, g, mods_l, n_lat)
    u = _in_proj(h, *w_in)
    log2e = math.log2(math.e)
    q_gain = q_norm.astype(F32) * (scale * log2e)
    k_gain = k_norm.astype(F32)
    heads_per_step = NA_HEADS_PER_STEP
    unused = jnp.zeros((t, HEAD_DIM), F32)
    qt, kn, vt = _attn_prep(u, 0, w, 2 * w, heads_per_step, heads_per_step, n_heads // heads_per_step,
                            q_gain, k_gain, unused, unused, False)
    bias_tbl = _na_bias_table(rel_bias.astype(F32), n_lat // GRID_W)
    bounded = _scores_bounded(q_gain, k_gain, jnp.max(jnp.abs(rel_bias)).astype(F32) * (log2e * 1.02))
    att = _na_attention(bounded, qt, kn, vt, u, n_heads, 3 * w, bias_tbl, n_lat)
    return _out_proj(att, 0, att, 1, *w_out, x_all, mods_l, n_lat, last)


def kernel(x, c, ctx, c_ctx, norm_g, ada_w, ada_b, e_w_in, e_w_out, hy_conv, hy_w1, hy_b1, hy_w2, hy_b2,
           hy_w3, hy_freq, hy_decay, hy_dskip, gqa_q_norm, gqa_k_norm, o_w_in, o_w_out, na_q_norm,
           na_k_norm, na_rel_bias):
    batch, n_lat, d = x.shape
    n_ctx = ctx.shape[1]
    depth = norm_g.shape[0]
    assert batch == 1 and n_lat % (2 * FFT_N2) == 0 and n_ctx == NA_Q_ROWS * GRID_W

    x_all = jnp.concatenate([x[0], ctx[0]], axis=0).astype(F32)
    cvecs = jnp.zeros((8, d), F32).at[0].set(c[0]).at[1].set(c_ctx)
    mods = _ada_mods(cvecs, ada_w, ada_b)

    rope = _rope_tables(n_lat, n_ctx)
    n1 = 2 * n_lat // FFT_N2
    kf_pad, ki, m = _dft_tables(n1, n1 // 2)
    kf_full, _, _ = _dft_tables(n1, n1)
    dft = (kf_pad, kf_full, ki, m)
    ctx_dft = _ctx_dft_tables(n_ctx)

    for layer in range(depth):
        i = layer // 2
        last = layer == depth - 1
        if layer % 2 == 0:
            x_all = _even_layer(x_all, mods[layer], norm_g[layer], n_lat,
                                (e_w_in, i), (e_w_out, i), hy_conv[i],
                                hy_w1[i], hy_b1[i], hy_w2[i], hy_b2[i], hy_w3[i], hy_freq[i],
                                hy_decay[i], hy_dskip[i], gqa_q_norm[i], gqa_k_norm[i],
                                rope, dft, ctx_dft, last)
        else:
            x_all = _odd_layer(x_all, mods[layer], norm_g[layer], n_lat,
                               (o_w_in, i), (o_w_out, i),
                               na_q_norm[i], na_k_norm[i], na_rel_bias[i], last)
    return x_all[None]
```

```python
import functools
import math

import numpy as np
import jax
import jax.numpy as jnp
from jax import lax
from jax.experimental import pallas as pl
from jax.experimental.pallas import tpu as pltpu

F32 = jnp.float32
BF16 = jnp.bfloat16

HEAD_DIM = 128
GRID_W = 64
EPS = 1e-6
ROPE_THETA = 10000.0
NA_KH = 8
NA_KW = 16
NA_Q_ROWS = 4
NA_WIN_ROWS = 12
NA_HEADS_PER_STEP = 4
HY_SHORT = 3
FFT_N2 = 128
NEG_INF = -1e30

V7X_VMEM_BYTES = 64 * 1024 * 1024
LANE = 128


def _params(semantics, vmem_mb):
    assert vmem_mb * 1024 * 1024 < V7X_VMEM_BYTES
    return pltpu.CompilerParams(dimension_semantics=semantics,
                                vmem_limit_bytes=vmem_mb * 1024 * 1024)


def _largest_tile(n, target, quantum):
    best = None
    t = quantum
    while t <= min(n, target):
        if n % t == 0:
            best = t
        t += quantum
    assert best is not None, (n, target, quantum)
    return best


def _silu(x):
    return x * jax.nn.sigmoid(x)


def _mods_kernel(c_ref, w_ref, b_ref, o_ref):
    s = _silu(c_ref[...]).astype(BF16)
    o_ref[...] = jnp.dot(s, w_ref[...].astype(BF16), preferred_element_type=F32) + b_ref[...]


def _ada_mods(cvecs, ada_w, ada_b):
    depth, d, n = ada_w.shape
    tn = _largest_tile(n, 512, LANE)
    return pl.pallas_call(
        _mods_kernel,
        grid=(depth, n // tn),
        in_specs=[pl.BlockSpec((8, d), lambda l, j: (0, 0)),
                  pl.BlockSpec((None, d, tn), lambda l, j: (l, 0, j)),
                  pl.BlockSpec((None, 1, tn), lambda l, j: (l, 0, j))],
        out_specs=pl.BlockSpec((None, 8, tn), lambda l, j: (l, 0, j)),
        out_shape=jax.ShapeDtypeStruct((depth, 8, n), F32),
        compiler_params=_params(("arbitrary", "arbitrary"), 40),
        name="ada_mods",
    )(cvecs, ada_w, ada_b.reshape(depth, 1, n))


def _norm_mod_kernel(x_ref, g_ref, sh_ref, sc_ref, o_ref, *, n_lat_tiles):
    x = x_ref[...]
    y = x * lax.rsqrt(jnp.mean(x * x, axis=-1, keepdims=True) + EPS) * g_ref[...]
    is_ctx = pl.program_id(0) >= n_lat_tiles
    sh = jnp.where(is_ctx, sh_ref[1:2, :], sh_ref[0:1, :])
    sc = jnp.where(is_ctx, sc_ref[1:2, :], sc_ref[0:1, :])
    o_ref[...] = (y * (1.0 + sc) + sh).astype(o_ref.dtype)


def _norm_mod(x_all, g, mods_l, n_lat):
    t, d = x_all.shape
    tr = 256
    assert t % tr == 0 and n_lat % tr == 0
    return pl.pallas_call(
        functools.partial(_norm_mod_kernel, n_lat_tiles=n_lat // tr),
        grid=(t // tr,),
        in_specs=[pl.BlockSpec((tr, d), lambda i: (i, 0)),
                  pl.BlockSpec((1, d), lambda i: (0, 0)),
                  pl.BlockSpec((8, d), lambda i: (0, 0)),
                  pl.BlockSpec((8, d), lambda i: (0, 1))],
        out_specs=pl.BlockSpec((tr, d), lambda i: (i, 0)),
        out_shape=jax.ShapeDtypeStruct((t, d), BF16),
        compiler_params=_params(("arbitrary",), 40),
        name="norm_mod",
    )(x_all, g.reshape(1, d), mods_l, mods_l)


def _in_proj_kernel(a_ref, w_ref, o_ref, wb):
    @pl.when(pl.program_id(1) == 0)
    def _():
        wb[...] = w_ref[...].astype(BF16)

    o_ref[...] = jnp.dot(a_ref[...], wb[...], preferred_element_type=F32).astype(o_ref.dtype)


def _in_proj(h, w_all, layer_idx):
    m, k = h.shape
    n = w_all.shape[2]
    tm = _largest_tile(m, 384, 128)
    tn = _largest_tile(n, 1024, 256)
    return pl.pallas_call(
        _in_proj_kernel,
        grid=(n // tn, m // tm),
        in_specs=[pl.BlockSpec((tm, k), lambda j, i: (i, 0)),
                  pl.BlockSpec((None, k, tn), lambda j, i: (layer_idx, 0, j))],
        out_specs=pl.BlockSpec((tm, tn), lambda j, i: (i, j)),
        out_shape=jax.ShapeDtypeStruct((m, n), BF16),
        scratch_shapes=[pltpu.VMEM((k, tn), BF16)],
        compiler_params=_params(("arbitrary", "arbitrary"), 56),
        name="in_proj",
    )(h, w_all)


def _out_proj_kernel(a1_ref, a2_ref, w1_ref, w2_ref, x_ref, gate_ref, o_ref, wb1, wb2, *, tm, n_lat):
    i = pl.program_id(1)

    @pl.when(i == 0)
    def _():
        wb1[...] = w1_ref[...].astype(BF16)
        wb2[...] = w2_ref[...].astype(BF16)

    y = jnp.dot(a1_ref[...], wb1[...], preferred_element_type=F32)
    y = y + jnp.dot(a2_ref[...], wb2[...], preferred_element_type=F32)
    rows = i * tm + lax.broadcasted_iota(jnp.int32, (tm, 1), 0)
    gate = jnp.where(rows >= n_lat, gate_ref[1:2, :], gate_ref[0:1, :])
    o_ref[...] = x_ref[...] + gate * y


def _out_proj(a1, a1_blk, a2, a2_blk, w_all, layer_idx, x_all, mods_l, n_lat, latent_only):
    d = x_all.shape[1]
    t = n_lat if latent_only else x_all.shape[0]
    kh = w_all.shape[1] // 2
    tm = _largest_tile(t, 1024, 256)
    tn = _largest_tile(d, 512, 256)
    gate_blk0 = 2 * d // tn
    return pl.pallas_call(
        functools.partial(_out_proj_kernel, tm=tm, n_lat=n_lat),
        grid=(d // tn, t // tm),
        in_specs=[pl.BlockSpec((tm, kh), lambda j, i: (i, a1_blk)),
                  pl.BlockSpec((tm, kh), lambda j, i: (i, a2_blk)),
                  pl.BlockSpec((None, kh, tn), lambda j, i: (layer_idx, 0, j)),
                  pl.BlockSpec((None, kh, tn), lambda j, i: (layer_idx, 1, j)),
                  pl.BlockSpec((tm, tn), lambda j, i: (i, j)),
                  pl.BlockSpec((8, tn), lambda j, i: (0, gate_blk0 + j))],
        out_specs=pl.BlockSpec((tm, tn), lambda j, i: (i, j)),
        out_shape=jax.ShapeDtypeStruct((t, d), F32),
        scratch_shapes=[pltpu.VMEM((kh, tn), BF16), pltpu.VMEM((kh, tn), BF16)],
        compiler_params=_params(("arbitrary", "arbitrary"), 48),
        name="out_proj",
    )(a1, a2, w_all, w_all, x_all, mods_l)


def _rope_tables(n_lat, n_ctx):
    quarter = HEAD_DIM // 4
    inv = ROPE_THETA ** (-jnp.arange(quarter, dtype=F32) / quarter)
    tok = jnp.arange(n_lat)
    rows = (tok // GRID_W).astype(F32)[:, None] * inv[None]
    cols = (tok % GRID_W).astype(F32)[:, None] * inv[None]
    cos = jnp.concatenate([jnp.cos(rows), jnp.cos(rows), jnp.cos(cols), jnp.cos(cols)], axis=-1)
    sin = jnp.concatenate([-jnp.sin(rows), jnp.sin(rows), -jnp.sin(cols), jnp.sin(cols)], axis=-1)
    cos = jnp.concatenate([cos, jnp.ones((n_ctx, HEAD_DIM), F32)], axis=0)
    sin = jnp.concatenate([sin, jnp.zeros((n_ctx, HEAD_DIM), F32)], axis=0)
    return cos, sin


GQA_ONES_ROWS = 16


def _attn_prep_kernel(q_ref, k_ref, v_ref, g_ref, cos_ref, sin_ref, qt_ref, ko_ref, vt_ref,
                      *, n_q, n_kv, rope):
    if rope:
        cos = cos_ref[...]
        sin = sin_ref[...]
        lane = lax.broadcasted_iota(jnp.int32, cos.shape, 1)
        first_quarter = (lane % 64) < 32

    def norm_rope(x, g):
        y = x * lax.rsqrt(jnp.mean(x * x, axis=-1, keepdims=True) + EPS) * g
        if not rope:
            return y
        partner = jnp.where(first_quarter, pltpu.roll(y, HEAD_DIM - 32, 1), pltpu.roll(y, 32, 1))
        return y * cos + partner * sin

    for h in range(n_q):
        sl = slice(h * HEAD_DIM, (h + 1) * HEAD_DIM)
        qt_ref[sl, :] = norm_rope(q_ref[:, sl].astype(F32), g_ref[:, sl]).T.astype(qt_ref.dtype)
    for h in range(n_kv):
        sl = slice(h * HEAD_DIM, (h + 1) * HEAD_DIM)
        gk = g_ref[:, (n_q + h) * HEAD_DIM:(n_q + h + 1) * HEAD_DIM]
        ko_ref[:, sl] = norm_rope(k_ref[:, sl].astype(F32), gk).astype(ko_ref.dtype)
        r0 = h * (HEAD_DIM + GQA_ONES_ROWS)
        vt_ref[r0:r0 + HEAD_DIM, :] = v_ref[:, sl].astype(F32).T.astype(vt_ref.dtype)
        vt_ref[r0 + HEAD_DIM:r0 + HEAD_DIM + GQA_ONES_ROWS, :] = jnp.ones(
            (GQA_ONES_ROWS, vt_ref.shape[1]), vt_ref.dtype)


def _attn_prep(u, off_q, off_k, off_v, n_q, n_kv, n_groups, q_gain, k_gain, cos_t, sin_t, rope):
    t = u.shape[0]
    qw, kw = n_q * HEAD_DIM, n_kv * HEAD_DIM
    tr = _largest_tile(t, 768, 256)
    vrows = n_kv * (HEAD_DIM + GQA_ONES_ROWS)
    assert off_q % qw == 0 and off_k % kw == 0 and off_v % kw == 0 and t % tr == 0
    q_blk, k_blk, v_blk = off_q // qw, off_k // kw, off_v // kw
    gvec = jnp.concatenate([jnp.tile(q_gain, n_q), jnp.tile(k_gain, n_kv)])[None]
    return pl.pallas_call(
        functools.partial(_attn_prep_kernel, n_q=n_q, n_kv=n_kv, rope=rope),
        grid=(t // tr, n_groups),
        in_specs=[pl.BlockSpec((tr, qw), lambda i, j: (i, q_blk + j)),
                  pl.BlockSpec((tr, kw), lambda i, j: (i, k_blk + j)),
                  pl.BlockSpec((tr, kw), lambda i, j: (i, v_blk + j)),
                  pl.BlockSpec((1, qw + kw), lambda i, j: (0, 0)),
                  pl.BlockSpec((tr, HEAD_DIM), lambda i, j: (i, 0)),
                  pl.BlockSpec((tr, HEAD_DIM), lambda i, j: (i, 0))],
        out_specs=[pl.BlockSpec((qw, tr), lambda i, j: (j, i)),
                   pl.BlockSpec((tr, kw), lambda i, j: (i, j)),
                   pl.BlockSpec((vrows, tr), lambda i, j: (j, i))],
        out_shape=[jax.ShapeDtypeStruct((n_groups * qw, t), BF16),
                   jax.ShapeDtypeStruct((t, n_groups * kw), BF16),
                   jax.ShapeDtypeStruct((n_groups * vrows, t), BF16)],
        compiler_params=_params(("arbitrary", "arbitrary"), 48),
        name="attn_prep",
    )(u, u, u, gvec, cos_t, sin_t)


def _gqa_kernel(bounded_ref, qt_ref, k_ref, vt_ref, gate_ref, o_ref, m_s, acc_s,
                *, group, tq, tk, n_lat, n_ctx):
    acc_s[...] = jnp.zeros(acc_s.shape, F32)

    def chunk_bounded(kc, vtc):
        ps = []
        for j in range(group):
            qt = qt_ref[j * HEAD_DIM:(j + 1) * HEAD_DIM, :]
            s = jnp.dot(kc, qt, preferred_element_type=F32)
            ps.append(jnp.exp2(s).astype(BF16))
        for j in range(group):
            acc_s[j] += jnp.dot(vtc, ps[j], preferred_element_type=F32)

    def chunk_online(kc, vtc):
        for j in range(group):
            qt = qt_ref[j * HEAD_DIM:(j + 1) * HEAD_DIM, :]
            s = jnp.dot(kc, qt, preferred_element_type=F32)
            m = m_s[j]
            m_new = jnp.maximum(m, jnp.max(s, axis=0, keepdims=True))
            p = jnp.exp2(s - m_new).astype(BF16)
            acc_s[j] = jnp.exp2(m - m_new) * acc_s[j] + jnp.dot(vtc, p, preferred_element_type=F32)
            m_s[j] = m_new

    def sweep(chunk):
        @pl.when(pl.program_id(1) * tq < n_lat)
        def _():
            def body(c, carry):
                r0 = pl.multiple_of(c * tk, tk)
                chunk(k_ref[pl.ds(r0, tk), :], vt_ref[:, pl.ds(r0, tk)])
                return carry

            lax.fori_loop(0, n_lat // tk, body, 0)

        chunk(k_ref[n_lat:n_lat + n_ctx, :], vt_ref[:, n_lat:n_lat + n_ctx])

    @pl.when(bounded_ref[0] != 0)
    def _():
        sweep(chunk_bounded)

    @pl.when(bounded_ref[0] == 0)
    def _():
        m_s[...] = jnp.full(m_s.shape, NEG_INF, F32)
        sweep(chunk_online)

    for j in range(group):
        sl = slice(j * HEAD_DIM, (j + 1) * HEAD_DIM)
        acc = acc_s[j]
        o = (acc[:HEAD_DIM] / acc[HEAD_DIM:HEAD_DIM + 1]).T
        o_ref[:, sl] = (o * _silu(gate_ref[:, sl].astype(F32))).astype(o_ref.dtype)


SCORE_BOUND_LOG2 = 60.0


def _scores_bounded(q_gain, k_gain, extra=0.0):
    bound = HEAD_DIM * jnp.max(jnp.abs(q_gain)) * jnp.max(jnp.abs(k_gain)) * 1.02 + extra
    return (bound <= SCORE_BOUND_LOG2).astype(jnp.int32).reshape(1)


def _gqa_attention(bounded, qt, k, vt, u, n_q_heads, n_kv_heads, off_gate, n_lat):
    t = u.shape[0]
    group = n_q_heads // n_kv_heads
    gw = group * HEAD_DIM
    tq = 256
    tk = _largest_tile(n_lat, 1024, 256)
    vrows = HEAD_DIM + GQA_ONES_ROWS
    assert t % tq == 0 and n_lat % tq == 0 and off_gate % gw == 0
    g_blk0 = off_gate // gw
    grid_spec = pltpu.PrefetchScalarGridSpec(
        num_scalar_prefetch=1,
        grid=(n_kv_heads, t // tq),
        in_specs=[pl.BlockSpec((gw, tq), lambda g, i, b: (g, i)),
                  pl.BlockSpec((t, HEAD_DIM), lambda g, i, b: (0, g)),
                  pl.BlockSpec((vrows, t), lambda g, i, b: (g, 0)),
                  pl.BlockSpec((tq, gw), lambda g, i, b: (i, g_blk0 + g))],
        out_specs=pl.BlockSpec((tq, gw), lambda g, i, b: (i, g)),
        scratch_shapes=[pltpu.VMEM((group, 1, tq), F32),
                        pltpu.VMEM((group, vrows, tq), F32)])
    return pl.pallas_call(
        functools.partial(_gqa_kernel, group=group, tq=tq, tk=tk, n_lat=n_lat, n_ctx=t - n_lat),
        grid_spec=grid_spec,
        out_shape=jax.ShapeDtypeStruct((t, n_q_heads * HEAD_DIM), BF16),
        compiler_params=_params(("arbitrary", "arbitrary"), 48),
        name="gqa_attention",
    )(bounded, qt, k, vt, u)


def _na_kernel(bounded_ref, qt_ref, k0_ref, k1_ref, k2_ref, kc_ref, v0_ref, v1_ref, v2_ref, vc_ref,
               bias_ref, gate_ref, o_ref, *, n_lat_blocks):
    is_latent = pl.program_id(1) < n_lat_blocks
    is_bounded = bounded_ref[0] != 0
    vrows = HEAD_DIM + GQA_ONES_ROWS

    def probabilities(j, latent, bounded):
        hs = slice(j * HEAD_DIM, (j + 1) * HEAD_DIM)
        qt = qt_ref[hs, :]
        scores = [jnp.dot(kc_ref[:, hs], qt, preferred_element_type=F32)]
        if latent:
            k = jnp.concatenate([k0_ref[:, hs], k1_ref[:, hs], k2_ref[:, hs]], axis=0)
            scores.append(jnp.dot(k, qt, preferred_element_type=F32) + bias_ref[j].astype(F32))
        if not bounded:
            m = functools.reduce(jnp.maximum, [jnp.max(s, axis=0, keepdims=True) for s in scores])
            scores = [s - m for s in scores]
        return [jnp.exp2(s).astype(BF16) for s in scores]

    def output(j, probs, latent):
        hs = slice(j * HEAD_DIM, (j + 1) * HEAD_DIM)
        vs = slice(j * vrows, (j + 1) * vrows)
        acc = jnp.dot(vc_ref[vs, :], probs[0], preferred_element_type=F32)
        if latent:
            vt = jnp.concatenate([v0_ref[vs, :], v1_ref[vs, :], v2_ref[vs, :]], axis=1)
            acc = acc + jnp.dot(vt, probs[1], preferred_element_type=F32)
        o = (acc[:HEAD_DIM] / acc[HEAD_DIM:HEAD_DIM + 1]).T
        o_ref[:, hs] = (o * _silu(gate_ref[:, hs].astype(F32))).astype(o_ref.dtype)

    heads = range(qt_ref.shape[0] // HEAD_DIM)
    for latent in (True, False):
        for bounded in (True, False):
            @pl.when(jnp.logical_and(is_latent == latent, is_bounded == bounded))
            def _():
                probs = [probabilities(j, latent, bounded) for j in heads]
                for j in heads:
                    output(j, probs[j], latent)


def _na_bias_table(rel_bias, n_rows):
    c = np.arange(GRID_W)[None, :]
    kj = np.arange(GRID_W)[:, None]
    cs = np.clip(c - NA_KW // 2, 0, GRID_W - NA_KW)
    col_ok = (kj >= cs) & (kj < cs + NA_KW)
    sel_c = (np.arange(2 * NA_KW - 1)[:, None, None] == (kj - c + NA_KW - 1)[None]) & col_ok[None]
    cols = jnp.einsum('hde,ekc->hdkc', rel_bias * math.log2(math.e), jnp.asarray(sel_c, F32),
                      precision=lax.Precision.HIGHEST)
    cols = jnp.where(jnp.asarray(col_ok)[None, None], cols, NEG_INF).astype(BF16)
    neg = jnp.full((rel_bias.shape[0], GRID_W, GRID_W), NEG_INF, BF16)
    tables = []
    for r0, ws in ((0, 0), (NA_Q_ROWS, 0), (n_rows - NA_Q_ROWS, n_rows - NA_WIN_ROWS)):
        key_rows = []
        for i in range(NA_WIN_ROWS):
            blocks = []
            for a in range(NA_Q_ROWS):
                r, kr = r0 + a, ws + i
                rs = min(max(r - NA_KH // 2, 0), n_rows - NA_KH)
                blocks.append(cols[:, kr - r + NA_KH - 1] if rs <= kr < rs + NA_KH else neg)
            key_rows.append(jnp.concatenate(blocks, axis=-1))
        tables.append(jnp.concatenate(key_rows, axis=-2))
    return jnp.stack(tables, axis=0)


def _na_attention(bounded, qt, kn, vt, u, n_heads, off_gate, bias_tbl, n_lat):
    t = u.shape[0]
    tb = NA_Q_ROWS * GRID_W
    n_lat_blocks = n_lat // tb
    ctx_blk = t // tb - 1
    hp = NA_HEADS_PER_STEP
    hw = hp * HEAD_DIM
    vrows = hp * (HEAD_DIM + GQA_ONES_ROWS)
    assert t - n_lat == tb and n_lat_blocks >= 3 and off_gate % hw == 0 and n_heads % hp == 0
    g_blk0 = off_gate // hw

    def win(i, j):
        return jnp.clip(i - 1, 0, n_lat_blocks - 3) + j

    def k_spec(j):
        return pl.BlockSpec((tb, hw), lambda h, i, b: (win(i, j), h))

    def v_spec(j):
        return pl.BlockSpec((vrows, tb), lambda h, i, b: (h, win(i, j)))

    def bias_cls(h, i, b):
        return (jnp.where(i == 0, 0, jnp.where(i >= n_lat_blocks - 1, 2, 1)), h, 0, 0)

    grid_spec = pltpu.PrefetchScalarGridSpec(
        num_scalar_prefetch=1,
        grid=(n_heads // hp, t // tb),
        in_specs=[pl.BlockSpec((hw, tb), lambda h, i, b: (h, i)),
                  k_spec(0), k_spec(1), k_spec(2),
                  pl.BlockSpec((tb, hw), lambda h, i, b: (ctx_blk, h)),
                  v_spec(0), v_spec(1), v_spec(2),
                  pl.BlockSpec((vrows, tb), lambda h, i, b: (h, ctx_blk)),
                  pl.BlockSpec((None, hp, NA_WIN_ROWS * GRID_W, tb), bias_cls),
                  pl.BlockSpec((tb, hw), lambda h, i, b: (i, g_blk0 + h))],
        out_specs=pl.BlockSpec((tb, hw), lambda h, i, b: (i, h)))
    return pl.pallas_call(
        functools.partial(_na_kernel, n_lat_blocks=n_lat_blocks),
        grid_spec=grid_spec,
        out_shape=jax.ShapeDtypeStruct((t, n_heads * HEAD_DIM), BF16),
        compiler_params=_params(("arbitrary", "arbitrary"), 32),
        name="na_attention",
    )(bounded, qt, kn, kn, kn, kn, vt, vt, vt, vt, bias_tbl, u)


def _short_conv_kernel(x_ref, prev_ref, next_ref, w_ref, o_ref, *, tr):
    i = pl.program_id(0)
    n = pl.num_programs(0)
    x = x_ref[...].astype(F32)
    halo = prev_ref.shape[0]
    prev_row = jnp.where(i == 0, 0.0, prev_ref[halo - 1:halo, :].astype(F32))
    next_row = jnp.where(i == n - 1, 0.0, next_ref[0:1, :].astype(F32))
    row = lax.broadcasted_iota(jnp.int32, x.shape, 0)
    up = jnp.where(row == 0, prev_row, pltpu.roll(x, 1, 0))
    down = jnp.where(row == tr - 1, next_row, pltpu.roll(x, tr - 1, 0))
    y = (up * w_ref[0:1, :] + x * w_ref[1:2, :] + down * w_ref[2:3, :]).astype(o_ref.dtype)
    o_ref[...] = y.reshape(o_ref.shape)


def _short_conv(u, width, conv_w, n_lat):
    tr = _largest_tile(n_lat, 1024, 256)
    tc = _largest_tile(width, 1024, 256)
    halo = 16
    hb = tr // halo
    n_tiles = n_lat // tr
    wpad = jnp.zeros((8, width), F32).at[:HY_SHORT].set(conv_w.astype(F32))
    return pl.pallas_call(
        functools.partial(_short_conv_kernel, tr=tr),
        grid=(n_tiles, width // tc),
        in_specs=[pl.BlockSpec((tr, tc), lambda i, j: (i, j)),
                  pl.BlockSpec((halo, tc), lambda i, j: (jnp.maximum(i * hb - 1, 0), j)),
                  pl.BlockSpec((halo, tc), lambda i, j: (jnp.minimum((i + 1) * hb, n_tiles * hb - 1), j)),
                  pl.BlockSpec((8, tc), lambda i, j: (0, j))],
        out_specs=pl.BlockSpec((tr // FFT_N2, FFT_N2, tc), lambda i, j: (i, 0, j)),
        out_shape=jax.ShapeDtypeStruct((n_lat // FFT_N2, FFT_N2, width), BF16),
        compiler_params=_params(("arbitrary", "arbitrary"), 32),
        name="short_conv",
    )(u, u, u, wpad)


def _hy_gate2_kernel(x2_ref, c_ref, y1_ref, d_ref, g_ref, ctx_ref, o_ref, *, n_lat_tiles):
    i = pl.program_id(0)

    @pl.when(i < n_lat_tiles)
    def _():
        rows = lambda ref: ref[...].reshape(o_ref.shape).astype(F32)
        y1 = rows(y1_ref)
        z = rows(x2_ref) * (rows(c_ref) + y1 * d_ref[1:2, :])
        o_ref[...] = (z * _silu(g_ref[...].astype(F32))).astype(o_ref.dtype)

    @pl.when(i >= n_lat_tiles)
    def _():
        o_ref[...] = ctx_ref[...]


def _hy_gate2(vx3, c2, y1, dskip8, u, off_gate, hy_ctx, hy):
    t = u.shape[0]
    n_lat = vx3.shape[0] * FFT_N2
    tr = t - n_lat
    tc = _largest_tile(hy, 512, 256)
    nb = hy // tc
    n_lat_tiles = n_lat // tr
    assert n_lat % tr == 0 and off_gate % tc == 0 and tr % FFT_N2 == 0
    g_blk0 = off_gate // tc
    lat = lambda i: jnp.minimum(i, n_lat_tiles - 1)
    slab = lambda c0: pl.BlockSpec((tr // FFT_N2, FFT_N2, tc), lambda i, j: (lat(i), 0, c0 + j))
    return pl.pallas_call(
        functools.partial(_hy_gate2_kernel, n_lat_tiles=n_lat_tiles),
        grid=(t // tr, nb),
        in_specs=[slab(2 * nb), slab(0), slab(0),
                  pl.BlockSpec((8, tc), lambda i, j: (0, j)),
                  pl.BlockSpec((tr, tc), lambda i, j: (i, g_blk0 + j)),
                  pl.BlockSpec((tr, tc), lambda i, j: (0, j))],
        out_specs=pl.BlockSpec((tr, tc), lambda i, j: (i, j)),
        out_shape=jax.ShapeDtypeStruct((t, hy), BF16),
        compiler_params=_params(("arbitrary", "arbitrary"), 32),
        name="hy_gate2",
    )(vx3, c2, y1, dskip8, u, hy_ctx)


def _filter_hidden_kernel(z_ref, w1_ref, b1_ref, w2_ref, b2_ref, fr_ref, o_ref):
    hp = lax.Precision.HIGHEST
    fr = fr_ref[...]
    h = jnp.sin(fr * (jnp.dot(z_ref[...], w1_ref[...], precision=hp,
                              preferred_element_type=F32) + b1_ref[...]))
    o_ref[...] = jnp.sin(fr * (jnp.dot(h, w2_ref[...], precision=hp,
                                       preferred_element_type=F32) + b2_ref[...]))


def _pad2(a, rows, cols):
    return jnp.zeros((rows, cols), F32).at[:a.shape[0], :a.shape[1]].set(a.astype(F32))


def _filter_hidden(feat, w1, b1, w2, b2, freq):
    r = feat.shape[0]
    tr = _largest_tile(r, 1024, 256)
    full = lambda i: (0, 0)
    return pl.pallas_call(
        _filter_hidden_kernel,
        grid=(r // tr,),
        in_specs=[pl.BlockSpec((tr, LANE), lambda i: (i, 0)),
                  pl.BlockSpec((LANE, LANE), full), pl.BlockSpec((1, LANE), full),
                  pl.BlockSpec((LANE, LANE), full), pl.BlockSpec((1, LANE), full),
                  pl.BlockSpec((1, LANE), full)],
        out_specs=pl.BlockSpec((tr, LANE), lambda i: (i, 0)),
        out_shape=jax.ShapeDtypeStruct((r, LANE), F32),
        compiler_params=_params(("arbitrary",), 32),
        name="filter_hidden",
    )(feat, _pad2(w1, LANE, LANE), _pad2(b1[None], 1, LANE), _pad2(w2, LANE, LANE),
      _pad2(b2[None], 1, LANE), _pad2(freq[None], 1, LANE))


def _two_sided_positions(n):
    idx = jnp.arange(2 * n)
    return jnp.where(idx < n, idx, 2 * n - idx) % n


def _filter_features(n, n_bands):
    pos = _two_sided_positions(n).astype(F32)[:, None]
    t = pos / (n - 1)
    w = (2.0 * math.pi / n) * pos
    bands = jnp.linspace(1e-4, n_bands - 1, n_bands, dtype=F32)[None, :]
    z = jnp.concatenate([t, jnp.cos(bands * w), -jnp.sin(bands * w)], axis=-1)
    return _pad2(z, 2 * n, LANE), t


def _filter_gen_kernel(h_ref, w_ref, t_ref, dec_ref, o_ref, *, tr, n):
    i = pl.program_id(0)
    f = jnp.dot(h_ref[...].astype(BF16), w_ref[...], preferred_element_type=F32)
    f = f * jnp.exp(-t_ref[...] * jnp.abs(dec_ref[0:1, :]))
    row = i * tr + lax.broadcasted_iota(jnp.int32, (tr, 1), 0)
    o_ref[...] = jnp.where(row == n, 0.0, f).astype(o_ref.dtype).reshape(o_ref.shape)


def _filter_gen(hidden, row0, w3p, tcol, decay8, n, order, hy):
    tr = _largest_tile(n, 512, 256)
    tc = _largest_tile(hy, 1024, 256)
    nb = hy // tc
    n_half = n // tr
    hb0 = row0 // tr
    assert row0 % tr == 0
    return pl.pallas_call(
        functools.partial(_filter_gen_kernel, tr=tr, n=n),
        grid=(2 * n // tr, order * nb),
        in_specs=[pl.BlockSpec((tr, LANE), lambda i, j: (hb0 + i, 0)),
                  pl.BlockSpec((LANE, tc), lambda i, j: (0, (j // nb) * 2 * nb + (i // n_half) * nb + j % nb)),
                  pl.BlockSpec((tr, 1), lambda i, j: (i, 0)),
                  pl.BlockSpec((None, 8, tc), lambda i, j: (j // nb, 0, j % nb))],
        out_specs=pl.BlockSpec((tr // FFT_N2, FFT_N2, tc), lambda i, j: (i, 0, j)),
        out_shape=jax.ShapeDtypeStruct((2 * n // FFT_N2, FFT_N2, order * hy), BF16),
        compiler_params=_params(("arbitrary", "arbitrary"), 32),
        name="filter_gen",
    )(hidden, w3p, tcol, decay8)


def _dft_tables(n1, nz):
    n2 = FFT_N2
    n = n1 * n2
    half = n1 // 2
    k1 = jnp.arange(half + 1)
    two_pi = 2.0 * math.pi

    def cs(idx, period):
        ang = (idx % period).astype(F32) * (two_pi / period)
        return jnp.cos(ang), -jnp.sin(ang)

    eye8 = jnp.eye(8, dtype=F32)
    fr, fi = cs(k1[:, None] * jnp.arange(nz)[None, :], n1)
    kf = jnp.stack([fr, fi], axis=1)
    kf = jnp.einsum('kpn,jl->kpjnl', kf, eye8).reshape((half + 1) * 16, nz * 8)
    gr, gi = cs(jnp.arange(half)[:, None] * k1[None, :], n1)
    weight = jnp.where((k1 == 0) | (k1 == half), 1.0, 2.0) / n
    ki = jnp.stack([gr, gi], axis=2) * weight[None, :, None]
    ki = jnp.einsum('nkp,jl->njkpl', ki, eye8).reshape(half * 8, (half + 1) * 16)
    r = jnp.arange(2 * n2)
    part = (r % 16) // 8
    n2_of_r = 8 * (r // 16) + r % 8
    k2 = jnp.arange(n2)
    idx = k2[None, :, None] * n2_of_r[None, None, :] * n1 + k1[:, None, None] * n2_of_r[None, None, :]
    cr, ci = cs(idx, n)
    top = jnp.where(part[None, None, :] == 0, cr, -ci)
    bot = jnp.where(part[None, None, :] == 0, ci, cr)
    m = jnp.concatenate([top, bot], axis=1)
    return kf.astype(BF16), ki.astype(BF16), m.astype(BF16)


def _cross_slab_forward(z_ref, kf_ref, a_scr, nz, nh, ct):
    def body(b, carry):
        z = z_ref[:, pl.ds(pl.multiple_of(16 * b, 16), 16), :].astype(F32)
        for half in range(2):
            g = z[:, 8 * half:8 * half + 8, :].reshape(nz * 8, ct).astype(BF16)
            p = jnp.dot(kf_ref[...], g, preferred_element_type=F32)
            start = pl.multiple_of(32 * b + 16 * half, 16)
            a_scr[:, pl.ds(start, 16), :] = p.astype(BF16).reshape(nh, 16, ct)
        return carry

    lax.fori_loop(0, FFT_N2 // 16, body, 0)


def _slab_block(nh):
    return max(k for k in range(1, 17) if nh % k == 0)


def _fft_fwd_kernel(z_ref, kf_ref, m_ref, o_ref, a_scr, *, nz, nh, kb, ct):
    kblk = pl.program_id(1)

    @pl.when(kblk == 0)
    def _():
        _cross_slab_forward(z_ref, kf_ref, a_scr, nz, nh, ct)

    for k in range(kb):
        x = jnp.dot(m_ref[k], a_scr[kblk * kb + k], preferred_element_type=F32)
        o_ref[k] = x.astype(o_ref.dtype)


def _fft_forward(z3, kf, m, ct=256):
    nz, _, c = z3.shape
    nh = m.shape[0]
    kb = _slab_block(nh)
    return pl.pallas_call(
        functools.partial(_fft_fwd_kernel, nz=nz, nh=nh, kb=kb, ct=ct),
        grid=(c // ct, nh // kb),
        in_specs=[pl.BlockSpec((nz, FFT_N2, ct), lambda j, k: (0, 0, j)),
                  pl.BlockSpec(kf.shape, lambda j, k: (0, 0)),
                  pl.BlockSpec((kb, 2 * FFT_N2, 2 * FFT_N2), lambda j, k: (k, 0, 0))],
        out_specs=pl.BlockSpec((kb, 2 * FFT_N2, ct), lambda j, k: (k, 0, j)),
        out_shape=jax.ShapeDtypeStruct((nh, 2 * FFT_N2, c), BF16),
        scratch_shapes=[pltpu.VMEM((nh, 2 * FFT_N2, ct), BF16)],
        compiler_params=_params(("arbitrary", "arbitrary"), 56),
        name="fft_forward",
    )(z3, kf, m)


def _fft_conv_kernel(z_ref, kf_ref, ki_ref, m_ref, h_ref, *rest, nz, nh, kb, ct, skip_row):
    gated = len(rest) == 4
    o_ref, a_scr = rest[-2:]
    kblk = pl.program_id(1)
    n2 = FFT_N2

    @pl.when(kblk == 0)
    def _():
        _cross_slab_forward(z_ref, kf_ref, a_scr, nz, nh, ct)

    for k in range(kb):
        mk = m_ref[k]
        x = jnp.dot(mk, a_scr[kblk * kb + k], preferred_element_type=F32)
        h = h_ref[k].astype(F32)
        xr, xi, hr, hi = x[:n2], x[n2:], h[:n2], h[n2:]
        y = jnp.concatenate([xr * hr - xi * hi, xr * hi + xi * hr], axis=0).astype(BF16)
        b = lax.dot_general(mk, y, (((0,), (0,)), ((), ())), preferred_element_type=F32)
        a_scr[kblk * kb + k] = b.astype(BF16)

    @pl.when(kblk == pl.num_programs(1) - 1)
    def _():
        def body(b, carry):
            halves = []
            for half in range(2):
                start = pl.multiple_of(32 * b + 16 * half, 16)
                r = a_scr[:, pl.ds(start, 16), :].reshape(nh * 16, ct)
                p = jnp.dot(ki_ref[...], r, preferred_element_type=F32)
                halves.append(p.reshape(nz, 8, ct))
            out = jnp.concatenate(halves, axis=1)
            rows = pl.ds(pl.multiple_of(16 * b, 16), 16)
            if gated:
                x_ref, d_ref = rest[:2]
                z = z_ref[:, rows, :].astype(F32)
                out = x_ref[:, rows, :].astype(F32) * (out + z * d_ref[skip_row:skip_row + 1, :])
            o_ref[:, rows, :] = out.astype(o_ref.dtype)
            return carry

        lax.fori_loop(0, n2 // 16, body, 0)


def _fft_conv(z3, z_blk0, spec, spec_blk0, kf, ki, m, c, gate=None, ct=256):
    nz = z3.shape[0]
    nh = m.shape[0]
    kb = _slab_block(nh)
    assert nh == nz + 1
    in_specs = [pl.BlockSpec((nz, FFT_N2, ct), lambda j, k: (0, 0, z_blk0 + j)),
                pl.BlockSpec(kf.shape, lambda j, k: (0, 0)),
                pl.BlockSpec(ki.shape, lambda j, k: (0, 0)),
                pl.BlockSpec((kb, 2 * FFT_N2, 2 * FFT_N2), lambda j, k: (k, 0, 0)),
                pl.BlockSpec((kb, 2 * FFT_N2, ct), lambda j, k: (k, 0, spec_blk0 + j))]
    operands = [z3, kf, ki, m, spec]
    skip_row = 0
    if gate is not None:
        x3, x_blk0, dskip8, skip_row = gate
        in_specs += [pl.BlockSpec((nz, FFT_N2, ct), lambda j, k: (0, 0, x_blk0 + j)),
                     pl.BlockSpec((8, ct), lambda j, k: (0, j))]
        operands += [x3, dskip8]
    return pl.pallas_call(
        functools.partial(_fft_conv_kernel, nz=nz, nh=nh, kb=kb, ct=ct, skip_row=skip_row),
        grid=(c // ct, nh // kb),
        in_specs=in_specs,
        out_specs=pl.BlockSpec((nz, FFT_N2, ct), lambda j, k: (0, 0, j)),
        out_shape=jax.ShapeDtypeStruct((nz, FFT_N2, c), BF16),
        scratch_shapes=[pltpu.VMEM((nh, 2 * FFT_N2, ct), BF16)],
        compiler_params=_params(("arbitrary", "arbitrary"), 56),
        name="fft_conv",
    )(*operands)


def _ctx_dft_tables(n):
    nn = 2 * n
    k = jnp.arange(nn)
    ang = ((k[:, None] * k[None, :]) % nn).astype(F32) * (2.0 * math.pi / nn)
    cr, ci = jnp.cos(ang), -jnp.sin(ang)
    full = jnp.concatenate([cr, ci], axis=0)
    inv = jnp.concatenate([cr[:n], ci[:n]], axis=1) / nn
    return full[:, :n].astype(BF16), full.astype(BF16), inv.astype(BF16)


def _ctx_hyena_kernel(v_ref, x1_ref, x2_ref, g_ref, cw0_ref, cw1_ref, cw2_ref, hid_ref,
                      w10_ref, w11_ref, w20_ref, w21_ref, t_ref, dec_ref, dsk_ref,
                      ff_ref, ffull_ref, finv_ref, o_ref, *, n):
    row = lax.broadcasted_iota(jnp.int32, (n, 1), 0)

    def sconv(x_ref, w_ref):
        x = x_ref[...].astype(F32)
        up = jnp.where(row == 0, 0.0, pltpu.roll(x, 1, 0))
        down = jnp.where(row == n - 1, 0.0, pltpu.roll(x, n - 1, 0))
        y = up * w_ref[0:1, :] + x * w_ref[1:2, :] + down * w_ref[2:3, :]
        return y.astype(BF16).astype(F32)

    hid = hid_ref[...].astype(BF16)
    tcol = t_ref[...]
    row2 = lax.broadcasted_iota(jnp.int32, (2 * n, 1), 0)

    def spectrum(wf_ref, wb_ref, o):
        fwd = jnp.dot(hid[:n], wf_ref[...], preferred_element_type=F32)
        bwd = jnp.dot(hid[n:], wb_ref[...], preferred_element_type=F32)
        f = jnp.concatenate([fwd, bwd], axis=0) * jnp.exp(-tcol * jnp.abs(dec_ref[o:o + 1, :]))
        f = jnp.where(row2 == n, 0.0, f).astype(BF16)
        return jnp.dot(ffull_ref[...], f, preferred_element_type=F32)

    def conv(z, h):
        x = jnp.dot(ff_ref[...], z.astype(BF16), preferred_element_type=F32)
        h = h.astype(BF16).astype(F32)
        nn = 2 * n
        xr, xi, hr, hi = x[:nn], x[nn:], h[:nn], h[nn:]
        y = jnp.concatenate([xr * hr - xi * hi, xr * hi + xi * hr], axis=0).astype(BF16)
        return jnp.dot(finv_ref[...], y, preferred_element_type=F32)

    v = sconv(v_ref, cw0_ref)
    x1 = sconv(x1_ref, cw1_ref)
    x2 = sconv(x2_ref, cw2_ref)
    c1 = conv(v, spectrum(w10_ref, w11_ref, 0)).astype(BF16).astype(F32)
    y1 = (x1 * (c1 + v * dsk_ref[0:1, :])).astype(BF16).astype(F32)
    c2 = conv(y1, spectrum(w20_ref, w21_ref, 1)).astype(BF16).astype(F32)
    z = x2 * (c2 + y1 * dsk_ref[1:2, :])
    o_ref[...] = (z * _silu(g_ref[...].astype(F32))).astype(o_ref.dtype)


def _ctx_hyena(u, n_lat, hy, order, off_gate, conv8, hidden, hid_row0, w3p, tcol, decay8, dskip8, tables):
    t = u.shape[0]
    n = t - n_lat
    assert order == 2 and n_lat % n == 0 and hid_row0 % (2 * n) == 0
    ct = 256
    nb = hy // ct
    rb = n_lat // n
    ff, ffull, finv = tables
    const = lambda j: (0, 0)
    ublk = lambda c0: pl.BlockSpec((n, ct), lambda j: (rb, c0 + j))
    wblk = lambda c0: pl.BlockSpec((LANE, ct), lambda j: (0, c0 + j))
    cwblk = lambda c0: pl.BlockSpec((8, ct), lambda j: (0, c0 + j))
    return pl.pallas_call(
        functools.partial(_ctx_hyena_kernel, n=n),
        grid=(nb,),
        in_specs=[ublk(0), ublk(nb), ublk(2 * nb), ublk(off_gate // ct),
                  cwblk(0), cwblk(nb), cwblk(2 * nb),
                  pl.BlockSpec((2 * n, LANE), lambda j: (hid_row0 // (2 * n), 0)),
                  wblk(0), wblk(nb), wblk(2 * nb), wblk(3 * nb),
                  pl.BlockSpec((2 * n, 1), const),
                  pl.BlockSpec((8, ct), lambda j: (0, j)),
                  pl.BlockSpec((8, ct), lambda j: (0, j)),
                  pl.BlockSpec(ff.shape, const), pl.BlockSpec(ffull.shape, const),
                  pl.BlockSpec(finv.shape, const)],
        out_specs=pl.BlockSpec((n, ct), lambda j: (0, j)),
        out_shape=jax.ShapeDtypeStruct((n, hy), BF16),
        compiler_params=_params(("arbitrary",), 40),
        name="ctx_hyena",
    )(u, u, u, u, conv8, conv8, conv8, hidden, w3p, w3p, w3p, w3p, tcol, decay8, dskip8,
      ff, ffull, finv)


def _even_layer(x_all, mods_l, g, n_lat, w_in, w_out, conv_w, w1, b1, w2, b2, w3, freq, decay, dskip,
                q_norm, k_norm, rope, dft, ctx_dft, last):
    t, d = x_all.shape
    n_ctx = t - n_lat
    order, hy = decay.shape
    gqa_w = d - hy
    n_q = gqa_w // HEAD_DIM
    n_kv = n_q // 4
    kv_w = n_kv * HEAD_DIM
    off_hy_gate = (order + 1) * hy
    off_q = off_hy_gate + hy
    off_v = off_q + gqa_w + kv_w
    off_at_gate = off_v + kv_w
    scale = HEAD_DIM ** -0.5

    h = _norm_mod(x_all, g, mods_l, n_lat)
    u = _in_proj(h, *w_in)

    q_gain = q_norm.astype(F32) * (scale * math.log2(math.e))
    k_gain = k_norm.astype(F32)
    qt, kn, vt = _attn_prep(u, off_q, off_q + gqa_w, off_v, n_q, n_kv, 1, q_gain, k_gain,
                            rope[0], rope[1], True)
    att = _gqa_attention(_scores_bounded(q_gain, k_gain), qt, kn, vt, u, n_q, n_kv, off_at_gate, n_lat)

    n_bands = (w1.shape[0] - 1) // 2
    feat_l, t_l = _filter_features(n_lat, n_bands)
    feat_c, t_c = _filter_features(n_ctx, n_bands)
    hidden = _filter_hidden(jnp.concatenate([feat_l, feat_c], axis=0), w1, b1, w2, b2, freq)
    w3p = jnp.zeros((LANE, w3.shape[1]), BF16).at[:w3.shape[0]].set(w3.astype(BF16))
    decay8 = jnp.zeros((order, 8, hy), F32).at[:, 0].set(decay.astype(F32))
    dskip8 = jnp.zeros((8, hy), F32).at[:order].set(dskip.astype(F32))
    conv8 = jnp.zeros((8, conv_w.shape[1]), F32).at[:HY_SHORT].set(conv_w.astype(F32))
    kf_pad, kf_full, ki, m = dft
    n1 = 2 * n_lat // FFT_N2
    filt = _filter_gen(hidden, 0, w3p, t_l, decay8, n_lat, order, hy)
    spec = _fft_forward(filt, kf_full, m)

    vx3 = _short_conv(u, (order + 1) * hy, conv_w, n_lat)
    ct = 256
    y1 = _fft_conv(vx3, 0, spec, 0, kf_pad, ki, m, hy, gate=(vx3, hy // ct, dskip8, 0))
    c2 = _fft_conv(y1, 0, spec, hy // ct, kf_pad, ki, m, hy)

    decay_c8 = jnp.zeros((8, hy), F32).at[:order].set(decay.astype(F32))
    hy_ctx = _ctx_hyena(u, n_lat, hy, order, off_hy_gate, conv8, hidden, 2 * n_lat, w3p, t_c,
                        decay_c8, dskip8, ctx_dft)
    hy_mix = _hy_gate2(vx3, c2, y1, dskip8, u, off_hy_gate, hy_ctx, hy)

    return _out_proj(hy_mix, 0, att, 0, *w_out, x_all, mods_l, n_lat, last)


def _odd_layer(x_all, mods_l, g, n_lat, w_in, w_out, q_norm, k_norm, rel_bias, last):
    t, d = x_all.shape
    n_heads = rel_bias.shape[0]
    w = n_heads * HEAD_DIM
    scale = HEAD_DIM ** -0.5
    h = _norm_mod(x_all, g, mods_l, n_lat)
    u = _in_proj(h, *w_in)
    log2e = math.log2(math.e)
    q_gain = q_norm.astype(F32) * (scale * log2e)
    k_gain = k_norm.astype(F32)
    heads_per_step = NA_HEADS_PER_STEP
    unused = jnp.zeros((t, HEAD_DIM), F32)
    qt, kn, vt = _attn_prep(u, 0, w, 2 * w, heads_per_step, heads_per_step, n_heads // heads_per_step,
                            q_gain, k_gain, unused, unused, False)
    bias_tbl = _na_bias_table(rel_bias.astype(F32), n_lat // GRID_W)
    bounded = _scores_bounded(q_gain, k_gain, jnp.max(jnp.abs(rel_bias)).astype(F32) * (log2e * 1.02))
    att = _na_attention(bounded, qt, kn, vt, u, n_heads, 3 * w, bias_tbl, n_lat)
    return _out_proj(att, 0, att, 1, *w_out, x_all, mods_l, n_lat, last)


def kernel(x, c, ctx, c_ctx, norm_g, ada_w, ada_b, e_w_in, e_w_out, hy_conv, hy_w1, hy_b1, hy_w2, hy_b2,
           hy_w3, hy_freq, hy_decay, hy_dskip, gqa_q_norm, gqa_k_norm, o_w_in, o_w_out, na_q_norm,
           na_k_norm, na_rel_bias):
    batch, n_lat, d = x.shape
    n_ctx = ctx.shape[1]
    depth = norm_g.shape[0]
    assert batch == 1 and n_lat % (2 * FFT_N2) == 0 and n_ctx == NA_Q_ROWS * GRID_W

    x_all = jnp.concatenate([x[0], ctx[0]], axis=0).astype(F32)
    cvecs = jnp.zeros((8, d), F32).at[0].set(c[0]).at[1].set(c_ctx)
    mods = _ada_mods(cvecs, ada_w, ada_b)

    rope = _rope_tables(n_lat, n_ctx)
    n1 = 2 * n_lat // FFT_N2
    kf_pad, ki, m = _dft_tables(n1, n1 // 2)
    kf_full, _, _ = _dft_tables(n1, n1)
    dft = (kf_pad, kf_full, ki, m)
    ctx_dft = _ctx_dft_tables(n_ctx)

    for layer in range(depth):
        i = layer // 2
        last = layer == depth - 1
        if layer % 2 == 0:
            x_all = _even_layer(x_all, mods[layer], norm_g[layer], n_lat,
                                (e_w_in, i), (e_w_out, i), hy_conv[i],
                                hy_w1[i], hy_b1[i], hy_w2[i], hy_b2[i], hy_w3[i], hy_freq[i],
                                hy_decay[i], hy_dskip[i], gqa_q_norm[i], gqa_k_norm[i],
                                rope, dft, ctx_dft, last)
        else:
            x_all = _odd_layer(x_all, mods[layer], norm_g[layer], n_lat,
                               (o_w_in, i), (o_w_out, i),
                               na_q_norm[i], na_k_norm[i], na_rel_bias[i], last)
    return x_all[None]
```

```python
import functools
import math

import numpy as np
import jax
import jax.numpy as jnp
from jax import lax
from jax.experimental import pallas as pl
from jax.experimental.pallas import tpu as pltpu

F32 = jnp.float32
BF16 = jnp.bfloat16

HEAD_DIM = 128
GRID_W = 64
EPS = 1e-6
ROPE_THETA = 10000.0
NA_KH = 8
NA_KW = 16
NA_Q_ROWS = 4
NA_WIN_ROWS = 12
NA_HEADS_PER_STEP = 8
HY_SHORT = 3
FFT_N2 = 128
NEG_INF = -1e30

V7X_VMEM_BYTES = 64 * 1024 * 1024
LANE = 128


def _params(semantics, vmem_mb):
    assert vmem_mb * 1024 * 1024 < V7X_VMEM_BYTES
    return pltpu.CompilerParams(dimension_semantics=semantics,
                                vmem_limit_bytes=vmem_mb * 1024 * 1024)


def _largest_tile(n, target, quantum):
    best = None
    t = quantum
    while t <= min(n, target):
        if n % t == 0:
            best = t
        t += quantum
    assert best is not None, (n, target, quantum)
    return best


def _silu(x):
    return x * jax.nn.sigmoid(x)


def _mods_kernel(c_ref, w_ref, b_ref, o_ref):
    s = _silu(c_ref[...]).astype(BF16)
    o_ref[...] = jnp.dot(s, w_ref[...].astype(BF16), preferred_element_type=F32) + b_ref[...]


def _ada_mods(cvecs, ada_w, ada_b):
    depth, d, n = ada_w.shape
    tn = _largest_tile(n, 512, LANE)
    return pl.pallas_call(
        _mods_kernel,
        grid=(depth, n // tn),
        in_specs=[pl.BlockSpec((8, d), lambda l, j: (0, 0)),
                  pl.BlockSpec((None, d, tn), lambda l, j: (l, 0, j)),
                  pl.BlockSpec((None, 1, tn), lambda l, j: (l, 0, j))],
        out_specs=pl.BlockSpec((None, 8, tn), lambda l, j: (l, 0, j)),
        out_shape=jax.ShapeDtypeStruct((depth, 8, n), F32),
        compiler_params=_params(("arbitrary", "arbitrary"), 40),
        name="ada_mods",
    )(cvecs, ada_w, ada_b.reshape(depth, 1, n))


def _norm_mod_kernel(x_ref, g_ref, sh_ref, sc_ref, o_ref, *, n_lat_tiles):
    x = x_ref[...]
    y = x * lax.rsqrt(jnp.mean(x * x, axis=-1, keepdims=True) + EPS) * g_ref[...]
    is_ctx = pl.program_id(0) >= n_lat_tiles
    sh = jnp.where(is_ctx, sh_ref[1:2, :], sh_ref[0:1, :])
    sc = jnp.where(is_ctx, sc_ref[1:2, :], sc_ref[0:1, :])
    o_ref[...] = (y * (1.0 + sc) + sh).astype(o_ref.dtype)


def _norm_mod(x_all, g, mods_l, n_lat):
    t, d = x_all.shape
    tr = 256
    assert t % tr == 0 and n_lat % tr == 0
    return pl.pallas_call(
        functools.partial(_norm_mod_kernel, n_lat_tiles=n_lat // tr),
        grid=(t // tr,),
        in_specs=[pl.BlockSpec((tr, d), lambda i: (i, 0)),
                  pl.BlockSpec((1, d), lambda i: (0, 0)),
                  pl.BlockSpec((8, d), lambda i: (0, 0)),
                  pl.BlockSpec((8, d), lambda i: (0, 1))],
        out_specs=pl.BlockSpec((tr, d), lambda i: (i, 0)),
        out_shape=jax.ShapeDtypeStruct((t, d), BF16),
        compiler_params=_params(("arbitrary",), 40),
        name="norm_mod",
    )(x_all, g.reshape(1, d), mods_l, mods_l)


def _in_proj_kernel(a_ref, w_ref, o_ref):
    w = w_ref[...].astype(BF16)
    o_ref[...] = jnp.dot(a_ref[...], w, preferred_element_type=F32).astype(o_ref.dtype)


def _in_proj(h, w_all, layer_idx):
    m, k = h.shape
    n = w_all.shape[2]
    tm = _largest_tile(m, 1408, 128)
    tn = _largest_tile(n, 512, 256)
    return pl.pallas_call(
        _in_proj_kernel,
        grid=(m // tm, n // tn),
        in_specs=[pl.BlockSpec((tm, k), lambda i, j: (i, 0)),
                  pl.BlockSpec((None, k, tn), lambda i, j: (layer_idx, 0, j))],
        out_specs=pl.BlockSpec((tm, tn), lambda i, j: (i, j)),
        out_shape=jax.ShapeDtypeStruct((m, n), BF16),
        compiler_params=_params(("arbitrary", "arbitrary"), 56),
        name="in_proj",
    )(h, w_all)


def _out_proj_kernel(a1_ref, a2_ref, w1_ref, w2_ref, x_ref, gate_ref, o_ref, wb1, wb2, *, tm, n_lat):
    i = pl.program_id(1)

    @pl.when(i == 0)
    def _():
        wb1[...] = w1_ref[...].astype(BF16)
        wb2[...] = w2_ref[...].astype(BF16)

    y = jnp.dot(a1_ref[...], wb1[...], preferred_element_type=F32)
    y = y + jnp.dot(a2_ref[...], wb2[...], preferred_element_type=F32)
    rows = i * tm + lax.broadcasted_iota(jnp.int32, (tm, 1), 0)
    gate = jnp.where(rows >= n_lat, gate_ref[1:2, :], gate_ref[0:1, :])
    o_ref[...] = x_ref[...] + gate * y


def _out_proj(a1, a1_blk, a2, a2_blk, w_all, layer_idx, x_all, mods_l, n_lat, latent_only):
    d = x_all.shape[1]
    t = n_lat if latent_only else x_all.shape[0]
    kh = w_all.shape[1] // 2
    tm = _largest_tile(t, 1024, 256)
    tn = _largest_tile(d, 512, 256)
    gate_blk0 = 2 * d // tn
    return pl.pallas_call(
        functools.partial(_out_proj_kernel, tm=tm, n_lat=n_lat),
        grid=(d // tn, t // tm),
        in_specs=[pl.BlockSpec((tm, kh), lambda j, i: (i, a1_blk)),
                  pl.BlockSpec((tm, kh), lambda j, i: (i, a2_blk)),
                  pl.BlockSpec((None, kh, tn), lambda j, i: (layer_idx, 0, j)),
                  pl.BlockSpec((None, kh, tn), lambda j, i: (layer_idx, 1, j)),
                  pl.BlockSpec((tm, tn), lambda j, i: (i, j)),
                  pl.BlockSpec((8, tn), lambda j, i: (0, gate_blk0 + j))],
        out_specs=pl.BlockSpec((tm, tn), lambda j, i: (i, j)),
        out_shape=jax.ShapeDtypeStruct((t, d), F32),
        scratch_shapes=[pltpu.VMEM((kh, tn), BF16), pltpu.VMEM((kh, tn), BF16)],
        compiler_params=_params(("arbitrary", "arbitrary"), 48),
        name="out_proj",
    )(a1, a2, w_all, w_all, x_all, mods_l)


def _rope_tables(n_lat, n_ctx):
    quarter = HEAD_DIM // 4
    inv = ROPE_THETA ** (-jnp.arange(quarter, dtype=F32) / quarter)
    tok = jnp.arange(n_lat)
    rows = (tok // GRID_W).astype(F32)[:, None] * inv[None]
    cols = (tok % GRID_W).astype(F32)[:, None] * inv[None]
    cos = jnp.concatenate([jnp.cos(rows), jnp.cos(rows), jnp.cos(cols), jnp.cos(cols)], axis=-1)
    sin = jnp.concatenate([-jnp.sin(rows), jnp.sin(rows), -jnp.sin(cols), jnp.sin(cols)], axis=-1)
    cos = jnp.concatenate([cos, jnp.ones((n_ctx, HEAD_DIM), F32)], axis=0)
    sin = jnp.concatenate([sin, jnp.zeros((n_ctx, HEAD_DIM), F32)], axis=0)
    return cos, sin


GQA_ONES_ROWS = 16


def _attn_prep_kernel(q_ref, k_ref, v_ref, g_ref, cos_ref, sin_ref, qt_ref, ko_ref, vt_ref,
                      *, n_q, n_kv, rope):
    if rope:
        cos = cos_ref[...]
        sin = sin_ref[...]
        lane = lax.broadcasted_iota(jnp.int32, cos.shape, 1)
        first_quarter = (lane % 64) < 32

    def norm_rope(x, g):
        y = x * lax.rsqrt(jnp.mean(x * x, axis=-1, keepdims=True) + EPS) * g
        if not rope:
            return y
        partner = jnp.where(first_quarter, pltpu.roll(y, HEAD_DIM - 32, 1), pltpu.roll(y, 32, 1))
        return y * cos + partner * sin

    for h in range(n_q):
        sl = slice(h * HEAD_DIM, (h + 1) * HEAD_DIM)
        qt_ref[sl, :] = norm_rope(q_ref[:, sl].astype(F32), g_ref[:, sl]).T.astype(qt_ref.dtype)
    for h in range(n_kv):
        sl = slice(h * HEAD_DIM, (h + 1) * HEAD_DIM)
        gk = g_ref[:, (n_q + h) * HEAD_DIM:(n_q + h + 1) * HEAD_DIM]
        ko_ref[:, sl] = norm_rope(k_ref[:, sl].astype(F32), gk).astype(ko_ref.dtype)
        r0 = h * (HEAD_DIM + GQA_ONES_ROWS)
        vt_ref[r0:r0 + HEAD_DIM, :] = v_ref[:, sl].astype(F32).T.astype(vt_ref.dtype)
        vt_ref[r0 + HEAD_DIM:r0 + HEAD_DIM + GQA_ONES_ROWS, :] = jnp.ones(
            (GQA_ONES_ROWS, vt_ref.shape[1]), vt_ref.dtype)


def _attn_prep(u, off_q, off_k, off_v, n_q, n_kv, n_groups, q_gain, k_gain, cos_t, sin_t, rope):
    t = u.shape[0]
    qw, kw = n_q * HEAD_DIM, n_kv * HEAD_DIM
    tr = _largest_tile(t, 768, 256)
    vrows = n_kv * (HEAD_DIM + GQA_ONES_ROWS)
    assert off_q % qw == 0 and off_k % kw == 0 and off_v % kw == 0 and t % tr == 0
    q_blk, k_blk, v_blk = off_q // qw, off_k // kw, off_v // kw
    gvec = jnp.concatenate([jnp.tile(q_gain, n_q), jnp.tile(k_gain, n_kv)])[None]
    return pl.pallas_call(
        functools.partial(_attn_prep_kernel, n_q=n_q, n_kv=n_kv, rope=rope),
        grid=(t // tr, n_groups),
        in_specs=[pl.BlockSpec((tr, qw), lambda i, j: (i, q_blk + j)),
                  pl.BlockSpec((tr, kw), lambda i, j: (i, k_blk + j)),
                  pl.BlockSpec((tr, kw), lambda i, j: (i, v_blk + j)),
                  pl.BlockSpec((1, qw + kw), lambda i, j: (0, 0)),
                  pl.BlockSpec((tr, HEAD_DIM), lambda i, j: (i, 0)),
                  pl.BlockSpec((tr, HEAD_DIM), lambda i, j: (i, 0))],
        out_specs=[pl.BlockSpec((qw, tr), lambda i, j: (j, i)),
                   pl.BlockSpec((tr, kw), lambda i, j: (i, j)),
                   pl.BlockSpec((vrows, tr), lambda i, j: (j, i))],
        out_shape=[jax.ShapeDtypeStruct((n_groups * qw, t), BF16),
                   jax.ShapeDtypeStruct((t, n_groups * kw), BF16),
                   jax.ShapeDtypeStruct((n_groups * vrows, t), BF16)],
        compiler_params=_params(("arbitrary", "arbitrary"), 48),
        name="attn_prep",
    )(u, u, u, gvec, cos_t, sin_t)


def _gqa_kernel(bounded_ref, qt_ref, k_ref, vt_ref, gate_ref, o_ref, m_s, acc_s,
                *, group, tq, tk, n_lat, n_ctx):
    acc_s[...] = jnp.zeros(acc_s.shape, F32)

    def chunk_bounded(kc, vtc):
        ps = []
        for j in range(group):
            qt = qt_ref[j * HEAD_DIM:(j + 1) * HEAD_DIM, :]
            s = jnp.dot(kc, qt, preferred_element_type=F32)
            ps.append(jnp.exp2(s).astype(BF16))
        for j in range(group):
            acc_s[j] += jnp.dot(vtc, ps[j], preferred_element_type=F32)

    def chunk_online(kc, vtc):
        for j in range(group):
            qt = qt_ref[j * HEAD_DIM:(j + 1) * HEAD_DIM, :]
            s = jnp.dot(kc, qt, preferred_element_type=F32)
            m = m_s[j]
            m_new = jnp.maximum(m, jnp.max(s, axis=0, keepdims=True))
            p = jnp.exp2(s - m_new).astype(BF16)
            acc_s[j] = jnp.exp2(m - m_new) * acc_s[j] + jnp.dot(vtc, p, preferred_element_type=F32)
            m_s[j] = m_new

    def sweep(chunk):
        @pl.when(pl.program_id(1) * tq < n_lat)
        def _():
            def body(c, carry):
                r0 = pl.multiple_of(c * tk, tk)
                chunk(k_ref[pl.ds(r0, tk), :], vt_ref[:, pl.ds(r0, tk)])
                return carry

            lax.fori_loop(0, n_lat // tk, body, 0)

        chunk(k_ref[n_lat:n_lat + n_ctx, :], vt_ref[:, n_lat:n_lat + n_ctx])

    @pl.when(bounded_ref[0] != 0)
    def _():
        sweep(chunk_bounded)

    @pl.when(bounded_ref[0] == 0)
    def _():
        m_s[...] = jnp.full(m_s.shape, NEG_INF, F32)
        sweep(chunk_online)

    for j in range(group):
        sl = slice(j * HEAD_DIM, (j + 1) * HEAD_DIM)
        acc = acc_s[j]
        o = (acc[:HEAD_DIM] / acc[HEAD_DIM:HEAD_DIM + 1]).T
        o_ref[:, sl] = (o * _silu(gate_ref[:, sl].astype(F32))).astype(o_ref.dtype)


SCORE_BOUND_LOG2 = 60.0


def _scores_bounded(q_gain, k_gain, extra=0.0):
    bound = HEAD_DIM * jnp.max(jnp.abs(q_gain)) * jnp.max(jnp.abs(k_gain)) * 1.02 + extra
    return (bound <= SCORE_BOUND_LOG2).astype(jnp.int32).reshape(1)


def _gqa_attention(bounded, qt, k, vt, u, n_q_heads, n_kv_heads, off_gate, n_lat):
    t = u.shape[0]
    group = n_q_heads // n_kv_heads
    gw = group * HEAD_DIM
    tq = 256
    tk = _largest_tile(n_lat, 1024, 256)
    vrows = HEAD_DIM + GQA_ONES_ROWS
    assert t % tq == 0 and n_lat % tq == 0 and off_gate % gw == 0
    g_blk0 = off_gate // gw
    grid_spec = pltpu.PrefetchScalarGridSpec(
        num_scalar_prefetch=1,
        grid=(n_kv_heads, t // tq),
        in_specs=[pl.BlockSpec((gw, tq), lambda g, i, b: (g, i)),
                  pl.BlockSpec((t, HEAD_DIM), lambda g, i, b: (0, g)),
                  pl.BlockSpec((vrows, t), lambda g, i, b: (g, 0)),
                  pl.BlockSpec((tq, gw), lambda g, i, b: (i, g_blk0 + g))],
        out_specs=pl.BlockSpec((tq, gw), lambda g, i, b: (i, g)),
        scratch_shapes=[pltpu.VMEM((group, 1, tq), F32),
                        pltpu.VMEM((group, vrows, tq), F32)])
    return pl.pallas_call(
        functools.partial(_gqa_kernel, group=group, tq=tq, tk=tk, n_lat=n_lat, n_ctx=t - n_lat),
        grid_spec=grid_spec,
        out_shape=jax.ShapeDtypeStruct((t, n_q_heads * HEAD_DIM), BF16),
        compiler_params=_params(("arbitrary", "arbitrary"), 48),
        name="gqa_attention",
    )(bounded, qt, k, vt, u)


def _na_kernel(bounded_ref, qt_ref, k0_ref, k1_ref, k2_ref, kc_ref, v0_ref, v1_ref, v2_ref, vc_ref,
               bias_ref, gate_ref, o_ref, *, n_lat_blocks):
    is_latent = pl.program_id(1) < n_lat_blocks
    is_bounded = bounded_ref[0] != 0
    vrows = HEAD_DIM + GQA_ONES_ROWS

    def probabilities(j, latent, bounded):
        hs = slice(j * HEAD_DIM, (j + 1) * HEAD_DIM)
        qt = qt_ref[hs, :]
        scores = [jnp.dot(kc_ref[:, hs], qt, preferred_element_type=F32)]
        if latent:
            k = jnp.concatenate([k0_ref[:, hs], k1_ref[:, hs], k2_ref[:, hs]], axis=0)
            scores.append(jnp.dot(k, qt, preferred_element_type=F32) + bias_ref[j].astype(F32))
        if not bounded:
            m = functools.reduce(jnp.maximum, [jnp.max(s, axis=0, keepdims=True) for s in scores])
            scores = [s - m for s in scores]
        return [jnp.exp2(s).astype(BF16) for s in scores]

    def output(j, probs, latent):
        hs = slice(j * HEAD_DIM, (j + 1) * HEAD_DIM)
        vs = slice(j * vrows, (j + 1) * vrows)
        acc = jnp.dot(vc_ref[vs, :], probs[0], preferred_element_type=F32)
        if latent:
            vt = jnp.concatenate([v0_ref[vs, :], v1_ref[vs, :], v2_ref[vs, :]], axis=1)
            acc = acc + jnp.dot(vt, probs[1], preferred_element_type=F32)
        o = (acc[:HEAD_DIM] / acc[HEAD_DIM:HEAD_DIM + 1]).T
        o_ref[:, hs] = (o * _silu(gate_ref[:, hs].astype(F32))).astype(o_ref.dtype)

    heads = range(qt_ref.shape[0] // HEAD_DIM)
    for latent in (True, False):
        for bounded in (True, False):
            @pl.when(jnp.logical_and(is_latent == latent, is_bounded == bounded))
            def _():
                probs = [probabilities(j, latent, bounded) for j in heads]
                for j in heads:
                    output(j, probs[j], latent)


def _na_bias_table(rel_bias, n_rows):
    c = np.arange(GRID_W)[None, :]
    kj = np.arange(GRID_W)[:, None]
    cs = np.clip(c - NA_KW // 2, 0, GRID_W - NA_KW)
    col_ok = (kj >= cs) & (kj < cs + NA_KW)
    sel_c = (np.arange(2 * NA_KW - 1)[:, None, None] == (kj - c + NA_KW - 1)[None]) & col_ok[None]
    cols = jnp.einsum('hde,ekc->hdkc', rel_bias * math.log2(math.e), jnp.asarray(sel_c, F32),
                      precision=lax.Precision.HIGHEST)
    cols = jnp.where(jnp.asarray(col_ok)[None, None], cols, NEG_INF).astype(BF16)
    neg = jnp.full((rel_bias.shape[0], GRID_W, GRID_W), NEG_INF, BF16)
    tables = []
    for r0, ws in ((0, 0), (NA_Q_ROWS, 0), (n_rows - NA_Q_ROWS, n_rows - NA_WIN_ROWS)):
        key_rows = []
        for i in range(NA_WIN_ROWS):
            blocks = []
            for a in range(NA_Q_ROWS):
                r, kr = r0 + a, ws + i
                rs = min(max(r - NA_KH // 2, 0), n_rows - NA_KH)
                blocks.append(cols[:, kr - r + NA_KH - 1] if rs <= kr < rs + NA_KH else neg)
            key_rows.append(jnp.concatenate(blocks, axis=-1))
        tables.append(jnp.concatenate(key_rows, axis=-2))
    return jnp.stack(tables, axis=0)


def _na_attention(bounded, qt, kn, vt, u, n_heads, off_gate, bias_tbl, n_lat):
    t = u.shape[0]
    tb = NA_Q_ROWS * GRID_W
    n_lat_blocks = n_lat // tb
    ctx_blk = t // tb - 1
    hp = NA_HEADS_PER_STEP
    hw = hp * HEAD_DIM
    vrows = hp * (HEAD_DIM + GQA_ONES_ROWS)
    assert t - n_lat == tb and n_lat_blocks >= 3 and off_gate % hw == 0 and n_heads % hp == 0
    g_blk0 = off_gate // hw

    def win(i, j):
        return jnp.clip(i - 1, 0, n_lat_blocks - 3) + j

    def k_spec(j):
        return pl.BlockSpec((tb, hw), lambda h, i, b: (win(i, j), h))

    def v_spec(j):
        return pl.BlockSpec((vrows, tb), lambda h, i, b: (h, win(i, j)))

    def bias_cls(h, i, b):
        return (jnp.where(i == 0, 0, jnp.where(i >= n_lat_blocks - 1, 2, 1)), h, 0, 0)

    grid_spec = pltpu.PrefetchScalarGridSpec(
        num_scalar_prefetch=1,
        grid=(n_heads // hp, t // tb),
        in_specs=[pl.BlockSpec((hw, tb), lambda h, i, b: (h, i)),
                  k_spec(0), k_spec(1), k_spec(2),
                  pl.BlockSpec((tb, hw), lambda h, i, b: (ctx_blk, h)),
                  v_spec(0), v_spec(1), v_spec(2),
                  pl.BlockSpec((vrows, tb), lambda h, i, b: (h, ctx_blk)),
                  pl.BlockSpec((None, hp, NA_WIN_ROWS * GRID_W, tb), bias_cls),
                  pl.BlockSpec((tb, hw), lambda h, i, b: (i, g_blk0 + h))],
        out_specs=pl.BlockSpec((tb, hw), lambda h, i, b: (i, h)))
    return pl.pallas_call(
        functools.partial(_na_kernel, n_lat_blocks=n_lat_blocks),
        grid_spec=grid_spec,
        out_shape=jax.ShapeDtypeStruct((t, n_heads * HEAD_DIM), BF16),
        compiler_params=_params(("arbitrary", "arbitrary"), 32),
        name="na_attention",
    )(bounded, qt, kn, kn, kn, kn, vt, vt, vt, vt, bias_tbl, u)


def _short_conv_kernel(x_ref, prev_ref, next_ref, w_ref, o_ref, *, tr):
    i = pl.program_id(0)
    n = pl.num_programs(0)
    x = x_ref[...].astype(F32)
    halo = prev_ref.shape[0]
    prev_row = jnp.where(i == 0, 0.0, prev_ref[halo - 1:halo, :].astype(F32))
    next_row = jnp.where(i == n - 1, 0.0, next_ref[0:1, :].astype(F32))
    row = lax.broadcasted_iota(jnp.int32, x.shape, 0)
    up = jnp.where(row == 0, prev_row, pltpu.roll(x, 1, 0))
    down = jnp.where(row == tr - 1, next_row, pltpu.roll(x, tr - 1, 0))
    y = (up * w_ref[0:1, :] + x * w_ref[1:2, :] + down * w_ref[2:3, :]).astype(o_ref.dtype)
    o_ref[...] = y.reshape(o_ref.shape)


def _short_conv(u, width, conv_w, n_lat):
    tr = _largest_tile(n_lat, 1024, 256)
    tc = _largest_tile(width, 1024, 256)
    halo = 16
    hb = tr // halo
    n_tiles = n_lat // tr
    wpad = jnp.zeros((8, width), F32).at[:HY_SHORT].set(conv_w.astype(F32))
    return pl.pallas_call(
        functools.partial(_short_conv_kernel, tr=tr),
        grid=(n_tiles, width // tc),
        in_specs=[pl.BlockSpec((tr, tc), lambda i, j: (i, j)),
                  pl.BlockSpec((halo, tc), lambda i, j: (jnp.maximum(i * hb - 1, 0), j)),
                  pl.BlockSpec((halo, tc), lambda i, j: (jnp.minimum((i + 1) * hb, n_tiles * hb - 1), j)),
                  pl.BlockSpec((8, tc), lambda i, j: (0, j))],
        out_specs=pl.BlockSpec((tr // FFT_N2, FFT_N2, tc), lambda i, j: (i, 0, j)),
        out_shape=jax.ShapeDtypeStruct((n_lat // FFT_N2, FFT_N2, width), BF16),
        compiler_params=_params(("arbitrary", "arbitrary"), 32),
        name="short_conv",
    )(u, u, u, wpad)


def _hy_gate2_kernel(x2_ref, c_ref, y1_ref, d_ref, g_ref, ctx_ref, o_ref, *, n_lat_tiles):
    i = pl.program_id(0)

    @pl.when(i < n_lat_tiles)
    def _():
        rows = lambda ref: ref[...].reshape(o_ref.shape).astype(F32)
        y1 = rows(y1_ref)
        z = rows(x2_ref) * (rows(c_ref) + y1 * d_ref[1:2, :])
        o_ref[...] = (z * _silu(g_ref[...].astype(F32))).astype(o_ref.dtype)

    @pl.when(i >= n_lat_tiles)
    def _():
        o_ref[...] = ctx_ref[...]


def _hy_gate2(vx3, c2, y1, dskip8, u, off_gate, hy_ctx, hy):
    t = u.shape[0]
    n_lat = vx3.shape[0] * FFT_N2
    tr = t - n_lat
    tc = _largest_tile(hy, 512, 256)
    nb = hy // tc
    n_lat_tiles = n_lat // tr
    assert n_lat % tr == 0 and off_gate % tc == 0 and tr % FFT_N2 == 0
    g_blk0 = off_gate // tc
    lat = lambda i: jnp.minimum(i, n_lat_tiles - 1)
    slab = lambda c0: pl.BlockSpec((tr // FFT_N2, FFT_N2, tc), lambda i, j: (lat(i), 0, c0 + j))
    return pl.pallas_call(
        functools.partial(_hy_gate2_kernel, n_lat_tiles=n_lat_tiles),
        grid=(t // tr, nb),
        in_specs=[slab(2 * nb), slab(0), slab(0),
                  pl.BlockSpec((8, tc), lambda i, j: (0, j)),
                  pl.BlockSpec((tr, tc), lambda i, j: (i, g_blk0 + j)),
                  pl.BlockSpec((tr, tc), lambda i, j: (0, j))],
        out_specs=pl.BlockSpec((tr, tc), lambda i, j: (i, j)),
        out_shape=jax.ShapeDtypeStruct((t, hy), BF16),
        compiler_params=_params(("arbitrary", "arbitrary"), 32),
        name="hy_gate2",
    )(vx3, c2, y1, dskip8, u, hy_ctx)


def _filter_hidden_kernel(z_ref, w1_ref, b1_ref, w2_ref, b2_ref, fr_ref, o_ref):
    hp = lax.Precision.HIGHEST
    fr = fr_ref[...]
    h = jnp.sin(fr * (jnp.dot(z_ref[...], w1_ref[...], precision=hp,
                              preferred_element_type=F32) + b1_ref[...]))
    o_ref[...] = jnp.sin(fr * (jnp.dot(h, w2_ref[...], precision=hp,
                                       preferred_element_type=F32) + b2_ref[...]))


def _pad2(a, rows, cols):
    return jnp.zeros((rows, cols), F32).at[:a.shape[0], :a.shape[1]].set(a.astype(F32))


def _filter_hidden(feat, w1, b1, w2, b2, freq):
    r = feat.shape[0]
    tr = _largest_tile(r, 1024, 256)
    full = lambda i: (0, 0)
    return pl.pallas_call(
        _filter_hidden_kernel,
        grid=(r // tr,),
        in_specs=[pl.BlockSpec((tr, LANE), lambda i: (i, 0)),
                  pl.BlockSpec((LANE, LANE), full), pl.BlockSpec((1, LANE), full),
                  pl.BlockSpec((LANE, LANE), full), pl.BlockSpec((1, LANE), full),
                  pl.BlockSpec((1, LANE), full)],
        out_specs=pl.BlockSpec((tr, LANE), lambda i: (i, 0)),
        out_shape=jax.ShapeDtypeStruct((r, LANE), F32),
        compiler_params=_params(("arbitrary",), 32),
        name="filter_hidden",
    )(feat, _pad2(w1, LANE, LANE), _pad2(b1[None], 1, LANE), _pad2(w2, LANE, LANE),
      _pad2(b2[None], 1, LANE), _pad2(freq[None], 1, LANE))


def _two_sided_positions(n):
    idx = jnp.arange(2 * n)
    return jnp.where(idx < n, idx, 2 * n - idx) % n


def _filter_features(n, n_bands):
    pos = _two_sided_positions(n).astype(F32)[:, None]
    t = pos / (n - 1)
    w = (2.0 * math.pi / n) * pos
    bands = jnp.linspace(1e-4, n_bands - 1, n_bands, dtype=F32)[None, :]
    z = jnp.concatenate([t, jnp.cos(bands * w), -jnp.sin(bands * w)], axis=-1)
    return _pad2(z, 2 * n, LANE), t


def _filter_gen_kernel(h_ref, w_ref, t_ref, dec_ref, o_ref, *, tr, n):
    i = pl.program_id(0)
    f = jnp.dot(h_ref[...].astype(BF16), w_ref[...], preferred_element_type=F32)
    f = f * jnp.exp(-t_ref[...] * jnp.abs(dec_ref[0:1, :]))
    row = i * tr + lax.broadcasted_iota(jnp.int32, (tr, 1), 0)
    o_ref[...] = jnp.where(row == n, 0.0, f).astype(o_ref.dtype).reshape(o_ref.shape)


def _filter_gen(hidden, row0, w3p, tcol, decay8, n, order, hy):
    tr = _largest_tile(n, 512, 256)
    tc = _largest_tile(hy, 1024, 256)
    nb = hy // tc
    n_half = n // tr
    hb0 = row0 // tr
    assert row0 % tr == 0
    return pl.pallas_call(
        functools.partial(_filter_gen_kernel, tr=tr, n=n),
        grid=(2 * n // tr, order * nb),
        in_specs=[pl.BlockSpec((tr, LANE), lambda i, j: (hb0 + i, 0)),
                  pl.BlockSpec((LANE, tc), lambda i, j: (0, (j // nb) * 2 * nb + (i // n_half) * nb + j % nb)),
                  pl.BlockSpec((tr, 1), lambda i, j: (i, 0)),
                  pl.BlockSpec((None, 8, tc), lambda i, j: (j // nb, 0, j % nb))],
        out_specs=pl.BlockSpec((tr // FFT_N2, FFT_N2, tc), lambda i, j: (i, 0, j)),
        out_shape=jax.ShapeDtypeStruct((2 * n // FFT_N2, FFT_N2, order * hy), BF16),
        compiler_params=_params(("arbitrary", "arbitrary"), 32),
        name="filter_gen",
    )(hidden, w3p, tcol, decay8)


def _dft_tables(n1, nz):
    n2 = FFT_N2
    n = n1 * n2
    half = n1 // 2
    k1 = jnp.arange(half + 1)
    two_pi = 2.0 * math.pi

    def cs(idx, period):
        ang = (idx % period).astype(F32) * (two_pi / period)
        return jnp.cos(ang), -jnp.sin(ang)

    eye8 = jnp.eye(8, dtype=F32)
    fr, fi = cs(k1[:, None] * jnp.arange(nz)[None, :], n1)
    kf = jnp.stack([fr, fi], axis=1)
    kf = jnp.einsum('kpn,jl->kpjnl', kf, eye8).reshape((half + 1) * 16, nz * 8)
    gr, gi = cs(jnp.arange(half)[:, None] * k1[None, :], n1)
    weight = jnp.where((k1 == 0) | (k1 == half), 1.0, 2.0) / n
    ki = jnp.stack([gr, gi], axis=2) * weight[None, :, None]
    ki = jnp.einsum('nkp,jl->njkpl', ki, eye8).reshape(half * 8, (half + 1) * 16)
    r = jnp.arange(2 * n2)
    part = (r % 16) // 8
    n2_of_r = 8 * (r // 16) + r % 8
    k2 = jnp.arange(n2)
    idx = k2[None, :, None] * n2_of_r[None, None, :] * n1 + k1[:, None, None] * n2_of_r[None, None, :]
    cr, ci = cs(idx, n)
    top = jnp.where(part[None, None, :] == 0, cr, -ci)
    bot = jnp.where(part[None, None, :] == 0, ci, cr)
    m = jnp.concatenate([top, bot], axis=1)
    return kf.astype(BF16), ki.astype(BF16), m.astype(BF16)


def _cross_slab_forward(z_ref, kf_ref, a_scr, nz, nh, ct):
    def body(b, carry):
        z = z_ref[:, pl.ds(pl.multiple_of(16 * b, 16), 16), :].astype(F32)
        for half in range(2):
            g = z[:, 8 * half:8 * half + 8, :].reshape(nz * 8, ct).astype(BF16)
            p = jnp.dot(kf_ref[...], g, preferred_element_type=F32)
            start = pl.multiple_of(32 * b + 16 * half, 16)
            a_scr[:, pl.ds(start, 16), :] = p.astype(BF16).reshape(nh, 16, ct)
        return carry

    lax.fori_loop(0, FFT_N2 // 16, body, 0)


def _slab_block(nh):
    return max(k for k in range(1, 17) if nh % k == 0)


def _fft_fwd_kernel(z_ref, kf_ref, m_ref, o_ref, a_scr, *, nz, nh, kb, ct):
    kblk = pl.program_id(1)

    @pl.when(kblk == 0)
    def _():
        _cross_slab_forward(z_ref, kf_ref, a_scr, nz, nh, ct)

    for k in range(kb):
        x = jnp.dot(m_ref[k], a_scr[kblk * kb + k], preferred_element_type=F32)
        o_ref[k] = x.astype(o_ref.dtype)


def _fft_forward(z3, kf, m, ct=256):
    nz, _, c = z3.shape
    nh = m.shape[0]
    kb = _slab_block(nh)
    return pl.pallas_call(
        functools.partial(_fft_fwd_kernel, nz=nz, nh=nh, kb=kb, ct=ct),
        grid=(c // ct, nh // kb),
        in_specs=[pl.BlockSpec((nz, FFT_N2, ct), lambda j, k: (0, 0, j)),
                  pl.BlockSpec(kf.shape, lambda j, k: (0, 0)),
                  pl.BlockSpec((kb, 2 * FFT_N2, 2 * FFT_N2), lambda j, k: (k, 0, 0))],
        out_specs=pl.BlockSpec((kb, 2 * FFT_N2, ct), lambda j, k: (k, 0, j)),
        out_shape=jax.ShapeDtypeStruct((nh, 2 * FFT_N2, c), BF16),
        scratch_shapes=[pltpu.VMEM((nh, 2 * FFT_N2, ct), BF16)],
        compiler_params=_params(("arbitrary", "arbitrary"), 56),
        name="fft_forward",
    )(z3, kf, m)


def _fft_conv_kernel(z_ref, kf_ref, ki_ref, m_ref, h_ref, *rest, nz, nh, kb, ct, skip_row):
    gated = len(rest) == 4
    o_ref, a_scr = rest[-2:]
    kblk = pl.program_id(1)
    n2 = FFT_N2

    @pl.when(kblk == 0)
    def _():
        _cross_slab_forward(z_ref, kf_ref, a_scr, nz, nh, ct)

    for k in range(kb):
        mk = m_ref[k]
        x = jnp.dot(mk, a_scr[kblk * kb + k], preferred_element_type=F32)
        h = h_ref[k].astype(F32)
        xr, xi, hr, hi = x[:n2], x[n2:], h[:n2], h[n2:]
        y = jnp.concatenate([xr * hr - xi * hi, xr * hi + xi * hr], axis=0).astype(BF16)
        b = lax.dot_general(mk, y, (((0,), (0,)), ((), ())), preferred_element_type=F32)
        a_scr[kblk * kb + k] = b.astype(BF16)

    @pl.when(kblk == pl.num_programs(1) - 1)
    def _():
        def body(b, carry):
            halves = []
            for half in range(2):
                start = pl.multiple_of(32 * b + 16 * half, 16)
                r = a_scr[:, pl.ds(start, 16), :].reshape(nh * 16, ct)
                p = jnp.dot(ki_ref[...], r, preferred_element_type=F32)
                halves.append(p.reshape(nz, 8, ct))
            out = jnp.concatenate(halves, axis=1)
            rows = pl.ds(pl.multiple_of(16 * b, 16), 16)
            if gated:
                x_ref, d_ref = rest[:2]
                z = z_ref[:, rows, :].astype(F32)
                out = x_ref[:, rows, :].astype(F32) * (out + z * d_ref[skip_row:skip_row + 1, :])
            o_ref[:, rows, :] = out.astype(o_ref.dtype)
            return carry

        lax.fori_loop(0, n2 // 16, body, 0)


def _fft_conv(z3, z_blk0, spec, spec_blk0, kf, ki, m, c, gate=None, ct=256):
    nz = z3.shape[0]
    nh = m.shape[0]
    kb = _slab_block(nh)
    assert nh == nz + 1
    in_specs = [pl.BlockSpec((nz, FFT_N2, ct), lambda j, k: (0, 0, z_blk0 + j)),
                pl.BlockSpec(kf.shape, lambda j, k: (0, 0)),
                pl.BlockSpec(ki.shape, lambda j, k: (0, 0)),
                pl.BlockSpec((kb, 2 * FFT_N2, 2 * FFT_N2), lambda j, k: (k, 0, 0)),
                pl.BlockSpec((kb, 2 * FFT_N2, ct), lambda j, k: (k, 0, spec_blk0 + j))]
    operands = [z3, kf, ki, m, spec]
    skip_row = 0
    if gate is not None:
        x3, x_blk0, dskip8, skip_row = gate
        in_specs += [pl.BlockSpec((nz, FFT_N2, ct), lambda j, k: (0, 0, x_blk0 + j)),
                     pl.BlockSpec((8, ct), lambda j, k: (0, j))]
        operands += [x3, dskip8]
    return pl.pallas_call(
        functools.partial(_fft_conv_kernel, nz=nz, nh=nh, kb=kb, ct=ct, skip_row=skip_row),
        grid=(c // ct, nh // kb),
        in_specs=in_specs,
        out_specs=pl.BlockSpec((nz, FFT_N2, ct), lambda j, k: (0, 0, j)),
        out_shape=jax.ShapeDtypeStruct((nz, FFT_N2, c), BF16),
        scratch_shapes=[pltpu.VMEM((nh, 2 * FFT_N2, ct), BF16)],
        compiler_params=_params(("arbitrary", "arbitrary"), 56),
        name="fft_conv",
    )(*operands)


def _ctx_dft_tables(n):
    nn = 2 * n
    k = jnp.arange(nn)
    ang = ((k[:, None] * k[None, :]) % nn).astype(F32) * (2.0 * math.pi / nn)
    cr, ci = jnp.cos(ang), -jnp.sin(ang)
    full = jnp.concatenate([cr, ci], axis=0)
    inv = jnp.concatenate([cr[:n], ci[:n]], axis=1) / nn
    return full[:, :n].astype(BF16), full.astype(BF16), inv.astype(BF16)


def _ctx_hyena_kernel(v_ref, x1_ref, x2_ref, g_ref, cw0_ref, cw1_ref, cw2_ref, hid_ref,
                      w10_ref, w11_ref, w20_ref, w21_ref, t_ref, dec_ref, dsk_ref,
                      ff_ref, ffull_ref, finv_ref, o_ref, *, n):
    row = lax.broadcasted_iota(jnp.int32, (n, 1), 0)

    def sconv(x_ref, w_ref):
        x = x_ref[...].astype(F32)
        up = jnp.where(row == 0, 0.0, pltpu.roll(x, 1, 0))
        down = jnp.where(row == n - 1, 0.0, pltpu.roll(x, n - 1, 0))
        y = up * w_ref[0:1, :] + x * w_ref[1:2, :] + down * w_ref[2:3, :]
        return y.astype(BF16).astype(F32)

    hid = hid_ref[...].astype(BF16)
    tcol = t_ref[...]
    row2 = lax.broadcasted_iota(jnp.int32, (2 * n, 1), 0)

    def spectrum(wf_ref, wb_ref, o):
        fwd = jnp.dot(hid[:n], wf_ref[...], preferred_element_type=F32)
        bwd = jnp.dot(hid[n:], wb_ref[...], preferred_element_type=F32)
        f = jnp.concatenate([fwd, bwd], axis=0) * jnp.exp(-tcol * jnp.abs(dec_ref[o:o + 1, :]))
        f = jnp.where(row2 == n, 0.0, f).astype(BF16)
        return jnp.dot(ffull_ref[...], f, preferred_element_type=F32)

    def conv(z, h):
        x = jnp.dot(ff_ref[...], z.astype(BF16), preferred_element_type=F32)
        h = h.astype(BF16).astype(F32)
        nn = 2 * n
        xr, xi, hr, hi = x[:nn], x[nn:], h[:nn], h[nn:]
        y = jnp.concatenate([xr * hr - xi * hi, xr * hi + xi * hr], axis=0).astype(BF16)
        return jnp.dot(finv_ref[...], y, preferred_element_type=F32)

    v = sconv(v_ref, cw0_ref)
    x1 = sconv(x1_ref, cw1_ref)
    x2 = sconv(x2_ref, cw2_ref)
    c1 = conv(v, spectrum(w10_ref, w11_ref, 0)).astype(BF16).astype(F32)
    y1 = (x1 * (c1 + v * dsk_ref[0:1, :])).astype(BF16).astype(F32)
    c2 = conv(y1, spectrum(w20_ref, w21_ref, 1)).astype(BF16).astype(F32)
    z = x2 * (c2 + y1 * dsk_ref[1:2, :])
    o_ref[...] = (z * _silu(g_ref[...].astype(F32))).astype(o_ref.dtype)


def _ctx_hyena(u, n_lat, hy, order, off_gate, conv8, hidden, hid_row0, w3p, tcol, decay8, dskip8, tables):
    t = u.shape[0]
    n = t - n_lat
    assert order == 2 and n_lat % n == 0 and hid_row0 % (2 * n) == 0
    ct = 256
    nb = hy // ct
    rb = n_lat // n
    ff, ffull, finv = tables
    const = lambda j: (0, 0)
    ublk = lambda c0: pl.BlockSpec((n, ct), lambda j: (rb, c0 + j))
    wblk = lambda c0: pl.BlockSpec((LANE, ct), lambda j: (0, c0 + j))
    cwblk = lambda c0: pl.BlockSpec((8, ct), lambda j: (0, c0 + j))
    return pl.pallas_call(
        functools.partial(_ctx_hyena_kernel, n=n),
        grid=(nb,),
        in_specs=[ublk(0), ublk(nb), ublk(2 * nb), ublk(off_gate // ct),
                  cwblk(0), cwblk(nb), cwblk(2 * nb),
                  pl.BlockSpec((2 * n, LANE), lambda j: (hid_row0 // (2 * n), 0)),
                  wblk(0), wblk(nb), wblk(2 * nb), wblk(3 * nb),
                  pl.BlockSpec((2 * n, 1), const),
                  pl.BlockSpec((8, ct), lambda j: (0, j)),
                  pl.BlockSpec((8, ct), lambda j: (0, j)),
                  pl.BlockSpec(ff.shape, const), pl.BlockSpec(ffull.shape, const),
                  pl.BlockSpec(finv.shape, const)],
        out_specs=pl.BlockSpec((n, ct), lambda j: (0, j)),
        out_shape=jax.ShapeDtypeStruct((n, hy), BF16),
        compiler_params=_params(("arbitrary",), 40),
        name="ctx_hyena",
    )(u, u, u, u, conv8, conv8, conv8, hidden, w3p, w3p, w3p, w3p, tcol, decay8, dskip8,
      ff, ffull, finv)


def _even_layer(x_all, mods_l, g, n_lat, w_in, w_out, conv_w, w1, b1, w2, b2, w3, freq, decay, dskip,
                q_norm, k_norm, rope, dft, ctx_dft, last):
    t, d = x_all.shape
    n_ctx = t - n_lat
    order, hy = decay.shape
    gqa_w = d - hy
    n_q = gqa_w // HEAD_DIM
    n_kv = n_q // 4
    kv_w = n_kv * HEAD_DIM
    off_hy_gate = (order + 1) * hy
    off_q = off_hy_gate + hy
    off_v = off_q + gqa_w + kv_w
    off_at_gate = off_v + kv_w
    scale = HEAD_DIM ** -0.5

    h = _norm_mod(x_all, g, mods_l, n_lat)
    u = _in_proj(h, *w_in)

    q_gain = q_norm.astype(F32) * (scale * math.log2(math.e))
    k_gain = k_norm.astype(F32)
    qt, kn, vt = _attn_prep(u, off_q, off_q + gqa_w, off_v, n_q, n_kv, 1, q_gain, k_gain,
                            rope[0], rope[1], True)
    att = _gqa_attention(_scores_bounded(q_gain, k_gain), qt, kn, vt, u, n_q, n_kv, off_at_gate, n_lat)

    n_bands = (w1.shape[0] - 1) // 2
    feat_l, t_l = _filter_features(n_lat, n_bands)
    feat_c, t_c = _filter_features(n_ctx, n_bands)
    hidden = _filter_hidden(jnp.concatenate([feat_l, feat_c], axis=0), w1, b1, w2, b2, freq)
    w3p = jnp.zeros((LANE, w3.shape[1]), BF16).at[:w3.shape[0]].set(w3.astype(BF16))
    decay8 = jnp.zeros((order, 8, hy), F32).at[:, 0].set(decay.astype(F32))
    dskip8 = jnp.zeros((8, hy), F32).at[:order].set(dskip.astype(F32))
    conv8 = jnp.zeros((8, conv_w.shape[1]), F32).at[:HY_SHORT].set(conv_w.astype(F32))
    kf_pad, kf_full, ki, m = dft
    n1 = 2 * n_lat // FFT_N2
    filt = _filter_gen(hidden, 0, w3p, t_l, decay8, n_lat, order, hy)
    spec = _fft_forward(filt, kf_full, m)

    vx3 = _short_conv(u, (order + 1) * hy, conv_w, n_lat)
    ct = 256
    y1 = _fft_conv(vx3, 0, spec, 0, kf_pad, ki, m, hy, gate=(vx3, hy // ct, dskip8, 0))
    c2 = _fft_conv(y1, 0, spec, hy // ct, kf_pad, ki, m, hy)

    decay_c8 = jnp.zeros((8, hy), F32).at[:order].set(decay.astype(F32))
    hy_ctx = _ctx_hyena(u, n_lat, hy, order, off_hy_gate, conv8, hidden, 2 * n_lat, w3p, t_c,
                        decay_c8, dskip8, ctx_dft)
    hy_mix = _hy_gate2(vx3, c2, y1, dskip8, u, off_hy_gate, hy_ctx, hy)

    return _out_proj(hy_mix, 0, att, 0, *w_out, x_all, mods_l, n_lat, last)


def _odd_layer(x_all, mods_l, g, n_lat, w_in, w_out, q_norm, k_norm, rel_bias, last):
    t, d = x_all.shape
    n_heads = rel_bias.shape[0]
    w = n_heads * HEAD_DIM
    scale = HEAD_DIM ** -0.5
    h = _norm_mod(x_all, g, mods_l, n_lat)
    u = _in_proj(h, *w_in)
    log2e = math.log2(math.e)
    q_gain = q_norm.astype(F32) * (scale * log2e)
    k_gain = k_norm.astype(F32)
    heads_per_step = NA_HEADS_PER_STEP
    unused = jnp.zeros((t, HEAD_DIM), F32)
    qt, kn, vt = _attn_prep(u, 0, w, 2 * w, heads_per_step, heads_per_step, n_heads // heads_per_step,
                            q_gain, k_gain, unused, unused, False)
    bias_tbl = _na_bias_table(rel_bias.astype(F32), n_lat // GRID_W)
    bounded = _scores_bounded(q_gain, k_gain, jnp.max(jnp.abs(rel_bias)).astype(F32) * (log2e * 1.02))
    att = _na_attention(bounded, qt, kn, vt, u, n_heads, 3 * w, bias_tbl, n_lat)
    return _out_proj(att, 0, att, 1, *w_out, x_all, mods_l, n_lat, last)


def kernel(x, c, ctx, c_ctx, norm_g, ada_w, ada_b, e_w_in, e_w_out, hy_conv, hy_w1, hy_b1, hy_w2, hy_b2,
           hy_w3, hy_freq, hy_decay, hy_dskip, gqa_q_norm, gqa_k_norm, o_w_in, o_w_out, na_q_norm,
           na_k_norm, na_rel_bias):
    batch, n_lat, d = x.shape
    n_ctx = ctx.shape[1]
    depth = norm_g.shape[0]
    assert batch == 1 and n_lat % (2 * FFT_N2) == 0 and n_ctx == NA_Q_ROWS * GRID_W

    x_all = jnp.concatenate([x[0], ctx[0]], axis=0).astype(F32)
    cvecs = jnp.zeros((8, d), F32).at[0].set(c[0]).at[1].set(c_ctx)
    mods = _ada_mods(cvecs, ada_w, ada_b)

    rope = _rope_tables(n_lat, n_ctx)
    n1 = 2 * n_lat // FFT_N2
    kf_pad, ki, m = _dft_tables(n1, n1 // 2)
    kf_full, _, _ = _dft_tables(n1, n1)
    dft = (kf_pad, kf_full, ki, m)
    ctx_dft = _ctx_dft_tables(n_ctx)

    for layer in range(depth):
        i = layer // 2
        last = layer == depth - 1
        if layer % 2 == 0:
            x_all = _even_layer(x_all, mods[layer], norm_g[layer], n_lat,
                                (e_w_in, i), (e_w_out, i), hy_conv[i],
                                hy_w1[i], hy_b1[i], hy_w2[i], hy_b2[i], hy_w3[i], hy_freq[i],
                                hy_decay[i], hy_dskip[i], gqa_q_norm[i], gqa_k_norm[i],
                                rope, dft, ctx_dft, last)
        else:
            x_all = _odd_layer(x_all, mods[layer], norm_g[layer], n_lat,
                               (o_w_in, i), (o_w_out, i),
                               na_q_norm[i], na_k_norm[i], na_rel_bias[i], last)
    return x_all[None]
```

```python
import functools
import math

import numpy as np
import jax
import jax.numpy as jnp
from jax import lax
from jax.experimental import pallas as pl
from jax.experimental.pallas import tpu as pltpu

F32 = jnp.float32
BF16 = jnp.bfloat16

HEAD_DIM = 128
GRID_W = 64
EPS = 1e-6
ROPE_THETA = 10000.0
NA_KH = 8
NA_KW = 16
NA_Q_ROWS = 4
NA_WIN_ROWS = 12
NA_HEADS_PER_STEP = 8
HY_SHORT = 3
FFT_N2 = 128
NEG_INF = -1e30

V7X_VMEM_BYTES = 64 * 1024 * 1024
LANE = 128


def _params(semantics, vmem_mb):
    assert vmem_mb * 1024 * 1024 < V7X_VMEM_BYTES
    return pltpu.CompilerParams(dimension_semantics=semantics,
                                vmem_limit_bytes=vmem_mb * 1024 * 1024)


def _largest_tile(n, target, quantum):
    best = None
    t = quantum
    while t <= min(n, target):
        if n % t == 0:
            best = t
        t += quantum
    assert best is not None, (n, target, quantum)
    return best


def _silu(x):
    return x * jax.nn.sigmoid(x)


def _mods_kernel(c_ref, w_ref, b_ref, o_ref):
    s = _silu(c_ref[...]).astype(BF16)
    o_ref[...] = jnp.dot(s, w_ref[...].astype(BF16), preferred_element_type=F32) + b_ref[...]


def _ada_mods(cvecs, ada_w, ada_b):
    depth, d, n = ada_w.shape
    tn = _largest_tile(n, 512, LANE)
    return pl.pallas_call(
        _mods_kernel,
        grid=(depth, n // tn),
        in_specs=[pl.BlockSpec((8, d), lambda l, j: (0, 0)),
                  pl.BlockSpec((None, d, tn), lambda l, j: (l, 0, j)),
                  pl.BlockSpec((None, 1, tn), lambda l, j: (l, 0, j))],
        out_specs=pl.BlockSpec((None, 8, tn), lambda l, j: (l, 0, j)),
        out_shape=jax.ShapeDtypeStruct((depth, 8, n), F32),
        compiler_params=_params(("arbitrary", "arbitrary"), 40),
        name="ada_mods",
    )(cvecs, ada_w, ada_b.reshape(depth, 1, n))


def _norm_mod_kernel(x_ref, g_ref, sh_ref, sc_ref, o_ref, *, n_lat_tiles):
    x = x_ref[...]
    y = x * lax.rsqrt(jnp.mean(x * x, axis=-1, keepdims=True) + EPS) * g_ref[...]
    is_ctx = pl.program_id(0) >= n_lat_tiles
    sh = jnp.where(is_ctx, sh_ref[1:2, :], sh_ref[0:1, :])
    sc = jnp.where(is_ctx, sc_ref[1:2, :], sc_ref[0:1, :])
    o_ref[...] = (y * (1.0 + sc) + sh).astype(o_ref.dtype)


def _norm_mod(x_all, g, mods_l, n_lat):
    t, d = x_all.shape
    tr = 256
    assert t % tr == 0 and n_lat % tr == 0
    return pl.pallas_call(
        functools.partial(_norm_mod_kernel, n_lat_tiles=n_lat // tr),
        grid=(t // tr,),
        in_specs=[pl.BlockSpec((tr, d), lambda i: (i, 0)),
                  pl.BlockSpec((1, d), lambda i: (0, 0)),
                  pl.BlockSpec((8, d), lambda i: (0, 0)),
                  pl.BlockSpec((8, d), lambda i: (0, 1))],
        out_specs=pl.BlockSpec((tr, d), lambda i: (i, 0)),
        out_shape=jax.ShapeDtypeStruct((t, d), BF16),
        compiler_params=_params(("arbitrary",), 40),
        name="norm_mod",
    )(x_all, g.reshape(1, d), mods_l, mods_l)


def _in_proj_kernel(a_ref, w_ref, o_ref):
    w = w_ref[...].astype(BF16)
    o_ref[...] = jnp.dot(a_ref[...], w, preferred_element_type=F32).astype(o_ref.dtype)


def _in_proj(h, w_all, layer_idx):
    m, k = h.shape
    n = w_all.shape[2]
    tm = _largest_tile(m, 1408, 128)
    tn = _largest_tile(n, 512, 256)
    return pl.pallas_call(
        _in_proj_kernel,
        grid=(m // tm, n // tn),
        in_specs=[pl.BlockSpec((tm, k), lambda i, j: (i, 0)),
                  pl.BlockSpec((None, k, tn), lambda i, j: (layer_idx, 0, j))],
        out_specs=pl.BlockSpec((tm, tn), lambda i, j: (i, j)),
        out_shape=jax.ShapeDtypeStruct((m, n), BF16),
        compiler_params=_params(("arbitrary", "arbitrary"), 56),
        name="in_proj",
    )(h, w_all)


def _out_proj_kernel(a1_ref, a2_ref, w1_ref, w2_ref, x_ref, gate_ref, o_ref, wb1, wb2, *, tm, n_lat):
    i = pl.program_id(1)

    @pl.when(i == 0)
    def _():
        wb1[...] = w1_ref[...].astype(BF16)
        wb2[...] = w2_ref[...].astype(BF16)

    y = jnp.dot(a1_ref[...], wb1[...], preferred_element_type=F32)
    y = y + jnp.dot(a2_ref[...], wb2[...], preferred_element_type=F32)
    rows = i * tm + lax.broadcasted_iota(jnp.int32, (tm, 1), 0)
    gate = jnp.where(rows >= n_lat, gate_ref[1:2, :], gate_ref[0:1, :])
    o_ref[...] = x_ref[...] + gate * y


def _out_proj(a1, a1_blk, a2, a2_blk, w_all, layer_idx, x_all, mods_l, n_lat, latent_only):
    d = x_all.shape[1]
    t = n_lat if latent_only else x_all.shape[0]
    kh = w_all.shape[1] // 2
    tm = _largest_tile(t, 1024, 256)
    tn = _largest_tile(d, 512, 256)
    gate_blk0 = 2 * d // tn
    return pl.pallas_call(
        functools.partial(_out_proj_kernel, tm=tm, n_lat=n_lat),
        grid=(d // tn, t // tm),
        in_specs=[pl.BlockSpec((tm, kh), lambda j, i: (i, a1_blk)),
                  pl.BlockSpec((tm, kh), lambda j, i: (i, a2_blk)),
                  pl.BlockSpec((None, kh, tn), lambda j, i: (layer_idx, 0, j)),
                  pl.BlockSpec((None, kh, tn), lambda j, i: (layer_idx, 1, j)),
                  pl.BlockSpec((tm, tn), lambda j, i: (i, j)),
                  pl.BlockSpec((8, tn), lambda j, i: (0, gate_blk0 + j))],
        out_specs=pl.BlockSpec((tm, tn), lambda j, i: (i, j)),
        out_shape=jax.ShapeDtypeStruct((t, d), F32),
        scratch_shapes=[pltpu.VMEM((kh, tn), BF16), pltpu.VMEM((kh, tn), BF16)],
        compiler_params=_params(("arbitrary", "arbitrary"), 48),
        name="out_proj",
    )(a1, a2, w_all, w_all, x_all, mods_l)


def _rope_tables(n_lat, n_ctx):
    quarter = HEAD_DIM // 4
    inv = ROPE_THETA ** (-jnp.arange(quarter, dtype=F32) / quarter)
    tok = jnp.arange(n_lat)
    rows = (tok // GRID_W).astype(F32)[:, None] * inv[None]
    cols = (tok % GRID_W).astype(F32)[:, None] * inv[None]
    cos = jnp.concatenate([jnp.cos(rows), jnp.cos(rows), jnp.cos(cols), jnp.cos(cols)], axis=-1)
    sin = jnp.concatenate([-jnp.sin(rows), jnp.sin(rows), -jnp.sin(cols), jnp.sin(cols)], axis=-1)
    cos = jnp.concatenate([cos, jnp.ones((n_ctx, HEAD_DIM), F32)], axis=0)
    sin = jnp.concatenate([sin, jnp.zeros((n_ctx, HEAD_DIM), F32)], axis=0)
    return cos, sin


GQA_ONES_ROWS = 16


def _attn_prep_kernel(q_ref, k_ref, v_ref, g_ref, cos_ref, sin_ref, qt_ref, ko_ref, vt_ref,
                      *, n_q, n_kv, rope):
    if rope:
        cos = cos_ref[...]
        sin = sin_ref[...]
        lane = lax.broadcasted_iota(jnp.int32, cos.shape, 1)
        first_quarter = (lane % 64) < 32

    def norm_rope(x, g):
        y = x * lax.rsqrt(jnp.mean(x * x, axis=-1, keepdims=True) + EPS) * g
        if not rope:
            return y
        partner = jnp.where(first_quarter, pltpu.roll(y, HEAD_DIM - 32, 1), pltpu.roll(y, 32, 1))
        return y * cos + partner * sin

    for h in range(n_q):
        sl = slice(h * HEAD_DIM, (h + 1) * HEAD_DIM)
        qt_ref[sl, :] = norm_rope(q_ref[:, sl].astype(F32), g_ref[:, sl]).T.astype(qt_ref.dtype)
    for h in range(n_kv):
        sl = slice(h * HEAD_DIM, (h + 1) * HEAD_DIM)
        gk = g_ref[:, (n_q + h) * HEAD_DIM:(n_q + h + 1) * HEAD_DIM]
        ko_ref[:, sl] = norm_rope(k_ref[:, sl].astype(F32), gk).astype(ko_ref.dtype)
        r0 = h * (HEAD_DIM + GQA_ONES_ROWS)
        vt_ref[r0:r0 + HEAD_DIM, :] = v_ref[:, sl].astype(F32).T.astype(vt_ref.dtype)
        vt_ref[r0 + HEAD_DIM:r0 + HEAD_DIM + GQA_ONES_ROWS, :] = jnp.ones(
            (GQA_ONES_ROWS, vt_ref.shape[1]), vt_ref.dtype)


def _attn_prep(u, off_q, off_k, off_v, n_q, n_kv, n_groups, q_gain, k_gain, cos_t, sin_t, rope):
    t = u.shape[0]
    qw, kw = n_q * HEAD_DIM, n_kv * HEAD_DIM
    tr = _largest_tile(t, 768, 256)
    vrows = n_kv * (HEAD_DIM + GQA_ONES_ROWS)
    assert off_q % qw == 0 and off_k % kw == 0 and off_v % kw == 0 and t % tr == 0
    q_blk, k_blk, v_blk = off_q // qw, off_k // kw, off_v // kw
    gvec = jnp.concatenate([jnp.tile(q_gain, n_q), jnp.tile(k_gain, n_kv)])[None]
    return pl.pallas_call(
        functools.partial(_attn_prep_kernel, n_q=n_q, n_kv=n_kv, rope=rope),
        grid=(t // tr, n_groups),
        in_specs=[pl.BlockSpec((tr, qw), lambda i, j: (i, q_blk + j)),
                  pl.BlockSpec((tr, kw), lambda i, j: (i, k_blk + j)),
                  pl.BlockSpec((tr, kw), lambda i, j: (i, v_blk + j)),
                  pl.BlockSpec((1, qw + kw), lambda i, j: (0, 0)),
                  pl.BlockSpec((tr, HEAD_DIM), lambda i, j: (i, 0)),
                  pl.BlockSpec((tr, HEAD_DIM), lambda i, j: (i, 0))],
        out_specs=[pl.BlockSpec((qw, tr), lambda i, j: (j, i)),
                   pl.BlockSpec((tr, kw), lambda i, j: (i, j)),
                   pl.BlockSpec((vrows, tr), lambda i, j: (j, i))],
        out_shape=[jax.ShapeDtypeStruct((n_groups * qw, t), BF16),
                   jax.ShapeDtypeStruct((t, n_groups * kw), BF16),
                   jax.ShapeDtypeStruct((n_groups * vrows, t), BF16)],
        compiler_params=_params(("arbitrary", "arbitrary"), 48),
        name="attn_prep",
    )(u, u, u, gvec, cos_t, sin_t)


def _gqa_kernel(bounded_ref, qt_ref, k_ref, vt_ref, gate_ref, o_ref, m_s, acc_s,
                *, group, tq, tk, n_lat, n_ctx):
    acc_s[...] = jnp.zeros(acc_s.shape, F32)

    def chunk_bounded(kc, vtc):
        ps = []
        for j in range(group):
            qt = qt_ref[j * HEAD_DIM:(j + 1) * HEAD_DIM, :]
            s = jnp.dot(kc, qt, preferred_element_type=F32)
            ps.append(jnp.exp2(s).astype(BF16))
        for j in range(group):
            acc_s[j] += jnp.dot(vtc, ps[j], preferred_element_type=F32)

    def chunk_online(kc, vtc):
        for j in range(group):
            qt = qt_ref[j * HEAD_DIM:(j + 1) * HEAD_DIM, :]
            s = jnp.dot(kc, qt, preferred_element_type=F32)
            m = m_s[j]
            m_new = jnp.maximum(m, jnp.max(s, axis=0, keepdims=True))
            p = jnp.exp2(s - m_new).astype(BF16)
            acc_s[j] = jnp.exp2(m - m_new) * acc_s[j] + jnp.dot(vtc, p, preferred_element_type=F32)
            m_s[j] = m_new

    def sweep(chunk):
        @pl.when(pl.program_id(1) * tq < n_lat)
        def _():
            def body(c, carry):
                r0 = pl.multiple_of(c * tk, tk)
                chunk(k_ref[pl.ds(r0, tk), :], vt_ref[:, pl.ds(r0, tk)])
                return carry

            lax.fori_loop(0, n_lat // tk, body, 0)

        chunk(k_ref[n_lat:n_lat + n_ctx, :], vt_ref[:, n_lat:n_lat + n_ctx])

    @pl.when(bounded_ref[0] != 0)
    def _():
        sweep(chunk_bounded)

    @pl.when(bounded_ref[0] == 0)
    def _():
        m_s[...] = jnp.full(m_s.shape, NEG_INF, F32)
        sweep(chunk_online)

    for j in range(group):
        sl = slice(j * HEAD_DIM, (j + 1) * HEAD_DIM)
        acc = acc_s[j]
        o = (acc[:HEAD_DIM] / acc[HEAD_DIM:HEAD_DIM + 1]).T
        o_ref[:, sl] = (o * _silu(gate_ref[:, sl].astype(F32))).astype(o_ref.dtype)


SCORE_BOUND_LOG2 = 60.0


def _scores_bounded(q_gain, k_gain, extra=0.0):
    bound = HEAD_DIM * jnp.max(jnp.abs(q_gain)) * jnp.max(jnp.abs(k_gain)) * 1.02 + extra
    return (bound <= SCORE_BOUND_LOG2).astype(jnp.int32).reshape(1)


def _gqa_attention(bounded, qt, k, vt, u, n_q_heads, n_kv_heads, off_gate, n_lat):
    t = u.shape[0]
    group = n_q_heads // n_kv_heads
    gw = group * HEAD_DIM
    tq = 256
    tk = _largest_tile(n_lat, 1024, 256)
    vrows = HEAD_DIM + GQA_ONES_ROWS
    assert t % tq == 0 and n_lat % tq == 0 and off_gate % gw == 0
    g_blk0 = off_gate // gw
    grid_spec = pltpu.PrefetchScalarGridSpec(
        num_scalar_prefetch=1,
        grid=(n_kv_heads, t // tq),
        in_specs=[pl.BlockSpec((gw, tq), lambda g, i, b: (g, i)),
                  pl.BlockSpec((t, HEAD_DIM), lambda g, i, b: (0, g)),
                  pl.BlockSpec((vrows, t), lambda g, i, b: (g, 0)),
                  pl.BlockSpec((tq, gw), lambda g, i, b: (i, g_blk0 + g))],
        out_specs=pl.BlockSpec((tq, gw), lambda g, i, b: (i, g)),
        scratch_shapes=[pltpu.VMEM((group, 1, tq), F32),
                        pltpu.VMEM((group, vrows, tq), F32)])
    return pl.pallas_call(
        functools.partial(_gqa_kernel, group=group, tq=tq, tk=tk, n_lat=n_lat, n_ctx=t - n_lat),
        grid_spec=grid_spec,
        out_shape=jax.ShapeDtypeStruct((t, n_q_heads * HEAD_DIM), BF16),
        compiler_params=_params(("arbitrary", "arbitrary"), 48),
        name="gqa_attention",
    )(bounded, qt, k, vt, u)


def _na_kernel(bounded_ref, qt_ref, k0_ref, k1_ref, k2_ref, kc_ref, v0_ref, v1_ref, v2_ref, vc_ref,
               bias_ref, gate_ref, o_ref, *, n_lat_blocks):
    is_latent = pl.program_id(1) < n_lat_blocks
    is_bounded = bounded_ref[0] != 0
    vrows = HEAD_DIM + GQA_ONES_ROWS

    def probabilities(j, latent, bounded):
        hs = slice(j * HEAD_DIM, (j + 1) * HEAD_DIM)
        qt = qt_ref[hs, :]
        scores = [jnp.dot(kc_ref[:, hs], qt, preferred_element_type=F32)]
        if latent:
            k = jnp.concatenate([k0_ref[:, hs], k1_ref[:, hs], k2_ref[:, hs]], axis=0)
            scores.append(jnp.dot(k, qt, preferred_element_type=F32) + bias_ref[j].astype(F32))
        if not bounded:
            m = functools.reduce(jnp.maximum, [jnp.max(s, axis=0, keepdims=True) for s in scores])
            scores = [s - m for s in scores]
        return [jnp.exp2(s).astype(BF16) for s in scores]

    def output(j, probs, latent):
        hs = slice(j * HEAD_DIM, (j + 1) * HEAD_DIM)
        vs = slice(j * vrows, (j + 1) * vrows)
        acc = jnp.dot(vc_ref[vs, :], probs[0], preferred_element_type=F32)
        if latent:
            vt = jnp.concatenate([v0_ref[vs, :], v1_ref[vs, :], v2_ref[vs, :]], axis=1)
            acc = acc + jnp.dot(vt, probs[1], preferred_element_type=F32)
        o = (acc[:HEAD_DIM] / acc[HEAD_DIM:HEAD_DIM + 1]).T
        o_ref[:, hs] = (o * _silu(gate_ref[:, hs].astype(F32))).astype(o_ref.dtype)

    heads = range(qt_ref.shape[0] // HEAD_DIM)
    for latent in (True, False):
        for bounded in (True, False):
            @pl.when(jnp.logical_and(is_latent == latent, is_bounded == bounded))
            def _():
                probs = [probabilities(j, latent, bounded) for j in heads]
                for j in heads:
                    output(j, probs[j], latent)


def _na_bias_table(rel_bias, n_rows):
    c = np.arange(GRID_W)[None, :]
    kj = np.arange(GRID_W)[:, None]
    cs = np.clip(c - NA_KW // 2, 0, GRID_W - NA_KW)
    col_ok = (kj >= cs) & (kj < cs + NA_KW)
    sel_c = (np.arange(2 * NA_KW - 1)[:, None, None] == (kj - c + NA_KW - 1)[None]) & col_ok[None]
    cols = jnp.einsum('hde,ekc->hdkc', rel_bias * math.log2(math.e), jnp.asarray(sel_c, F32),
                      precision=lax.Precision.HIGHEST)
    cols = jnp.where(jnp.asarray(col_ok)[None, None], cols, NEG_INF).astype(BF16)
    neg = jnp.full((rel_bias.shape[0], GRID_W, GRID_W), NEG_INF, BF16)
    tables = []
    for r0, ws in ((0, 0), (NA_Q_ROWS, 0), (n_rows - NA_Q_ROWS, n_rows - NA_WIN_ROWS)):
        key_rows = []
        for i in range(NA_WIN_ROWS):
            blocks = []
            for a in range(NA_Q_ROWS):
                r, kr = r0 + a, ws + i
                rs = min(max(r - NA_KH // 2, 0), n_rows - NA_KH)
                blocks.append(cols[:, kr - r + NA_KH - 1] if rs <= kr < rs + NA_KH else neg)
            key_rows.append(jnp.concatenate(blocks, axis=-1))
        tables.append(jnp.concatenate(key_rows, axis=-2))
    return jnp.stack(tables, axis=0)


def _na_attention(bounded, qt, kn, vt, u, n_heads, off_gate, bias_tbl, n_lat):
    t = u.shape[0]
    tb = NA_Q_ROWS * GRID_W
    n_lat_blocks = n_lat // tb
    ctx_blk = t // tb - 1
    hp = NA_HEADS_PER_STEP
    hw = hp * HEAD_DIM
    vrows = hp * (HEAD_DIM + GQA_ONES_ROWS)
    assert t - n_lat == tb and n_lat_blocks >= 3 and off_gate % hw == 0 and n_heads % hp == 0
    g_blk0 = off_gate // hw

    def win(i, j):
        return jnp.clip(i - 1, 0, n_lat_blocks - 3) + j

    def k_spec(j):
        return pl.BlockSpec((tb, hw), lambda h, i, b: (win(i, j), h))

    def v_spec(j):
        return pl.BlockSpec((vrows, tb), lambda h, i, b: (h, win(i, j)))

    def bias_cls(h, i, b):
        return (jnp.where(i == 0, 0, jnp.where(i >= n_lat_blocks - 1, 2, 1)), h, 0, 0)

    grid_spec = pltpu.PrefetchScalarGridSpec(
        num_scalar_prefetch=1,
        grid=(n_heads // hp, t // tb),
        in_specs=[pl.BlockSpec((hw, tb), lambda h, i, b: (h, i)),
                  k_spec(0), k_spec(1), k_spec(2),
                  pl.BlockSpec((tb, hw), lambda h, i, b: (ctx_blk, h)),
                  v_spec(0), v_spec(1), v_spec(2),
                  pl.BlockSpec((vrows, tb), lambda h, i, b: (h, ctx_blk)),
                  pl.BlockSpec((None, hp, NA_WIN_ROWS * GRID_W, tb), bias_cls),
                  pl.BlockSpec((tb, hw), lambda h, i, b: (i, g_blk0 + h))],
        out_specs=pl.BlockSpec((tb, hw), lambda h, i, b: (i, h)))
    return pl.pallas_call(
        functools.partial(_na_kernel, n_lat_blocks=n_lat_blocks),
        grid_spec=grid_spec,
        out_shape=jax.ShapeDtypeStruct((t, n_heads * HEAD_DIM), BF16),
        compiler_params=_params(("arbitrary", "arbitrary"), 32),
        name="na_attention",
    )(bounded, qt, kn, kn, kn, kn, vt, vt, vt, vt, bias_tbl, u)


def _short_conv_kernel(x_ref, prev_ref, next_ref, w_ref, o_ref, *, tr):
    i = pl.program_id(0)
    n = pl.num_programs(0)
    x = x_ref[...].astype(F32)
    halo = prev_ref.shape[0]
    prev_row = jnp.where(i == 0, 0.0, prev_ref[halo - 1:halo, :].astype(F32))
    next_row = jnp.where(i == n - 1, 0.0, next_ref[0:1, :].astype(F32))
    row = lax.broadcasted_iota(jnp.int32, x.shape, 0)
    up = jnp.where(row == 0, prev_row, pltpu.roll(x, 1, 0))
    down = jnp.where(row == tr - 1, next_row, pltpu.roll(x, tr - 1, 0))
    y = (up * w_ref[0:1, :] + x * w_ref[1:2, :] + down * w_ref[2:3, :]).astype(o_ref.dtype)
    o_ref[...] = y.reshape(o_ref.shape)


def _short_conv(u, width, conv_w, n_lat):
    tr = _largest_tile(n_lat, 1024, 256)
    tc = _largest_tile(width, 1024, 256)
    halo = 16
    hb = tr // halo
    n_tiles = n_lat // tr
    wpad = jnp.zeros((8, width), F32).at[:HY_SHORT].set(conv_w.astype(F32))
    return pl.pallas_call(
        functools.partial(_short_conv_kernel, tr=tr),
        grid=(n_tiles, width // tc),
        in_specs=[pl.BlockSpec((tr, tc), lambda i, j: (i, j)),
                  pl.BlockSpec((halo, tc), lambda i, j: (jnp.maximum(i * hb - 1, 0), j)),
                  pl.BlockSpec((halo, tc), lambda i, j: (jnp.minimum((i + 1) * hb, n_tiles * hb - 1), j)),
                  pl.BlockSpec((8, tc), lambda i, j: (0, j))],
        out_specs=pl.BlockSpec((tr // FFT_N2, FFT_N2, tc), lambda i, j: (i, 0, j)),
        out_shape=jax.ShapeDtypeStruct((n_lat // FFT_N2, FFT_N2, width), BF16),
        compiler_params=_params(("arbitrary", "arbitrary"), 32),
        name="short_conv",
    )(u, u, u, wpad)


def _hy_gate2_kernel(x2_ref, c_ref, y1_ref, d_ref, g_ref, ctx_ref, o_ref, *, n_lat_tiles):
    i = pl.program_id(0)

    @pl.when(i < n_lat_tiles)
    def _():
        rows = lambda ref: ref[...].reshape(o_ref.shape).astype(F32)
        y1 = rows(y1_ref)
        z = rows(x2_ref) * (rows(c_ref) + y1 * d_ref[1:2, :])
        o_ref[...] = (z * _silu(g_ref[...].astype(F32))).astype(o_ref.dtype)

    @pl.when(i >= n_lat_tiles)
    def _():
        o_ref[...] = ctx_ref[...]


def _hy_gate2(vx3, c2, y1, dskip8, u, off_gate, hy_ctx, hy):
    t = u.shape[0]
    n_lat = vx3.shape[0] * FFT_N2
    tr = t - n_lat
    tc = _largest_tile(hy, 2048, 256)
    nb = hy // tc
    n_lat_tiles = n_lat // tr
    assert n_lat % tr == 0 and off_gate % tc == 0 and tr % FFT_N2 == 0
    g_blk0 = off_gate // tc
    lat = lambda i: jnp.minimum(i, n_lat_tiles - 1)
    slab = lambda c0: pl.BlockSpec((tr // FFT_N2, FFT_N2, tc), lambda i, j: (lat(i), 0, c0 + j))
    return pl.pallas_call(
        functools.partial(_hy_gate2_kernel, n_lat_tiles=n_lat_tiles),
        grid=(t // tr, nb),
        in_specs=[slab(2 * nb), slab(0), slab(0),
                  pl.BlockSpec((8, tc), lambda i, j: (0, j)),
                  pl.BlockSpec((tr, tc), lambda i, j: (i, g_blk0 + j)),
                  pl.BlockSpec((tr, tc), lambda i, j: (0, j))],
        out_specs=pl.BlockSpec((tr, tc), lambda i, j: (i, j)),
        out_shape=jax.ShapeDtypeStruct((t, hy), BF16),
        compiler_params=_params(("arbitrary", "arbitrary"), 32),
        name="hy_gate2",
    )(vx3, c2, y1, dskip8, u, hy_ctx)


def _filter_hidden_kernel(z_ref, w1_ref, b1_ref, w2_ref, b2_ref, fr_ref, o_ref):
    hp = lax.Precision.HIGHEST
    fr = fr_ref[...]
    h = jnp.sin(fr * (jnp.dot(z_ref[...], w1_ref[...], precision=hp,
                              preferred_element_type=F32) + b1_ref[...]))
    o_ref[...] = jnp.sin(fr * (jnp.dot(h, w2_ref[...], precision=hp,
                                       preferred_element_type=F32) + b2_ref[...]))


def _pad2(a, rows, cols):
    return jnp.zeros((rows, cols), F32).at[:a.shape[0], :a.shape[1]].set(a.astype(F32))


def _filter_hidden(feat, w1, b1, w2, b2, freq):
    r = feat.shape[0]
    tr = _largest_tile(r, 1024, 256)
    full = lambda i: (0, 0)
    return pl.pallas_call(
        _filter_hidden_kernel,
        grid=(r // tr,),
        in_specs=[pl.BlockSpec((tr, LANE), lambda i: (i, 0)),
                  pl.BlockSpec((LANE, LANE), full), pl.BlockSpec((1, LANE), full),
                  pl.BlockSpec((LANE, LANE), full), pl.BlockSpec((1, LANE), full),
                  pl.BlockSpec((1, LANE), full)],
        out_specs=pl.BlockSpec((tr, LANE), lambda i: (i, 0)),
        out_shape=jax.ShapeDtypeStruct((r, LANE), F32),
        compiler_params=_params(("arbitrary",), 32),
        name="filter_hidden",
    )(feat, _pad2(w1, LANE, LANE), _pad2(b1[None], 1, LANE), _pad2(w2, LANE, LANE),
      _pad2(b2[None], 1, LANE), _pad2(freq[None], 1, LANE))


def _two_sided_positions(n):
    idx = jnp.arange(2 * n)
    return jnp.where(idx < n, idx, 2 * n - idx) % n


def _filter_features(n, n_bands):
    pos = _two_sided_positions(n).astype(F32)[:, None]
    t = pos / (n - 1)
    w = (2.0 * math.pi / n) * pos
    bands = jnp.linspace(1e-4, n_bands - 1, n_bands, dtype=F32)[None, :]
    z = jnp.concatenate([t, jnp.cos(bands * w), -jnp.sin(bands * w)], axis=-1)
    return _pad2(z, 2 * n, LANE), t


def _filter_gen_kernel(h_ref, w_ref, t_ref, dec_ref, o_ref, *, tr, n):
    i = pl.program_id(0)
    f = jnp.dot(h_ref[...].astype(BF16), w_ref[...], preferred_element_type=F32)
    f = f * jnp.exp(-t_ref[...] * jnp.abs(dec_ref[0:1, :]))
    row = i * tr + lax.broadcasted_iota(jnp.int32, (tr, 1), 0)
    o_ref[...] = jnp.where(row == n, 0.0, f).astype(o_ref.dtype).reshape(o_ref.shape)


def _filter_gen(hidden, row0, w3p, tcol, decay8, n, order, hy):
    tr = _largest_tile(n, 1024, 256)
    tc = _largest_tile(hy, 1024, 256)
    nb = hy // tc
    n_half = n // tr
    hb0 = row0 // tr
    assert row0 % tr == 0
    return pl.pallas_call(
        functools.partial(_filter_gen_kernel, tr=tr, n=n),
        grid=(2 * n // tr, order * nb),
        in_specs=[pl.BlockSpec((tr, LANE), lambda i, j: (hb0 + i, 0)),
                  pl.BlockSpec((LANE, tc), lambda i, j: (0, (j // nb) * 2 * nb + (i // n_half) * nb + j % nb)),
                  pl.BlockSpec((tr, 1), lambda i, j: (i, 0)),
                  pl.BlockSpec((None, 8, tc), lambda i, j: (j // nb, 0, j % nb))],
        out_specs=pl.BlockSpec((tr // FFT_N2, FFT_N2, tc), lambda i, j: (i, 0, j)),
        out_shape=jax.ShapeDtypeStruct((2 * n // FFT_N2, FFT_N2, order * hy), BF16),
        compiler_params=_params(("arbitrary", "arbitrary"), 32),
        name="filter_gen",
    )(hidden, w3p, tcol, decay8)


def _dft_tables(n1):
    n2 = FFT_N2
    n = n1 * n2
    half = n1 // 2
    nz = half
    k1 = jnp.concatenate([jnp.arange(0, half + 1, 2), jnp.arange(1, half + 1, 2)])
    two_pi = 2.0 * math.pi

    def cs(idx, period):
        ang = (idx % period).astype(F32) * (two_pi / period)
        return jnp.cos(ang), -jnp.sin(ang)

    eye8 = jnp.eye(8, dtype=F32)
    fr, fi = cs(k1[:, None] * jnp.arange(nz)[None, :], n1)
    kf = jnp.stack([fr, fi], axis=1)
    kf = jnp.einsum('kpn,jl->kpjnl', kf, eye8).reshape((half + 1) * 16, nz * 8)
    gr, gi = cs(jnp.arange(half)[:, None] * k1[None, :], n1)
    weight = jnp.where((k1 == 0) | (k1 == half), 1.0, 2.0) / n
    ki = jnp.stack([gr, gi], axis=2) * weight[None, :, None]
    ki = jnp.einsum('nkp,jl->njkpl', ki, eye8).reshape(half * 8, (half + 1) * 16)
    r = jnp.arange(2 * n2)
    part = (r % 16) // 8
    n2_of_r = 8 * (r // 16) + r % 8
    k2 = jnp.arange(n2)
    idx = k2[None, :, None] * n2_of_r[None, None, :] * n1 + k1[:, None, None] * n2_of_r[None, None, :]
    cr, ci = cs(idx, n)
    top = jnp.where(part[None, None, :] == 0, cr, -ci)
    bot = jnp.where(part[None, None, :] == 0, ci, cr)
    m = jnp.concatenate([top, bot], axis=1)
    return kf.astype(BF16), ki.astype(BF16), m.astype(BF16)


def _cross_slab_forward(z_ref, kf_ref, a_scr, nz, nh, ct):
    def body(b, carry):
        z = z_ref[:, pl.ds(pl.multiple_of(16 * b, 16), 16), :].astype(F32)
        for half in range(2):
            g = z[:, 8 * half:8 * half + 8, :].reshape(nz * 8, ct).astype(BF16)
            p = jnp.dot(kf_ref[...], g, preferred_element_type=F32)
            start = pl.multiple_of(32 * b + 16 * half, 16)
            a_scr[:, pl.ds(start, 16), :] = p.astype(BF16).reshape(nh, 16, ct)
        return carry

    lax.fori_loop(0, FFT_N2 // 16, body, 0)


def _slab_block(nh):
    return max(k for k in range(1, 17) if nh % k == 0)


def _cross_slab_forward_full(z_ref, kf_ref, a_scr, nh, ct):
    half = nh - 1
    n_even = half // 2 + 1

    def body(b, carry):
        rows = pl.ds(pl.multiple_of(16 * b, 16), 16)
        lo = z_ref[0:half, rows, :].astype(F32)
        hi = z_ref[half:2 * half, rows, :].astype(F32)
        for sub in range(2):
            sl = slice(8 * sub, 8 * sub + 8)
            g_even = (lo[:, sl, :] + hi[:, sl, :]).reshape(half * 8, ct).astype(BF16)
            g_odd = (lo[:, sl, :] - hi[:, sl, :]).reshape(half * 8, ct).astype(BF16)
            p_even = jnp.dot(kf_ref[0:n_even * 16, :], g_even, preferred_element_type=F32)
            p_odd = jnp.dot(kf_ref[n_even * 16:nh * 16, :], g_odd, preferred_element_type=F32)
            start = pl.ds(pl.multiple_of(32 * b + 16 * sub, 16), 16)
            a_scr[0:n_even, start, :] = p_even.astype(BF16).reshape(n_even, 16, ct)
            a_scr[n_even:nh, start, :] = p_odd.astype(BF16).reshape(nh - n_even, 16, ct)
        return carry

    lax.fori_loop(0, FFT_N2 // 16, body, 0)


def _fft_fwd_kernel(z_ref, kf_ref, m_ref, o_ref, a_scr, *, nh, kb, ct):
    kblk = pl.program_id(1)

    @pl.when(kblk == 0)
    def _():
        _cross_slab_forward_full(z_ref, kf_ref, a_scr, nh, ct)

    for k in range(kb):
        x = jnp.dot(m_ref[k], a_scr[kblk * kb + k], preferred_element_type=F32)
        o_ref[k] = x.astype(o_ref.dtype)


def _fft_forward(z3, kf, m, ct=256):
    nz, _, c = z3.shape
    nh = m.shape[0]
    kb = _slab_block(nh)
    assert nz == 2 * (nh - 1)
    return pl.pallas_call(
        functools.partial(_fft_fwd_kernel, nh=nh, kb=kb, ct=ct),
        grid=(c // ct, nh // kb),
        in_specs=[pl.BlockSpec((nz, FFT_N2, ct), lambda j, k: (0, 0, j)),
                  pl.BlockSpec(kf.shape, lambda j, k: (0, 0)),
                  pl.BlockSpec((kb, 2 * FFT_N2, 2 * FFT_N2), lambda j, k: (k, 0, 0))],
        out_specs=pl.BlockSpec((kb, 2 * FFT_N2, ct), lambda j, k: (k, 0, j)),
        out_shape=jax.ShapeDtypeStruct((nh, 2 * FFT_N2, c), BF16),
        scratch_shapes=[pltpu.VMEM((nh, 2 * FFT_N2, ct), BF16)],
        compiler_params=_params(("arbitrary", "arbitrary"), 56),
        name="fft_forward",
    )(z3, kf, m)


def _fft_conv_kernel(z_ref, kf_ref, ki_ref, m_ref, h_ref, *rest, nz, nh, kb, ct, skip_row):
    gated = len(rest) == 4
    o_ref, a_scr = rest[-2:]
    kblk = pl.program_id(1)
    n2 = FFT_N2

    @pl.when(kblk == 0)
    def _():
        _cross_slab_forward(z_ref, kf_ref, a_scr, nz, nh, ct)

    for k in range(kb):
        mk = m_ref[k]
        x = jnp.dot(mk, a_scr[kblk * kb + k], preferred_element_type=F32)
        h = h_ref[k].astype(F32)
        xr, xi, hr, hi = x[:n2], x[n2:], h[:n2], h[n2:]
        y = jnp.concatenate([xr * hr - xi * hi, xr * hi + xi * hr], axis=0).astype(BF16)
        b = lax.dot_general(mk, y, (((0,), (0,)), ((), ())), preferred_element_type=F32)
        a_scr[kblk * kb + k] = b.astype(BF16)

    @pl.when(kblk == pl.num_programs(1) - 1)
    def _():
        def body(b, carry):
            halves = []
            for half in range(2):
                start = pl.multiple_of(32 * b + 16 * half, 16)
                r = a_scr[:, pl.ds(start, 16), :].reshape(nh * 16, ct)
                p = jnp.dot(ki_ref[...], r, preferred_element_type=F32)
                halves.append(p.reshape(nz, 8, ct))
            out = jnp.concatenate(halves, axis=1)
            rows = pl.ds(pl.multiple_of(16 * b, 16), 16)
            if gated:
                x_ref, d_ref = rest[:2]
                z = z_ref[:, rows, :].astype(F32)
                out = x_ref[:, rows, :].astype(F32) * (out + z * d_ref[skip_row:skip_row + 1, :])
            o_ref[:, rows, :] = out.astype(o_ref.dtype)
            return carry

        lax.fori_loop(0, n2 // 16, body, 0)


def _fft_conv(z3, z_blk0, spec, spec_blk0, kf, ki, m, c, gate=None, ct=256):
    nz = z3.shape[0]
    nh = m.shape[0]
    kb = _slab_block(nh)
    assert nh == nz + 1
    in_specs = [pl.BlockSpec((nz, FFT_N2, ct), lambda j, k: (0, 0, z_blk0 + j)),
                pl.BlockSpec(kf.shape, lambda j, k: (0, 0)),
                pl.BlockSpec(ki.shape, lambda j, k: (0, 0)),
                pl.BlockSpec((kb, 2 * FFT_N2, 2 * FFT_N2), lambda j, k: (k, 0, 0)),
                pl.BlockSpec((kb, 2 * FFT_N2, ct), lambda j, k: (k, 0, spec_blk0 + j))]
    operands = [z3, kf, ki, m, spec]
    skip_row = 0
    if gate is not None:
        x3, x_blk0, dskip8, skip_row = gate
        in_specs += [pl.BlockSpec((nz, FFT_N2, ct), lambda j, k: (0, 0, x_blk0 + j)),
                     pl.BlockSpec((8, ct), lambda j, k: (0, j))]
        operands += [x3, dskip8]
    return pl.pallas_call(
        functools.partial(_fft_conv_kernel, nz=nz, nh=nh, kb=kb, ct=ct, skip_row=skip_row),
        grid=(c // ct, nh // kb),
        in_specs=in_specs,
        out_specs=pl.BlockSpec((nz, FFT_N2, ct), lambda j, k: (0, 0, j)),
        out_shape=jax.ShapeDtypeStruct((nz, FFT_N2, c), BF16),
        scratch_shapes=[pltpu.VMEM((nh, 2 * FFT_N2, ct), BF16)],
        compiler_params=_params(("arbitrary", "arbitrary"), 56),
        name="fft_conv",
    )(*operands)


def _ctx_dft_tables(n):
    nn = 2 * n
    k = jnp.arange(nn)
    ang = ((k[:, None] * k[None, :]) % nn).astype(F32) * (2.0 * math.pi / nn)
    cr, ci = jnp.cos(ang), -jnp.sin(ang)
    full = jnp.concatenate([cr, ci], axis=0)
    inv = jnp.concatenate([cr[:n], ci[:n]], axis=1) / nn
    return full[:, :n].astype(BF16), full.astype(BF16), inv.astype(BF16)


def _ctx_hyena_kernel(v_ref, x1_ref, x2_ref, g_ref, cw0_ref, cw1_ref, cw2_ref, hid_ref,
                      w10_ref, w11_ref, w20_ref, w21_ref, t_ref, dec_ref, dsk_ref,
                      ff_ref, ffull_ref, finv_ref, o_ref, *, n):
    row = lax.broadcasted_iota(jnp.int32, (n, 1), 0)

    def sconv(x_ref, w_ref):
        x = x_ref[...].astype(F32)
        up = jnp.where(row == 0, 0.0, pltpu.roll(x, 1, 0))
        down = jnp.where(row == n - 1, 0.0, pltpu.roll(x, n - 1, 0))
        y = up * w_ref[0:1, :] + x * w_ref[1:2, :] + down * w_ref[2:3, :]
        return y.astype(BF16).astype(F32)

    hid = hid_ref[...].astype(BF16)
    tcol = t_ref[...]
    row2 = lax.broadcasted_iota(jnp.int32, (2 * n, 1), 0)

    def spectrum(wf_ref, wb_ref, o):
        fwd = jnp.dot(hid[:n], wf_ref[...], preferred_element_type=F32)
        bwd = jnp.dot(hid[n:], wb_ref[...], preferred_element_type=F32)
        f = jnp.concatenate([fwd, bwd], axis=0) * jnp.exp(-tcol * jnp.abs(dec_ref[o:o + 1, :]))
        f = jnp.where(row2 == n, 0.0, f).astype(BF16)
        return jnp.dot(ffull_ref[...], f, preferred_element_type=F32)

    def conv(z, h):
        x = jnp.dot(ff_ref[...], z.astype(BF16), preferred_element_type=F32)
        h = h.astype(BF16).astype(F32)
        nn = 2 * n
        xr, xi, hr, hi = x[:nn], x[nn:], h[:nn], h[nn:]
        y = jnp.concatenate([xr * hr - xi * hi, xr * hi + xi * hr], axis=0).astype(BF16)
        return jnp.dot(finv_ref[...], y, preferred_element_type=F32)

    v = sconv(v_ref, cw0_ref)
    x1 = sconv(x1_ref, cw1_ref)
    x2 = sconv(x2_ref, cw2_ref)
    c1 = conv(v, spectrum(w10_ref, w11_ref, 0)).astype(BF16).astype(F32)
    y1 = (x1 * (c1 + v * dsk_ref[0:1, :])).astype(BF16).astype(F32)
    c2 = conv(y1, spectrum(w20_ref, w21_ref, 1)).astype(BF16).astype(F32)
    z = x2 * (c2 + y1 * dsk_ref[1:2, :])
    o_ref[...] = (z * _silu(g_ref[...].astype(F32))).astype(o_ref.dtype)


def _ctx_hyena(u, n_lat, hy, order, off_gate, conv8, hidden, hid_row0, w3p, tcol, decay8, dskip8, tables):
    t = u.shape[0]
    n = t - n_lat
    assert order == 2 and n_lat % n == 0 and hid_row0 % (2 * n) == 0
    ct = 256
    nb = hy // ct
    rb = n_lat // n
    ff, ffull, finv = tables
    const = lambda j: (0, 0)
    ublk = lambda c0: pl.BlockSpec((n, ct), lambda j: (rb, c0 + j))
    wblk = lambda c0: pl.BlockSpec((LANE, ct), lambda j: (0, c0 + j))
    cwblk = lambda c0: pl.BlockSpec((8, ct), lambda j: (0, c0 + j))
    return pl.pallas_call(
        functools.partial(_ctx_hyena_kernel, n=n),
        grid=(nb,),
        in_specs=[ublk(0), ublk(nb), ublk(2 * nb), ublk(off_gate // ct),
                  cwblk(0), cwblk(nb), cwblk(2 * nb),
                  pl.BlockSpec((2 * n, LANE), lambda j: (hid_row0 // (2 * n), 0)),
                  wblk(0), wblk(nb), wblk(2 * nb), wblk(3 * nb),
                  pl.BlockSpec((2 * n, 1), const),
                  pl.BlockSpec((8, ct), lambda j: (0, j)),
                  pl.BlockSpec((8, ct), lambda j: (0, j)),
                  pl.BlockSpec(ff.shape, const), pl.BlockSpec(ffull.shape, const),
                  pl.BlockSpec(finv.shape, const)],
        out_specs=pl.BlockSpec((n, ct), lambda j: (0, j)),
        out_shape=jax.ShapeDtypeStruct((n, hy), BF16),
        compiler_params=_params(("arbitrary",), 40),
        name="ctx_hyena",
    )(u, u, u, u, conv8, conv8, conv8, hidden, w3p, w3p, w3p, w3p, tcol, decay8, dskip8,
      ff, ffull, finv)


def _even_layer(x_all, mods_l, g, n_lat, w_in, w_out, conv_w, w1, b1, w2, b2, w3, freq, decay, dskip,
                q_norm, k_norm, rope, dft, ctx_dft, last):
    t, d = x_all.shape
    n_ctx = t - n_lat
    order, hy = decay.shape
    gqa_w = d - hy
    n_q = gqa_w // HEAD_DIM
    n_kv = n_q // 4
    kv_w = n_kv * HEAD_DIM
    off_hy_gate = (order + 1) * hy
    off_q = off_hy_gate + hy
    off_v = off_q + gqa_w + kv_w
    off_at_gate = off_v + kv_w
    scale = HEAD_DIM ** -0.5

    h = _norm_mod(x_all, g, mods_l, n_lat)
    u = _in_proj(h, *w_in)

    q_gain = q_norm.astype(F32) * (scale * math.log2(math.e))
    k_gain = k_norm.astype(F32)
    qt, kn, vt = _attn_prep(u, off_q, off_q + gqa_w, off_v, n_q, n_kv, 1, q_gain, k_gain,
                            rope[0], rope[1], True)
    att = _gqa_attention(_scores_bounded(q_gain, k_gain), qt, kn, vt, u, n_q, n_kv, off_at_gate, n_lat)

    n_bands = (w1.shape[0] - 1) // 2
    feat_l, t_l = _filter_features(n_lat, n_bands)
    feat_c, t_c = _filter_features(n_ctx, n_bands)
    hidden = _filter_hidden(jnp.concatenate([feat_l, feat_c], axis=0), w1, b1, w2, b2, freq)
    w3p = jnp.zeros((LANE, w3.shape[1]), BF16).at[:w3.shape[0]].set(w3.astype(BF16))
    decay8 = jnp.zeros((order, 8, hy), F32).at[:, 0].set(decay.astype(F32))
    dskip8 = jnp.zeros((8, hy), F32).at[:order].set(dskip.astype(F32))
    conv8 = jnp.zeros((8, conv_w.shape[1]), F32).at[:HY_SHORT].set(conv_w.astype(F32))
    kf_pad, ki, m = dft
    n1 = 2 * n_lat // FFT_N2
    filt = _filter_gen(hidden, 0, w3p, t_l, decay8, n_lat, order, hy)
    spec = _fft_forward(filt, kf_pad, m)

    vx3 = _short_conv(u, (order + 1) * hy, conv_w, n_lat)
    ct = 256
    y1 = _fft_conv(vx3, 0, spec, 0, kf_pad, ki, m, hy, gate=(vx3, hy // ct, dskip8, 0))
    c2 = _fft_conv(y1, 0, spec, hy // ct, kf_pad, ki, m, hy)

    decay_c8 = jnp.zeros((8, hy), F32).at[:order].set(decay.astype(F32))
    hy_ctx = _ctx_hyena(u, n_lat, hy, order, off_hy_gate, conv8, hidden, 2 * n_lat, w3p, t_c,
                        decay_c8, dskip8, ctx_dft)
    hy_mix = _hy_gate2(vx3, c2, y1, dskip8, u, off_hy_gate, hy_ctx, hy)

    return _out_proj(hy_mix, 0, att, 0, *w_out, x_all, mods_l, n_lat, last)


def _odd_layer(x_all, mods_l, g, n_lat, w_in, w_out, q_norm, k_norm, rel_bias, last):
    t, d = x_all.shape
    n_heads = rel_bias.shape[0]
    w = n_heads * HEAD_DIM
    scale = HEAD_DIM ** -0.5
    h = _norm_mod(x_all, g, mods_l, n_lat)
    u = _in_proj(h, *w_in)
    log2e = math.log2(math.e)
    q_gain = q_norm.astype(F32) * (scale * log2e)
    k_gain = k_norm.astype(F32)
    heads_per_step = NA_HEADS_PER_STEP
    unused = jnp.zeros((t, HEAD_DIM), F32)
    qt, kn, vt = _attn_prep(u, 0, w, 2 * w, heads_per_step, heads_per_step, n_heads // heads_per_step,
                            q_gain, k_gain, unused, unused, False)
    bias_tbl = _na_bias_table(rel_bias.astype(F32), n_lat // GRID_W)
    bounded = _scores_bounded(q_gain, k_gain, jnp.max(jnp.abs(rel_bias)).astype(F32) * (log2e * 1.02))
    att = _na_attention(bounded, qt, kn, vt, u, n_heads, 3 * w, bias_tbl, n_lat)
    return _out_proj(att, 0, att, 1, *w_out, x_all, mods_l, n_lat, last)


def kernel(x, c, ctx, c_ctx, norm_g, ada_w, ada_b, e_w_in, e_w_out, hy_conv, hy_w1, hy_b1, hy_w2, hy_b2,
           hy_w3, hy_freq, hy_decay, hy_dskip, gqa_q_norm, gqa_k_norm, o_w_in, o_w_out, na_q_norm,
           na_k_norm, na_rel_bias):
    batch, n_lat, d = x.shape
    n_ctx = ctx.shape[1]
    depth = norm_g.shape[0]
    assert batch == 1 and n_lat % (2 * FFT_N2) == 0 and n_ctx == NA_Q_ROWS * GRID_W

    x_all = jnp.concatenate([x[0], ctx[0]], axis=0).astype(F32)
    cvecs = jnp.zeros((8, d), F32).at[0].set(c[0]).at[1].set(c_ctx)
    mods = _ada_mods(cvecs, ada_w, ada_b)

    rope = _rope_tables(n_lat, n_ctx)
    n1 = 2 * n_lat // FFT_N2
    dft = _dft_tables(n1)
    ctx_dft = _ctx_dft_tables(n_ctx)

    for layer in range(depth):
        i = layer // 2
        last = layer == depth - 1
        if layer % 2 == 0:
            x_all = _even_layer(x_all, mods[layer], norm_g[layer], n_lat,
                                (e_w_in, i), (e_w_out, i), hy_conv[i],
                                hy_w1[i], hy_b1[i], hy_w2[i], hy_b2[i], hy_w3[i], hy_freq[i],
                                hy_decay[i], hy_dskip[i], gqa_q_norm[i], gqa_k_norm[i],
                                rope, dft, ctx_dft, last)
        else:
            x_all = _odd_layer(x_all, mods[layer], norm_g[layer], n_lat,
                               (o_w_in, i), (o_w_out, i),
                               na_q_norm[i], na_k_norm[i], na_rel_bias[i], last)
    return x_all[None]
```

```python
import functools
import math

import numpy as np
import jax
import jax.numpy as jnp
from jax import lax
from jax.experimental import pallas as pl
from jax.experimental.pallas import tpu as pltpu

F32 = jnp.float32
BF16 = jnp.bfloat16

HEAD_DIM = 128
GRID_W = 64
EPS = 1e-6
ROPE_THETA = 10000.0
NA_KH = 8
NA_KW = 16
NA_Q_ROWS = 4
NA_WIN_ROWS = 12
NA_HEADS_PER_STEP = 8
HY_SHORT = 3
FFT_N2 = 128
NEG_INF = -1e30

V7X_VMEM_BYTES = 64 * 1024 * 1024
LANE = 128


def _params(semantics, vmem_mb):
    assert vmem_mb * 1024 * 1024 < V7X_VMEM_BYTES
    return pltpu.CompilerParams(dimension_semantics=semantics,
                                vmem_limit_bytes=vmem_mb * 1024 * 1024)


def _largest_tile(n, target, quantum):
    best = None
    t = quantum
    while t <= min(n, target):
        if n % t == 0:
            best = t
        t += quantum
    assert best is not None, (n, target, quantum)
    return best


def _silu(x):
    return x * jax.nn.sigmoid(x)


def _mods_kernel(c_ref, w_ref, b_ref, o_ref):
    s = _silu(c_ref[...]).astype(BF16)
    o_ref[...] = jnp.dot(s, w_ref[...].astype(BF16), preferred_element_type=F32) + b_ref[...]


def _ada_mods(cvecs, ada_w, ada_b):
    depth, d, n = ada_w.shape
    tn = _largest_tile(n, 512, LANE)
    return pl.pallas_call(
        _mods_kernel,
        grid=(depth, n // tn),
        in_specs=[pl.BlockSpec((8, d), lambda l, j: (0, 0)),
                  pl.BlockSpec((None, d, tn), lambda l, j: (l, 0, j)),
                  pl.BlockSpec((None, 1, tn), lambda l, j: (l, 0, j))],
        out_specs=pl.BlockSpec((None, 8, tn), lambda l, j: (l, 0, j)),
        out_shape=jax.ShapeDtypeStruct((depth, 8, n), F32),
        compiler_params=_params(("arbitrary", "arbitrary"), 40),
        name="ada_mods",
    )(cvecs, ada_w, ada_b.reshape(depth, 1, n))


def _norm_mod_kernel(x_ref, g_ref, sh_ref, sc_ref, o_ref, *, n_lat_tiles):
    x = x_ref[...]
    y = x * lax.rsqrt(jnp.mean(x * x, axis=-1, keepdims=True) + EPS) * g_ref[...]
    is_ctx = pl.program_id(0) >= n_lat_tiles
    sh = jnp.where(is_ctx, sh_ref[1:2, :], sh_ref[0:1, :])
    sc = jnp.where(is_ctx, sc_ref[1:2, :], sc_ref[0:1, :])
    o_ref[...] = (y * (1.0 + sc) + sh).astype(o_ref.dtype)


def _norm_mod(x_all, g, mods_l, n_lat):
    t, d = x_all.shape
    tr = 256
    assert t % tr == 0 and n_lat % tr == 0
    return pl.pallas_call(
        functools.partial(_norm_mod_kernel, n_lat_tiles=n_lat // tr),
        grid=(t // tr,),
        in_specs=[pl.BlockSpec((tr, d), lambda i: (i, 0)),
                  pl.BlockSpec((1, d), lambda i: (0, 0)),
                  pl.BlockSpec((8, d), lambda i: (0, 0)),
                  pl.BlockSpec((8, d), lambda i: (0, 1))],
        out_specs=pl.BlockSpec((tr, d), lambda i: (i, 0)),
        out_shape=jax.ShapeDtypeStruct((t, d), BF16),
        compiler_params=_params(("arbitrary",), 40),
        name="norm_mod",
    )(x_all, g.reshape(1, d), mods_l, mods_l)


def _norm_mod_join_kernel(x_ref, c_ref, g_ref, sh_ref, sc_ref, o_ref, xo_ref, *, n_lat_tiles):
    is_ctx = pl.program_id(0) >= n_lat_tiles
    x = jnp.where(is_ctx, c_ref[...], x_ref[...])
    y = x * lax.rsqrt(jnp.mean(x * x, axis=-1, keepdims=True) + EPS) * g_ref[...]
    sh = jnp.where(is_ctx, sh_ref[1:2, :], sh_ref[0:1, :])
    sc = jnp.where(is_ctx, sc_ref[1:2, :], sc_ref[0:1, :])
    o_ref[...] = (y * (1.0 + sc) + sh).astype(o_ref.dtype)
    xo_ref[...] = x


def _norm_mod_join(x_lat, x_ctx, g, mods_l):
    n_lat, d = x_lat.shape
    n_ctx = x_ctx.shape[0]
    tr = n_ctx
    assert n_lat % tr == 0
    n_lat_tiles = n_lat // tr
    t = n_lat + n_ctx
    return pl.pallas_call(
        functools.partial(_norm_mod_join_kernel, n_lat_tiles=n_lat_tiles),
        grid=(t // tr,),
        in_specs=[pl.BlockSpec((tr, d), lambda i: (jnp.minimum(i, n_lat_tiles - 1), 0)),
                  pl.BlockSpec((tr, d), lambda i: (0, 0)),
                  pl.BlockSpec((1, d), lambda i: (0, 0)),
                  pl.BlockSpec((8, d), lambda i: (0, 0)),
                  pl.BlockSpec((8, d), lambda i: (0, 1))],
        out_specs=[pl.BlockSpec((tr, d), lambda i: (i, 0)),
                   pl.BlockSpec((tr, d), lambda i: (i, 0))],
        out_shape=[jax.ShapeDtypeStruct((t, d), BF16), jax.ShapeDtypeStruct((t, d), F32)],
        compiler_params=_params(("arbitrary",), 48),
        name="norm_mod_join",
    )(x_lat, x_ctx, g.reshape(1, d), mods_l, mods_l)


def _in_proj_kernel(a_ref, w_ref, o_ref):
    w = w_ref[...].astype(BF16)
    o_ref[...] = jnp.dot(a_ref[...], w, preferred_element_type=F32).astype(o_ref.dtype)


def _in_proj(h, w_all, layer_idx):
    m, k = h.shape
    n = w_all.shape[2]
    tm = _largest_tile(m, 1408, 128)
    tn = _largest_tile(n, 512, 256)
    return pl.pallas_call(
        _in_proj_kernel,
        grid=(m // tm, n // tn),
        in_specs=[pl.BlockSpec((tm, k), lambda i, j: (i, 0)),
                  pl.BlockSpec((None, k, tn), lambda i, j: (layer_idx, 0, j))],
        out_specs=pl.BlockSpec((tm, tn), lambda i, j: (i, j)),
        out_shape=jax.ShapeDtypeStruct((m, n), BF16),
        compiler_params=_params(("arbitrary", "arbitrary"), 56),
        name="in_proj",
    )(h, w_all)


def _out_proj_kernel(a1_ref, a2_ref, w1_ref, w2_ref, x_ref, gate_ref, o_ref, wb1, wb2, *, tm, n_lat):
    i = pl.program_id(1)

    @pl.when(i == 0)
    def _():
        wb1[...] = w1_ref[...].astype(BF16)
        wb2[...] = w2_ref[...].astype(BF16)

    y = jnp.dot(a1_ref[...], wb1[...], preferred_element_type=F32)
    y = y + jnp.dot(a2_ref[...], wb2[...], preferred_element_type=F32)
    rows = i * tm + lax.broadcasted_iota(jnp.int32, (tm, 1), 0)
    gate = jnp.where(rows >= n_lat, gate_ref[1:2, :], gate_ref[0:1, :])
    o_ref[...] = x_ref[...] + gate * y


def _out_proj(a1, a1_blk, a2, a2_blk, w_all, layer_idx, x_all, mods_l, n_lat, latent_only):
    d = x_all.shape[1]
    t = n_lat if latent_only else x_all.shape[0]
    kh = w_all.shape[1] // 2
    tm = _largest_tile(t, 768, 256)
    tn = _largest_tile(d, 1024, 256)
    gate_blk0 = 2 * d // tn
    once = pl.Buffered(1)
    return pl.pallas_call(
        functools.partial(_out_proj_kernel, tm=tm, n_lat=n_lat),
        grid=(d // tn, t // tm),
        in_specs=[pl.BlockSpec((tm, kh), lambda j, i: (i, a1_blk)),
                  pl.BlockSpec((tm, kh), lambda j, i: (i, a2_blk)),
                  pl.BlockSpec((None, kh, tn), lambda j, i: (layer_idx, 0, j), pipeline_mode=once),
                  pl.BlockSpec((None, kh, tn), lambda j, i: (layer_idx, 1, j), pipeline_mode=once),
                  pl.BlockSpec((tm, tn), lambda j, i: (i, j)),
                  pl.BlockSpec((8, tn), lambda j, i: (0, gate_blk0 + j))],
        out_specs=pl.BlockSpec((tm, tn), lambda j, i: (i, j)),
        out_shape=jax.ShapeDtypeStruct((t, d), F32),
        scratch_shapes=[pltpu.VMEM((kh, tn), BF16), pltpu.VMEM((kh, tn), BF16)],
        compiler_params=_params(("arbitrary", "arbitrary"), 56),
        name="out_proj",
    )(a1, a2, w_all, w_all, x_all, mods_l)


def _rope_tables(n_lat, n_ctx):
    quarter = HEAD_DIM // 4
    inv = ROPE_THETA ** (-jnp.arange(quarter, dtype=F32) / quarter)
    tok = jnp.arange(n_lat)
    rows = (tok // GRID_W).astype(F32)[:, None] * inv[None]
    cols = (tok % GRID_W).astype(F32)[:, None] * inv[None]
    cos = jnp.concatenate([jnp.cos(rows), jnp.cos(rows), jnp.cos(cols), jnp.cos(cols)], axis=-1)
    sin = jnp.concatenate([-jnp.sin(rows), jnp.sin(rows), -jnp.sin(cols), jnp.sin(cols)], axis=-1)
    cos = jnp.concatenate([cos, jnp.ones((n_ctx, HEAD_DIM), F32)], axis=0)
    sin = jnp.concatenate([sin, jnp.zeros((n_ctx, HEAD_DIM), F32)], axis=0)
    return cos, sin


GQA_ONES_ROWS = 16


def _attn_prep_kernel(q_ref, k_ref, v_ref, g_ref, cos_ref, sin_ref, qt_ref, ko_ref, vt_ref,
                      *, n_q, n_kv, rope):
    if rope:
        cos = cos_ref[...]
        sin = sin_ref[...]
        lane = lax.broadcasted_iota(jnp.int32, cos.shape, 1)
        first_quarter = (lane % 64) < 32

    def norm_rope(x, g):
        y = x * lax.rsqrt(jnp.mean(x * x, axis=-1, keepdims=True) + EPS) * g
        if not rope:
            return y
        partner = jnp.where(first_quarter, pltpu.roll(y, HEAD_DIM - 32, 1), pltpu.roll(y, 32, 1))
        return y * cos + partner * sin

    for h in range(n_q):
        sl = slice(h * HEAD_DIM, (h + 1) * HEAD_DIM)
        qt_ref[sl, :] = norm_rope(q_ref[:, sl].astype(F32), g_ref[:, sl]).T.astype(qt_ref.dtype)
    for h in range(n_kv):
        sl = slice(h * HEAD_DIM, (h + 1) * HEAD_DIM)
        gk = g_ref[:, (n_q + h) * HEAD_DIM:(n_q + h + 1) * HEAD_DIM]
        ko_ref[:, sl] = norm_rope(k_ref[:, sl].astype(F32), gk).astype(ko_ref.dtype)
        r0 = h * (HEAD_DIM + GQA_ONES_ROWS)
        vt_ref[r0:r0 + HEAD_DIM, :] = v_ref[:, sl].astype(F32).T.astype(vt_ref.dtype)
        vt_ref[r0 + HEAD_DIM:r0 + HEAD_DIM + GQA_ONES_ROWS, :] = jnp.ones(
            (GQA_ONES_ROWS, vt_ref.shape[1]), vt_ref.dtype)


def _attn_prep(u, off_q, off_k, off_v, n_q, n_kv, n_groups, q_gain, k_gain, cos_t, sin_t, rope):
    t = u.shape[0]
    qw, kw = n_q * HEAD_DIM, n_kv * HEAD_DIM
    tr = _largest_tile(t, 768, 256)
    vrows = n_kv * (HEAD_DIM + GQA_ONES_ROWS)
    assert off_q % qw == 0 and off_k % kw == 0 and off_v % kw == 0 and t % tr == 0
    q_blk, k_blk, v_blk = off_q // qw, off_k // kw, off_v // kw
    gvec = jnp.concatenate([jnp.tile(q_gain, n_q), jnp.tile(k_gain, n_kv)])[None]
    return pl.pallas_call(
        functools.partial(_attn_prep_kernel, n_q=n_q, n_kv=n_kv, rope=rope),
        grid=(t // tr, n_groups),
        in_specs=[pl.BlockSpec((tr, qw), lambda i, j: (i, q_blk + j)),
                  pl.BlockSpec((tr, kw), lambda i, j: (i, k_blk + j)),
                  pl.BlockSpec((tr, kw), lambda i, j: (i, v_blk + j)),
                  pl.BlockSpec((1, qw + kw), lambda i, j: (0, 0)),
                  pl.BlockSpec((tr, HEAD_DIM), lambda i, j: (i, 0)),
                  pl.BlockSpec((tr, HEAD_DIM), lambda i, j: (i, 0))],
        out_specs=[pl.BlockSpec((qw, tr), lambda i, j: (j, i)),
                   pl.BlockSpec((tr, kw), lambda i, j: (i, j)),
                   pl.BlockSpec((vrows, tr), lambda i, j: (j, i))],
        out_shape=[jax.ShapeDtypeStruct((n_groups * qw, t), BF16),
                   jax.ShapeDtypeStruct((t, n_groups * kw), BF16),
                   jax.ShapeDtypeStruct((n_groups * vrows, t), BF16)],
        compiler_params=_params(("arbitrary", "arbitrary"), 48),
        name="attn_prep",
    )(u, u, u, gvec, cos_t, sin_t)


def _gqa_kernel(bounded_ref, qt_ref, k_ref, vt_ref, gate_ref, o_ref, m_s, acc_s,
                *, group, tq, tk, n_lat, n_ctx):
    acc_s[...] = jnp.zeros(acc_s.shape, F32)

    def chunk_bounded(kc, vtc):
        ps = []
        for j in range(group):
            qt = qt_ref[j * HEAD_DIM:(j + 1) * HEAD_DIM, :]
            s = jnp.dot(kc, qt, preferred_element_type=F32)
            ps.append(jnp.exp2(s).astype(BF16))
        for j in range(group):
            acc_s[j] += jnp.dot(vtc, ps[j], preferred_element_type=F32)

    def chunk_online(kc, vtc):
        for j in range(group):
            qt = qt_ref[j * HEAD_DIM:(j + 1) * HEAD_DIM, :]
            s = jnp.dot(kc, qt, preferred_element_type=F32)
            m = m_s[j]
            m_new = jnp.maximum(m, jnp.max(s, axis=0, keepdims=True))
            p = jnp.exp2(s - m_new).astype(BF16)
            acc_s[j] = jnp.exp2(m - m_new) * acc_s[j] + jnp.dot(vtc, p, preferred_element_type=F32)
            m_s[j] = m_new

    def sweep(chunk):
        @pl.when(pl.program_id(1) * tq < n_lat)
        def _():
            def body(c, carry):
                r0 = pl.multiple_of(c * tk, tk)
                chunk(k_ref[pl.ds(r0, tk), :], vt_ref[:, pl.ds(r0, tk)])
                return carry

            lax.fori_loop(0, n_lat // tk, body, 0)

        chunk(k_ref[n_lat:n_lat + n_ctx, :], vt_ref[:, n_lat:n_lat + n_ctx])

    @pl.when(bounded_ref[0] != 0)
    def _():
        sweep(chunk_bounded)

    @pl.when(bounded_ref[0] == 0)
    def _():
        m_s[...] = jnp.full(m_s.shape, NEG_INF, F32)
        sweep(chunk_online)

    for j in range(group):
        sl = slice(j * HEAD_DIM, (j + 1) * HEAD_DIM)
        acc = acc_s[j]
        o = (acc[:HEAD_DIM] / acc[HEAD_DIM:HEAD_DIM + 1]).T
        o_ref[:, sl] = (o * _silu(gate_ref[:, sl].astype(F32))).astype(o_ref.dtype)


SCORE_BOUND_LOG2 = 60.0


def _scores_bounded(q_gain, k_gain, extra=0.0):
    bound = HEAD_DIM * jnp.max(jnp.abs(q_gain)) * jnp.max(jnp.abs(k_gain)) * 1.02 + extra
    return (bound <= SCORE_BOUND_LOG2).astype(jnp.int32).reshape(1)


def _gqa_attention(bounded, qt, k, vt, u, n_q_heads, n_kv_heads, off_gate, n_lat):
    t = u.shape[0]
    group = n_q_heads // n_kv_heads
    gw = group * HEAD_DIM
    tq = 256
    tk = _largest_tile(n_lat, 1024, 256)
    vrows = HEAD_DIM + GQA_ONES_ROWS
    assert t % tq == 0 and n_lat % tq == 0 and off_gate % gw == 0
    g_blk0 = off_gate // gw
    grid_spec = pltpu.PrefetchScalarGridSpec(
        num_scalar_prefetch=1,
        grid=(n_kv_heads, t // tq),
        in_specs=[pl.BlockSpec((gw, tq), lambda g, i, b: (g, i)),
                  pl.BlockSpec((t, HEAD_DIM), lambda g, i, b: (0, g)),
                  pl.BlockSpec((vrows, t), lambda g, i, b: (g, 0)),
                  pl.BlockSpec((tq, gw), lambda g, i, b: (i, g_blk0 + g))],
        out_specs=pl.BlockSpec((tq, gw), lambda g, i, b: (i, g)),
        scratch_shapes=[pltpu.VMEM((group, 1, tq), F32),
                        pltpu.VMEM((group, vrows, tq), F32)])
    return pl.pallas_call(
        functools.partial(_gqa_kernel, group=group, tq=tq, tk=tk, n_lat=n_lat, n_ctx=t - n_lat),
        grid_spec=grid_spec,
        out_shape=jax.ShapeDtypeStruct((t, n_q_heads * HEAD_DIM), BF16),
        compiler_params=_params(("arbitrary", "arbitrary"), 48),
        name="gqa_attention",
    )(bounded, qt, k, vt, u)


def _na_kernel(bounded_ref, qt_ref, k0_ref, k1_ref, k2_ref, kc_ref, v0_ref, v1_ref, v2_ref, vc_ref,
               bias_ref, gate_ref, o_ref, *, n_lat_blocks):
    is_latent = pl.program_id(1) < n_lat_blocks
    is_bounded = bounded_ref[0] != 0
    vrows = HEAD_DIM + GQA_ONES_ROWS

    def probabilities(j, latent, bounded):
        hs = slice(j * HEAD_DIM, (j + 1) * HEAD_DIM)
        qt = qt_ref[hs, :]
        scores = [jnp.dot(kc_ref[:, hs], qt, preferred_element_type=F32)]
        if latent:
            k = jnp.concatenate([k0_ref[:, hs], k1_ref[:, hs], k2_ref[:, hs]], axis=0)
            scores.append(jnp.dot(k, qt, preferred_element_type=F32) + bias_ref[j].astype(F32))
        if not bounded:
            m = functools.reduce(jnp.maximum, [jnp.max(s, axis=0, keepdims=True) for s in scores])
            scores = [s - m for s in scores]
        return [jnp.exp2(s).astype(BF16) for s in scores]

    def output(j, probs, latent):
        hs = slice(j * HEAD_DIM, (j + 1) * HEAD_DIM)
        vs = slice(j * vrows, (j + 1) * vrows)
        acc = jnp.dot(vc_ref[vs, :], probs[0], preferred_element_type=F32)
        if latent:
            vt = jnp.concatenate([v0_ref[vs, :], v1_ref[vs, :], v2_ref[vs, :]], axis=1)
            acc = acc + jnp.dot(vt, probs[1], preferred_element_type=F32)
        o = (acc[:HEAD_DIM] / acc[HEAD_DIM:HEAD_DIM + 1]).T
        o_ref[:, hs] = (o * _silu(gate_ref[:, hs].astype(F32))).astype(o_ref.dtype)

    heads = range(qt_ref.shape[0] // HEAD_DIM)
    for latent in (True, False):
        for bounded in (True, False):
            @pl.when(jnp.logical_and(is_latent == latent, is_bounded == bounded))
            def _():
                probs = [probabilities(j, latent, bounded) for j in heads]
                for j in heads:
                    output(j, probs[j], latent)


def _na_bias_table(rel_bias, n_rows):
    c = np.arange(GRID_W)[None, :]
    kj = np.arange(GRID_W)[:, None]
    cs = np.clip(c - NA_KW // 2, 0, GRID_W - NA_KW)
    col_ok = (kj >= cs) & (kj < cs + NA_KW)
    sel_c = (np.arange(2 * NA_KW - 1)[:, None, None] == (kj - c + NA_KW - 1)[None]) & col_ok[None]
    cols = jnp.einsum('hde,ekc->hdkc', rel_bias * math.log2(math.e), jnp.asarray(sel_c, F32),
                      precision=lax.Precision.HIGHEST)
    cols = jnp.where(jnp.asarray(col_ok)[None, None], cols, NEG_INF).astype(BF16)
    neg = jnp.full((rel_bias.shape[0], GRID_W, GRID_W), NEG_INF, BF16)
    tables = []
    for r0, ws in ((0, 0), (NA_Q_ROWS, 0), (n_rows - NA_Q_ROWS, n_rows - NA_WIN_ROWS)):
        key_rows = []
        for i in range(NA_WIN_ROWS):
            blocks = []
            for a in range(NA_Q_ROWS):
                r, kr = r0 + a, ws + i
                rs = min(max(r - NA_KH // 2, 0), n_rows - NA_KH)
                blocks.append(cols[:, kr - r + NA_KH - 1] if rs <= kr < rs + NA_KH else neg)
            key_rows.append(jnp.concatenate(blocks, axis=-1))
        tables.append(jnp.concatenate(key_rows, axis=-2))
    return jnp.stack(tables, axis=0)


def _na_attention(bounded, qt, kn, vt, u, n_heads, off_gate, bias_tbl, n_lat):
    t = u.shape[0]
    tb = NA_Q_ROWS * GRID_W
    n_lat_blocks = n_lat // tb
    ctx_blk = t // tb - 1
    hp = NA_HEADS_PER_STEP
    hw = hp * HEAD_DIM
    vrows = hp * (HEAD_DIM + GQA_ONES_ROWS)
    assert t - n_lat == tb and n_lat_blocks >= 3 and off_gate % hw == 0 and n_heads % hp == 0
    g_blk0 = off_gate // hw

    def win(i, j):
        return jnp.clip(i - 1, 0, n_lat_blocks - 3) + j

    def k_spec(j):
        return pl.BlockSpec((tb, hw), lambda h, i, b: (win(i, j), h))

    def v_spec(j):
        return pl.BlockSpec((vrows, tb), lambda h, i, b: (h, win(i, j)))

    def bias_cls(h, i, b):
        return (jnp.where(i == 0, 0, jnp.where(i >= n_lat_blocks - 1, 2, 1)), h, 0, 0)

    grid_spec = pltpu.PrefetchScalarGridSpec(
        num_scalar_prefetch=1,
        grid=(n_heads // hp, t // tb),
        in_specs=[pl.BlockSpec((hw, tb), lambda h, i, b: (h, i)),
                  k_spec(0), k_spec(1), k_spec(2),
                  pl.BlockSpec((tb, hw), lambda h, i, b: (ctx_blk, h)),
                  v_spec(0), v_spec(1), v_spec(2),
                  pl.BlockSpec((vrows, tb), lambda h, i, b: (h, ctx_blk)),
                  pl.BlockSpec((None, hp, NA_WIN_ROWS * GRID_W, tb), bias_cls),
                  pl.BlockSpec((tb, hw), lambda h, i, b: (i, g_blk0 + h))],
        out_specs=pl.BlockSpec((tb, hw), lambda h, i, b: (i, h)))
    return pl.pallas_call(
        functools.partial(_na_kernel, n_lat_blocks=n_lat_blocks),
        grid_spec=grid_spec,
        out_shape=jax.ShapeDtypeStruct((t, n_heads * HEAD_DIM), BF16),
        compiler_params=_params(("arbitrary", "arbitrary"), 32),
        name="na_attention",
    )(bounded, qt, kn, kn, kn, kn, vt, vt, vt, vt, bias_tbl, u)


def _short_conv_kernel(x_ref, prev_ref, next_ref, w_ref, o_ref, *, tr):
    i = pl.program_id(0)
    n = pl.num_programs(0)
    x = x_ref[...].astype(F32)
    halo = prev_ref.shape[0]
    prev_row = jnp.where(i == 0, 0.0, prev_ref[halo - 1:halo, :].astype(F32))
    next_row = jnp.where(i == n - 1, 0.0, next_ref[0:1, :].astype(F32))
    row = lax.broadcasted_iota(jnp.int32, x.shape, 0)
    up = jnp.where(row == 0, prev_row, pltpu.roll(x, 1, 0))
    down = jnp.where(row == tr - 1, next_row, pltpu.roll(x, tr - 1, 0))
    y = (up * w_ref[0:1, :] + x * w_ref[1:2, :] + down * w_ref[2:3, :]).astype(o_ref.dtype)
    o_ref[...] = y.reshape(o_ref.shape)


def _short_conv(u, width, conv_w, n_lat):
    tr = _largest_tile(n_lat, 1024, 256)
    tc = _largest_tile(width, 1024, 256)
    halo = 16
    hb = tr // halo
    n_tiles = n_lat // tr
    wpad = jnp.zeros((8, width), F32).at[:HY_SHORT].set(conv_w.astype(F32))
    return pl.pallas_call(
        functools.partial(_short_conv_kernel, tr=tr),
        grid=(n_tiles, width // tc),
        in_specs=[pl.BlockSpec((tr, tc), lambda i, j: (i, j)),
                  pl.BlockSpec((halo, tc), lambda i, j: (jnp.maximum(i * hb - 1, 0), j)),
                  pl.BlockSpec((halo, tc), lambda i, j: (jnp.minimum((i + 1) * hb, n_tiles * hb - 1), j)),
                  pl.BlockSpec((8, tc), lambda i, j: (0, j))],
        out_specs=pl.BlockSpec((tr // FFT_N2, FFT_N2, tc), lambda i, j: (i, 0, j)),
        out_shape=jax.ShapeDtypeStruct((n_lat // FFT_N2, FFT_N2, width), BF16),
        compiler_params=_params(("arbitrary", "arbitrary"), 32),
        name="short_conv",
    )(u, u, u, wpad)


def _hy_gate2_kernel(x2_ref, c_ref, y1_ref, d_ref, g_ref, ctx_ref, o_ref, *, n_lat_tiles):
    i = pl.program_id(0)

    @pl.when(i < n_lat_tiles)
    def _():
        rows = lambda ref: ref[...].reshape(o_ref.shape).astype(F32)
        y1 = rows(y1_ref)
        z = rows(x2_ref) * (rows(c_ref) + y1 * d_ref[1:2, :])
        o_ref[...] = (z * _silu(g_ref[...].astype(F32))).astype(o_ref.dtype)

    @pl.when(i >= n_lat_tiles)
    def _():
        o_ref[...] = ctx_ref[...]


def _hy_gate2(vx3, c2, y1, dskip8, u, off_gate, hy_ctx, hy):
    t = u.shape[0]
    n_lat = vx3.shape[0] * FFT_N2
    tr = t - n_lat
    tc = _largest_tile(hy, 2048, 256)
    nb = hy // tc
    n_lat_tiles = n_lat // tr
    assert n_lat % tr == 0 and off_gate % tc == 0 and tr % FFT_N2 == 0
    g_blk0 = off_gate // tc
    lat = lambda i: jnp.minimum(i, n_lat_tiles - 1)
    slab = lambda c0: pl.BlockSpec((tr // FFT_N2, FFT_N2, tc), lambda i, j: (lat(i), 0, c0 + j))
    return pl.pallas_call(
        functools.partial(_hy_gate2_kernel, n_lat_tiles=n_lat_tiles),
        grid=(t // tr, nb),
        in_specs=[slab(2 * nb), slab(0), slab(0),
                  pl.BlockSpec((8, tc), lambda i, j: (0, j)),
                  pl.BlockSpec((tr, tc), lambda i, j: (i, g_blk0 + j)),
                  pl.BlockSpec((tr, tc), lambda i, j: (0, j))],
        out_specs=pl.BlockSpec((tr, tc), lambda i, j: (i, j)),
        out_shape=jax.ShapeDtypeStruct((t, hy), BF16),
        compiler_params=_params(("arbitrary", "arbitrary"), 32),
        name="hy_gate2",
    )(vx3, c2, y1, dskip8, u, hy_ctx)


def _filter_hidden_kernel(z_ref, w1_ref, b1_ref, w2_ref, b2_ref, fr_ref, o_ref):
    hp = lax.Precision.HIGHEST
    fr = fr_ref[...]
    h = jnp.sin(fr * (jnp.dot(z_ref[...], w1_ref[...], precision=hp,
                              preferred_element_type=F32) + b1_ref[...]))
    o_ref[...] = jnp.sin(fr * (jnp.dot(h, w2_ref[...], precision=hp,
                                       preferred_element_type=F32) + b2_ref[...]))


def _pad2(a, rows, cols):
    return jnp.zeros((rows, cols), F32).at[:a.shape[0], :a.shape[1]].set(a.astype(F32))


def _filter_hidden(feat, w1, b1, w2, b2, freq):
    r = feat.shape[0]
    tr = _largest_tile(r, 1024, 256)
    full = lambda i: (0, 0)
    return pl.pallas_call(
        _filter_hidden_kernel,
        grid=(r // tr,),
        in_specs=[pl.BlockSpec((tr, LANE), lambda i: (i, 0)),
                  pl.BlockSpec((LANE, LANE), full), pl.BlockSpec((1, LANE), full),
                  pl.BlockSpec((LANE, LANE), full), pl.BlockSpec((1, LANE), full),
                  pl.BlockSpec((1, LANE), full)],
        out_specs=pl.BlockSpec((tr, LANE), lambda i: (i, 0)),
        out_shape=jax.ShapeDtypeStruct((r, LANE), F32),
        compiler_params=_params(("arbitrary",), 32),
        name="filter_hidden",
    )(feat, _pad2(w1, LANE, LANE), _pad2(b1[None], 1, LANE), _pad2(w2, LANE, LANE),
      _pad2(b2[None], 1, LANE), _pad2(freq[None], 1, LANE))


def _two_sided_positions(n):
    idx = jnp.arange(2 * n)
    return jnp.where(idx < n, idx, 2 * n - idx) % n


def _filter_features(n, n_bands):
    pos = _two_sided_positions(n).astype(F32)[:, None]
    t = pos / (n - 1)
    w = (2.0 * math.pi / n) * pos
    bands = jnp.linspace(1e-4, n_bands - 1, n_bands, dtype=F32)[None, :]
    z = jnp.concatenate([t, jnp.cos(bands * w), -jnp.sin(bands * w)], axis=-1)
    return _pad2(z, 2 * n, LANE), t


def _filter_gen_kernel(h_ref, w_ref, t_ref, dec_ref, o_ref, *, tr, n):
    i = pl.program_id(0)
    f = jnp.dot(h_ref[...].astype(BF16), w_ref[...], preferred_element_type=F32)
    f = f * jnp.exp(-t_ref[...] * jnp.abs(dec_ref[0:1, :]))
    row = i * tr + lax.broadcasted_iota(jnp.int32, (tr, 1), 0)
    o_ref[...] = jnp.where(row == n, 0.0, f).astype(o_ref.dtype).reshape(o_ref.shape)


def _filter_gen(hidden, row0, w3p, tcol, decay8, n, order, hy):
    tr = _largest_tile(n, 1024, 256)
    tc = _largest_tile(hy, 1024, 256)
    nb = hy // tc
    n_half = n // tr
    hb0 = row0 // tr
    assert row0 % tr == 0
    return pl.pallas_call(
        functools.partial(_filter_gen_kernel, tr=tr, n=n),
        grid=(2 * n // tr, order * nb),
        in_specs=[pl.BlockSpec((tr, LANE), lambda i, j: (hb0 + i, 0)),
                  pl.BlockSpec((LANE, tc), lambda i, j: (0, (j // nb) * 2 * nb + (i // n_half) * nb + j % nb)),
                  pl.BlockSpec((tr, 1), lambda i, j: (i, 0)),
                  pl.BlockSpec((None, 8, tc), lambda i, j: (j // nb, 0, j % nb))],
        out_specs=pl.BlockSpec((tr // FFT_N2, FFT_N2, tc), lambda i, j: (i, 0, j)),
        out_shape=jax.ShapeDtypeStruct((2 * n // FFT_N2, FFT_N2, order * hy), BF16),
        compiler_params=_params(("arbitrary", "arbitrary"), 32),
        name="filter_gen",
    )(hidden, w3p, tcol, decay8)


def _dft_tables(n1):
    n2 = FFT_N2
    n = n1 * n2
    half = n1 // 2
    nz = half
    k1 = jnp.concatenate([jnp.arange(0, half + 1, 2), jnp.arange(1, half + 1, 2)])
    two_pi = 2.0 * math.pi

    def cs(idx, period):
        ang = (idx % period).astype(F32) * (two_pi / period)
        return jnp.cos(ang), -jnp.sin(ang)

    eye8 = jnp.eye(8, dtype=F32)
    fr, fi = cs(k1[:, None] * jnp.arange(nz)[None, :], n1)
    kf = jnp.stack([fr, fi], axis=1)
    kf = jnp.einsum('kpn,jl->kpjnl', kf, eye8).reshape((half + 1) * 16, nz * 8)
    gr, gi = cs(jnp.arange(half)[:, None] * k1[None, :], n1)
    weight = jnp.where((k1 == 0) | (k1 == half), 1.0, 2.0) / n
    ki = jnp.stack([gr, gi], axis=2) * weight[None, :, None]
    ki = jnp.einsum('nkp,jl->njkpl', ki, eye8).reshape(half * 8, (half + 1) * 16)
    r = jnp.arange(2 * n2)
    part = (r % 16) // 8
    n2_of_r = 8 * (r // 16) + r % 8
    k2 = jnp.arange(n2)
    idx = k2[None, :, None] * n2_of_r[None, None, :] * n1 + k1[:, None, None] * n2_of_r[None, None, :]
    cr, ci = cs(idx, n)
    top = jnp.where(part[None, None, :] == 0, cr, -ci)
    bot = jnp.where(part[None, None, :] == 0, ci, cr)
    m = jnp.concatenate([top, bot], axis=1)
    return kf.astype(BF16), ki.astype(BF16), m.astype(BF16)


def _cross_slab_forward(z_ref, kf_ref, a_scr, nz, nh, ct):
    def body(b, carry):
        z = z_ref[:, pl.ds(pl.multiple_of(16 * b, 16), 16), :].astype(F32)
        for half in range(2):
            g = z[:, 8 * half:8 * half + 8, :].reshape(nz * 8, ct).astype(BF16)
            p = jnp.dot(kf_ref[...], g, preferred_element_type=F32)
            start = pl.multiple_of(32 * b + 16 * half, 16)
            a_scr[:, pl.ds(start, 16), :] = p.astype(BF16).reshape(nh, 16, ct)
        return carry

    lax.fori_loop(0, FFT_N2 // 16, body, 0)


def _slab_block(nh):
    return max(k for k in range(1, 17) if nh % k == 0)


def _cross_slab_forward_full(z_ref, kf_ref, a_scr, nh, ct):
    half = nh - 1
    n_even = half // 2 + 1

    def body(b, carry):
        rows = pl.ds(pl.multiple_of(16 * b, 16), 16)
        lo = z_ref[0:half, rows, :].astype(F32)
        hi = z_ref[half:2 * half, rows, :].astype(F32)
        for sub in range(2):
            sl = slice(8 * sub, 8 * sub + 8)
            g_even = (lo[:, sl, :] + hi[:, sl, :]).reshape(half * 8, ct).astype(BF16)
            g_odd = (lo[:, sl, :] - hi[:, sl, :]).reshape(half * 8, ct).astype(BF16)
            p_even = jnp.dot(kf_ref[0:n_even * 16, :], g_even, preferred_element_type=F32)
            p_odd = jnp.dot(kf_ref[n_even * 16:nh * 16, :], g_odd, preferred_element_type=F32)
            start = pl.ds(pl.multiple_of(32 * b + 16 * sub, 16), 16)
            a_scr[0:n_even, start, :] = p_even.astype(BF16).reshape(n_even, 16, ct)
            a_scr[n_even:nh, start, :] = p_odd.astype(BF16).reshape(nh - n_even, 16, ct)
        return carry

    lax.fori_loop(0, FFT_N2 // 16, body, 0)


def _fft_fwd_kernel(z_ref, kf_ref, m_ref, o_ref, a_scr, *, nh, kb, ct):
    kblk = pl.program_id(1)

    @pl.when(kblk == 0)
    def _():
        _cross_slab_forward_full(z_ref, kf_ref, a_scr, nh, ct)

    for k in range(kb):
        x = jnp.dot(m_ref[k], a_scr[kblk * kb + k], preferred_element_type=F32)
        o_ref[k] = x.astype(o_ref.dtype)


def _fft_forward(z3, kf, m, ct=256):
    nz, _, c = z3.shape
    nh = m.shape[0]
    kb = _slab_block(nh)
    assert nz == 2 * (nh - 1)
    return pl.pallas_call(
        functools.partial(_fft_fwd_kernel, nh=nh, kb=kb, ct=ct),
        grid=(c // ct, nh // kb),
        in_specs=[pl.BlockSpec((nz, FFT_N2, ct), lambda j, k: (0, 0, j)),
                  pl.BlockSpec(kf.shape, lambda j, k: (0, 0)),
                  pl.BlockSpec((kb, 2 * FFT_N2, 2 * FFT_N2), lambda j, k: (k, 0, 0))],
        out_specs=pl.BlockSpec((kb, 2 * FFT_N2, ct), lambda j, k: (k, 0, j)),
        out_shape=jax.ShapeDtypeStruct((nh, 2 * FFT_N2, c), BF16),
        scratch_shapes=[pltpu.VMEM((nh, 2 * FFT_N2, ct), BF16)],
        compiler_params=_params(("arbitrary", "arbitrary"), 56),
        name="fft_forward",
    )(z3, kf, m)


def _fft_conv_kernel(z_ref, kf_ref, ki_ref, m_ref, h_ref, *rest, nz, nh, kb, ct, skip_row):
    gated = len(rest) == 4
    o_ref, a_scr = rest[-2:]
    kblk = pl.program_id(1)
    n2 = FFT_N2

    @pl.when(kblk == 0)
    def _():
        _cross_slab_forward(z_ref, kf_ref, a_scr, nz, nh, ct)

    for k in range(kb):
        mk = m_ref[k]
        x = jnp.dot(mk, a_scr[kblk * kb + k], preferred_element_type=F32)
        h = h_ref[k].astype(F32)
        xr, xi, hr, hi = x[:n2], x[n2:], h[:n2], h[n2:]
        y = jnp.concatenate([xr * hr - xi * hi, xr * hi + xi * hr], axis=0).astype(BF16)
        b = lax.dot_general(mk, y, (((0,), (0,)), ((), ())), preferred_element_type=F32)
        a_scr[kblk * kb + k] = b.astype(BF16)

    @pl.when(kblk == pl.num_programs(1) - 1)
    def _():
        def body(b, carry):
            halves = []
            for half in range(2):
                start = pl.multiple_of(32 * b + 16 * half, 16)
                r = a_scr[:, pl.ds(start, 16), :].reshape(nh * 16, ct)
                p = jnp.dot(ki_ref[...], r, preferred_element_type=F32)
                halves.append(p.reshape(nz, 8, ct))
            out = jnp.concatenate(halves, axis=1)
            rows = pl.ds(pl.multiple_of(16 * b, 16), 16)
            if gated:
                x_ref, d_ref = rest[:2]
                z = z_ref[:, rows, :].astype(F32)
                out = x_ref[:, rows, :].astype(F32) * (out + z * d_ref[skip_row:skip_row + 1, :])
            o_ref[:, rows, :] = out.astype(o_ref.dtype)
            return carry

        lax.fori_loop(0, n2 // 16, body, 0)


def _fft_conv(z3, z_blk0, spec, spec_blk0, kf, ki, m, c, gate=None, ct=256):
    nz = z3.shape[0]
    nh = m.shape[0]
    kb = _slab_block(nh)
    assert nh == nz + 1
    in_specs = [pl.BlockSpec((nz, FFT_N2, ct), lambda j, k: (0, 0, z_blk0 + j)),
                pl.BlockSpec(kf.shape, lambda j, k: (0, 0)),
                pl.BlockSpec(ki.shape, lambda j, k: (0, 0)),
                pl.BlockSpec((kb, 2 * FFT_N2, 2 * FFT_N2), lambda j, k: (k, 0, 0)),
                pl.BlockSpec((kb, 2 * FFT_N2, ct), lambda j, k: (k, 0, spec_blk0 + j))]
    operands = [z3, kf, ki, m, spec]
    skip_row = 0
    if gate is not None:
        x3, x_blk0, dskip8, skip_row = gate
        in_specs += [pl.BlockSpec((nz, FFT_N2, ct), lambda j, k: (0, 0, x_blk0 + j)),
                     pl.BlockSpec((8, ct), lambda j, k: (0, j))]
        operands += [x3, dskip8]
    return pl.pallas_call(
        functools.partial(_fft_conv_kernel, nz=nz, nh=nh, kb=kb, ct=ct, skip_row=skip_row),
        grid=(c // ct, nh // kb),
        in_specs=in_specs,
        out_specs=pl.BlockSpec((nz, FFT_N2, ct), lambda j, k: (0, 0, j)),
        out_shape=jax.ShapeDtypeStruct((nz, FFT_N2, c), BF16),
        scratch_shapes=[pltpu.VMEM((nh, 2 * FFT_N2, ct), BF16)],
        compiler_params=_params(("arbitrary", "arbitrary"), 56),
        name="fft_conv",
    )(*operands)


def _ctx_dft_tables(n):
    nn = 2 * n
    k = jnp.arange(nn)
    ang = ((k[:, None] * k[None, :]) % nn).astype(F32) * (2.0 * math.pi / nn)
    cr, ci = jnp.cos(ang), -jnp.sin(ang)
    full = jnp.concatenate([cr, ci], axis=0)
    inv = jnp.concatenate([cr[:n], ci[:n]], axis=1) / nn
    return full[:, :n].astype(BF16), full.astype(BF16), inv.astype(BF16)


def _ctx_hyena_kernel(v_ref, x1_ref, x2_ref, g_ref, cw0_ref, cw1_ref, cw2_ref, hid_ref,
                      w10_ref, w11_ref, w20_ref, w21_ref, t_ref, dec_ref, dsk_ref,
                      ff_ref, ffull_ref, finv_ref, o_ref, *, n):
    row = lax.broadcasted_iota(jnp.int32, (n, 1), 0)

    def sconv(x_ref, w_ref):
        x = x_ref[...].astype(F32)
        up = jnp.where(row == 0, 0.0, pltpu.roll(x, 1, 0))
        down = jnp.where(row == n - 1, 0.0, pltpu.roll(x, n - 1, 0))
        y = up * w_ref[0:1, :] + x * w_ref[1:2, :] + down * w_ref[2:3, :]
        return y.astype(BF16).astype(F32)

    hid = hid_ref[...].astype(BF16)
    tcol = t_ref[...]
    row2 = lax.broadcasted_iota(jnp.int32, (2 * n, 1), 0)

    def spectrum(wf_ref, wb_ref, o):
        fwd = jnp.dot(hid[:n], wf_ref[...], preferred_element_type=F32)
        bwd = jnp.dot(hid[n:], wb_ref[...], preferred_element_type=F32)
        f = jnp.concatenate([fwd, bwd], axis=0) * jnp.exp(-tcol * jnp.abs(dec_ref[o:o + 1, :]))
        f = jnp.where(row2 == n, 0.0, f).astype(BF16)
        return jnp.dot(ffull_ref[...], f, preferred_element_type=F32)

    def conv(z, h):
        x = jnp.dot(ff_ref[...], z.astype(BF16), preferred_element_type=F32)
        h = h.astype(BF16).astype(F32)
        nn = 2 * n
        xr, xi, hr, hi = x[:nn], x[nn:], h[:nn], h[nn:]
        y = jnp.concatenate([xr * hr - xi * hi, xr * hi + xi * hr], axis=0).astype(BF16)
        return jnp.dot(finv_ref[...], y, preferred_element_type=F32)

    v = sconv(v_ref, cw0_ref)
    x1 = sconv(x1_ref, cw1_ref)
    x2 = sconv(x2_ref, cw2_ref)
    c1 = conv(v, spectrum(w10_ref, w11_ref, 0)).astype(BF16).astype(F32)
    y1 = (x1 * (c1 + v * dsk_ref[0:1, :])).astype(BF16).astype(F32)
    c2 = conv(y1, spectrum(w20_ref, w21_ref, 1)).astype(BF16).astype(F32)
    z = x2 * (c2 + y1 * dsk_ref[1:2, :])
    o_ref[...] = (z * _silu(g_ref[...].astype(F32))).astype(o_ref.dtype)


def _ctx_hyena(u, n_lat, hy, order, off_gate, conv8, hidden, hid_row0, w3p, tcol, decay8, dskip8, tables):
    t = u.shape[0]
    n = t - n_lat
    assert order == 2 and n_lat % n == 0 and hid_row0 % (2 * n) == 0
    ct = 256
    nb = hy // ct
    rb = n_lat // n
    ff, ffull, finv = tables
    const = lambda j: (0, 0)
    ublk = lambda c0: pl.BlockSpec((n, ct), lambda j: (rb, c0 + j))
    wblk = lambda c0: pl.BlockSpec((LANE, ct), lambda j: (0, c0 + j))
    cwblk = lambda c0: pl.BlockSpec((8, ct), lambda j: (0, c0 + j))
    return pl.pallas_call(
        functools.partial(_ctx_hyena_kernel, n=n),
        grid=(nb,),
        in_specs=[ublk(0), ublk(nb), ublk(2 * nb), ublk(off_gate // ct),
                  cwblk(0), cwblk(nb), cwblk(2 * nb),
                  pl.BlockSpec((2 * n, LANE), lambda j: (hid_row0 // (2 * n), 0)),
                  wblk(0), wblk(nb), wblk(2 * nb), wblk(3 * nb),
                  pl.BlockSpec((2 * n, 1), const),
                  pl.BlockSpec((8, ct), lambda j: (0, j)),
                  pl.BlockSpec((8, ct), lambda j: (0, j)),
                  pl.BlockSpec(ff.shape, const), pl.BlockSpec(ffull.shape, const),
                  pl.BlockSpec(finv.shape, const)],
        out_specs=pl.BlockSpec((n, ct), lambda j: (0, j)),
        out_shape=jax.ShapeDtypeStruct((n, hy), BF16),
        compiler_params=_params(("arbitrary",), 40),
        name="ctx_hyena",
    )(u, u, u, u, conv8, conv8, conv8, hidden, w3p, w3p, w3p, w3p, tcol, decay8, dskip8,
      ff, ffull, finv)


def _even_layer(x_all, h, mods_l, n_lat, w_in, w_out, conv_w, w1, b1, w2, b2, w3, freq, decay, dskip,
                q_norm, k_norm, rope, dft, ctx_dft, last):
    t, d = x_all.shape
    n_ctx = t - n_lat
    order, hy = decay.shape
    gqa_w = d - hy
    n_q = gqa_w // HEAD_DIM
    n_kv = n_q // 4
    kv_w = n_kv * HEAD_DIM
    off_hy_gate = (order + 1) * hy
    off_q = off_hy_gate + hy
    off_v = off_q + gqa_w + kv_w
    off_at_gate = off_v + kv_w
    scale = HEAD_DIM ** -0.5

    u = _in_proj(h, *w_in)

    q_gain = q_norm.astype(F32) * (scale * math.log2(math.e))
    k_gain = k_norm.astype(F32)
    qt, kn, vt = _attn_prep(u, off_q, off_q + gqa_w, off_v, n_q, n_kv, 1, q_gain, k_gain,
                            rope[0], rope[1], True)
    att = _gqa_attention(_scores_bounded(q_gain, k_gain), qt, kn, vt, u, n_q, n_kv, off_at_gate, n_lat)

    n_bands = (w1.shape[0] - 1) // 2
    feat_l, t_l = _filter_features(n_lat, n_bands)
    feat_c, t_c = _filter_features(n_ctx, n_bands)
    hidden = _filter_hidden(jnp.concatenate([feat_l, feat_c], axis=0), w1, b1, w2, b2, freq)
    w3p = jnp.zeros((LANE, w3.shape[1]), BF16).at[:w3.shape[0]].set(w3.astype(BF16))
    decay8 = jnp.zeros((order, 8, hy), F32).at[:, 0].set(decay.astype(F32))
    dskip8 = jnp.zeros((8, hy), F32).at[:order].set(dskip.astype(F32))
    conv8 = jnp.zeros((8, conv_w.shape[1]), F32).at[:HY_SHORT].set(conv_w.astype(F32))
    kf_pad, ki, m = dft
    n1 = 2 * n_lat // FFT_N2
    filt = _filter_gen(hidden, 0, w3p, t_l, decay8, n_lat, order, hy)
    spec = _fft_forward(filt, kf_pad, m)

    vx3 = _short_conv(u, (order + 1) * hy, conv_w, n_lat)
    ct = 256
    y1 = _fft_conv(vx3, 0, spec, 0, kf_pad, ki, m, hy, gate=(vx3, hy // ct, dskip8, 0))
    c2 = _fft_conv(y1, 0, spec, hy // ct, kf_pad, ki, m, hy)

    decay_c8 = jnp.zeros((8, hy), F32).at[:order].set(decay.astype(F32))
    hy_ctx = _ctx_hyena(u, n_lat, hy, order, off_hy_gate, conv8, hidden, 2 * n_lat, w3p, t_c,
                        decay_c8, dskip8, ctx_dft)
    hy_mix = _hy_gate2(vx3, c2, y1, dskip8, u, off_hy_gate, hy_ctx, hy)

    return _out_proj(hy_mix, 0, att, 0, *w_out, x_all, mods_l, n_lat, last)


def _odd_layer(x_all, h, mods_l, n_lat, w_in, w_out, q_norm, k_norm, rel_bias, last):
    t, d = x_all.shape
    n_heads = rel_bias.shape[0]
    w = n_heads * HEAD_DIM
    scale = HEAD_DIM ** -0.5
    u = _in_proj(h, *w_in)
    log2e = math.log2(math.e)
    q_gain = q_norm.astype(F32) * (scale * log2e)
    k_gain = k_norm.astype(F32)
    heads_per_step = NA_HEADS_PER_STEP
    unused = jnp.zeros((t, HEAD_DIM), F32)
    qt, kn, vt = _attn_prep(u, 0, w, 2 * w, heads_per_step, heads_per_step, n_heads // heads_per_step,
                            q_gain, k_gain, unused, unused, False)
    bias_tbl = _na_bias_table(rel_bias.astype(F32), n_lat // GRID_W)
    bounded = _scores_bounded(q_gain, k_gain, jnp.max(jnp.abs(rel_bias)).astype(F32) * (log2e * 1.02))
    att = _na_attention(bounded, qt, kn, vt, u, n_heads, 3 * w, bias_tbl, n_lat)
    return _out_proj(att, 0, att, 1, *w_out, x_all, mods_l, n_lat, last)


def kernel(x, c, ctx, c_ctx, norm_g, ada_w, ada_b, e_w_in, e_w_out, hy_conv, hy_w1, hy_b1, hy_w2, hy_b2,
           hy_w3, hy_freq, hy_decay, hy_dskip, gqa_q_norm, gqa_k_norm, o_w_in, o_w_out, na_q_norm,
           na_k_norm, na_rel_bias):
    batch, n_lat, d = x.shape
    n_ctx = ctx.shape[1]
    depth = norm_g.shape[0]
    assert batch == 1 and n_lat % (2 * FFT_N2) == 0 and n_ctx == NA_Q_ROWS * GRID_W

    cvecs = jnp.zeros((8, d), F32).at[0].set(c[0]).at[1].set(c_ctx)
    mods = _ada_mods(cvecs, ada_w, ada_b)

    rope = _rope_tables(n_lat, n_ctx)
    n1 = 2 * n_lat // FFT_N2
    dft = _dft_tables(n1)
    ctx_dft = _ctx_dft_tables(n_ctx)

    for layer in range(depth):
        i = layer // 2
        last = layer == depth - 1
        if layer == 0:
            h, x_all = _norm_mod_join(x[0].astype(F32), ctx[0].astype(F32), norm_g[0], mods[0])
        else:
            h = _norm_mod(x_all, norm_g[layer], mods[layer], n_lat)
        if layer % 2 == 0:
            x_all = _even_layer(x_all, h, mods[layer], n_lat,
                                (e_w_in, i), (e_w_out, i), hy_conv[i],
                                hy_w1[i], hy_b1[i], hy_w2[i], hy_b2[i], hy_w3[i], hy_freq[i],
                                hy_decay[i], hy_dskip[i], gqa_q_norm[i], gqa_k_norm[i],
                                rope, dft, ctx_dft, last)
        else:
            x_all = _odd_layer(x_all, h, mods[layer], n_lat,
                               (o_w_in, i), (o_w_out, i),
                               na_q_norm[i], na_k_norm[i], na_rel_bias[i], last)
    return x_all[None]
```

```python
import functools
import math

import numpy as np
import jax
import jax.numpy as jnp
from jax import lax
from jax.experimental import pallas as pl
from jax.experimental.pallas import tpu as pltpu

F32 = jnp.float32
BF16 = jnp.bfloat16

HEAD_DIM = 128
GRID_W = 64
EPS = 1e-6
ROPE_THETA = 10000.0
NA_KH = 8
NA_KW = 16
NA_Q_ROWS = 4
NA_WIN_ROWS = 12
NA_HEADS_PER_STEP = 8
HY_SHORT = 3
FFT_N2 = 128
NEG_INF = -1e30

V7X_VMEM_BYTES = 64 * 1024 * 1024
LANE = 128


def _params(semantics, vmem_mb):
    assert vmem_mb * 1024 * 1024 < V7X_VMEM_BYTES
    return pltpu.CompilerParams(dimension_semantics=semantics,
                                vmem_limit_bytes=vmem_mb * 1024 * 1024)


def _largest_tile(n, target, quantum):
    best = None
    t = quantum
    while t <= min(n, target):
        if n % t == 0:
            best = t
        t += quantum
    assert best is not None, (n, target, quantum)
    return best


def _silu(x):
    return x * jax.nn.sigmoid(x)


def _mods_kernel(c_ref, w_ref, b_ref, o_ref):
    s = _silu(c_ref[...]).astype(BF16)
    o_ref[...] = jnp.dot(s, w_ref[...].astype(BF16), preferred_element_type=F32) + b_ref[...]


def _ada_mods(cvecs, ada_w, ada_b):
    depth, d, n = ada_w.shape
    tn = _largest_tile(n, 512, LANE)
    return pl.pallas_call(
        _mods_kernel,
        grid=(depth, n // tn),
        in_specs=[pl.BlockSpec((8, d), lambda l, j: (0, 0)),
                  pl.BlockSpec((None, d, tn), lambda l, j: (l, 0, j)),
                  pl.BlockSpec((None, 1, tn), lambda l, j: (l, 0, j))],
        out_specs=pl.BlockSpec((None, 8, tn), lambda l, j: (l, 0, j)),
        out_shape=jax.ShapeDtypeStruct((depth, 8, n), F32),
        compiler_params=_params(("arbitrary", "arbitrary"), 40),
        name="ada_mods",
    )(cvecs, ada_w, ada_b.reshape(depth, 1, n))


def _norm_mod_kernel(x_ref, g_ref, sh_ref, sc_ref, o_ref, *, n_lat_tiles):
    x = x_ref[...]
    y = x * lax.rsqrt(jnp.mean(x * x, axis=-1, keepdims=True) + EPS) * g_ref[...]
    is_ctx = pl.program_id(0) >= n_lat_tiles
    sh = jnp.where(is_ctx, sh_ref[1:2, :], sh_ref[0:1, :])
    sc = jnp.where(is_ctx, sc_ref[1:2, :], sc_ref[0:1, :])
    o_ref[...] = (y * (1.0 + sc) + sh).astype(o_ref.dtype)


def _norm_mod(x_all, g, mods_l, n_lat):
    t, d = x_all.shape
    tr = 256
    assert t % tr == 0 and n_lat % tr == 0
    return pl.pallas_call(
        functools.partial(_norm_mod_kernel, n_lat_tiles=n_lat // tr),
        grid=(t // tr,),
        in_specs=[pl.BlockSpec((tr, d), lambda i: (i, 0)),
                  pl.BlockSpec((1, d), lambda i: (0, 0)),
                  pl.BlockSpec((8, d), lambda i: (0, 0)),
                  pl.BlockSpec((8, d), lambda i: (0, 1))],
        out_specs=pl.BlockSpec((tr, d), lambda i: (i, 0)),
        out_shape=jax.ShapeDtypeStruct((t, d), BF16),
        compiler_params=_params(("arbitrary",), 40),
        name="norm_mod",
    )(x_all, g.reshape(1, d), mods_l, mods_l)


def _norm_mod_join_kernel(x_ref, c_ref, g_ref, sh_ref, sc_ref, o_ref, xo_ref, *, n_lat_tiles):
    is_ctx = pl.program_id(0) >= n_lat_tiles
    x = jnp.where(is_ctx, c_ref[...], x_ref[...])
    y = x * lax.rsqrt(jnp.mean(x * x, axis=-1, keepdims=True) + EPS) * g_ref[...]
    sh = jnp.where(is_ctx, sh_ref[1:2, :], sh_ref[0:1, :])
    sc = jnp.where(is_ctx, sc_ref[1:2, :], sc_ref[0:1, :])
    o_ref[...] = (y * (1.0 + sc) + sh).astype(o_ref.dtype)
    xo_ref[...] = x


def _norm_mod_join(x_lat, x_ctx, g, mods_l):
    n_lat, d = x_lat.shape
    n_ctx = x_ctx.shape[0]
    tr = n_ctx
    assert n_lat % tr == 0
    n_lat_tiles = n_lat // tr
    t = n_lat + n_ctx
    return pl.pallas_call(
        functools.partial(_norm_mod_join_kernel, n_lat_tiles=n_lat_tiles),
        grid=(t // tr,),
        in_specs=[pl.BlockSpec((tr, d), lambda i: (jnp.minimum(i, n_lat_tiles - 1), 0)),
                  pl.BlockSpec((tr, d), lambda i: (0, 0)),
                  pl.BlockSpec((1, d), lambda i: (0, 0)),
                  pl.BlockSpec((8, d), lambda i: (0, 0)),
                  pl.BlockSpec((8, d), lambda i: (0, 1))],
        out_specs=[pl.BlockSpec((tr, d), lambda i: (i, 0)),
                   pl.BlockSpec((tr, d), lambda i: (i, 0))],
        out_shape=[jax.ShapeDtypeStruct((t, d), BF16), jax.ShapeDtypeStruct((t, d), F32)],
        compiler_params=_params(("arbitrary",), 48),
        name="norm_mod_join",
    )(x_lat, x_ctx, g.reshape(1, d), mods_l, mods_l)


def _in_proj_kernel(a_ref, w_ref, o_ref):
    w = w_ref[...].astype(BF16)
    o_ref[...] = jnp.dot(a_ref[...], w, preferred_element_type=F32).astype(o_ref.dtype)


def _in_proj(h, w_all, layer_idx):
    m, k = h.shape
    n = w_all.shape[2]
    tm = _largest_tile(m, 1408, 128)
    tn = _largest_tile(n, 512, 256)
    return pl.pallas_call(
        _in_proj_kernel,
        grid=(m // tm, n // tn),
        in_specs=[pl.BlockSpec((tm, k), lambda i, j: (i, 0)),
                  pl.BlockSpec((None, k, tn), lambda i, j: (layer_idx, 0, j))],
        out_specs=pl.BlockSpec((tm, tn), lambda i, j: (i, j)),
        out_shape=jax.ShapeDtypeStruct((m, n), BF16),
        compiler_params=_params(("arbitrary", "arbitrary"), 56),
        name="in_proj",
    )(h, w_all)


def _out_proj_kernel(a1_ref, a2_ref, w1_ref, w2_ref, x_ref, gate_ref, o_ref, wb1, wb2, *, tm, n_lat):
    i = pl.program_id(1)

    @pl.when(i == 0)
    def _():
        wb1[...] = w1_ref[...].astype(BF16)
        wb2[...] = w2_ref[...].astype(BF16)

    y = jnp.dot(a1_ref[...], wb1[...], preferred_element_type=F32)
    y = y + jnp.dot(a2_ref[...], wb2[...], preferred_element_type=F32)
    rows = i * tm + lax.broadcasted_iota(jnp.int32, (tm, 1), 0)
    gate = jnp.where(rows >= n_lat, gate_ref[1:2, :], gate_ref[0:1, :])
    o_ref[...] = x_ref[...] + gate * y


def _out_proj(a1, a1_blk, a2, a2_blk, w_all, layer_idx, x_all, mods_l, n_lat, latent_only):
    d = x_all.shape[1]
    t = n_lat if latent_only else x_all.shape[0]
    kh = w_all.shape[1] // 2
    tm = _largest_tile(t, 768, 256)
    tn = _largest_tile(d, 1024, 256)
    gate_blk0 = 2 * d // tn
    once = pl.Buffered(1)
    return pl.pallas_call(
        functools.partial(_out_proj_kernel, tm=tm, n_lat=n_lat),
        grid=(d // tn, t // tm),
        in_specs=[pl.BlockSpec((tm, kh), lambda j, i: (i, a1_blk)),
                  pl.BlockSpec((tm, kh), lambda j, i: (i, a2_blk)),
                  pl.BlockSpec((None, kh, tn), lambda j, i: (layer_idx, 0, j), pipeline_mode=once),
                  pl.BlockSpec((None, kh, tn), lambda j, i: (layer_idx, 1, j), pipeline_mode=once),
                  pl.BlockSpec((tm, tn), lambda j, i: (i, j)),
                  pl.BlockSpec((8, tn), lambda j, i: (0, gate_blk0 + j))],
        out_specs=pl.BlockSpec((tm, tn), lambda j, i: (i, j)),
        out_shape=jax.ShapeDtypeStruct((t, d), F32),
        scratch_shapes=[pltpu.VMEM((kh, tn), BF16), pltpu.VMEM((kh, tn), BF16)],
        compiler_params=_params(("arbitrary", "arbitrary"), 56),
        name="out_proj",
    )(a1, a2, w_all, w_all, x_all, mods_l)


def _rope_tables(n_lat, n_ctx):
    quarter = HEAD_DIM // 4
    inv = ROPE_THETA ** (-jnp.arange(quarter, dtype=F32) / quarter)
    tok = jnp.arange(n_lat)
    rows = (tok // GRID_W).astype(F32)[:, None] * inv[None]
    cols = (tok % GRID_W).astype(F32)[:, None] * inv[None]
    cos = jnp.concatenate([jnp.cos(rows), jnp.cos(rows), jnp.cos(cols), jnp.cos(cols)], axis=-1)
    sin = jnp.concatenate([-jnp.sin(rows), jnp.sin(rows), -jnp.sin(cols), jnp.sin(cols)], axis=-1)
    cos = jnp.concatenate([cos, jnp.ones((n_ctx, HEAD_DIM), F32)], axis=0)
    sin = jnp.concatenate([sin, jnp.zeros((n_ctx, HEAD_DIM), F32)], axis=0)
    return cos, sin


GQA_ONES_ROWS = 16


def _attn_prep_kernel(q_ref, k_ref, v_ref, g_ref, cos_ref, sin_ref, qt_ref, ko_ref, vt_ref,
                      *, n_q, n_kv, rope):
    if rope:
        cos = cos_ref[...]
        sin = sin_ref[...]
        lane = lax.broadcasted_iota(jnp.int32, cos.shape, 1)
        first_quarter = (lane % 64) < 32

    def norm_rope(x, g):
        y = x * lax.rsqrt(jnp.mean(x * x, axis=-1, keepdims=True) + EPS) * g
        if not rope:
            return y
        partner = jnp.where(first_quarter, pltpu.roll(y, HEAD_DIM - 32, 1), pltpu.roll(y, 32, 1))
        return y * cos + partner * sin

    for h in range(n_q):
        sl = slice(h * HEAD_DIM, (h + 1) * HEAD_DIM)
        qt_ref[sl, :] = norm_rope(q_ref[:, sl].astype(F32), g_ref[:, sl]).T.astype(qt_ref.dtype)
    for h in range(n_kv):
        sl = slice(h * HEAD_DIM, (h + 1) * HEAD_DIM)
        gk = g_ref[:, (n_q + h) * HEAD_DIM:(n_q + h + 1) * HEAD_DIM]
        ko_ref[:, sl] = norm_rope(k_ref[:, sl].astype(F32), gk).astype(ko_ref.dtype)
        r0 = h * (HEAD_DIM + GQA_ONES_ROWS)
        vt_ref[r0:r0 + HEAD_DIM, :] = v_ref[:, sl].astype(F32).T.astype(vt_ref.dtype)
        vt_ref[r0 + HEAD_DIM:r0 + HEAD_DIM + GQA_ONES_ROWS, :] = jnp.ones(
            (GQA_ONES_ROWS, vt_ref.shape[1]), vt_ref.dtype)


def _attn_prep(u, off_q, off_k, off_v, n_q, n_kv, n_groups, q_gain, k_gain, cos_t, sin_t, rope):
    t = u.shape[0]
    qw, kw = n_q * HEAD_DIM, n_kv * HEAD_DIM
    tr = _largest_tile(t, 768, 256)
    vrows = n_kv * (HEAD_DIM + GQA_ONES_ROWS)
    assert off_q % qw == 0 and off_k % kw == 0 and off_v % kw == 0 and t % tr == 0
    q_blk, k_blk, v_blk = off_q // qw, off_k // kw, off_v // kw
    gvec = jnp.concatenate([jnp.tile(q_gain, n_q), jnp.tile(k_gain, n_kv)])[None]
    return pl.pallas_call(
        functools.partial(_attn_prep_kernel, n_q=n_q, n_kv=n_kv, rope=rope),
        grid=(t // tr, n_groups),
        in_specs=[pl.BlockSpec((tr, qw), lambda i, j: (i, q_blk + j)),
                  pl.BlockSpec((tr, kw), lambda i, j: (i, k_blk + j)),
                  pl.BlockSpec((tr, kw), lambda i, j: (i, v_blk + j)),
                  pl.BlockSpec((1, qw + kw), lambda i, j: (0, 0)),
                  pl.BlockSpec((tr, HEAD_DIM), lambda i, j: (i, 0)),
                  pl.BlockSpec((tr, HEAD_DIM), lambda i, j: (i, 0))],
        out_specs=[pl.BlockSpec((qw, tr), lambda i, j: (j, i)),
                   pl.BlockSpec((tr, kw), lambda i, j: (i, j)),
                   pl.BlockSpec((vrows, tr), lambda i, j: (j, i))],
        out_shape=[jax.ShapeDtypeStruct((n_groups * qw, t), BF16),
                   jax.ShapeDtypeStruct((t, n_groups * kw), BF16),
                   jax.ShapeDtypeStruct((n_groups * vrows, t), BF16)],
        compiler_params=_params(("arbitrary", "arbitrary"), 48),
        name="attn_prep",
    )(u, u, u, gvec, cos_t, sin_t)


def _gqa_kernel(bounded_ref, qt_ref, k_ref, vt_ref, gate_ref, o_ref, m_s, acc_s,
                *, group, tq, tk, n_lat, n_ctx):
    acc_s[...] = jnp.zeros(acc_s.shape, F32)

    def chunk_bounded(kc, vtc):
        ps = []
        for j in range(group):
            qt = qt_ref[j * HEAD_DIM:(j + 1) * HEAD_DIM, :]
            s = jnp.dot(kc, qt, preferred_element_type=F32)
            ps.append(jnp.exp2(s).astype(BF16))
        for j in range(group):
            acc_s[j] += jnp.dot(vtc, ps[j], preferred_element_type=F32)

    def chunk_online(kc, vtc):
        for j in range(group):
            qt = qt_ref[j * HEAD_DIM:(j + 1) * HEAD_DIM, :]
            s = jnp.dot(kc, qt, preferred_element_type=F32)
            m = m_s[j]
            m_new = jnp.maximum(m, jnp.max(s, axis=0, keepdims=True))
            p = jnp.exp2(s - m_new).astype(BF16)
            acc_s[j] = jnp.exp2(m - m_new) * acc_s[j] + jnp.dot(vtc, p, preferred_element_type=F32)
            m_s[j] = m_new

    def sweep(chunk):
        @pl.when(pl.program_id(1) * tq < n_lat)
        def _():
            def body(c, carry):
                r0 = pl.multiple_of(c * tk, tk)
                chunk(k_ref[pl.ds(r0, tk), :], vt_ref[:, pl.ds(r0, tk)])
                return carry

            lax.fori_loop(0, n_lat // tk, body, 0)

        chunk(k_ref[n_lat:n_lat + n_ctx, :], vt_ref[:, n_lat:n_lat + n_ctx])

    @pl.when(bounded_ref[0] != 0)
    def _():
        sweep(chunk_bounded)

    @pl.when(bounded_ref[0] == 0)
    def _():
        m_s[...] = jnp.full(m_s.shape, NEG_INF, F32)
        sweep(chunk_online)

    for j in range(group):
        sl = slice(j * HEAD_DIM, (j + 1) * HEAD_DIM)
        acc = acc_s[j]
        o = (acc[:HEAD_DIM] / acc[HEAD_DIM:HEAD_DIM + 1]).T
        o_ref[:, sl] = (o * _silu(gate_ref[:, sl].astype(F32))).astype(o_ref.dtype)


SCORE_BOUND_LOG2 = 60.0


def _scores_bounded(q_gain, k_gain, extra=0.0):
    bound = HEAD_DIM * jnp.max(jnp.abs(q_gain)) * jnp.max(jnp.abs(k_gain)) * 1.02 + extra
    return (bound <= SCORE_BOUND_LOG2).astype(jnp.int32).reshape(1)


def _gqa_attention(bounded, qt, k, vt, u, n_q_heads, n_kv_heads, off_gate, n_lat):
    t = u.shape[0]
    group = n_q_heads // n_kv_heads
    gw = group * HEAD_DIM
    tq = 256
    tk = _largest_tile(n_lat, 4096, 256)
    vrows = HEAD_DIM + GQA_ONES_ROWS
    assert t % tq == 0 and n_lat % tq == 0 and off_gate % gw == 0
    g_blk0 = off_gate // gw
    grid_spec = pltpu.PrefetchScalarGridSpec(
        num_scalar_prefetch=1,
        grid=(n_kv_heads, t // tq),
        in_specs=[pl.BlockSpec((gw, tq), lambda g, i, b: (g, i)),
                  pl.BlockSpec((t, HEAD_DIM), lambda g, i, b: (0, g)),
                  pl.BlockSpec((vrows, t), lambda g, i, b: (g, 0)),
                  pl.BlockSpec((tq, gw), lambda g, i, b: (i, g_blk0 + g))],
        out_specs=pl.BlockSpec((tq, gw), lambda g, i, b: (i, g)),
        scratch_shapes=[pltpu.VMEM((group, 1, tq), F32),
                        pltpu.VMEM((group, vrows, tq), F32)])
    return pl.pallas_call(
        functools.partial(_gqa_kernel, group=group, tq=tq, tk=tk, n_lat=n_lat, n_ctx=t - n_lat),
        grid_spec=grid_spec,
        out_shape=jax.ShapeDtypeStruct((t, n_q_heads * HEAD_DIM), BF16),
        compiler_params=_params(("arbitrary", "arbitrary"), 48),
        name="gqa_attention",
    )(bounded, qt, k, vt, u)


def _na_kernel(bounded_ref, qt_ref, k0_ref, k1_ref, k2_ref, kc_ref, v0_ref, v1_ref, v2_ref, vc_ref,
               bias_ref, gate_ref, o_ref, *, n_lat_blocks):
    is_latent = pl.program_id(1) < n_lat_blocks
    is_bounded = bounded_ref[0] != 0
    vrows = HEAD_DIM + GQA_ONES_ROWS

    def probabilities(j, latent, bounded):
        hs = slice(j * HEAD_DIM, (j + 1) * HEAD_DIM)
        qt = qt_ref[hs, :]
        scores = [jnp.dot(kc_ref[:, hs], qt, preferred_element_type=F32)]
        if latent:
            k = jnp.concatenate([k0_ref[:, hs], k1_ref[:, hs], k2_ref[:, hs]], axis=0)
            scores.append(jnp.dot(k, qt, preferred_element_type=F32) + bias_ref[j].astype(F32))
        if not bounded:
            m = functools.reduce(jnp.maximum, [jnp.max(s, axis=0, keepdims=True) for s in scores])
            scores = [s - m for s in scores]
        return [jnp.exp2(s).astype(BF16) for s in scores]

    def output(j, probs, latent):
        hs = slice(j * HEAD_DIM, (j + 1) * HEAD_DIM)
        vs = slice(j * vrows, (j + 1) * vrows)
        acc = jnp.dot(vc_ref[vs, :], probs[0], preferred_element_type=F32)
        if latent:
            vt = jnp.concatenate([v0_ref[vs, :], v1_ref[vs, :], v2_ref[vs, :]], axis=1)
            acc = acc + jnp.dot(vt, probs[1], preferred_element_type=F32)
        o = (acc[:HEAD_DIM] / acc[HEAD_DIM:HEAD_DIM + 1]).T
        o_ref[:, hs] = (o * _silu(gate_ref[:, hs].astype(F32))).astype(o_ref.dtype)

    heads = range(qt_ref.shape[0] // HEAD_DIM)
    for latent in (True, False):
        for bounded in (True, False):
            @pl.when(jnp.logical_and(is_latent == latent, is_bounded == bounded))
            def _():
                probs = [probabilities(j, latent, bounded) for j in heads]
                for j in heads:
                    output(j, probs[j], latent)


def _na_bias_table(rel_bias, n_rows):
    c = np.arange(GRID_W)[None, :]
    kj = np.arange(GRID_W)[:, None]
    cs = np.clip(c - NA_KW // 2, 0, GRID_W - NA_KW)
    col_ok = (kj >= cs) & (kj < cs + NA_KW)
    sel_c = (np.arange(2 * NA_KW - 1)[:, None, None] == (kj - c + NA_KW - 1)[None]) & col_ok[None]
    cols = jnp.einsum('hde,ekc->hdkc', rel_bias * math.log2(math.e), jnp.asarray(sel_c, F32),
                      precision=lax.Precision.HIGHEST)
    cols = jnp.where(jnp.asarray(col_ok)[None, None], cols, NEG_INF).astype(BF16)
    neg = jnp.full((rel_bias.shape[0], GRID_W, GRID_W), NEG_INF, BF16)
    tables = []
    for r0, ws in ((0, 0), (NA_Q_ROWS, 0), (n_rows - NA_Q_ROWS, n_rows - NA_WIN_ROWS)):
        key_rows = []
        for i in range(NA_WIN_ROWS):
            blocks = []
            for a in range(NA_Q_ROWS):
                r, kr = r0 + a, ws + i
                rs = min(max(r - NA_KH // 2, 0), n_rows - NA_KH)
                blocks.append(cols[:, kr - r + NA_KH - 1] if rs <= kr < rs + NA_KH else neg)
            key_rows.append(jnp.concatenate(blocks, axis=-1))
        tables.append(jnp.concatenate(key_rows, axis=-2))
    return jnp.stack(tables, axis=0)


def _na_attention(bounded, qt, kn, vt, u, n_heads, off_gate, bias_tbl, n_lat):
    t = u.shape[0]
    tb = NA_Q_ROWS * GRID_W
    n_lat_blocks = n_lat // tb
    ctx_blk = t // tb - 1
    hp = NA_HEADS_PER_STEP
    hw = hp * HEAD_DIM
    vrows = hp * (HEAD_DIM + GQA_ONES_ROWS)
    assert t - n_lat == tb and n_lat_blocks >= 3 and off_gate % hw == 0 and n_heads % hp == 0
    g_blk0 = off_gate // hw

    def win(i, j):
        return jnp.clip(i - 1, 0, n_lat_blocks - 3) + j

    def k_spec(j):
        return pl.BlockSpec((tb, hw), lambda h, i, b: (win(i, j), h))

    def v_spec(j):
        return pl.BlockSpec((vrows, tb), lambda h, i, b: (h, win(i, j)))

    def bias_cls(h, i, b):
        return (jnp.where(i == 0, 0, jnp.where(i >= n_lat_blocks - 1, 2, 1)), h, 0, 0)

    grid_spec = pltpu.PrefetchScalarGridSpec(
        num_scalar_prefetch=1,
        grid=(n_heads // hp, t // tb),
        in_specs=[pl.BlockSpec((hw, tb), lambda h, i, b: (h, i)),
                  k_spec(0), k_spec(1), k_spec(2),
                  pl.BlockSpec((tb, hw), lambda h, i, b: (ctx_blk, h)),
                  v_spec(0), v_spec(1), v_spec(2),
                  pl.BlockSpec((vrows, tb), lambda h, i, b: (h, ctx_blk)),
                  pl.BlockSpec((None, hp, NA_WIN_ROWS * GRID_W, tb), bias_cls),
                  pl.BlockSpec((tb, hw), lambda h, i, b: (i, g_blk0 + h))],
        out_specs=pl.BlockSpec((tb, hw), lambda h, i, b: (i, h)))
    return pl.pallas_call(
        functools.partial(_na_kernel, n_lat_blocks=n_lat_blocks),
        grid_spec=grid_spec,
        out_shape=jax.ShapeDtypeStruct((t, n_heads * HEAD_DIM), BF16),
        compiler_params=_params(("arbitrary", "arbitrary"), 32),
        name="na_attention",
    )(bounded, qt, kn, kn, kn, kn, vt, vt, vt, vt, bias_tbl, u)


def _short_conv_kernel(x_ref, prev_ref, next_ref, w_ref, o_ref, *, tr):
    i = pl.program_id(0)
    n = pl.num_programs(0)
    x = x_ref[...].astype(F32)
    halo = prev_ref.shape[0]
    prev_row = jnp.where(i == 0, 0.0, prev_ref[halo - 1:halo, :].astype(F32))
    next_row = jnp.where(i == n - 1, 0.0, next_ref[0:1, :].astype(F32))
    row = lax.broadcasted_iota(jnp.int32, x.shape, 0)
    up = jnp.where(row == 0, prev_row, pltpu.roll(x, 1, 0))
    down = jnp.where(row == tr - 1, next_row, pltpu.roll(x, tr - 1, 0))
    y = (up * w_ref[0:1, :] + x * w_ref[1:2, :] + down * w_ref[2:3, :]).astype(o_ref.dtype)
    o_ref[...] = y.reshape(o_ref.shape)


def _short_conv(u, width, conv_w, n_lat):
    tr = _largest_tile(n_lat, 1024, 256)
    tc = _largest_tile(width, 1024, 256)
    halo = 16
    hb = tr // halo
    n_tiles = n_lat // tr
    wpad = jnp.zeros((8, width), F32).at[:HY_SHORT].set(conv_w.astype(F32))
    return pl.pallas_call(
        functools.partial(_short_conv_kernel, tr=tr),
        grid=(n_tiles, width // tc),
        in_specs=[pl.BlockSpec((tr, tc), lambda i, j: (i, j)),
                  pl.BlockSpec((halo, tc), lambda i, j: (jnp.maximum(i * hb - 1, 0), j)),
                  pl.BlockSpec((halo, tc), lambda i, j: (jnp.minimum((i + 1) * hb, n_tiles * hb - 1), j)),
                  pl.BlockSpec((8, tc), lambda i, j: (0, j))],
        out_specs=pl.BlockSpec((tr // FFT_N2, FFT_N2, tc), lambda i, j: (i, 0, j)),
        out_shape=jax.ShapeDtypeStruct((n_lat // FFT_N2, FFT_N2, width), BF16),
        compiler_params=_params(("arbitrary", "arbitrary"), 32),
        name="short_conv",
    )(u, u, u, wpad)


def _hy_gate2_kernel(x2_ref, c_ref, y1_ref, d_ref, g_ref, ctx_ref, o_ref, *, n_lat_tiles):
    i = pl.program_id(0)

    @pl.when(i < n_lat_tiles)
    def _():
        rows = lambda ref: ref[...].reshape(o_ref.shape).astype(F32)
        y1 = rows(y1_ref)
        z = rows(x2_ref) * (rows(c_ref) + y1 * d_ref[1:2, :])
        o_ref[...] = (z * _silu(g_ref[...].astype(F32))).astype(o_ref.dtype)

    @pl.when(i >= n_lat_tiles)
    def _():
        o_ref[...] = ctx_ref[...]


def _hy_gate2(vx3, c2, y1, dskip8, u, off_gate, hy_ctx, hy):
    t = u.shape[0]
    n_lat = vx3.shape[0] * FFT_N2
    tr = t - n_lat
    tc = _largest_tile(hy, 2048, 256)
    nb = hy // tc
    n_lat_tiles = n_lat // tr
    assert n_lat % tr == 0 and off_gate % tc == 0 and tr % FFT_N2 == 0
    g_blk0 = off_gate // tc
    lat = lambda i: jnp.minimum(i, n_lat_tiles - 1)
    slab = lambda c0: pl.BlockSpec((tr // FFT_N2, FFT_N2, tc), lambda i, j: (lat(i), 0, c0 + j))
    return pl.pallas_call(
        functools.partial(_hy_gate2_kernel, n_lat_tiles=n_lat_tiles),
        grid=(t // tr, nb),
        in_specs=[slab(2 * nb), slab(0), slab(0),
                  pl.BlockSpec((8, tc), lambda i, j: (0, j)),
                  pl.BlockSpec((tr, tc), lambda i, j: (i, g_blk0 + j)),
                  pl.BlockSpec((tr, tc), lambda i, j: (0, j))],
        out_specs=pl.BlockSpec((tr, tc), lambda i, j: (i, j)),
        out_shape=jax.ShapeDtypeStruct((t, hy), BF16),
        compiler_params=_params(("arbitrary", "arbitrary"), 32),
        name="hy_gate2",
    )(vx3, c2, y1, dskip8, u, hy_ctx)


def _filter_hidden_kernel(z_ref, w1_ref, b1_ref, w2_ref, b2_ref, fr_ref, o_ref):
    hp = lax.Precision.HIGHEST
    fr = fr_ref[...]
    h = jnp.sin(fr * (jnp.dot(z_ref[...], w1_ref[...], precision=hp,
                              preferred_element_type=F32) + b1_ref[...]))
    o_ref[...] = jnp.sin(fr * (jnp.dot(h, w2_ref[...], precision=hp,
                                       preferred_element_type=F32) + b2_ref[...]))


def _pad2(a, rows, cols):
    return jnp.zeros((rows, cols), F32).at[:a.shape[0], :a.shape[1]].set(a.astype(F32))


def _filter_hidden(feat, w1, b1, w2, b2, freq):
    r = feat.shape[0]
    tr = _largest_tile(r, 1024, 256)
    full = lambda i: (0, 0)
    return pl.pallas_call(
        _filter_hidden_kernel,
        grid=(r // tr,),
        in_specs=[pl.BlockSpec((tr, LANE), lambda i: (i, 0)),
                  pl.BlockSpec((LANE, LANE), full), pl.BlockSpec((1, LANE), full),
                  pl.BlockSpec((LANE, LANE), full), pl.BlockSpec((1, LANE), full),
                  pl.BlockSpec((1, LANE), full)],
        out_specs=pl.BlockSpec((tr, LANE), lambda i: (i, 0)),
        out_shape=jax.ShapeDtypeStruct((r, LANE), F32),
        compiler_params=_params(("arbitrary",), 32),
        name="filter_hidden",
    )(feat, _pad2(w1, LANE, LANE), _pad2(b1[None], 1, LANE), _pad2(w2, LANE, LANE),
      _pad2(b2[None], 1, LANE), _pad2(freq[None], 1, LANE))


def _two_sided_positions(n):
    idx = jnp.arange(2 * n)
    return jnp.where(idx < n, idx, 2 * n - idx) % n


def _filter_features(n, n_bands):
    pos = _two_sided_positions(n).astype(F32)[:, None]
    t = pos / (n - 1)
    w = (2.0 * math.pi / n) * pos
    bands = jnp.linspace(1e-4, n_bands - 1, n_bands, dtype=F32)[None, :]
    z = jnp.concatenate([t, jnp.cos(bands * w), -jnp.sin(bands * w)], axis=-1)
    return _pad2(z, 2 * n, LANE), t


def _filter_gen_kernel(h_ref, w_ref, t_ref, dec_ref, o_ref, *, tr, n):
    i = pl.program_id(0)
    f = jnp.dot(h_ref[...].astype(BF16), w_ref[...], preferred_element_type=F32)
    f = f * jnp.exp(-t_ref[...] * jnp.abs(dec_ref[0:1, :]))
    row = i * tr + lax.broadcasted_iota(jnp.int32, (tr, 1), 0)
    o_ref[...] = jnp.where(row == n, 0.0, f).astype(o_ref.dtype).reshape(o_ref.shape)


def _filter_gen(hidden, row0, w3p, tcol, decay8, n, order, hy):
    tr = _largest_tile(n, 1024, 256)
    tc = _largest_tile(hy, 1024, 256)
    nb = hy // tc
    n_half = n // tr
    hb0 = row0 // tr
    assert row0 % tr == 0
    return pl.pallas_call(
        functools.partial(_filter_gen_kernel, tr=tr, n=n),
        grid=(2 * n // tr, order * nb),
        in_specs=[pl.BlockSpec((tr, LANE), lambda i, j: (hb0 + i, 0)),
                  pl.BlockSpec((LANE, tc), lambda i, j: (0, (j // nb) * 2 * nb + (i // n_half) * nb + j % nb)),
                  pl.BlockSpec((tr, 1), lambda i, j: (i, 0)),
                  pl.BlockSpec((None, 8, tc), lambda i, j: (j // nb, 0, j % nb))],
        out_specs=pl.BlockSpec((tr // FFT_N2, FFT_N2, tc), lambda i, j: (i, 0, j)),
        out_shape=jax.ShapeDtypeStruct((2 * n // FFT_N2, FFT_N2, order * hy), BF16),
        compiler_params=_params(("arbitrary", "arbitrary"), 32),
        name="filter_gen",
    )(hidden, w3p, tcol, decay8)


def _dft_tables(n1):
    n2 = FFT_N2
    n = n1 * n2
    half = n1 // 2
    nz = half
    k1 = jnp.concatenate([jnp.arange(0, half + 1, 2), jnp.arange(1, half + 1, 2)])
    two_pi = 2.0 * math.pi

    def cs(idx, period):
        ang = (idx % period).astype(F32) * (two_pi / period)
        return jnp.cos(ang), -jnp.sin(ang)

    eye8 = jnp.eye(8, dtype=F32)
    fr, fi = cs(k1[:, None] * jnp.arange(nz)[None, :], n1)
    kf = jnp.stack([fr, fi], axis=1)
    kf = jnp.einsum('kpn,jl->kpjnl', kf, eye8).reshape((half + 1) * 16, nz * 8)
    gr, gi = cs(jnp.arange(half)[:, None] * k1[None, :], n1)
    weight = jnp.where((k1 == 0) | (k1 == half), 1.0, 2.0) / n
    ki = jnp.stack([gr, gi], axis=2) * weight[None, :, None]
    ki = jnp.einsum('nkp,jl->njkpl', ki, eye8).reshape(half * 8, (half + 1) * 16)
    r = jnp.arange(2 * n2)
    part = (r % 16) // 8
    n2_of_r = 8 * (r // 16) + r % 8
    k2 = jnp.arange(n2)
    idx = k2[None, :, None] * n2_of_r[None, None, :] * n1 + k1[:, None, None] * n2_of_r[None, None, :]
    cr, ci = cs(idx, n)
    top = jnp.where(part[None, None, :] == 0, cr, -ci)
    bot = jnp.where(part[None, None, :] == 0, ci, cr)
    m = jnp.concatenate([top, bot], axis=1)
    return kf.astype(BF16), ki.astype(BF16), m.astype(BF16)


def _cross_slab_forward(z_ref, kf_ref, a_scr, nz, nh, ct):
    def body(b, carry):
        z = z_ref[:, pl.ds(pl.multiple_of(16 * b, 16), 16), :].astype(F32)
        for half in range(2):
            g = z[:, 8 * half:8 * half + 8, :].reshape(nz * 8, ct).astype(BF16)
            p = jnp.dot(kf_ref[...], g, preferred_element_type=F32)
            start = pl.multiple_of(32 * b + 16 * half, 16)
            a_scr[:, pl.ds(start, 16), :] = p.astype(BF16).reshape(nh, 16, ct)
        return carry

    lax.fori_loop(0, FFT_N2 // 16, body, 0)


def _slab_block(nh):
    return max(k for k in range(1, 17) if nh % k == 0)


def _cross_slab_forward_full(z_ref, kf_ref, a_scr, nh, ct):
    half = nh - 1
    n_even = half // 2 + 1

    def body(b, carry):
        rows = pl.ds(pl.multiple_of(16 * b, 16), 16)
        lo = z_ref[0:half, rows, :].astype(F32)
        hi = z_ref[half:2 * half, rows, :].astype(F32)
        for sub in range(2):
            sl = slice(8 * sub, 8 * sub + 8)
            g_even = (lo[:, sl, :] + hi[:, sl, :]).reshape(half * 8, ct).astype(BF16)
            g_odd = (lo[:, sl, :] - hi[:, sl, :]).reshape(half * 8, ct).astype(BF16)
            p_even = jnp.dot(kf_ref[0:n_even * 16, :], g_even, preferred_element_type=F32)
            p_odd = jnp.dot(kf_ref[n_even * 16:nh * 16, :], g_odd, preferred_element_type=F32)
            start = pl.ds(pl.multiple_of(32 * b + 16 * sub, 16), 16)
            a_scr[0:n_even, start, :] = p_even.astype(BF16).reshape(n_even, 16, ct)
            a_scr[n_even:nh, start, :] = p_odd.astype(BF16).reshape(nh - n_even, 16, ct)
        return carry

    lax.fori_loop(0, FFT_N2 // 16, body, 0)


def _fft_fwd_kernel(z_ref, kf_ref, m_ref, o_ref, a_scr, *, nh, kb, ct):
    kblk = pl.program_id(1)

    @pl.when(kblk == 0)
    def _():
        _cross_slab_forward_full(z_ref, kf_ref, a_scr, nh, ct)

    for k in range(kb):
        x = jnp.dot(m_ref[k], a_scr[kblk * kb + k], preferred_element_type=F32)
        o_ref[k] = x.astype(o_ref.dtype)


def _fft_forward(z3, kf, m, ct=256):
    nz, _, c = z3.shape
    nh = m.shape[0]
    kb = _slab_block(nh)
    assert nz == 2 * (nh - 1)
    return pl.pallas_call(
        functools.partial(_fft_fwd_kernel, nh=nh, kb=kb, ct=ct),
        grid=(c // ct, nh // kb),
        in_specs=[pl.BlockSpec((nz, FFT_N2, ct), lambda j, k: (0, 0, j)),
                  pl.BlockSpec(kf.shape, lambda j, k: (0, 0)),
                  pl.BlockSpec((kb, 2 * FFT_N2, 2 * FFT_N2), lambda j, k: (k, 0, 0))],
        out_specs=pl.BlockSpec((kb, 2 * FFT_N2, ct), lambda j, k: (k, 0, j)),
        out_shape=jax.ShapeDtypeStruct((nh, 2 * FFT_N2, c), BF16),
        scratch_shapes=[pltpu.VMEM((nh, 2 * FFT_N2, ct), BF16)],
        compiler_params=_params(("arbitrary", "arbitrary"), 56),
        name="fft_forward",
    )(z3, kf, m)


def _fft_conv_kernel(z_ref, kf_ref, ki_ref, m_ref, h_ref, *rest, nz, nh, kb, ct, skip_row):
    gated = len(rest) == 4
    o_ref, a_scr = rest[-2:]
    kblk = pl.program_id(1)
    n2 = FFT_N2

    @pl.when(kblk == 0)
    def _():
        _cross_slab_forward(z_ref, kf_ref, a_scr, nz, nh, ct)

    for k in range(kb):
        mk = m_ref[k]
        x = jnp.dot(mk, a_scr[kblk * kb + k], preferred_element_type=F32)
        h = h_ref[k].astype(F32)
        xr, xi, hr, hi = x[:n2], x[n2:], h[:n2], h[n2:]
        y = jnp.concatenate([xr * hr - xi * hi, xr * hi + xi * hr], axis=0).astype(BF16)
        b = lax.dot_general(mk, y, (((0,), (0,)), ((), ())), preferred_element_type=F32)
        a_scr[kblk * kb + k] = b.astype(BF16)

    @pl.when(kblk == pl.num_programs(1) - 1)
    def _():
        def body(b, carry):
            halves = []
            for half in range(2):
                start = pl.multiple_of(32 * b + 16 * half, 16)
                r = a_scr[:, pl.ds(start, 16), :].reshape(nh * 16, ct)
                p = jnp.dot(ki_ref[...], r, preferred_element_type=F32)
                halves.append(p.reshape(nz, 8, ct))
            out = jnp.concatenate(halves, axis=1)
            rows = pl.ds(pl.multiple_of(16 * b, 16), 16)
            if gated:
                x_ref, d_ref = rest[:2]
                z = z_ref[:, rows, :].astype(F32)
                out = x_ref[:, rows, :].astype(F32) * (out + z * d_ref[skip_row:skip_row + 1, :])
            o_ref[:, rows, :] = out.astype(o_ref.dtype)
            return carry

        lax.fori_loop(0, n2 // 16, body, 0)


def _fft_conv(z3, z_blk0, spec, spec_blk0, kf, ki, m, c, gate=None, ct=256):
    nz = z3.shape[0]
    nh = m.shape[0]
    kb = _slab_block(nh)
    assert nh == nz + 1
    in_specs = [pl.BlockSpec((nz, FFT_N2, ct), lambda j, k: (0, 0, z_blk0 + j)),
                pl.BlockSpec(kf.shape, lambda j, k: (0, 0)),
                pl.BlockSpec(ki.shape, lambda j, k: (0, 0)),
                pl.BlockSpec((kb, 2 * FFT_N2, 2 * FFT_N2), lambda j, k: (k, 0, 0)),
                pl.BlockSpec((kb, 2 * FFT_N2, ct), lambda j, k: (k, 0, spec_blk0 + j))]
    operands = [z3, kf, ki, m, spec]
    skip_row = 0
    if gate is not None:
        x3, x_blk0, dskip8, skip_row = gate
        in_specs += [pl.BlockSpec((nz, FFT_N2, ct), lambda j, k: (0, 0, x_blk0 + j)),
                     pl.BlockSpec((8, ct), lambda j, k: (0, j))]
        operands += [x3, dskip8]
    return pl.pallas_call(
        functools.partial(_fft_conv_kernel, nz=nz, nh=nh, kb=kb, ct=ct, skip_row=skip_row),
        grid=(c // ct, nh // kb),
        in_specs=in_specs,
        out_specs=pl.BlockSpec((nz, FFT_N2, ct), lambda j, k: (0, 0, j)),
        out_shape=jax.ShapeDtypeStruct((nz, FFT_N2, c), BF16),
        scratch_shapes=[pltpu.VMEM((nh, 2 * FFT_N2, ct), BF16)],
        compiler_params=_params(("arbitrary", "arbitrary"), 56),
        name="fft_conv",
    )(*operands)


def _ctx_dft_tables(n):
    nn = 2 * n
    k = jnp.arange(nn)
    ang = ((k[:, None] * k[None, :]) % nn).astype(F32) * (2.0 * math.pi / nn)
    cr, ci = jnp.cos(ang), -jnp.sin(ang)
    full = jnp.concatenate([cr, ci], axis=0)
    inv = jnp.concatenate([cr[:n], ci[:n]], axis=1) / nn
    return full[:, :n].astype(BF16), full.astype(BF16), inv.astype(BF16)


def _ctx_hyena_kernel(v_ref, x1_ref, x2_ref, g_ref, cw0_ref, cw1_ref, cw2_ref, hid_ref,
                      w10_ref, w11_ref, w20_ref, w21_ref, t_ref, dec_ref, dsk_ref,
                      ff_ref, ffull_ref, finv_ref, o_ref, *, n):
    row = lax.broadcasted_iota(jnp.int32, (n, 1), 0)

    def sconv(x_ref, w_ref):
        x = x_ref[...].astype(F32)
        up = jnp.where(row == 0, 0.0, pltpu.roll(x, 1, 0))
        down = jnp.where(row == n - 1, 0.0, pltpu.roll(x, n - 1, 0))
        y = up * w_ref[0:1, :] + x * w_ref[1:2, :] + down * w_ref[2:3, :]
        return y.astype(BF16).astype(F32)

    hid = hid_ref[...].astype(BF16)
    tcol = t_ref[...]
    row2 = lax.broadcasted_iota(jnp.int32, (2 * n, 1), 0)

    def spectrum(wf_ref, wb_ref, o):
        fwd = jnp.dot(hid[:n], wf_ref[...], preferred_element_type=F32)
        bwd = jnp.dot(hid[n:], wb_ref[...], preferred_element_type=F32)
        f = jnp.concatenate([fwd, bwd], axis=0) * jnp.exp(-tcol * jnp.abs(dec_ref[o:o + 1, :]))
        f = jnp.where(row2 == n, 0.0, f).astype(BF16)
        return jnp.dot(ffull_ref[...], f, preferred_element_type=F32)

    def conv(z, h):
        x = jnp.dot(ff_ref[...], z.astype(BF16), preferred_element_type=F32)
        h = h.astype(BF16).astype(F32)
        nn = 2 * n
        xr, xi, hr, hi = x[:nn], x[nn:], h[:nn], h[nn:]
        y = jnp.concatenate([xr * hr - xi * hi, xr * hi + xi * hr], axis=0).astype(BF16)
        return jnp.dot(finv_ref[...], y, preferred_element_type=F32)

    v = sconv(v_ref, cw0_ref)
    x1 = sconv(x1_ref, cw1_ref)
    x2 = sconv(x2_ref, cw2_ref)
    c1 = conv(v, spectrum(w10_ref, w11_ref, 0)).astype(BF16).astype(F32)
    y1 = (x1 * (c1 + v * dsk_ref[0:1, :])).astype(BF16).astype(F32)
    c2 = conv(y1, spectrum(w20_ref, w21_ref, 1)).astype(BF16).astype(F32)
    z = x2 * (c2 + y1 * dsk_ref[1:2, :])
    o_ref[...] = (z * _silu(g_ref[...].astype(F32))).astype(o_ref.dtype)


def _ctx_hyena(u, n_lat, hy, order, off_gate, conv8, hidden, hid_row0, w3p, tcol, decay8, dskip8, tables):
    t = u.shape[0]
    n = t - n_lat
    assert order == 2 and n_lat % n == 0 and hid_row0 % (2 * n) == 0
    ct = 256
    nb = hy // ct
    rb = n_lat // n
    ff, ffull, finv = tables
    const = lambda j: (0, 0)
    ublk = lambda c0: pl.BlockSpec((n, ct), lambda j: (rb, c0 + j))
    wblk = lambda c0: pl.BlockSpec((LANE, ct), lambda j: (0, c0 + j))
    cwblk = lambda c0: pl.BlockSpec((8, ct), lambda j: (0, c0 + j))
    return pl.pallas_call(
        functools.partial(_ctx_hyena_kernel, n=n),
        grid=(nb,),
        in_specs=[ublk(0), ublk(nb), ublk(2 * nb), ublk(off_gate // ct),
                  cwblk(0), cwblk(nb), cwblk(2 * nb),
                  pl.BlockSpec((2 * n, LANE), lambda j: (hid_row0 // (2 * n), 0)),
                  wblk(0), wblk(nb), wblk(2 * nb), wblk(3 * nb),
                  pl.BlockSpec((2 * n, 1), const),
                  pl.BlockSpec((8, ct), lambda j: (0, j)),
                  pl.BlockSpec((8, ct), lambda j: (0, j)),
                  pl.BlockSpec(ff.shape, const), pl.BlockSpec(ffull.shape, const),
                  pl.BlockSpec(finv.shape, const)],
        out_specs=pl.BlockSpec((n, ct), lambda j: (0, j)),
        out_shape=jax.ShapeDtypeStruct((n, hy), BF16),
        compiler_params=_params(("arbitrary",), 40),
        name="ctx_hyena",
    )(u, u, u, u, conv8, conv8, conv8, hidden, w3p, w3p, w3p, w3p, tcol, decay8, dskip8,
      ff, ffull, finv)


def _even_layer(x_all, h, mods_l, n_lat, w_in, w_out, conv_w, w1, b1, w2, b2, w3, freq, decay, dskip,
                q_norm, k_norm, rope, dft, ctx_dft, last):
    t, d = x_all.shape
    n_ctx = t - n_lat
    order, hy = decay.shape
    gqa_w = d - hy
    n_q = gqa_w // HEAD_DIM
    n_kv = n_q // 4
    kv_w = n_kv * HEAD_DIM
    off_hy_gate = (order + 1) * hy
    off_q = off_hy_gate + hy
    off_v = off_q + gqa_w + kv_w
    off_at_gate = off_v + kv_w
    scale = HEAD_DIM ** -0.5

    u = _in_proj(h, *w_in)

    q_gain = q_norm.astype(F32) * (scale * math.log2(math.e))
    k_gain = k_norm.astype(F32)
    qt, kn, vt = _attn_prep(u, off_q, off_q + gqa_w, off_v, n_q, n_kv, 1, q_gain, k_gain,
                            rope[0], rope[1], True)
    att = _gqa_attention(_scores_bounded(q_gain, k_gain), qt, kn, vt, u, n_q, n_kv, off_at_gate, n_lat)

    n_bands = (w1.shape[0] - 1) // 2
    feat_l, t_l = _filter_features(n_lat, n_bands)
    feat_c, t_c = _filter_features(n_ctx, n_bands)
    hidden = _filter_hidden(jnp.concatenate([feat_l, feat_c], axis=0), w1, b1, w2, b2, freq)
    w3p = jnp.zeros((LANE, w3.shape[1]), BF16).at[:w3.shape[0]].set(w3.astype(BF16))
    decay8 = jnp.zeros((order, 8, hy), F32).at[:, 0].set(decay.astype(F32))
    dskip8 = jnp.zeros((8, hy), F32).at[:order].set(dskip.astype(F32))
    conv8 = jnp.zeros((8, conv_w.shape[1]), F32).at[:HY_SHORT].set(conv_w.astype(F32))
    kf_pad, ki, m = dft
    n1 = 2 * n_lat // FFT_N2
    filt = _filter_gen(hidden, 0, w3p, t_l, decay8, n_lat, order, hy)
    spec = _fft_forward(filt, kf_pad, m)

    vx3 = _short_conv(u, (order + 1) * hy, conv_w, n_lat)
    ct = 256
    y1 = _fft_conv(vx3, 0, spec, 0, kf_pad, ki, m, hy, gate=(vx3, hy // ct, dskip8, 0))
    c2 = _fft_conv(y1, 0, spec, hy // ct, kf_pad, ki, m, hy)

    decay_c8 = jnp.zeros((8, hy), F32).at[:order].set(decay.astype(F32))
    hy_ctx = _ctx_hyena(u, n_lat, hy, order, off_hy_gate, conv8, hidden, 2 * n_lat, w3p, t_c,
                        decay_c8, dskip8, ctx_dft)
    hy_mix = _hy_gate2(vx3, c2, y1, dskip8, u, off_hy_gate, hy_ctx, hy)

    return _out_proj(hy_mix, 0, att, 0, *w_out, x_all, mods_l, n_lat, last)


def _odd_layer(x_all, h, mods_l, n_lat, w_in, w_out, q_norm, k_norm, rel_bias, last):
    t, d = x_all.shape
    n_heads = rel_bias.shape[0]
    w = n_heads * HEAD_DIM
    scale = HEAD_DIM ** -0.5
    u = _in_proj(h, *w_in)
    log2e = math.log2(math.e)
    q_gain = q_norm.astype(F32) * (scale * log2e)
    k_gain = k_norm.astype(F32)
    heads_per_step = NA_HEADS_PER_STEP
    unused = jnp.zeros((t, HEAD_DIM), F32)
    qt, kn, vt = _attn_prep(u, 0, w, 2 * w, heads_per_step, heads_per_step, n_heads // heads_per_step,
                            q_gain, k_gain, unused, unused, False)
    bias_tbl = _na_bias_table(rel_bias.astype(F32), n_lat // GRID_W)
    bounded = _scores_bounded(q_gain, k_gain, jnp.max(jnp.abs(rel_bias)).astype(F32) * (log2e * 1.02))
    att = _na_attention(bounded, qt, kn, vt, u, n_heads, 3 * w, bias_tbl, n_lat)
    return _out_proj(att, 0, att, 1, *w_out, x_all, mods_l, n_lat, last)


def kernel(x, c, ctx, c_ctx, norm_g, ada_w, ada_b, e_w_in, e_w_out, hy_conv, hy_w1, hy_b1, hy_w2, hy_b2,
           hy_w3, hy_freq, hy_decay, hy_dskip, gqa_q_norm, gqa_k_norm, o_w_in, o_w_out, na_q_norm,
           na_k_norm, na_rel_bias):
    batch, n_lat, d = x.shape
    n_ctx = ctx.shape[1]
    depth = norm_g.shape[0]
    assert batch == 1 and n_lat % (2 * FFT_N2) == 0 and n_ctx == NA_Q_ROWS * GRID_W

    cvecs = jnp.zeros((8, d), F32).at[0].set(c[0]).at[1].set(c_ctx)
    mods = _ada_mods(cvecs, ada_w, ada_b)

    rope = _rope_tables(n_lat, n_ctx)
    n1 = 2 * n_lat // FFT_N2
    dft = _dft_tables(n1)
    ctx_dft = _ctx_dft_tables(n_ctx)

    for layer in range(depth):
        i = layer // 2
        last = layer == depth - 1
        if layer == 0:
            h, x_all = _norm_mod_join(x[0].astype(F32), ctx[0].astype(F32), norm_g[0], mods[0])
        else:
            h = _norm_mod(x_all, norm_g[layer], mods[layer], n_lat)
        if layer % 2 == 0:
            x_all = _even_layer(x_all, h, mods[layer], n_lat,
                                (e_w_in, i), (e_w_out, i), hy_conv[i],
                                hy_w1[i], hy_b1[i], hy_w2[i], hy_b2[i], hy_w3[i], hy_freq[i],
                                hy_decay[i], hy_dskip[i], gqa_q_norm[i], gqa_k_norm[i],
                                rope, dft, ctx_dft, last)
        else:
            x_all = _odd_layer(x_all, h, mods[layer], n_lat,
                               (o_w_in, i), (o_w_out, i),
                               na_q_norm[i], na_k_norm[i], na_rel_bias[i], last)
    return x_all[None]
```

```python
import functools
import math

import numpy as np
import jax
import jax.numpy as jnp
from jax import lax
from jax.experimental import pallas as pl
from jax.experimental.pallas import tpu as pltpu

F32 = jnp.float32
BF16 = jnp.bfloat16

HEAD_DIM = 128
GRID_W = 64
EPS = 1e-6
ROPE_THETA = 10000.0
NA_KH = 8
NA_KW = 16
NA_Q_ROWS = 4
NA_WIN_ROWS = 12
NA_HEADS_PER_STEP = 8
GQA_GROUP = 4
HY_SHORT = 3
FFT_N2 = 128
FFT_CT = 256
NEG_INF = -1e30

V7X_VMEM_BYTES = 64 * 1024 * 1024
LANE = 128


def _params(semantics, vmem_mb):
    assert vmem_mb * 1024 * 1024 < V7X_VMEM_BYTES
    return pltpu.CompilerParams(dimension_semantics=semantics,
                                vmem_limit_bytes=vmem_mb * 1024 * 1024)


def _largest_tile(n, target, quantum):
    best = None
    t = quantum
    while t <= min(n, target):
        if n % t == 0:
            best = t
        t += quantum
    assert best is not None, (n, target, quantum)
    return best


def _silu(x):
    return x * jax.nn.sigmoid(x)


def _mods_kernel(c_ref, w_ref, b_ref, o_ref):
    s = _silu(c_ref[...]).astype(BF16)
    o_ref[...] = jnp.dot(s, w_ref[...].astype(BF16), preferred_element_type=F32) + b_ref[...]


def _ada_mods(cvecs, ada_w, ada_b):
    depth, d, n = ada_w.shape
    tn = _largest_tile(n, 512, LANE)
    return pl.pallas_call(
        _mods_kernel,
        grid=(depth, n // tn),
        in_specs=[pl.BlockSpec((8, d), lambda l, j: (0, 0)),
                  pl.BlockSpec((None, d, tn), lambda l, j: (l, 0, j)),
                  pl.BlockSpec((None, 1, tn), lambda l, j: (l, 0, j))],
        out_specs=pl.BlockSpec((None, 8, tn), lambda l, j: (l, 0, j)),
        out_shape=jax.ShapeDtypeStruct((depth, 8, n), F32),
        compiler_params=_params(("arbitrary", "arbitrary"), 40),
        name="ada_mods",
    )(cvecs, ada_w, ada_b.reshape(depth, 1, n))


def _norm_mod_kernel(x_ref, g_ref, sh_ref, sc_ref, o_ref, *, n_lat_tiles):
    x = x_ref[...]
    y = x * lax.rsqrt(jnp.mean(x * x, axis=-1, keepdims=True) + EPS) * g_ref[...]
    is_ctx = pl.program_id(0) >= n_lat_tiles
    sh = jnp.where(is_ctx, sh_ref[1:2, :], sh_ref[0:1, :])
    sc = jnp.where(is_ctx, sc_ref[1:2, :], sc_ref[0:1, :])
    o_ref[...] = (y * (1.0 + sc) + sh).astype(o_ref.dtype)


def _norm_mod(x_all, g, mods_l, n_lat):
    t, d = x_all.shape
    tr = 256
    assert t % tr == 0 and n_lat % tr == 0
    return pl.pallas_call(
        functools.partial(_norm_mod_kernel, n_lat_tiles=n_lat // tr),
        grid=(t // tr,),
        in_specs=[pl.BlockSpec((tr, d), lambda i: (i, 0)),
                  pl.BlockSpec((1, d), lambda i: (0, 0)),
                  pl.BlockSpec((8, d), lambda i: (0, 0)),
                  pl.BlockSpec((8, d), lambda i: (0, 1))],
        out_specs=pl.BlockSpec((tr, d), lambda i: (i, 0)),
        out_shape=jax.ShapeDtypeStruct((t, d), BF16),
        compiler_params=_params(("arbitrary",), 40),
        name="norm_mod",
    )(x_all, g.reshape(1, d), mods_l, mods_l)


def _norm_mod_join_kernel(x_ref, c_ref, g_ref, sh_ref, sc_ref, o_ref, xo_ref, *, n_lat_tiles):
    is_ctx = pl.program_id(0) >= n_lat_tiles
    x = jnp.where(is_ctx, c_ref[...], x_ref[...])
    y = x * lax.rsqrt(jnp.mean(x * x, axis=-1, keepdims=True) + EPS) * g_ref[...]
    sh = jnp.where(is_ctx, sh_ref[1:2, :], sh_ref[0:1, :])
    sc = jnp.where(is_ctx, sc_ref[1:2, :], sc_ref[0:1, :])
    o_ref[...] = (y * (1.0 + sc) + sh).astype(o_ref.dtype)
    xo_ref[...] = x


def _norm_mod_join(x_lat, x_ctx, g, mods_l):
    n_lat, d = x_lat.shape
    n_ctx = x_ctx.shape[0]
    tr = n_ctx
    assert n_lat % tr == 0
    n_lat_tiles = n_lat // tr
    t = n_lat + n_ctx
    return pl.pallas_call(
        functools.partial(_norm_mod_join_kernel, n_lat_tiles=n_lat_tiles),
        grid=(t // tr,),
        in_specs=[pl.BlockSpec((tr, d), lambda i: (jnp.minimum(i, n_lat_tiles - 1), 0)),
                  pl.BlockSpec((tr, d), lambda i: (0, 0)),
                  pl.BlockSpec((1, d), lambda i: (0, 0)),
                  pl.BlockSpec((8, d), lambda i: (0, 0)),
                  pl.BlockSpec((8, d), lambda i: (0, 1))],
        out_specs=[pl.BlockSpec((tr, d), lambda i: (i, 0)),
                   pl.BlockSpec((tr, d), lambda i: (i, 0))],
        out_shape=[jax.ShapeDtypeStruct((t, d), BF16), jax.ShapeDtypeStruct((t, d), F32)],
        compiler_params=_params(("arbitrary",), 48),
        name="norm_mod_join",
    )(x_lat, x_ctx, g.reshape(1, d), mods_l, mods_l)


def _in_proj_kernel(a_ref, w_ref, o_ref):
    w = w_ref[...].astype(BF16)
    o_ref[...] = jnp.dot(a_ref[...], w, preferred_element_type=F32).astype(o_ref.dtype)


def _in_proj(h, w_all, layer_idx):
    m, k = h.shape
    n = w_all.shape[2]
    tm = _largest_tile(m, 1408, 128)
    tn = _largest_tile(n, 512, 256)
    return pl.pallas_call(
        _in_proj_kernel,
        grid=(m // tm, n // tn),
        in_specs=[pl.BlockSpec((tm, k), lambda i, j: (i, 0)),
                  pl.BlockSpec((None, k, tn), lambda i, j: (layer_idx, 0, j))],
        out_specs=pl.BlockSpec((tm, tn), lambda i, j: (i, j)),
        out_shape=jax.ShapeDtypeStruct((m, n), BF16),
        compiler_params=_params(("arbitrary", "arbitrary"), 56),
        name="in_proj",
    )(h, w_all)


def _out_proj_kernel(a1_ref, a2_ref, w1_ref, w2_ref, x_ref, gate_ref, o_ref, wb1, wb2, *, tm, n_lat):
    i = pl.program_id(1)

    @pl.when(i == 0)
    def _():
        wb1[...] = w1_ref[...].astype(BF16)
        wb2[...] = w2_ref[...].astype(BF16)

    y = jnp.dot(a1_ref[...], wb1[...], preferred_element_type=F32)
    y = y + jnp.dot(a2_ref[...], wb2[...], preferred_element_type=F32)
    rows = i * tm + lax.broadcasted_iota(jnp.int32, (tm, 1), 0)
    gate = jnp.where(rows >= n_lat, gate_ref[1:2, :], gate_ref[0:1, :])
    o_ref[...] = x_ref[...] + gate * y


def _out_proj(a1, a1_blk, a2, a2_blk, w_all, layer_idx, x_all, mods_l, n_lat, latent_only):
    d = x_all.shape[1]
    t = n_lat if latent_only else x_all.shape[0]
    kh = w_all.shape[1] // 2
    tm = _largest_tile(t, 768, 256)
    tn = _largest_tile(d, 1024, 256)
    gate_blk0 = 2 * d // tn
    once = pl.Buffered(1)
    return pl.pallas_call(
        functools.partial(_out_proj_kernel, tm=tm, n_lat=n_lat),
        grid=(d // tn, t // tm),
        in_specs=[pl.BlockSpec((tm, kh), lambda j, i: (i, a1_blk)),
                  pl.BlockSpec((tm, kh), lambda j, i: (i, a2_blk)),
                  pl.BlockSpec((None, kh, tn), lambda j, i: (layer_idx, 0, j), pipeline_mode=once),
                  pl.BlockSpec((None, kh, tn), lambda j, i: (layer_idx, 1, j), pipeline_mode=once),
                  pl.BlockSpec((tm, tn), lambda j, i: (i, j)),
                  pl.BlockSpec((8, tn), lambda j, i: (0, gate_blk0 + j))],
        out_specs=pl.BlockSpec((tm, tn), lambda j, i: (i, j)),
        out_shape=jax.ShapeDtypeStruct((t, d), F32),
        scratch_shapes=[pltpu.VMEM((kh, tn), BF16), pltpu.VMEM((kh, tn), BF16)],
        compiler_params=_params(("arbitrary", "arbitrary"), 56),
        name="out_proj",
    )(a1, a2, w_all, w_all, x_all, mods_l)


def _rope_tables(n_lat, n_ctx):
    quarter = HEAD_DIM // 4
    inv = ROPE_THETA ** (-jnp.arange(quarter, dtype=F32) / quarter)
    tok = jnp.arange(n_lat)
    rows = (tok // GRID_W).astype(F32)[:, None] * inv[None]
    cols = (tok % GRID_W).astype(F32)[:, None] * inv[None]
    cos = jnp.concatenate([jnp.cos(rows), jnp.cos(rows), jnp.cos(cols), jnp.cos(cols)], axis=-1)
    sin = jnp.concatenate([-jnp.sin(rows), jnp.sin(rows), -jnp.sin(cols), jnp.sin(cols)], axis=-1)
    cos = jnp.concatenate([cos, jnp.ones((n_ctx, HEAD_DIM), F32)], axis=0)
    sin = jnp.concatenate([sin, jnp.zeros((n_ctx, HEAD_DIM), F32)], axis=0)
    return cos, sin


GQA_ONES_ROWS = 16


def _attn_prep_kernel(q_ref, k_ref, v_ref, g_ref, cos_ref, sin_ref, qt_ref, ko_ref, vt_ref,
                      *, n_q, n_kv, rope):
    if rope:
        cos = cos_ref[...]
        sin = sin_ref[...]
        lane = lax.broadcasted_iota(jnp.int32, cos.shape, 1)
        first_quarter = (lane % 64) < 32

    def norm_rope(x, g):
        y = x * lax.rsqrt(jnp.mean(x * x, axis=-1, keepdims=True) + EPS) * g
        if not rope:
            return y
        partner = jnp.where(first_quarter, pltpu.roll(y, HEAD_DIM - 32, 1), pltpu.roll(y, 32, 1))
        return y * cos + partner * sin

    for h in range(n_q):
        sl = slice(h * HEAD_DIM, (h + 1) * HEAD_DIM)
        qt_ref[sl, :] = norm_rope(q_ref[:, sl].astype(F32), g_ref[:, sl]).T.astype(qt_ref.dtype)
    for h in range(n_kv):
        sl = slice(h * HEAD_DIM, (h + 1) * HEAD_DIM)
        gk = g_ref[:, (n_q + h) * HEAD_DIM:(n_q + h + 1) * HEAD_DIM]
        ko_ref[:, sl] = norm_rope(k_ref[:, sl].astype(F32), gk).astype(ko_ref.dtype)
        r0 = h * (HEAD_DIM + GQA_ONES_ROWS)
        vt_ref[r0:r0 + HEAD_DIM, :] = v_ref[:, sl].astype(F32).T.astype(vt_ref.dtype)
        vt_ref[r0 + HEAD_DIM:r0 + HEAD_DIM + GQA_ONES_ROWS, :] = jnp.ones(
            (GQA_ONES_ROWS, vt_ref.shape[1]), vt_ref.dtype)


def _attn_prep(u, off_q, off_k, off_v, n_q, n_kv, n_groups, q_gain, k_gain, cos_t, sin_t, rope):
    t = u.shape[0]
    qw, kw = n_q * HEAD_DIM, n_kv * HEAD_DIM
    tr = _largest_tile(t, 768, 256)
    vrows = n_kv * (HEAD_DIM + GQA_ONES_ROWS)
    assert off_q % qw == 0 and off_k % kw == 0 and off_v % kw == 0 and t % tr == 0
    q_blk, k_blk, v_blk = off_q // qw, off_k // kw, off_v // kw
    gvec = jnp.concatenate([jnp.tile(q_gain, n_q), jnp.tile(k_gain, n_kv)])[None]
    return pl.pallas_call(
        functools.partial(_attn_prep_kernel, n_q=n_q, n_kv=n_kv, rope=rope),
        grid=(t // tr, n_groups),
        in_specs=[pl.BlockSpec((tr, qw), lambda i, j: (i, q_blk + j)),
                  pl.BlockSpec((tr, kw), lambda i, j: (i, k_blk + j)),
                  pl.BlockSpec((tr, kw), lambda i, j: (i, v_blk + j)),
                  pl.BlockSpec((1, qw + kw), lambda i, j: (0, 0)),
                  pl.BlockSpec((tr, HEAD_DIM), lambda i, j: (i, 0)),
                  pl.BlockSpec((tr, HEAD_DIM), lambda i, j: (i, 0))],
        out_specs=[pl.BlockSpec((qw, tr), lambda i, j: (j, i)),
                   pl.BlockSpec((tr, kw), lambda i, j: (i, j)),
                   pl.BlockSpec((vrows, tr), lambda i, j: (j, i))],
        out_shape=[jax.ShapeDtypeStruct((n_groups * qw, t), BF16),
                   jax.ShapeDtypeStruct((t, n_groups * kw), BF16),
                   jax.ShapeDtypeStruct((n_groups * vrows, t), BF16)],
        compiler_params=_params(("arbitrary", "arbitrary"), 48),
        name="attn_prep",
    )(u, u, u, gvec, cos_t, sin_t)


def _gqa_kernel(bounded_ref, qt_ref, k_ref, vt_ref, gate_ref, o_ref, m_s, acc_s,
                *, group, tq, tk, n_lat, n_ctx):
    acc_s[...] = jnp.zeros(acc_s.shape, F32)

    def chunk_bounded(kc, vtc):
        ps = []
        for j in range(group):
            qt = qt_ref[j * HEAD_DIM:(j + 1) * HEAD_DIM, :]
            s = jnp.dot(kc, qt, preferred_element_type=F32)
            ps.append(jnp.exp2(s).astype(BF16))
        for j in range(group):
            acc_s[j] += jnp.dot(vtc, ps[j], preferred_element_type=F32)

    def chunk_online(kc, vtc):
        for j in range(group):
            qt = qt_ref[j * HEAD_DIM:(j + 1) * HEAD_DIM, :]
            s = jnp.dot(kc, qt, preferred_element_type=F32)
            m = m_s[j]
            m_new = jnp.maximum(m, jnp.max(s, axis=0, keepdims=True))
            p = jnp.exp2(s - m_new).astype(BF16)
            acc_s[j] = jnp.exp2(m - m_new) * acc_s[j] + jnp.dot(vtc, p, preferred_element_type=F32)
            m_s[j] = m_new

    def sweep(chunk):
        @pl.when(pl.program_id(1) * tq < n_lat)
        def _():
            def body(c, carry):
                r0 = pl.multiple_of(c * tk, tk)
                chunk(k_ref[pl.ds(r0, tk), :], vt_ref[:, pl.ds(r0, tk)])
                return carry

            lax.fori_loop(0, n_lat // tk, body, 0)

        chunk(k_ref[n_lat:n_lat + n_ctx, :], vt_ref[:, n_lat:n_lat + n_ctx])

    @pl.when(bounded_ref[0] != 0)
    def _():
        sweep(chunk_bounded)

    @pl.when(bounded_ref[0] == 0)
    def _():
        m_s[...] = jnp.full(m_s.shape, NEG_INF, F32)
        sweep(chunk_online)

    for j in range(group):
        sl = slice(j * HEAD_DIM, (j + 1) * HEAD_DIM)
        acc = acc_s[j]
        o = (acc[:HEAD_DIM] / acc[HEAD_DIM:HEAD_DIM + 1]).T
        o_ref[:, sl] = (o * _silu(gate_ref[:, sl].astype(F32))).astype(o_ref.dtype)


SCORE_BOUND_LOG2 = 60.0


def _scores_bounded(q_gain, k_gain, extra=0.0):
    bound = HEAD_DIM * jnp.max(jnp.abs(q_gain)) * jnp.max(jnp.abs(k_gain)) * 1.02 + extra
    return (bound <= SCORE_BOUND_LOG2).astype(jnp.int32).reshape(1)


def _gqa_attention(bounded, qt, k, vt, u, n_q_heads, n_kv_heads, off_gate, n_lat):
    t = u.shape[0]
    group = n_q_heads // n_kv_heads
    gw = group * HEAD_DIM
    tq = 256
    tk = _largest_tile(n_lat, 4096, 256)
    vrows = HEAD_DIM + GQA_ONES_ROWS
    assert t % tq == 0 and n_lat % tq == 0 and off_gate % gw == 0
    g_blk0 = off_gate // gw
    grid_spec = pltpu.PrefetchScalarGridSpec(
        num_scalar_prefetch=1,
        grid=(n_kv_heads, t // tq),
        in_specs=[pl.BlockSpec((gw, tq), lambda g, i, b: (g, i)),
                  pl.BlockSpec((t, HEAD_DIM), lambda g, i, b: (0, g)),
                  pl.BlockSpec((vrows, t), lambda g, i, b: (g, 0)),
                  pl.BlockSpec((tq, gw), lambda g, i, b: (i, g_blk0 + g))],
        out_specs=pl.BlockSpec((tq, gw), lambda g, i, b: (i, g)),
        scratch_shapes=[pltpu.VMEM((group, 1, tq), F32),
                        pltpu.VMEM((group, vrows, tq), F32)])
    return pl.pallas_call(
        functools.partial(_gqa_kernel, group=group, tq=tq, tk=tk, n_lat=n_lat, n_ctx=t - n_lat),
        grid_spec=grid_spec,
        out_shape=jax.ShapeDtypeStruct((t, n_q_heads * HEAD_DIM), BF16),
        compiler_params=_params(("arbitrary", "arbitrary"), 48),
        name="gqa_attention",
    )(bounded, qt, k, vt, u)


def _na_kernel(bounded_ref, qt_ref, k0_ref, k1_ref, k2_ref, kc_ref, v0_ref, v1_ref, v2_ref, vc_ref,
               bias_ref, gate_ref, o_ref, *, n_lat_blocks):
    is_latent = pl.program_id(1) < n_lat_blocks
    is_bounded = bounded_ref[0] != 0
    vrows = HEAD_DIM + GQA_ONES_ROWS

    def probabilities(j, latent, bounded):
        hs = slice(j * HEAD_DIM, (j + 1) * HEAD_DIM)
        qt = qt_ref[hs, :]
        scores = [jnp.dot(kc_ref[:, hs], qt, preferred_element_type=F32)]
        if latent:
            k = jnp.concatenate([k0_ref[:, hs], k1_ref[:, hs], k2_ref[:, hs]], axis=0)
            scores.append(jnp.dot(k, qt, preferred_element_type=F32) + bias_ref[j].astype(F32))
        if not bounded:
            m = functools.reduce(jnp.maximum, [jnp.max(s, axis=0, keepdims=True) for s in scores])
            scores = [s - m for s in scores]
        return [jnp.exp2(s).astype(BF16) for s in scores]

    def output(j, probs, latent):
        hs = slice(j * HEAD_DIM, (j + 1) * HEAD_DIM)
        vs = slice(j * vrows, (j + 1) * vrows)
        acc = jnp.dot(vc_ref[vs, :], probs[0], preferred_element_type=F32)
        if latent:
            vt = jnp.concatenate([v0_ref[vs, :], v1_ref[vs, :], v2_ref[vs, :]], axis=1)
            acc = acc + jnp.dot(vt, probs[1], preferred_element_type=F32)
        o = (acc[:HEAD_DIM] / acc[HEAD_DIM:HEAD_DIM + 1]).T
        o_ref[:, hs] = (o * _silu(gate_ref[:, hs].astype(F32))).astype(o_ref.dtype)

    heads = range(qt_ref.shape[0] // HEAD_DIM)
    for latent in (True, False):
        for bounded in (True, False):
            @pl.when(jnp.logical_and(is_latent == latent, is_bounded == bounded))
            def _():
                probs = [probabilities(j, latent, bounded) for j in heads]
                for j in heads:
                    output(j, probs[j], latent)


def _na_bias_table(rel_bias, n_rows):
    c = np.arange(GRID_W)[None, :]
    kj = np.arange(GRID_W)[:, None]
    cs = np.clip(c - NA_KW // 2, 0, GRID_W - NA_KW)
    col_ok = (kj >= cs) & (kj < cs + NA_KW)
    sel_c = (np.arange(2 * NA_KW - 1)[:, None, None] == (kj - c + NA_KW - 1)[None]) & col_ok[None]
    cols = jnp.einsum('hde,ekc->hdkc', rel_bias * math.log2(math.e), jnp.asarray(sel_c, F32),
                      precision=lax.Precision.HIGHEST)
    cols = jnp.where(jnp.asarray(col_ok)[None, None], cols, NEG_INF).astype(BF16)
    neg = jnp.full((rel_bias.shape[0], GRID_W, GRID_W), NEG_INF, BF16)
    tables = []
    for r0, ws in ((0, 0), (NA_Q_ROWS, 0), (n_rows - NA_Q_ROWS, n_rows - NA_WIN_ROWS)):
        key_rows = []
        for i in range(NA_WIN_ROWS):
            blocks = []
            for a in range(NA_Q_ROWS):
                r, kr = r0 + a, ws + i
                rs = min(max(r - NA_KH // 2, 0), n_rows - NA_KH)
                blocks.append(cols[:, kr - r + NA_KH - 1] if rs <= kr < rs + NA_KH else neg)
            key_rows.append(jnp.concatenate(blocks, axis=-1))
        tables.append(jnp.concatenate(key_rows, axis=-2))
    return jnp.stack(tables, axis=0)


def _na_attention(bounded, qt, kn, vt, u, n_heads, off_gate, bias_tbl, n_lat):
    t = u.shape[0]
    tb = NA_Q_ROWS * GRID_W
    n_lat_blocks = n_lat // tb
    ctx_blk = t // tb - 1
    hp = NA_HEADS_PER_STEP
    hw = hp * HEAD_DIM
    vrows = hp * (HEAD_DIM + GQA_ONES_ROWS)
    assert t - n_lat == tb and n_lat_blocks >= 3 and off_gate % hw == 0 and n_heads % hp == 0
    g_blk0 = off_gate // hw

    def win(i, j):
        return jnp.clip(i - 1, 0, n_lat_blocks - 3) + j

    def k_spec(j):
        return pl.BlockSpec((tb, hw), lambda h, i, b: (win(i, j), h))

    def v_spec(j):
        return pl.BlockSpec((vrows, tb), lambda h, i, b: (h, win(i, j)))

    def bias_cls(h, i, b):
        return (jnp.where(i == 0, 0, jnp.where(i >= n_lat_blocks - 1, 2, 1)), h, 0, 0)

    grid_spec = pltpu.PrefetchScalarGridSpec(
        num_scalar_prefetch=1,
        grid=(n_heads // hp, t // tb),
        in_specs=[pl.BlockSpec((hw, tb), lambda h, i, b: (h, i)),
                  k_spec(0), k_spec(1), k_spec(2),
                  pl.BlockSpec((tb, hw), lambda h, i, b: (ctx_blk, h)),
                  v_spec(0), v_spec(1), v_spec(2),
                  pl.BlockSpec((vrows, tb), lambda h, i, b: (h, ctx_blk)),
                  pl.BlockSpec((None, hp, NA_WIN_ROWS * GRID_W, tb), bias_cls),
                  pl.BlockSpec((tb, hw), lambda h, i, b: (i, g_blk0 + h))],
        out_specs=pl.BlockSpec((tb, hw), lambda h, i, b: (i, h)))
    return pl.pallas_call(
        functools.partial(_na_kernel, n_lat_blocks=n_lat_blocks),
        grid_spec=grid_spec,
        out_shape=jax.ShapeDtypeStruct((t, n_heads * HEAD_DIM), BF16),
        compiler_params=_params(("arbitrary", "arbitrary"), 32),
        name="na_attention",
    )(bounded, qt, kn, kn, kn, kn, vt, vt, vt, vt, bias_tbl, u)


def _short_conv_kernel(x_ref, prev_ref, next_ref, w_ref, o_ref, *, tr):
    i = pl.program_id(0)
    n = pl.num_programs(0)
    x = x_ref[...].astype(F32)
    halo = prev_ref.shape[0]
    prev_row = jnp.where(i == 0, 0.0, prev_ref[halo - 1:halo, :].astype(F32))
    next_row = jnp.where(i == n - 1, 0.0, next_ref[0:1, :].astype(F32))
    row = lax.broadcasted_iota(jnp.int32, x.shape, 0)
    up = jnp.where(row == 0, prev_row, pltpu.roll(x, 1, 0))
    down = jnp.where(row == tr - 1, next_row, pltpu.roll(x, tr - 1, 0))
    y = (up * w_ref[0:1, :] + x * w_ref[1:2, :] + down * w_ref[2:3, :]).astype(o_ref.dtype)
    o_ref[...] = y.reshape(o_ref.shape)


def _short_conv(u, width, conv_w, n_lat):
    tr = _largest_tile(n_lat, 1024, 256)
    tc = _largest_tile(width, 1024, 256)
    halo = 16
    hb = tr // halo
    n_tiles = n_lat // tr
    wpad = jnp.zeros((8, width), F32).at[:HY_SHORT].set(conv_w.astype(F32))
    return pl.pallas_call(
        functools.partial(_short_conv_kernel, tr=tr),
        grid=(n_tiles, width // tc),
        in_specs=[pl.BlockSpec((tr, tc), lambda i, j: (i, j)),
                  pl.BlockSpec((halo, tc), lambda i, j: (jnp.maximum(i * hb - 1, 0), j)),
                  pl.BlockSpec((halo, tc), lambda i, j: (jnp.minimum((i + 1) * hb, n_tiles * hb - 1), j)),
                  pl.BlockSpec((8, tc), lambda i, j: (0, j))],
        out_specs=pl.BlockSpec((tr // FFT_N2, FFT_N2, tc), lambda i, j: (i, 0, j)),
        out_shape=jax.ShapeDtypeStruct((n_lat // FFT_N2, FFT_N2, width), BF16),
        compiler_params=_params(("arbitrary", "arbitrary"), 32),
        name="short_conv",
    )(u, u, u, wpad)


def _hy_gate2_kernel(x2_ref, c_ref, y1_ref, d_ref, g_ref, ctx_ref, o_ref, *, n_lat_tiles):
    i = pl.program_id(0)

    @pl.when(i < n_lat_tiles)
    def _():
        rows = lambda ref: ref[...].reshape(o_ref.shape).astype(F32)
        y1 = rows(y1_ref)
        z = rows(x2_ref) * (rows(c_ref) + y1 * d_ref[1:2, :])
        o_ref[...] = (z * _silu(g_ref[...].astype(F32))).astype(o_ref.dtype)

    @pl.when(i >= n_lat_tiles)
    def _():
        o_ref[...] = ctx_ref[...]


def _hy_gate2(vx3, c2, y1, dskip8, u, off_gate, hy_ctx, hy):
    t = u.shape[0]
    n_lat = vx3.shape[0] * FFT_N2
    tr = t - n_lat
    tc = _largest_tile(hy, 2048, 256)
    nb = hy // tc
    n_lat_tiles = n_lat // tr
    assert n_lat % tr == 0 and off_gate % tc == 0 and tr % FFT_N2 == 0
    g_blk0 = off_gate // tc
    lat = lambda i: jnp.minimum(i, n_lat_tiles - 1)
    slab = lambda c0: pl.BlockSpec((tr // FFT_N2, FFT_N2, tc), lambda i, j: (lat(i), 0, c0 + j))
    return pl.pallas_call(
        functools.partial(_hy_gate2_kernel, n_lat_tiles=n_lat_tiles),
        grid=(t // tr, nb),
        in_specs=[slab(2 * nb), slab(0), slab(0),
                  pl.BlockSpec((8, tc), lambda i, j: (0, j)),
                  pl.BlockSpec((tr, tc), lambda i, j: (i, g_blk0 + j)),
                  pl.BlockSpec((tr, tc), lambda i, j: (0, j))],
        out_specs=pl.BlockSpec((tr, tc), lambda i, j: (i, j)),
        out_shape=jax.ShapeDtypeStruct((t, hy), BF16),
        compiler_params=_params(("arbitrary", "arbitrary"), 32),
        name="hy_gate2",
    )(vx3, c2, y1, dskip8, u, hy_ctx)


def _filter_hidden_kernel(z_ref, w1_ref, b1_ref, w2_ref, b2_ref, fr_ref, o_ref):
    hp = lax.Precision.HIGHEST
    fr = fr_ref[...]
    h = jnp.sin(fr * (jnp.dot(z_ref[...], w1_ref[...], precision=hp,
                              preferred_element_type=F32) + b1_ref[...]))
    o_ref[...] = jnp.sin(fr * (jnp.dot(h, w2_ref[...], precision=hp,
                                       preferred_element_type=F32) + b2_ref[...]))


def _pad2(a, rows, cols):
    return jnp.zeros((rows, cols), F32).at[:a.shape[0], :a.shape[1]].set(a.astype(F32))


def _filter_hidden(feat, w1, b1, w2, b2, freq):
    r = feat.shape[0]
    tr = _largest_tile(r, 1024, 256)
    full = lambda i: (0, 0)
    return pl.pallas_call(
        _filter_hidden_kernel,
        grid=(r // tr,),
        in_specs=[pl.BlockSpec((tr, LANE), lambda i: (i, 0)),
                  pl.BlockSpec((LANE, LANE), full), pl.BlockSpec((1, LANE), full),
                  pl.BlockSpec((LANE, LANE), full), pl.BlockSpec((1, LANE), full),
                  pl.BlockSpec((1, LANE), full)],
        out_specs=pl.BlockSpec((tr, LANE), lambda i: (i, 0)),
        out_shape=jax.ShapeDtypeStruct((r, LANE), F32),
        compiler_params=_params(("arbitrary",), 32),
        name="filter_hidden",
    )(feat, _pad2(w1, LANE, LANE), _pad2(b1[None], 1, LANE), _pad2(w2, LANE, LANE),
      _pad2(b2[None], 1, LANE), _pad2(freq[None], 1, LANE))


def _two_sided_positions(n):
    idx = jnp.arange(2 * n)
    return jnp.where(idx < n, idx, 2 * n - idx) % n


def _filter_features(n, n_bands):
    pos = _two_sided_positions(n).astype(F32)[:, None]
    t = pos / (n - 1)
    w = (2.0 * math.pi / n) * pos
    bands = jnp.linspace(1e-4, n_bands - 1, n_bands, dtype=F32)[None, :]
    z = jnp.concatenate([t, jnp.cos(bands * w), -jnp.sin(bands * w)], axis=-1)
    return _pad2(z, 2 * n, LANE), t


def _filter_gen_kernel(h_ref, w_ref, t_ref, dec_ref, o_ref, *, tr, n):
    i = pl.program_id(0)
    f = jnp.dot(h_ref[...].astype(BF16), w_ref[...], preferred_element_type=F32)
    f = f * jnp.exp(-t_ref[...] * jnp.abs(dec_ref[0:1, :]))
    row = i * tr + lax.broadcasted_iota(jnp.int32, (tr, 1), 0)
    o_ref[...] = jnp.where(row == n, 0.0, f).astype(o_ref.dtype).reshape(o_ref.shape)


def _filter_gen(hidden, row0, w3p, tcol, decay8, n, order, hy):
    tr = _largest_tile(n, 1024, 256)
    tc = _largest_tile(hy, 1024, 256)
    nb = hy // tc
    n_half = n // tr
    hb0 = row0 // tr
    assert row0 % tr == 0
    return pl.pallas_call(
        functools.partial(_filter_gen_kernel, tr=tr, n=n),
        grid=(2 * n // tr, order * nb),
        in_specs=[pl.BlockSpec((tr, LANE), lambda i, j: (hb0 + i, 0)),
                  pl.BlockSpec((LANE, tc), lambda i, j: (0, (j // nb) * 2 * nb + (i // n_half) * nb + j % nb)),
                  pl.BlockSpec((tr, 1), lambda i, j: (i, 0)),
                  pl.BlockSpec((None, 8, tc), lambda i, j: (j // nb, 0, j % nb))],
        out_specs=pl.BlockSpec((tr // FFT_N2, FFT_N2, tc), lambda i, j: (i, 0, j)),
        out_shape=jax.ShapeDtypeStruct((2 * n // FFT_N2, FFT_N2, order * hy), BF16),
        compiler_params=_params(("arbitrary", "arbitrary"), 32),
        name="filter_gen",
    )(hidden, w3p, tcol, decay8)


def _dft_tables(n1):
    n2 = FFT_N2
    n = n1 * n2
    half = n1 // 2
    nz = half
    k1 = jnp.concatenate([jnp.arange(0, half + 1, 2), jnp.arange(1, half + 1, 2)])
    two_pi = 2.0 * math.pi

    def cs(idx, period):
        ang = (idx % period).astype(F32) * (two_pi / period)
        return jnp.cos(ang), -jnp.sin(ang)

    eye8 = jnp.eye(8, dtype=F32)
    fr, fi = cs(k1[:, None] * jnp.arange(nz)[None, :], n1)
    kf = jnp.stack([fr, fi], axis=1)
    kf = jnp.einsum('kpn,jl->kpjnl', kf, eye8).reshape((half + 1) * 16, nz * 8)
    gr, gi = cs(jnp.arange(half)[:, None] * k1[None, :], n1)
    weight = jnp.where((k1 == 0) | (k1 == half), 1.0, 2.0) / n
    ki = jnp.stack([gr, gi], axis=2) * weight[None, :, None]
    ki = jnp.einsum('nkp,jl->njkpl', ki, eye8).reshape(half * 8, (half + 1) * 16)
    r = jnp.arange(2 * n2)
    part = (r % 16) // 8
    n2_of_r = 8 * (r // 16) + r % 8
    k2 = jnp.arange(n2)
    idx = k2[None, :, None] * n2_of_r[None, None, :] * n1 + k1[:, None, None] * n2_of_r[None, None, :]
    cr, ci = cs(idx, n)
    top = jnp.where(part[None, None, :] == 0, cr, -ci)
    bot = jnp.where(part[None, None, :] == 0, ci, cr)
    m = jnp.concatenate([top, bot], axis=1)
    return kf.astype(BF16), ki.astype(BF16), m.astype(BF16)


def _cross_slab_forward(z_ref, kf_ref, a_scr, nz, nh, ct):
    def body(b, carry):
        z = z_ref[:, pl.ds(pl.multiple_of(16 * b, 16), 16), :].astype(F32)
        for half in range(2):
            g = z[:, 8 * half:8 * half + 8, :].reshape(nz * 8, ct).astype(BF16)
            p = jnp.dot(kf_ref[...], g, preferred_element_type=F32)
            start = pl.multiple_of(32 * b + 16 * half, 16)
            a_scr[:, pl.ds(start, 16), :] = p.astype(BF16).reshape(nh, 16, ct)
        return carry

    lax.fori_loop(0, FFT_N2 // 16, body, 0)


def _slab_block(nh):
    return max(k for k in range(1, 17) if nh % k == 0)


def _cross_slab_forward_full(z_ref, kf_ref, a_scr, nh, ct):
    half = nh - 1
    n_even = half // 2 + 1

    def body(b, carry):
        rows = pl.ds(pl.multiple_of(16 * b, 16), 16)
        lo = z_ref[0:half, rows, :].astype(F32)
        hi = z_ref[half:2 * half, rows, :].astype(F32)
        for sub in range(2):
            sl = slice(8 * sub, 8 * sub + 8)
            g_even = (lo[:, sl, :] + hi[:, sl, :]).reshape(half * 8, ct).astype(BF16)
            g_odd = (lo[:, sl, :] - hi[:, sl, :]).reshape(half * 8, ct).astype(BF16)
            p_even = jnp.dot(kf_ref[0:n_even * 16, :], g_even, preferred_element_type=F32)
            p_odd = jnp.dot(kf_ref[n_even * 16:nh * 16, :], g_odd, preferred_element_type=F32)
            start = pl.ds(pl.multiple_of(32 * b + 16 * sub, 16), 16)
            a_scr[0:n_even, start, :] = p_even.astype(BF16).reshape(n_even, 16, ct)
            a_scr[n_even:nh, start, :] = p_odd.astype(BF16).reshape(nh - n_even, 16, ct)
        return carry

    lax.fori_loop(0, FFT_N2 // 16, body, 0)


def _fft_fwd_kernel(z_ref, kf_ref, m_ref, o_ref, a_scr, *, nh, kb, ct):
    kblk = pl.program_id(1)

    @pl.when(kblk == 0)
    def _():
        _cross_slab_forward_full(z_ref, kf_ref, a_scr, nh, ct)

    for k in range(kb):
        x = jnp.dot(m_ref[k], a_scr[kblk * kb + k], preferred_element_type=F32)
        o_ref[k] = x.astype(o_ref.dtype)


def _fft_forward(z3, kf, m, ct=FFT_CT):
    nz, _, c = z3.shape
    nh = m.shape[0]
    kb = _slab_block(nh)
    assert nz == 2 * (nh - 1)
    return pl.pallas_call(
        functools.partial(_fft_fwd_kernel, nh=nh, kb=kb, ct=ct),
        grid=(c // ct, nh // kb),
        in_specs=[pl.BlockSpec((nz, FFT_N2, ct), lambda j, k: (0, 0, j)),
                  pl.BlockSpec(kf.shape, lambda j, k: (0, 0)),
                  pl.BlockSpec((kb, 2 * FFT_N2, 2 * FFT_N2), lambda j, k: (k, 0, 0))],
        out_specs=pl.BlockSpec((kb, 2 * FFT_N2, ct), lambda j, k: (k, 0, j)),
        out_shape=jax.ShapeDtypeStruct((nh, 2 * FFT_N2, c), BF16),
        scratch_shapes=[pltpu.VMEM((nh, 2 * FFT_N2, ct), BF16)],
        compiler_params=_params(("arbitrary", "arbitrary"), 56),
        name="fft_forward",
    )(z3, kf, m)


def _fft_conv_kernel(z_ref, kf_ref, ki_ref, m_ref, h_ref, *rest, nz, nh, kb, ct, skip_row):
    gated = len(rest) == 4
    o_ref, a_scr = rest[-2:]
    kblk = pl.program_id(1)
    n2 = FFT_N2

    @pl.when(kblk == 0)
    def _():
        _cross_slab_forward(z_ref, kf_ref, a_scr, nz, nh, ct)

    for k in range(kb):
        mk = m_ref[k]
        x = jnp.dot(mk, a_scr[kblk * kb + k], preferred_element_type=F32)
        h = h_ref[k].astype(F32)
        xr, xi, hr, hi = x[:n2], x[n2:], h[:n2], h[n2:]
        y = jnp.concatenate([xr * hr - xi * hi, xr * hi + xi * hr], axis=0).astype(BF16)
        b = lax.dot_general(mk, y, (((0,), (0,)), ((), ())), preferred_element_type=F32)
        a_scr[kblk * kb + k] = b.astype(BF16)

    @pl.when(kblk == pl.num_programs(1) - 1)
    def _():
        def body(b, carry):
            halves = []
            for half in range(2):
                start = pl.multiple_of(32 * b + 16 * half, 16)
                r = a_scr[:, pl.ds(start, 16), :].reshape(nh * 16, ct)
                p = jnp.dot(ki_ref[...], r, preferred_element_type=F32)
                halves.append(p.reshape(nz, 8, ct))
            out = jnp.concatenate(halves, axis=1)
            rows = pl.ds(pl.multiple_of(16 * b, 16), 16)
            if gated:
                x_ref, d_ref = rest[:2]
                z = z_ref[:, rows, :].astype(F32)
                out = x_ref[:, rows, :].astype(F32) * (out + z * d_ref[skip_row:skip_row + 1, :])
            o_ref[:, rows, :] = out.astype(o_ref.dtype)
            return carry

        lax.fori_loop(0, n2 // 16, body, 0)


def _fft_conv(z3, z_blk0, spec, spec_blk0, kf, ki, m, c, gate=None, ct=FFT_CT):
    nz = z3.shape[0]
    nh = m.shape[0]
    kb = _slab_block(nh)
    assert nh == nz + 1
    in_specs = [pl.BlockSpec((nz, FFT_N2, ct), lambda j, k: (0, 0, z_blk0 + j)),
                pl.BlockSpec(kf.shape, lambda j, k: (0, 0)),
                pl.BlockSpec(ki.shape, lambda j, k: (0, 0)),
                pl.BlockSpec((kb, 2 * FFT_N2, 2 * FFT_N2), lambda j, k: (k, 0, 0)),
                pl.BlockSpec((kb, 2 * FFT_N2, ct), lambda j, k: (k, 0, spec_blk0 + j))]
    operands = [z3, kf, ki, m, spec]
    skip_row = 0
    if gate is not None:
        x3, x_blk0, dskip8, skip_row = gate
        in_specs += [pl.BlockSpec((nz, FFT_N2, ct), lambda j, k: (0, 0, x_blk0 + j)),
                     pl.BlockSpec((8, ct), lambda j, k: (0, j))]
        operands += [x3, dskip8]
    return pl.pallas_call(
        functools.partial(_fft_conv_kernel, nz=nz, nh=nh, kb=kb, ct=ct, skip_row=skip_row),
        grid=(c // ct, nh // kb),
        in_specs=in_specs,
        out_specs=pl.BlockSpec((nz, FFT_N2, ct), lambda j, k: (0, 0, j)),
        out_shape=jax.ShapeDtypeStruct((nz, FFT_N2, c), BF16),
        scratch_shapes=[pltpu.VMEM((nh, 2 * FFT_N2, ct), BF16)],
        compiler_params=_params(("arbitrary", "arbitrary"), 56),
        name="fft_conv",
    )(*operands)


def _ctx_dft_tables(n):
    nn = 2 * n
    k = jnp.arange(nn)
    ang = ((k[:, None] * k[None, :]) % nn).astype(F32) * (2.0 * math.pi / nn)
    cr, ci = jnp.cos(ang), -jnp.sin(ang)
    full = jnp.concatenate([cr, ci], axis=0)
    inv = jnp.concatenate([cr[:n], ci[:n]], axis=1) / nn
    return full[:, :n].astype(BF16), full.astype(BF16), inv.astype(BF16)


def _ctx_hyena_kernel(v_ref, x1_ref, x2_ref, g_ref, cw0_ref, cw1_ref, cw2_ref, hid_ref,
                      w10_ref, w11_ref, w20_ref, w21_ref, t_ref, dec_ref, dsk_ref,
                      ff_ref, ffull_ref, finv_ref, o_ref, *, n):
    row = lax.broadcasted_iota(jnp.int32, (n, 1), 0)

    def sconv(x_ref, w_ref):
        x = x_ref[...].astype(F32)
        up = jnp.where(row == 0, 0.0, pltpu.roll(x, 1, 0))
        down = jnp.where(row == n - 1, 0.0, pltpu.roll(x, n - 1, 0))
        y = up * w_ref[0:1, :] + x * w_ref[1:2, :] + down * w_ref[2:3, :]
        return y.astype(BF16).astype(F32)

    hid = hid_ref[...].astype(BF16)
    tcol = t_ref[...]
    row2 = lax.broadcasted_iota(jnp.int32, (2 * n, 1), 0)

    def spectrum(wf_ref, wb_ref, o):
        fwd = jnp.dot(hid[:n], wf_ref[...], preferred_element_type=F32)
        bwd = jnp.dot(hid[n:], wb_ref[...], preferred_element_type=F32)
        f = jnp.concatenate([fwd, bwd], axis=0) * jnp.exp(-tcol * jnp.abs(dec_ref[o:o + 1, :]))
        f = jnp.where(row2 == n, 0.0, f).astype(BF16)
        return jnp.dot(ffull_ref[...], f, preferred_element_type=F32)

    def conv(z, h):
        x = jnp.dot(ff_ref[...], z.astype(BF16), preferred_element_type=F32)
        h = h.astype(BF16).astype(F32)
        nn = 2 * n
        xr, xi, hr, hi = x[:nn], x[nn:], h[:nn], h[nn:]
        y = jnp.concatenate([xr * hr - xi * hi, xr * hi + xi * hr], axis=0).astype(BF16)
        return jnp.dot(finv_ref[...], y, preferred_element_type=F32)

    v = sconv(v_ref, cw0_ref)
    x1 = sconv(x1_ref, cw1_ref)
    x2 = sconv(x2_ref, cw2_ref)
    c1 = conv(v, spectrum(w10_ref, w11_ref, 0)).astype(BF16).astype(F32)
    y1 = (x1 * (c1 + v * dsk_ref[0:1, :])).astype(BF16).astype(F32)
    c2 = conv(y1, spectrum(w20_ref, w21_ref, 1)).astype(BF16).astype(F32)
    z = x2 * (c2 + y1 * dsk_ref[1:2, :])
    o_ref[...] = (z * _silu(g_ref[...].astype(F32))).astype(o_ref.dtype)


def _ctx_hyena(u, n_lat, hy, order, off_gate, conv8, hidden, hid_row0, w3p, tcol, decay8, dskip8, tables):
    t = u.shape[0]
    n = t - n_lat
    assert order == 2 and n_lat % n == 0 and hid_row0 % (2 * n) == 0
    ct = FFT_CT
    nb = hy // ct
    rb = n_lat // n
    ff, ffull, finv = tables
    const = lambda j: (0, 0)
    ublk = lambda c0: pl.BlockSpec((n, ct), lambda j: (rb, c0 + j))
    wblk = lambda c0: pl.BlockSpec((LANE, ct), lambda j: (0, c0 + j))
    cwblk = lambda c0: pl.BlockSpec((8, ct), lambda j: (0, c0 + j))
    return pl.pallas_call(
        functools.partial(_ctx_hyena_kernel, n=n),
        grid=(nb,),
        in_specs=[ublk(0), ublk(nb), ublk(2 * nb), ublk(off_gate // ct),
                  cwblk(0), cwblk(nb), cwblk(2 * nb),
                  pl.BlockSpec((2 * n, LANE), lambda j: (hid_row0 // (2 * n), 0)),
                  wblk(0), wblk(nb), wblk(2 * nb), wblk(3 * nb),
                  pl.BlockSpec((2 * n, 1), const),
                  pl.BlockSpec((8, ct), lambda j: (0, j)),
                  pl.BlockSpec((8, ct), lambda j: (0, j)),
                  pl.BlockSpec(ff.shape, const), pl.BlockSpec(ffull.shape, const),
                  pl.BlockSpec(finv.shape, const)],
        out_specs=pl.BlockSpec((n, ct), lambda j: (0, j)),
        out_shape=jax.ShapeDtypeStruct((n, hy), BF16),
        compiler_params=_params(("arbitrary",), 40),
        name="ctx_hyena",
    )(u, u, u, u, conv8, conv8, conv8, hidden, w3p, w3p, w3p, w3p, tcol, decay8, dskip8,
      ff, ffull, finv)


def _even_layer(x_all, h, mods_l, n_lat, w_in, w_out, conv_w, w1, b1, w2, b2, w3, freq, decay, dskip,
                q_norm, k_norm, rope, dft, ctx_dft, last):
    t, d = x_all.shape
    n_ctx = t - n_lat
    order, hy = decay.shape
    gqa_w = d - hy
    n_q = gqa_w // HEAD_DIM
    n_kv = n_q // GQA_GROUP
    kv_w = n_kv * HEAD_DIM
    off_hy_gate = (order + 1) * hy
    off_q = off_hy_gate + hy
    off_v = off_q + gqa_w + kv_w
    off_at_gate = off_v + kv_w
    scale = HEAD_DIM ** -0.5

    u = _in_proj(h, *w_in)

    q_gain = q_norm.astype(F32) * (scale * math.log2(math.e))
    k_gain = k_norm.astype(F32)
    qt, kn, vt = _attn_prep(u, off_q, off_q + gqa_w, off_v, n_q, n_kv, 1, q_gain, k_gain,
                            rope[0], rope[1], True)
    att = _gqa_attention(_scores_bounded(q_gain, k_gain), qt, kn, vt, u, n_q, n_kv, off_at_gate, n_lat)

    n_bands = (w1.shape[0] - 1) // 2
    feat_l, t_l = _filter_features(n_lat, n_bands)
    feat_c, t_c = _filter_features(n_ctx, n_bands)
    hidden = _filter_hidden(jnp.concatenate([feat_l, feat_c], axis=0), w1, b1, w2, b2, freq)
    w3p = jnp.zeros((LANE, w3.shape[1]), BF16).at[:w3.shape[0]].set(w3.astype(BF16))
    decay8 = jnp.zeros((order, 8, hy), F32).at[:, 0].set(decay.astype(F32))
    dskip8 = jnp.zeros((8, hy), F32).at[:order].set(dskip.astype(F32))
    conv8 = jnp.zeros((8, conv_w.shape[1]), F32).at[:HY_SHORT].set(conv_w.astype(F32))
    kf_pad, ki, m = dft
    filt = _filter_gen(hidden, 0, w3p, t_l, decay8, n_lat, order, hy)
    spec = _fft_forward(filt, kf_pad, m)

    vx3 = _short_conv(u, (order + 1) * hy, conv_w, n_lat)
    ct = FFT_CT
    y1 = _fft_conv(vx3, 0, spec, 0, kf_pad, ki, m, hy, gate=(vx3, hy // ct, dskip8, 0))
    c2 = _fft_conv(y1, 0, spec, hy // ct, kf_pad, ki, m, hy)

    decay_c8 = jnp.zeros((8, hy), F32).at[:order].set(decay.astype(F32))
    hy_ctx = _ctx_hyena(u, n_lat, hy, order, off_hy_gate, conv8, hidden, 2 * n_lat, w3p, t_c,
                        decay_c8, dskip8, ctx_dft)
    hy_mix = _hy_gate2(vx3, c2, y1, dskip8, u, off_hy_gate, hy_ctx, hy)

    return _out_proj(hy_mix, 0, att, 0, *w_out, x_all, mods_l, n_lat, last)


def _odd_layer(x_all, h, mods_l, n_lat, w_in, w_out, q_norm, k_norm, rel_bias, rope, last):
    n_heads = rel_bias.shape[0]
    w = n_heads * HEAD_DIM
    scale = HEAD_DIM ** -0.5
    u = _in_proj(h, *w_in)
    log2e = math.log2(math.e)
    q_gain = q_norm.astype(F32) * (scale * log2e)
    k_gain = k_norm.astype(F32)
    hp = NA_HEADS_PER_STEP
    qt, kn, vt = _attn_prep(u, 0, w, 2 * w, hp, hp, n_heads // hp, q_gain, k_gain, rope[0], rope[1], False)
    bias_tbl = _na_bias_table(rel_bias.astype(F32), n_lat // GRID_W)
    bounded = _scores_bounded(q_gain, k_gain, jnp.max(jnp.abs(rel_bias)).astype(F32) * (log2e * 1.02))
    att = _na_attention(bounded, qt, kn, vt, u, n_heads, 3 * w, bias_tbl, n_lat)
    return _out_proj(att, 0, att, 1, *w_out, x_all, mods_l, n_lat, last)


def kernel(x, c, ctx, c_ctx, norm_g, ada_w, ada_b, e_w_in, e_w_out, hy_conv, hy_w1, hy_b1, hy_w2, hy_b2,
           hy_w3, hy_freq, hy_decay, hy_dskip, gqa_q_norm, gqa_k_norm, o_w_in, o_w_out, na_q_norm,
           na_k_norm, na_rel_bias):
    batch, n_lat, d = x.shape
    n_ctx = ctx.shape[1]
    depth = norm_g.shape[0]
    assert batch == 1 and n_lat % (2 * FFT_N2) == 0 and n_ctx == NA_Q_ROWS * GRID_W

    cvecs = jnp.zeros((8, d), F32).at[0].set(c[0]).at[1].set(c_ctx)
    mods = _ada_mods(cvecs, ada_w, ada_b)

    rope = _rope_tables(n_lat, n_ctx)
    n1 = 2 * n_lat // FFT_N2
    dft = _dft_tables(n1)
    ctx_dft = _ctx_dft_tables(n_ctx)

    for layer in range(depth):
        i = layer // 2
        last = layer == depth - 1
        if layer == 0:
            h, x_all = _norm_mod_join(x[0].astype(F32), ctx[0].astype(F32), norm_g[0], mods[0])
        else:
            h = _norm_mod(x_all, norm_g[layer], mods[layer], n_lat)
        if layer % 2 == 0:
            x_all = _even_layer(x_all, h, mods[layer], n_lat,
                                (e_w_in, i), (e_w_out, i), hy_conv[i],
                                hy_w1[i], hy_b1[i], hy_w2[i], hy_b2[i], hy_w3[i], hy_freq[i],
                                hy_decay[i], hy_dskip[i], gqa_q_norm[i], gqa_k_norm[i],
                                rope, dft, ctx_dft, last)
        else:
            x_all = _odd_layer(x_all, h, mods[layer], n_lat,
                               (o_w_in, i), (o_w_out, i),
                               na_q_norm[i], na_k_norm[i], na_rel_bias[i], rope, last)
    return x_all[None]
```

```python
import functools
import math

import numpy as np
import jax
import jax.numpy as jnp
from jax import lax
from jax.experimental import pallas as pl
from jax.experimental.pallas import tpu as pltpu

F32 = jnp.float32
BF16 = jnp.bfloat16

HEAD_DIM = 128
GRID_W = 64
EPS = 1e-6
ROPE_THETA = 10000.0
NA_KH = 8
NA_KW = 16
NA_Q_ROWS = 4
NA_WIN_ROWS = 12
NA_HEADS_PER_STEP = 8
GQA_GROUP = 4
HY_SHORT = 3
FFT_N2 = 128
FFT_CT = 256
NEG_INF = -1e30

V7X_VMEM_BYTES = 64 * 1024 * 1024
LANE = 128


def _params(semantics, vmem_mb):
    assert vmem_mb * 1024 * 1024 < V7X_VMEM_BYTES
    return pltpu.CompilerParams(dimension_semantics=semantics,
                                vmem_limit_bytes=vmem_mb * 1024 * 1024)


def _largest_tile(n, target, quantum):
    best = None
    t = quantum
    while t <= min(n, target):
        if n % t == 0:
            best = t
        t += quantum
    assert best is not None, (n, target, quantum)
    return best


def _silu(x):
    return x * jax.nn.sigmoid(x)


def _mods_kernel(c_ref, w_ref, b_ref, o_ref):
    s = _silu(c_ref[...]).astype(BF16)
    o_ref[...] = jnp.dot(s, w_ref[...].astype(BF16), preferred_element_type=F32) + b_ref[...]


def _ada_mods(cvecs, ada_w, ada_b):
    depth, d, n = ada_w.shape
    tn = _largest_tile(n, 512, LANE)
    return pl.pallas_call(
        _mods_kernel,
        grid=(depth, n // tn),
        in_specs=[pl.BlockSpec((8, d), lambda l, j: (0, 0)),
                  pl.BlockSpec((None, d, tn), lambda l, j: (l, 0, j)),
                  pl.BlockSpec((None, 1, tn), lambda l, j: (l, 0, j))],
        out_specs=pl.BlockSpec((None, 8, tn), lambda l, j: (l, 0, j)),
        out_shape=jax.ShapeDtypeStruct((depth, 8, n), F32),
        compiler_params=_params(("arbitrary", "arbitrary"), 40),
        name="ada_mods",
    )(cvecs, ada_w, ada_b.reshape(depth, 1, n))


def _norm_mod_kernel(x_ref, g_ref, sh_ref, sc_ref, o_ref, *, n_lat_tiles):
    x = x_ref[...]
    y = x * lax.rsqrt(jnp.mean(x * x, axis=-1, keepdims=True) + EPS) * g_ref[...]
    is_ctx = pl.program_id(0) >= n_lat_tiles
    sh = jnp.where(is_ctx, sh_ref[1:2, :], sh_ref[0:1, :])
    sc = jnp.where(is_ctx, sc_ref[1:2, :], sc_ref[0:1, :])
    o_ref[...] = (y * (1.0 + sc) + sh).astype(o_ref.dtype)


def _norm_mod(x_all, g, mods_l, n_lat):
    t, d = x_all.shape
    tr = 256
    assert t % tr == 0 and n_lat % tr == 0
    return pl.pallas_call(
        functools.partial(_norm_mod_kernel, n_lat_tiles=n_lat // tr),
        grid=(t // tr,),
        in_specs=[pl.BlockSpec((tr, d), lambda i: (i, 0)),
                  pl.BlockSpec((1, d), lambda i: (0, 0)),
                  pl.BlockSpec((8, d), lambda i: (0, 0)),
                  pl.BlockSpec((8, d), lambda i: (0, 1))],
        out_specs=pl.BlockSpec((tr, d), lambda i: (i, 0)),
        out_shape=jax.ShapeDtypeStruct((t, d), BF16),
        compiler_params=_params(("arbitrary",), 40),
        name="norm_mod",
    )(x_all, g.reshape(1, d), mods_l, mods_l)


def _norm_mod_join_kernel(x_ref, c_ref, g_ref, sh_ref, sc_ref, o_ref, xo_ref, *, n_lat_tiles):
    is_ctx = pl.program_id(0) >= n_lat_tiles
    x = jnp.where(is_ctx, c_ref[...], x_ref[...])
    y = x * lax.rsqrt(jnp.mean(x * x, axis=-1, keepdims=True) + EPS) * g_ref[...]
    sh = jnp.where(is_ctx, sh_ref[1:2, :], sh_ref[0:1, :])
    sc = jnp.where(is_ctx, sc_ref[1:2, :], sc_ref[0:1, :])
    o_ref[...] = (y * (1.0 + sc) + sh).astype(o_ref.dtype)
    xo_ref[...] = x


def _norm_mod_join(x_lat, x_ctx, g, mods_l):
    n_lat, d = x_lat.shape
    n_ctx = x_ctx.shape[0]
    tr = n_ctx
    assert n_lat % tr == 0
    n_lat_tiles = n_lat // tr
    t = n_lat + n_ctx
    return pl.pallas_call(
        functools.partial(_norm_mod_join_kernel, n_lat_tiles=n_lat_tiles),
        grid=(t // tr,),
        in_specs=[pl.BlockSpec((tr, d), lambda i: (jnp.minimum(i, n_lat_tiles - 1), 0)),
                  pl.BlockSpec((tr, d), lambda i: (0, 0)),
                  pl.BlockSpec((1, d), lambda i: (0, 0)),
                  pl.BlockSpec((8, d), lambda i: (0, 0)),
                  pl.BlockSpec((8, d), lambda i: (0, 1))],
        out_specs=[pl.BlockSpec((tr, d), lambda i: (i, 0)),
                   pl.BlockSpec((tr, d), lambda i: (i, 0))],
        out_shape=[jax.ShapeDtypeStruct((t, d), BF16), jax.ShapeDtypeStruct((t, d), F32)],
        compiler_params=_params(("arbitrary",), 48),
        name="norm_mod_join",
    )(x_lat, x_ctx, g.reshape(1, d), mods_l, mods_l)


def _in_proj_kernel(a_ref, w_ref, o_ref):
    w = w_ref[...].astype(BF16)
    o_ref[...] = jnp.dot(a_ref[...], w, preferred_element_type=F32).astype(o_ref.dtype)


def _in_proj(h, w_all, layer_idx):
    m, k = h.shape
    n = w_all.shape[2]
    tm = _largest_tile(m, 1408, 128)
    tn = _largest_tile(n, 512, 256)
    return pl.pallas_call(
        _in_proj_kernel,
        grid=(m // tm, n // tn),
        in_specs=[pl.BlockSpec((tm, k), lambda i, j: (i, 0)),
                  pl.BlockSpec((None, k, tn), lambda i, j: (layer_idx, 0, j))],
        out_specs=pl.BlockSpec((tm, tn), lambda i, j: (i, j)),
        out_shape=jax.ShapeDtypeStruct((m, n), BF16),
        compiler_params=_params(("arbitrary", "arbitrary"), 56),
        name="in_proj",
    )(h, w_all)


def _out_proj_kernel(a1_ref, a2_ref, w1_ref, w2_ref, x_ref, gate_ref, o_ref, wb1, wb2, *, tm, n_lat):
    i = pl.program_id(1)

    @pl.when(i == 0)
    def _():
        wb1[...] = w1_ref[...].astype(BF16)
        wb2[...] = w2_ref[...].astype(BF16)

    y = jnp.dot(a1_ref[...], wb1[...], preferred_element_type=F32)
    y = y + jnp.dot(a2_ref[...], wb2[...], preferred_element_type=F32)
    rows = i * tm + lax.broadcasted_iota(jnp.int32, (tm, 1), 0)
    gate = jnp.where(rows >= n_lat, gate_ref[1:2, :], gate_ref[0:1, :])
    o_ref[...] = x_ref[...] + gate * y


def _out_proj(a1, a1_blk, a2, a2_blk, w_all, layer_idx, x_all, mods_l, n_lat, latent_only):
    d = x_all.shape[1]
    t = n_lat if latent_only else x_all.shape[0]
    kh = w_all.shape[1] // 2
    tm = _largest_tile(t, 768, 256)
    tn = _largest_tile(d, 1024, 256)
    gate_blk0 = 2 * d // tn
    once = pl.Buffered(1)
    return pl.pallas_call(
        functools.partial(_out_proj_kernel, tm=tm, n_lat=n_lat),
        grid=(d // tn, t // tm),
        in_specs=[pl.BlockSpec((tm, kh), lambda j, i: (i, a1_blk)),
                  pl.BlockSpec((tm, kh), lambda j, i: (i, a2_blk)),
                  pl.BlockSpec((None, kh, tn), lambda j, i: (layer_idx, 0, j), pipeline_mode=once),
                  pl.BlockSpec((None, kh, tn), lambda j, i: (layer_idx, 1, j), pipeline_mode=once),
                  pl.BlockSpec((tm, tn), lambda j, i: (i, j)),
                  pl.BlockSpec((8, tn), lambda j, i: (0, gate_blk0 + j))],
        out_specs=pl.BlockSpec((tm, tn), lambda j, i: (i, j)),
        out_shape=jax.ShapeDtypeStruct((t, d), F32),
        scratch_shapes=[pltpu.VMEM((kh, tn), BF16), pltpu.VMEM((kh, tn), BF16)],
        compiler_params=_params(("arbitrary", "arbitrary"), 56),
        name="out_proj",
    )(a1, a2, w_all, w_all, x_all, mods_l)


def _rope_tables(n_lat, n_ctx):
    quarter = HEAD_DIM // 4
    inv = ROPE_THETA ** (-np.arange(quarter, dtype=np.float64) / quarter)
    tok = np.arange(n_lat)
    rows = (tok // GRID_W)[:, None] * inv[None]
    cols = (tok % GRID_W)[:, None] * inv[None]
    cos = np.concatenate([np.cos(rows), np.cos(rows), np.cos(cols), np.cos(cols)], axis=-1)
    sin = np.concatenate([-np.sin(rows), np.sin(rows), -np.sin(cols), np.sin(cols)], axis=-1)
    cos = np.concatenate([cos, np.ones((n_ctx, HEAD_DIM))], axis=0)
    sin = np.concatenate([sin, np.zeros((n_ctx, HEAD_DIM))], axis=0)
    return jnp.asarray(cos, F32), jnp.asarray(sin, F32)


GQA_ONES_ROWS = 16


def _attn_prep_kernel(q_ref, k_ref, v_ref, g_ref, cos_ref, sin_ref, qt_ref, ko_ref, vt_ref,
                      *, n_q, n_kv, rope):
    if rope:
        cos = cos_ref[...]
        sin = sin_ref[...]
        lane = lax.broadcasted_iota(jnp.int32, cos.shape, 1)
        first_quarter = (lane % 64) < 32

    def norm_rope(x, g):
        y = x * lax.rsqrt(jnp.mean(x * x, axis=-1, keepdims=True) + EPS) * g
        if not rope:
            return y
        partner = jnp.where(first_quarter, pltpu.roll(y, HEAD_DIM - 32, 1), pltpu.roll(y, 32, 1))
        return y * cos + partner * sin

    for h in range(n_q):
        sl = slice(h * HEAD_DIM, (h + 1) * HEAD_DIM)
        qt_ref[sl, :] = norm_rope(q_ref[:, sl].astype(F32), g_ref[:, sl]).T.astype(qt_ref.dtype)
    for h in range(n_kv):
        sl = slice(h * HEAD_DIM, (h + 1) * HEAD_DIM)
        gk = g_ref[:, (n_q + h) * HEAD_DIM:(n_q + h + 1) * HEAD_DIM]
        ko_ref[:, sl] = norm_rope(k_ref[:, sl].astype(F32), gk).astype(ko_ref.dtype)
        r0 = h * (HEAD_DIM + GQA_ONES_ROWS)
        vt_ref[r0:r0 + HEAD_DIM, :] = v_ref[:, sl].astype(F32).T.astype(vt_ref.dtype)
        vt_ref[r0 + HEAD_DIM:r0 + HEAD_DIM + GQA_ONES_ROWS, :] = jnp.ones(
            (GQA_ONES_ROWS, vt_ref.shape[1]), vt_ref.dtype)


def _attn_prep(u, off_q, off_k, off_v, n_q, n_kv, n_groups, q_gain, k_gain, cos_t, sin_t, rope):
    t = u.shape[0]
    qw, kw = n_q * HEAD_DIM, n_kv * HEAD_DIM
    tr = _largest_tile(t, 768, 256)
    vrows = n_kv * (HEAD_DIM + GQA_ONES_ROWS)
    assert off_q % qw == 0 and off_k % kw == 0 and off_v % kw == 0 and t % tr == 0
    q_blk, k_blk, v_blk = off_q // qw, off_k // kw, off_v // kw
    gvec = jnp.concatenate([jnp.tile(q_gain, n_q), jnp.tile(k_gain, n_kv)])[None]
    return pl.pallas_call(
        functools.partial(_attn_prep_kernel, n_q=n_q, n_kv=n_kv, rope=rope),
        grid=(t // tr, n_groups),
        in_specs=[pl.BlockSpec((tr, qw), lambda i, j: (i, q_blk + j)),
                  pl.BlockSpec((tr, kw), lambda i, j: (i, k_blk + j)),
                  pl.BlockSpec((tr, kw), lambda i, j: (i, v_blk + j)),
                  pl.BlockSpec((1, qw + kw), lambda i, j: (0, 0)),
                  pl.BlockSpec((tr, HEAD_DIM), lambda i, j: (i, 0)),
                  pl.BlockSpec((tr, HEAD_DIM), lambda i, j: (i, 0))],
        out_specs=[pl.BlockSpec((qw, tr), lambda i, j: (j, i)),
                   pl.BlockSpec((tr, kw), lambda i, j: (i, j)),
                   pl.BlockSpec((vrows, tr), lambda i, j: (j, i))],
        out_shape=[jax.ShapeDtypeStruct((n_groups * qw, t), BF16),
                   jax.ShapeDtypeStruct((t, n_groups * kw), BF16),
                   jax.ShapeDtypeStruct((n_groups * vrows, t), BF16)],
        compiler_params=_params(("arbitrary", "arbitrary"), 48),
        name="attn_prep",
    )(u, u, u, gvec, cos_t, sin_t)


def _gqa_kernel(bounded_ref, qt_ref, k_ref, vt_ref, gate_ref, o_ref, m_s, acc_s,
                *, group, tq, tk, n_lat, n_ctx):
    acc_s[...] = jnp.zeros(acc_s.shape, F32)

    def chunk_bounded(kc, vtc):
        ps = []
        for j in range(group):
            qt = qt_ref[j * HEAD_DIM:(j + 1) * HEAD_DIM, :]
            s = jnp.dot(kc, qt, preferred_element_type=F32)
            ps.append(jnp.exp2(s).astype(BF16))
        for j in range(group):
            acc_s[j] += jnp.dot(vtc, ps[j], preferred_element_type=F32)

    def chunk_online(kc, vtc):
        for j in range(group):
            qt = qt_ref[j * HEAD_DIM:(j + 1) * HEAD_DIM, :]
            s = jnp.dot(kc, qt, preferred_element_type=F32)
            m = m_s[j]
            m_new = jnp.maximum(m, jnp.max(s, axis=0, keepdims=True))
            p = jnp.exp2(s - m_new).astype(BF16)
            acc_s[j] = jnp.exp2(m - m_new) * acc_s[j] + jnp.dot(vtc, p, preferred_element_type=F32)
            m_s[j] = m_new

    def sweep(chunk):
        is_latent = pl.program_id(1) * tq < n_lat

        @pl.when(is_latent)
        def _():
            def body(c, carry):
                r0 = pl.multiple_of(c * tk, LANE)
                chunk(k_ref[pl.ds(r0, tk), :], vt_ref[:, pl.ds(r0, tk)])
                return carry

            lax.fori_loop(0, (n_lat + n_ctx) // tk, body, 0)

        @pl.when(jnp.logical_not(is_latent))
        def _():
            chunk(k_ref[n_lat:n_lat + n_ctx, :], vt_ref[:, n_lat:n_lat + n_ctx])

    @pl.when(bounded_ref[0] != 0)
    def _():
        sweep(chunk_bounded)

    @pl.when(bounded_ref[0] == 0)
    def _():
        m_s[...] = jnp.full(m_s.shape, NEG_INF, F32)
        sweep(chunk_online)

    for j in range(group):
        sl = slice(j * HEAD_DIM, (j + 1) * HEAD_DIM)
        acc = acc_s[j]
        o = (acc[:HEAD_DIM] / acc[HEAD_DIM:HEAD_DIM + 1]).T
        o_ref[:, sl] = (o * _silu(gate_ref[:, sl].astype(F32))).astype(o_ref.dtype)


SCORE_BOUND_LOG2 = 60.0


def _scores_bounded(q_gain, k_gain, extra=0.0):
    bound = HEAD_DIM * jnp.max(jnp.abs(q_gain)) * jnp.max(jnp.abs(k_gain)) * 1.02 + extra
    return (bound <= SCORE_BOUND_LOG2).astype(jnp.int32).reshape(1)


def _gqa_attention(bounded, qt, k, vt, u, n_q_heads, n_kv_heads, off_gate, n_lat):
    t = u.shape[0]
    group = n_q_heads // n_kv_heads
    gw = group * HEAD_DIM
    tq = 256
    tk = _largest_tile(t, 4608, LANE)
    vrows = HEAD_DIM + GQA_ONES_ROWS
    assert t % tq == 0 and n_lat % tq == 0 and off_gate % gw == 0
    g_blk0 = off_gate // gw
    grid_spec = pltpu.PrefetchScalarGridSpec(
        num_scalar_prefetch=1,
        grid=(n_kv_heads, t // tq),
        in_specs=[pl.BlockSpec((gw, tq), lambda g, i, b: (g, i)),
                  pl.BlockSpec((t, HEAD_DIM), lambda g, i, b: (0, g)),
                  pl.BlockSpec((vrows, t), lambda g, i, b: (g, 0)),
                  pl.BlockSpec((tq, gw), lambda g, i, b: (i, g_blk0 + g))],
        out_specs=pl.BlockSpec((tq, gw), lambda g, i, b: (i, g)),
        scratch_shapes=[pltpu.VMEM((group, 1, tq), F32),
                        pltpu.VMEM((group, vrows, tq), F32)])
    return pl.pallas_call(
        functools.partial(_gqa_kernel, group=group, tq=tq, tk=tk, n_lat=n_lat, n_ctx=t - n_lat),
        grid_spec=grid_spec,
        out_shape=jax.ShapeDtypeStruct((t, n_q_heads * HEAD_DIM), BF16),
        compiler_params=_params(("arbitrary", "arbitrary"), 48),
        name="gqa_attention",
    )(bounded, qt, k, vt, u)


def _na_kernel(bounded_ref, qt_ref, k0_ref, k1_ref, k2_ref, kc_ref, v0_ref, v1_ref, v2_ref, vc_ref,
               bias_ref, gate_ref, o_ref, *, n_lat_blocks):
    is_latent = pl.program_id(1) < n_lat_blocks
    is_bounded = bounded_ref[0] != 0
    vrows = HEAD_DIM + GQA_ONES_ROWS

    def probabilities(j, latent, bounded):
        hs = slice(j * HEAD_DIM, (j + 1) * HEAD_DIM)
        qt = qt_ref[hs, :]
        if not latent:
            s = jnp.dot(kc_ref[:, hs], qt, preferred_element_type=F32)
        else:
            n_ctx = kc_ref.shape[0]
            k = jnp.concatenate([kc_ref[:, hs], k0_ref[:, hs], k1_ref[:, hs], k2_ref[:, hs]], axis=0)
            s = jnp.dot(k, qt, preferred_element_type=F32)
            s = jnp.concatenate([s[:n_ctx], s[n_ctx:] + bias_ref[j].astype(F32)], axis=0)
        if not bounded:
            s = s - jnp.max(s, axis=0, keepdims=True)
        return jnp.exp2(s).astype(BF16)

    def output(j, probs, latent):
        hs = slice(j * HEAD_DIM, (j + 1) * HEAD_DIM)
        vs = slice(j * vrows, (j + 1) * vrows)
        vt = vc_ref[vs, :]
        if latent:
            vt = jnp.concatenate([vt, v0_ref[vs, :], v1_ref[vs, :], v2_ref[vs, :]], axis=1)
        acc = jnp.dot(vt, probs, preferred_element_type=F32)
        o = (acc[:HEAD_DIM] / acc[HEAD_DIM:HEAD_DIM + 1]).T
        o_ref[:, hs] = (o * _silu(gate_ref[:, hs].astype(F32))).astype(o_ref.dtype)

    heads = range(qt_ref.shape[0] // HEAD_DIM)
    for latent in (True, False):
        for bounded in (True, False):
            @pl.when(jnp.logical_and(is_latent == latent, is_bounded == bounded))
            def _():
                probs = [probabilities(j, latent, bounded) for j in heads]
                for j in heads:
                    output(j, probs[j], latent)


def _na_bias_table(rel_bias, n_rows):
    c = np.arange(GRID_W)[None, :]
    kj = np.arange(GRID_W)[:, None]
    cs = np.clip(c - NA_KW // 2, 0, GRID_W - NA_KW)
    col_ok = (kj >= cs) & (kj < cs + NA_KW)
    sel_c = (np.arange(2 * NA_KW - 1)[:, None, None] == (kj - c + NA_KW - 1)[None]) & col_ok[None]
    cols = jnp.einsum('hde,ekc->hdkc', rel_bias * math.log2(math.e), jnp.asarray(sel_c, F32),
                      precision=lax.Precision.HIGHEST)
    cols = jnp.where(jnp.asarray(col_ok)[None, None], cols, NEG_INF).astype(BF16)
    neg = jnp.full((rel_bias.shape[0], GRID_W, GRID_W), NEG_INF, BF16)
    tables = []
    for r0, ws in ((0, 0), (NA_Q_ROWS, 0), (n_rows - NA_Q_ROWS, n_rows - NA_WIN_ROWS)):
        key_rows = []
        for i in range(NA_WIN_ROWS):
            blocks = []
            for a in range(NA_Q_ROWS):
                r, kr = r0 + a, ws + i
                rs = min(max(r - NA_KH // 2, 0), n_rows - NA_KH)
                blocks.append(cols[:, kr - r + NA_KH - 1] if rs <= kr < rs + NA_KH else neg)
            key_rows.append(jnp.concatenate(blocks, axis=-1))
        tables.append(jnp.concatenate(key_rows, axis=-2))
    return jnp.stack(tables, axis=0)


def _na_attention(bounded, qt, kn, vt, u, n_heads, off_gate, bias_tbl, n_lat):
    t = u.shape[0]
    tb = NA_Q_ROWS * GRID_W
    n_lat_blocks = n_lat // tb
    ctx_blk = t // tb - 1
    hp = NA_HEADS_PER_STEP
    hw = hp * HEAD_DIM
    vrows = hp * (HEAD_DIM + GQA_ONES_ROWS)
    assert t - n_lat == tb and n_lat_blocks >= 3 and off_gate % hw == 0 and n_heads % hp == 0
    g_blk0 = off_gate // hw

    def win(i, j):
        return jnp.clip(i - 1, 0, n_lat_blocks - 3) + j

    def k_spec(j):
        return pl.BlockSpec((tb, hw), lambda h, i, b: (win(i, j), h))

    def v_spec(j):
        return pl.BlockSpec((vrows, tb), lambda h, i, b: (h, win(i, j)))

    def bias_cls(h, i, b):
        return (jnp.where(i == 0, 0, jnp.where(i >= n_lat_blocks - 1, 2, 1)), h, 0, 0)

    grid_spec = pltpu.PrefetchScalarGridSpec(
        num_scalar_prefetch=1,
        grid=(n_heads // hp, t // tb),
        in_specs=[pl.BlockSpec((hw, tb), lambda h, i, b: (h, i)),
                  k_spec(0), k_spec(1), k_spec(2),
                  pl.BlockSpec((tb, hw), lambda h, i, b: (ctx_blk, h)),
                  v_spec(0), v_spec(1), v_spec(2),
                  pl.BlockSpec((vrows, tb), lambda h, i, b: (h, ctx_blk)),
                  pl.BlockSpec((None, hp, NA_WIN_ROWS * GRID_W, tb), bias_cls),
                  pl.BlockSpec((tb, hw), lambda h, i, b: (i, g_blk0 + h))],
        out_specs=pl.BlockSpec((tb, hw), lambda h, i, b: (i, h)))
    return pl.pallas_call(
        functools.partial(_na_kernel, n_lat_blocks=n_lat_blocks),
        grid_spec=grid_spec,
        out_shape=jax.ShapeDtypeStruct((t, n_heads * HEAD_DIM), BF16),
        compiler_params=_params(("arbitrary", "arbitrary"), 32),
        name="na_attention",
    )(bounded, qt, kn, kn, kn, kn, vt, vt, vt, vt, bias_tbl, u)


def _short_conv_kernel(x_ref, prev_ref, next_ref, w_ref, o_ref, *, tr):
    i = pl.program_id(0)
    n = pl.num_programs(0)
    x = x_ref[...].astype(F32)
    halo = prev_ref.shape[0]
    prev_row = jnp.where(i == 0, 0.0, prev_ref[halo - 1:halo, :].astype(F32))
    next_row = jnp.where(i == n - 1, 0.0, next_ref[0:1, :].astype(F32))
    row = lax.broadcasted_iota(jnp.int32, x.shape, 0)
    up = jnp.where(row == 0, prev_row, pltpu.roll(x, 1, 0))
    down = jnp.where(row == tr - 1, next_row, pltpu.roll(x, tr - 1, 0))
    y = (up * w_ref[0:1, :] + x * w_ref[1:2, :] + down * w_ref[2:3, :]).astype(o_ref.dtype)
    o_ref[...] = y.reshape(o_ref.shape)


def _short_conv(u, width, conv_w, n_lat):
    tr = _largest_tile(n_lat, 1024, 256)
    tc = _largest_tile(width, 1024, 256)
    halo = 16
    hb = tr // halo
    n_tiles = n_lat // tr
    wpad = jnp.zeros((8, width), F32).at[:HY_SHORT].set(conv_w.astype(F32))
    return pl.pallas_call(
        functools.partial(_short_conv_kernel, tr=tr),
        grid=(n_tiles, width // tc),
        in_specs=[pl.BlockSpec((tr, tc), lambda i, j: (i, j)),
                  pl.BlockSpec((halo, tc), lambda i, j: (jnp.maximum(i * hb - 1, 0), j)),
                  pl.BlockSpec((halo, tc), lambda i, j: (jnp.minimum((i + 1) * hb, n_tiles * hb - 1), j)),
                  pl.BlockSpec((8, tc), lambda i, j: (0, j))],
        out_specs=pl.BlockSpec((tr // FFT_N2, FFT_N2, tc), lambda i, j: (i, 0, j)),
        out_shape=jax.ShapeDtypeStruct((n_lat // FFT_N2, FFT_N2, width), BF16),
        compiler_params=_params(("arbitrary", "arbitrary"), 32),
        name="short_conv",
    )(u, u, u, wpad)


def _hy_gate2_kernel(x2_ref, c_ref, y1_ref, d_ref, g_ref, ctx_ref, o_ref, *, n_lat_tiles):
    i = pl.program_id(0)

    @pl.when(i < n_lat_tiles)
    def _():
        rows = lambda ref: ref[...].reshape(o_ref.shape).astype(F32)
        y1 = rows(y1_ref)
        z = rows(x2_ref) * (rows(c_ref) + y1 * d_ref[1:2, :])
        o_ref[...] = (z * _silu(g_ref[...].astype(F32))).astype(o_ref.dtype)

    @pl.when(i >= n_lat_tiles)
    def _():
        o_ref[...] = ctx_ref[...]


def _hy_gate2(vx3, c2, y1, dskip8, u, off_gate, hy_ctx, hy):
    t = u.shape[0]
    n_lat = vx3.shape[0] * FFT_N2
    tr = t - n_lat
    tc = _largest_tile(hy, 2048, 256)
    nb = hy // tc
    n_lat_tiles = n_lat // tr
    assert n_lat % tr == 0 and off_gate % tc == 0 and tr % FFT_N2 == 0
    g_blk0 = off_gate // tc
    lat = lambda i: jnp.minimum(i, n_lat_tiles - 1)
    slab = lambda c0: pl.BlockSpec((tr // FFT_N2, FFT_N2, tc), lambda i, j: (lat(i), 0, c0 + j))
    return pl.pallas_call(
        functools.partial(_hy_gate2_kernel, n_lat_tiles=n_lat_tiles),
        grid=(t // tr, nb),
        in_specs=[slab(2 * nb), slab(0), slab(0),
                  pl.BlockSpec((8, tc), lambda i, j: (0, j)),
                  pl.BlockSpec((tr, tc), lambda i, j: (i, g_blk0 + j)),
                  pl.BlockSpec((tr, tc), lambda i, j: (0, j))],
        out_specs=pl.BlockSpec((tr, tc), lambda i, j: (i, j)),
        out_shape=jax.ShapeDtypeStruct((t, hy), BF16),
        compiler_params=_params(("arbitrary", "arbitrary"), 32),
        name="hy_gate2",
    )(vx3, c2, y1, dskip8, u, hy_ctx)


def _filter_hidden_kernel(z_ref, w1_ref, b1_ref, w2_ref, b2_ref, fr_ref, o_ref):
    hp = lax.Precision.HIGHEST
    fr = fr_ref[...]
    h = jnp.sin(fr * (jnp.dot(z_ref[...], w1_ref[...], precision=hp,
                              preferred_element_type=F32) + b1_ref[...]))
    o_ref[...] = jnp.sin(fr * (jnp.dot(h, w2_ref[...], precision=hp,
                                       preferred_element_type=F32) + b2_ref[...]))


def _pad2(a, rows, cols):
    return jnp.zeros((rows, cols), F32).at[:a.shape[0], :a.shape[1]].set(a.astype(F32))


def _filter_hidden(feat, w1, b1, w2, b2, freq):
    r = feat.shape[0]
    tr = _largest_tile(r, 1024, 256)
    full = lambda i: (0, 0)
    return pl.pallas_call(
        _filter_hidden_kernel,
        grid=(r // tr,),
        in_specs=[pl.BlockSpec((tr, LANE), lambda i: (i, 0)),
                  pl.BlockSpec((LANE, LANE), full), pl.BlockSpec((1, LANE), full),
                  pl.BlockSpec((LANE, LANE), full), pl.BlockSpec((1, LANE), full),
                  pl.BlockSpec((1, LANE), full)],
        out_specs=pl.BlockSpec((tr, LANE), lambda i: (i, 0)),
        out_shape=jax.ShapeDtypeStruct((r, LANE), F32),
        compiler_params=_params(("arbitrary",), 32),
        name="filter_hidden",
    )(feat, _pad2(w1, LANE, LANE), _pad2(b1[None], 1, LANE), _pad2(w2, LANE, LANE),
      _pad2(b2[None], 1, LANE), _pad2(freq[None], 1, LANE))


def _two_sided_positions(n):
    idx = np.arange(2 * n)
    return np.where(idx < n, idx, 2 * n - idx) % n


def _filter_features(n, n_bands):
    pos = _two_sided_positions(n).astype(np.float64)[:, None]
    t = pos / (n - 1)
    w = (2.0 * math.pi / n) * pos
    bands = np.linspace(1e-4, n_bands - 1, n_bands)[None, :]
    z = np.zeros((2 * n, LANE))
    z[:, 0:1] = t
    z[:, 1:1 + n_bands] = np.cos(bands * w)
    z[:, 1 + n_bands:1 + 2 * n_bands] = -np.sin(bands * w)
    return jnp.asarray(z, F32), jnp.asarray(t, F32)


def _filter_gen_kernel(h_ref, w_ref, t_ref, dec_ref, o_ref, *, tr, n):
    i = pl.program_id(0)
    f = jnp.dot(h_ref[...].astype(BF16), w_ref[...], preferred_element_type=F32)
    f = f * jnp.exp(-t_ref[...] * jnp.abs(dec_ref[0:1, :]))
    row = i * tr + lax.broadcasted_iota(jnp.int32, (tr, 1), 0)
    o_ref[...] = jnp.where(row == n, 0.0, f).astype(o_ref.dtype).reshape(o_ref.shape)


def _filter_gen(hidden, row0, w3p, tcol, decay8, n, order, hy):
    tr = _largest_tile(n, 1024, 256)
    tc = _largest_tile(hy, 1024, 256)
    nb = hy // tc
    n_half = n // tr
    hb0 = row0 // tr
    assert row0 % tr == 0
    return pl.pallas_call(
        functools.partial(_filter_gen_kernel, tr=tr, n=n),
        grid=(2 * n // tr, order * nb),
        in_specs=[pl.BlockSpec((tr, LANE), lambda i, j: (hb0 + i, 0)),
                  pl.BlockSpec((LANE, tc), lambda i, j: (0, (j // nb) * 2 * nb + (i // n_half) * nb + j % nb)),
                  pl.BlockSpec((tr, 1), lambda i, j: (i, 0)),
                  pl.BlockSpec((None, 8, tc), lambda i, j: (j // nb, 0, j % nb))],
        out_specs=pl.BlockSpec((tr // FFT_N2, FFT_N2, tc), lambda i, j: (i, 0, j)),
        out_shape=jax.ShapeDtypeStruct((2 * n // FFT_N2, FFT_N2, order * hy), BF16),
        compiler_params=_params(("arbitrary", "arbitrary"), 32),
        name="filter_gen",
    )(hidden, w3p, tcol, decay8)


def _dft_tables(n1):
    n2 = FFT_N2
    n = n1 * n2
    half = n1 // 2
    nz = half
    k1 = np.concatenate([np.arange(0, half + 1, 2), np.arange(1, half + 1, 2)])

    def cs(idx, period):
        ang = (idx % period) * (2.0 * math.pi / period)
        return np.cos(ang), -np.sin(ang)

    eye8 = np.eye(8)
    fr, fi = cs(k1[:, None] * np.arange(nz)[None, :], n1)
    kf = np.stack([fr, fi], axis=1)
    kf = np.einsum('kpn,jl->kpjnl', kf, eye8).reshape((half + 1) * 16, nz * 8)
    gr, gi = cs(np.arange(half)[:, None] * k1[None, :], n1)
    weight = np.where((k1 == 0) | (k1 == half), 1.0, 2.0) / n
    ki = np.stack([gr, gi], axis=2) * weight[None, :, None]
    ki = np.einsum('nkp,jl->njkpl', ki, eye8).reshape(half * 8, (half + 1) * 16)
    r = np.arange(2 * n2)
    part = (r % 16) // 8
    n2_of_r = 8 * (r // 16) + r % 8
    k2 = np.arange(n2)
    idx = k2[None, :, None] * n2_of_r[None, None, :] * n1 + k1[:, None, None] * n2_of_r[None, None, :]
    cr, ci = cs(idx, n)
    top = np.where(part[None, None, :] == 0, cr, -ci)
    bot = np.where(part[None, None, :] == 0, ci, cr)
    m = np.concatenate([top, bot], axis=1)
    return tuple(jnp.asarray(a, F32).astype(BF16) for a in (kf, ki, m))


def _cross_slab_forward(z_ref, kf_ref, a_scr, nz, nh, ct):
    def body(b, carry):
        z = z_ref[:, pl.ds(pl.multiple_of(16 * b, 16), 16), :].astype(F32)
        for half in range(2):
            g = z[:, 8 * half:8 * half + 8, :].reshape(nz * 8, ct).astype(BF16)
            p = jnp.dot(kf_ref[...], g, preferred_element_type=F32)
            start = pl.multiple_of(32 * b + 16 * half, 16)
            a_scr[:, pl.ds(start, 16), :] = p.astype(BF16).reshape(nh, 16, ct)
        return carry

    lax.fori_loop(0, FFT_N2 // 16, body, 0)


def _slab_block(nh):
    return max(k for k in range(1, 17) if nh % k == 0)


def _cross_slab_forward_full(z_ref, kf_ref, a_scr, nh, ct):
    half = nh - 1
    n_even = half // 2 + 1

    def body(b, carry):
        rows = pl.ds(pl.multiple_of(16 * b, 16), 16)
        lo = z_ref[0:half, rows, :].astype(F32)
        hi = z_ref[half:2 * half, rows, :].astype(F32)
        for sub in range(2):
            sl = slice(8 * sub, 8 * sub + 8)
            g_even = (lo[:, sl, :] + hi[:, sl, :]).reshape(half * 8, ct).astype(BF16)
            g_odd = (lo[:, sl, :] - hi[:, sl, :]).reshape(half * 8, ct).astype(BF16)
            p_even = jnp.dot(kf_ref[0:n_even * 16, :], g_even, preferred_element_type=F32)
            p_odd = jnp.dot(kf_ref[n_even * 16:nh * 16, :], g_odd, preferred_element_type=F32)
            start = pl.ds(pl.multiple_of(32 * b + 16 * sub, 16), 16)
            a_scr[0:n_even, start, :] = p_even.astype(BF16).reshape(n_even, 16, ct)
            a_scr[n_even:nh, start, :] = p_odd.astype(BF16).reshape(nh - n_even, 16, ct)
        return carry

    lax.fori_loop(0, FFT_N2 // 16, body, 0)


def _fft_fwd_kernel(z_ref, kf_ref, m_ref, o_ref, a_scr, *, nh, kb, ct):
    kblk = pl.program_id(1)

    @pl.when(kblk == 0)
    def _():
        _cross_slab_forward_full(z_ref, kf_ref, a_scr, nh, ct)

    for k in range(kb):
        x = jnp.dot(m_ref[k], a_scr[kblk * kb + k], preferred_element_type=F32)
        o_ref[k] = x.astype(o_ref.dtype)


def _fft_forward(z3, kf, m, ct=FFT_CT):
    nz, _, c = z3.shape
    nh = m.shape[0]
    kb = _slab_block(nh)
    assert nz == 2 * (nh - 1)
    return pl.pallas_call(
        functools.partial(_fft_fwd_kernel, nh=nh, kb=kb, ct=ct),
        grid=(c // ct, nh // kb),
        in_specs=[pl.BlockSpec((nz, FFT_N2, ct), lambda j, k: (0, 0, j)),
                  pl.BlockSpec(kf.shape, lambda j, k: (0, 0)),
                  pl.BlockSpec((kb, 2 * FFT_N2, 2 * FFT_N2), lambda j, k: (k, 0, 0))],
        out_specs=pl.BlockSpec((kb, 2 * FFT_N2, ct), lambda j, k: (k, 0, j)),
        out_shape=jax.ShapeDtypeStruct((nh, 2 * FFT_N2, c), BF16),
        scratch_shapes=[pltpu.VMEM((nh, 2 * FFT_N2, ct), BF16)],
        compiler_params=_params(("arbitrary", "arbitrary"), 56),
        name="fft_forward",
    )(z3, kf, m)


def _fft_conv_kernel(z_ref, kf_ref, ki_ref, m_ref, h_ref, *rest, nz, nh, kb, ct, skip_row):
    gated = len(rest) == 4
    o_ref, a_scr = rest[-2:]
    kblk = pl.program_id(1)
    n2 = FFT_N2

    @pl.when(kblk == 0)
    def _():
        _cross_slab_forward(z_ref, kf_ref, a_scr, nz, nh, ct)

    for k in range(kb):
        mk = m_ref[k]
        x = jnp.dot(mk, a_scr[kblk * kb + k], preferred_element_type=F32)
        h = h_ref[k].astype(F32)
        xr, xi, hr, hi = x[:n2], x[n2:], h[:n2], h[n2:]
        y = jnp.concatenate([xr * hr - xi * hi, xr * hi + xi * hr], axis=0).astype(BF16)
        b = lax.dot_general(mk, y, (((0,), (0,)), ((), ())), preferred_element_type=F32)
        a_scr[kblk * kb + k] = b.astype(BF16)

    @pl.when(kblk == pl.num_programs(1) - 1)
    def _():
        def body(b, carry):
            halves = []
            for half in range(2):
                start = pl.multiple_of(32 * b + 16 * half, 16)
                r = a_scr[:, pl.ds(start, 16), :].reshape(nh * 16, ct)
                p = jnp.dot(ki_ref[...], r, preferred_element_type=F32)
                halves.append(p.reshape(nz, 8, ct))
            out = jnp.concatenate(halves, axis=1)
            rows = pl.ds(pl.multiple_of(16 * b, 16), 16)
            if gated:
                x_ref, d_ref = rest[:2]
                z = z_ref[:, rows, :].astype(F32)
                out = x_ref[:, rows, :].astype(F32) * (out + z * d_ref[skip_row:skip_row + 1, :])
            o_ref[:, rows, :] = out.astype(o_ref.dtype)
            return carry

        lax.fori_loop(0, n2 // 16, body, 0)


def _fft_conv(z3, z_blk0, spec, spec_blk0, kf, ki, m, c, gate=None, ct=FFT_CT):
    nz = z3.shape[0]
    nh = m.shape[0]
    kb = _slab_block(nh)
    assert nh == nz + 1
    in_specs = [pl.BlockSpec((nz, FFT_N2, ct), lambda j, k: (0, 0, z_blk0 + j)),
                pl.BlockSpec(kf.shape, lambda j, k: (0, 0)),
                pl.BlockSpec(ki.shape, lambda j, k: (0, 0)),
                pl.BlockSpec((kb, 2 * FFT_N2, 2 * FFT_N2), lambda j, k: (k, 0, 0)),
                pl.BlockSpec((kb, 2 * FFT_N2, ct), lambda j, k: (k, 0, spec_blk0 + j))]
    operands = [z3, kf, ki, m, spec]
    skip_row = 0
    if gate is not None:
        x3, x_blk0, dskip8, skip_row = gate
        in_specs += [pl.BlockSpec((nz, FFT_N2, ct), lambda j, k: (0, 0, x_blk0 + j)),
                     pl.BlockSpec((8, ct), lambda j, k: (0, j))]
        operands += [x3, dskip8]
    return pl.pallas_call(
        functools.partial(_fft_conv_kernel, nz=nz, nh=nh, kb=kb, ct=ct, skip_row=skip_row),
        grid=(c // ct, nh // kb),
        in_specs=in_specs,
        out_specs=pl.BlockSpec((nz, FFT_N2, ct), lambda j, k: (0, 0, j)),
        out_shape=jax.ShapeDtypeStruct((nz, FFT_N2, c), BF16),
        scratch_shapes=[pltpu.VMEM((nh, 2 * FFT_N2, ct), BF16)],
        compiler_params=_params(("arbitrary", "arbitrary"), 56),
        name="fft_conv",
    )(*operands)


def _ctx_dft_tables(n):
    nn = 2 * n
    k = np.arange(nn)
    ang = ((k[:, None] * k[None, :]) % nn) * (2.0 * math.pi / nn)
    cr, ci = np.cos(ang), -np.sin(ang)
    full = np.concatenate([cr, ci], axis=0)
    inv = np.concatenate([cr[:n], ci[:n]], axis=1) / nn
    return tuple(jnp.asarray(a, F32).astype(BF16) for a in (full[:, :n], full, inv))


def _ctx_hyena_kernel(v_ref, x1_ref, x2_ref, g_ref, cw0_ref, cw1_ref, cw2_ref, hid_ref,
                      w10_ref, w11_ref, w20_ref, w21_ref, t_ref, dec_ref, dsk_ref,
                      ff_ref, ffull_ref, finv_ref, o_ref, *, n):
    row = lax.broadcasted_iota(jnp.int32, (n, 1), 0)

    def sconv(x_ref, w_ref):
        x = x_ref[...].astype(F32)
        up = jnp.where(row == 0, 0.0, pltpu.roll(x, 1, 0))
        down = jnp.where(row == n - 1, 0.0, pltpu.roll(x, n - 1, 0))
        y = up * w_ref[0:1, :] + x * w_ref[1:2, :] + down * w_ref[2:3, :]
        return y.astype(BF16).astype(F32)

    hid = hid_ref[...].astype(BF16)
    tcol = t_ref[...]
    row2 = lax.broadcasted_iota(jnp.int32, (2 * n, 1), 0)

    def spectrum(wf_ref, wb_ref, o):
        fwd = jnp.dot(hid[:n], wf_ref[...], preferred_element_type=F32)
        bwd = jnp.dot(hid[n:], wb_ref[...], preferred_element_type=F32)
        f = jnp.concatenate([fwd, bwd], axis=0) * jnp.exp(-tcol * jnp.abs(dec_ref[o:o + 1, :]))
        f = jnp.where(row2 == n, 0.0, f).astype(BF16)
        return jnp.dot(ffull_ref[...], f, preferred_element_type=F32)

    def conv(z, h):
        x = jnp.dot(ff_ref[...], z.astype(BF16), preferred_element_type=F32)
        h = h.astype(BF16).astype(F32)
        nn = 2 * n
        xr, xi, hr, hi = x[:nn], x[nn:], h[:nn], h[nn:]
        y = jnp.concatenate([xr * hr - xi * hi, xr * hi + xi * hr], axis=0).astype(BF16)
        return jnp.dot(finv_ref[...], y, preferred_element_type=F32)

    v = sconv(v_ref, cw0_ref)
    x1 = sconv(x1_ref, cw1_ref)
    x2 = sconv(x2_ref, cw2_ref)
    c1 = conv(v, spectrum(w10_ref, w11_ref, 0)).astype(BF16).astype(F32)
    y1 = (x1 * (c1 + v * dsk_ref[0:1, :])).astype(BF16).astype(F32)
    c2 = conv(y1, spectrum(w20_ref, w21_ref, 1)).astype(BF16).astype(F32)
    z = x2 * (c2 + y1 * dsk_ref[1:2, :])
    o_ref[...] = (z * _silu(g_ref[...].astype(F32))).astype(o_ref.dtype)


def _ctx_hyena(u, n_lat, hy, order, off_gate, conv8, hidden, hid_row0, w3p, tcol, decay8, dskip8, tables):
    t = u.shape[0]
    n = t - n_lat
    assert order == 2 and n_lat % n == 0 and hid_row0 % (2 * n) == 0
    ct = FFT_CT
    nb = hy // ct
    rb = n_lat // n
    ff, ffull, finv = tables
    const = lambda j: (0, 0)
    ublk = lambda c0: pl.BlockSpec((n, ct), lambda j: (rb, c0 + j))
    wblk = lambda c0: pl.BlockSpec((LANE, ct), lambda j: (0, c0 + j))
    cwblk = lambda c0: pl.BlockSpec((8, ct), lambda j: (0, c0 + j))
    return pl.pallas_call(
        functools.partial(_ctx_hyena_kernel, n=n),
        grid=(nb,),
        in_specs=[ublk(0), ublk(nb), ublk(2 * nb), ublk(off_gate // ct),
                  cwblk(0), cwblk(nb), cwblk(2 * nb),
                  pl.BlockSpec((2 * n, LANE), lambda j: (hid_row0 // (2 * n), 0)),
                  wblk(0), wblk(nb), wblk(2 * nb), wblk(3 * nb),
                  pl.BlockSpec((2 * n, 1), const),
                  pl.BlockSpec((8, ct), lambda j: (0, j)),
                  pl.BlockSpec((8, ct), lambda j: (0, j)),
                  pl.BlockSpec(ff.shape, const), pl.BlockSpec(ffull.shape, const),
                  pl.BlockSpec(finv.shape, const)],
        out_specs=pl.BlockSpec((n, ct), lambda j: (0, j)),
        out_shape=jax.ShapeDtypeStruct((n, hy), BF16),
        compiler_params=_params(("arbitrary",), 40),
        name="ctx_hyena",
    )(u, u, u, u, conv8, conv8, conv8, hidden, w3p, w3p, w3p, w3p, tcol, decay8, dskip8,
      ff, ffull, finv)


def _even_layer(x_all, h, mods_l, n_lat, w_in, w_out, conv_w, w1, b1, w2, b2, w3, freq, decay, dskip,
                q_norm, k_norm, rope, dft, ctx_dft, last):
    t, d = x_all.shape
    n_ctx = t - n_lat
    order, hy = decay.shape
    gqa_w = d - hy
    n_q = gqa_w // HEAD_DIM
    n_kv = n_q // GQA_GROUP
    kv_w = n_kv * HEAD_DIM
    off_hy_gate = (order + 1) * hy
    off_q = off_hy_gate + hy
    off_v = off_q + gqa_w + kv_w
    off_at_gate = off_v + kv_w
    scale = HEAD_DIM ** -0.5

    u = _in_proj(h, *w_in)

    q_gain = q_norm.astype(F32) * (scale * math.log2(math.e))
    k_gain = k_norm.astype(F32)
    qt, kn, vt = _attn_prep(u, off_q, off_q + gqa_w, off_v, n_q, n_kv, 1, q_gain, k_gain,
                            rope[0], rope[1], True)
    att = _gqa_attention(_scores_bounded(q_gain, k_gain), qt, kn, vt, u, n_q, n_kv, off_at_gate, n_lat)

    n_bands = (w1.shape[0] - 1) // 2
    feat_l, t_l = _filter_features(n_lat, n_bands)
    feat_c, t_c = _filter_features(n_ctx, n_bands)
    hidden = _filter_hidden(jnp.concatenate([feat_l, feat_c], axis=0), w1, b1, w2, b2, freq)
    w3p = jnp.zeros((LANE, w3.shape[1]), BF16).at[:w3.shape[0]].set(w3.astype(BF16))
    decay8 = jnp.zeros((order, 8, hy), F32).at[:, 0].set(decay.astype(F32))
    dskip8 = jnp.zeros((8, hy), F32).at[:order].set(dskip.astype(F32))
    conv8 = jnp.zeros((8, conv_w.shape[1]), F32).at[:HY_SHORT].set(conv_w.astype(F32))
    kf_pad, ki, m = dft
    filt = _filter_gen(hidden, 0, w3p, t_l, decay8, n_lat, order, hy)
    spec = _fft_forward(filt, kf_pad, m)

    vx3 = _short_conv(u, (order + 1) * hy, conv_w, n_lat)
    ct = FFT_CT
    y1 = _fft_conv(vx3, 0, spec, 0, kf_pad, ki, m, hy, gate=(vx3, hy // ct, dskip8, 0))
    c2 = _fft_conv(y1, 0, spec, hy // ct, kf_pad, ki, m, hy)

    decay_c8 = jnp.zeros((8, hy), F32).at[:order].set(decay.astype(F32))
    hy_ctx = _ctx_hyena(u, n_lat, hy, order, off_hy_gate, conv8, hidden, 2 * n_lat, w3p, t_c,
                        decay_c8, dskip8, ctx_dft)
    hy_mix = _hy_gate2(vx3, c2, y1, dskip8, u, off_hy_gate, hy_ctx, hy)

    return _out_proj(hy_mix, 0, att, 0, *w_out, x_all, mods_l, n_lat, last)


def _odd_layer(x_all, h, mods_l, n_lat, w_in, w_out, q_norm, k_norm, rel_bias, rope, last):
    n_heads = rel_bias.shape[0]
    w = n_heads * HEAD_DIM
    scale = HEAD_DIM ** -0.5
    u = _in_proj(h, *w_in)
    log2e = math.log2(math.e)
    q_gain = q_norm.astype(F32) * (scale * log2e)
    k_gain = k_norm.astype(F32)
    hp = NA_HEADS_PER_STEP
    qt, kn, vt = _attn_prep(u, 0, w, 2 * w, hp, hp, n_heads // hp, q_gain, k_gain, rope[0], rope[1], False)
    bias_tbl = _na_bias_table(rel_bias.astype(F32), n_lat // GRID_W)
    bounded = _scores_bounded(q_gain, k_gain, jnp.max(jnp.abs(rel_bias)).astype(F32) * (log2e * 1.02))
    att = _na_attention(bounded, qt, kn, vt, u, n_heads, 3 * w, bias_tbl, n_lat)
    return _out_proj(att, 0, att, 1, *w_out, x_all, mods_l, n_lat, last)


def kernel(x, c, ctx, c_ctx, norm_g, ada_w, ada_b, e_w_in, e_w_out, hy_conv, hy_w1, hy_b1, hy_w2, hy_b2,
           hy_w3, hy_freq, hy_decay, hy_dskip, gqa_q_norm, gqa_k_norm, o_w_in, o_w_out, na_q_norm,
           na_k_norm, na_rel_bias):
    batch, n_lat, d = x.shape
    n_ctx = ctx.shape[1]
    depth = norm_g.shape[0]
    assert batch == 1 and n_lat % (2 * FFT_N2) == 0 and n_ctx == NA_Q_ROWS * GRID_W

    cvecs = jnp.zeros((8, d), F32).at[0].set(c[0]).at[1].set(c_ctx)
    mods = _ada_mods(cvecs, ada_w, ada_b)

    rope = _rope_tables(n_lat, n_ctx)
    n1 = 2 * n_lat // FFT_N2
    dft = _dft_tables(n1)
    ctx_dft = _ctx_dft_tables(n_ctx)

    for layer in range(depth):
        i = layer // 2
        last = layer == depth - 1
        if layer == 0:
            h, x_all = _norm_mod_join(x[0].astype(F32), ctx[0].astype(F32), norm_g[0], mods[0])
        else:
            h = _norm_mod(x_all, norm_g[layer], mods[layer], n_lat)
        if layer % 2 == 0:
            x_all = _even_layer(x_all, h, mods[layer], n_lat,
                                (e_w_in, i), (e_w_out, i), hy_conv[i],
                                hy_w1[i], hy_b1[i], hy_w2[i], hy_b2[i], hy_w3[i], hy_freq[i],
                                hy_decay[i], hy_dskip[i], gqa_q_norm[i], gqa_k_norm[i],
                                rope, dft, ctx_dft, last)
        else:
            x_all = _odd_layer(x_all, h, mods[layer], n_lat,
                               (o_w_in, i), (o_w_out, i),
                               na_q_norm[i], na_k_norm[i], na_rel_bias[i], rope, last)
    return x_all[None]
```

```python
import functools
import math

import numpy as np
import jax
import jax.numpy as jnp
from jax import lax
from jax.experimental import pallas as pl
from jax.experimental.pallas import tpu as pltpu

F32 = jnp.float32
BF16 = jnp.bfloat16

HEAD_DIM = 128
GRID_W = 64
EPS = 1e-6
ROPE_THETA = 10000.0
NA_KH = 8
NA_KW = 16
NA_Q_ROWS = 4
NA_WIN_ROWS = 12
NA_HEADS_PER_STEP = 8
GQA_GROUP = 4
HY_SHORT = 3
FFT_N2 = 128
FFT_CT = 256
NEG_INF = -1e30

V7X_VMEM_BYTES = 64 * 1024 * 1024
LANE = 128


def _params(semantics, vmem_mb):
    assert vmem_mb * 1024 * 1024 < V7X_VMEM_BYTES
    return pltpu.CompilerParams(dimension_semantics=semantics,
                                vmem_limit_bytes=vmem_mb * 1024 * 1024)


def _largest_tile(n, target, quantum):
    best = None
    t = quantum
    while t <= min(n, target):
        if n % t == 0:
            best = t
        t += quantum
    assert best is not None, (n, target, quantum)
    return best


def _silu(x):
    return x * jax.nn.sigmoid(x)


def _mods_kernel(c_ref, w_ref, b_ref, o_ref):
    s = _silu(c_ref[...]).astype(BF16)
    o_ref[...] = jnp.dot(s, w_ref[...].astype(BF16), preferred_element_type=F32) + b_ref[...]


def _ada_mods(cvecs, ada_w, ada_b):
    depth, d, n = ada_w.shape
    tn = _largest_tile(n, 512, LANE)
    return pl.pallas_call(
        _mods_kernel,
        grid=(depth, n // tn),
        in_specs=[pl.BlockSpec((8, d), lambda l, j: (0, 0)),
                  pl.BlockSpec((None, d, tn), lambda l, j: (l, 0, j)),
                  pl.BlockSpec((None, 1, tn), lambda l, j: (l, 0, j))],
        out_specs=pl.BlockSpec((None, 8, tn), lambda l, j: (l, 0, j)),
        out_shape=jax.ShapeDtypeStruct((depth, 8, n), F32),
        compiler_params=_params(("arbitrary", "arbitrary"), 40),
        name="ada_mods",
    )(cvecs, ada_w, ada_b.reshape(depth, 1, n))


def _norm_mod_kernel(x_ref, g_ref, sh_ref, sc_ref, o_ref, *, n_lat_tiles):
    x = x_ref[...]
    y = x * lax.rsqrt(jnp.mean(x * x, axis=-1, keepdims=True) + EPS) * g_ref[...]
    is_ctx = pl.program_id(0) >= n_lat_tiles
    sh = jnp.where(is_ctx, sh_ref[1:2, :], sh_ref[0:1, :])
    sc = jnp.where(is_ctx, sc_ref[1:2, :], sc_ref[0:1, :])
    o_ref[...] = (y * (1.0 + sc) + sh).astype(o_ref.dtype)


def _norm_mod(x_all, g, mods_l, n_lat):
    t, d = x_all.shape
    tr = 256
    assert t % tr == 0 and n_lat % tr == 0
    return pl.pallas_call(
        functools.partial(_norm_mod_kernel, n_lat_tiles=n_lat // tr),
        grid=(t // tr,),
        in_specs=[pl.BlockSpec((tr, d), lambda i: (i, 0)),
                  pl.BlockSpec((1, d), lambda i: (0, 0)),
                  pl.BlockSpec((8, d), lambda i: (0, 0)),
                  pl.BlockSpec((8, d), lambda i: (0, 1))],
        out_specs=pl.BlockSpec((tr, d), lambda i: (i, 0)),
        out_shape=jax.ShapeDtypeStruct((t, d), BF16),
        compiler_params=_params(("arbitrary",), 40),
        name="norm_mod",
    )(x_all, g.reshape(1, d), mods_l, mods_l)


def _norm_mod_join_kernel(x_ref, c_ref, g_ref, sh_ref, sc_ref, o_ref, xo_ref, *, n_lat_tiles):
    is_ctx = pl.program_id(0) >= n_lat_tiles
    x = jnp.where(is_ctx, c_ref[...], x_ref[...])
    y = x * lax.rsqrt(jnp.mean(x * x, axis=-1, keepdims=True) + EPS) * g_ref[...]
    sh = jnp.where(is_ctx, sh_ref[1:2, :], sh_ref[0:1, :])
    sc = jnp.where(is_ctx, sc_ref[1:2, :], sc_ref[0:1, :])
    o_ref[...] = (y * (1.0 + sc) + sh).astype(o_ref.dtype)
    xo_ref[...] = x


def _norm_mod_join(x_lat, x_ctx, g, mods_l):
    n_lat, d = x_lat.shape
    n_ctx = x_ctx.shape[0]
    tr = n_ctx
    assert n_lat % tr == 0
    n_lat_tiles = n_lat // tr
    t = n_lat + n_ctx
    return pl.pallas_call(
        functools.partial(_norm_mod_join_kernel, n_lat_tiles=n_lat_tiles),
        grid=(t // tr,),
        in_specs=[pl.BlockSpec((tr, d), lambda i: (jnp.minimum(i, n_lat_tiles - 1), 0)),
                  pl.BlockSpec((tr, d), lambda i: (0, 0)),
                  pl.BlockSpec((1, d), lambda i: (0, 0)),
                  pl.BlockSpec((8, d), lambda i: (0, 0)),
                  pl.BlockSpec((8, d), lambda i: (0, 1))],
        out_specs=[pl.BlockSpec((tr, d), lambda i: (i, 0)),
                   pl.BlockSpec((tr, d), lambda i: (i, 0))],
        out_shape=[jax.ShapeDtypeStruct((t, d), BF16), jax.ShapeDtypeStruct((t, d), F32)],
        compiler_params=_params(("arbitrary",), 48),
        name="norm_mod_join",
    )(x_lat, x_ctx, g.reshape(1, d), mods_l, mods_l)


def _in_proj_kernel(a_ref, w_ref, o_ref):
    w = w_ref[...].astype(BF16)
    o_ref[...] = jnp.dot(a_ref[...], w, preferred_element_type=F32).astype(o_ref.dtype)


def _in_proj(h, w_all, layer_idx):
    m, k = h.shape
    n = w_all.shape[2]
    tm = _largest_tile(m, 1408, 128)
    tn = _largest_tile(n, 512, 256)
    return pl.pallas_call(
        _in_proj_kernel,
        grid=(m // tm, n // tn),
        in_specs=[pl.BlockSpec((tm, k), lambda i, j: (i, 0)),
                  pl.BlockSpec((None, k, tn), lambda i, j: (layer_idx, 0, j))],
        out_specs=pl.BlockSpec((tm, tn), lambda i, j: (i, j)),
        out_shape=jax.ShapeDtypeStruct((m, n), BF16),
        compiler_params=_params(("arbitrary", "arbitrary"), 56),
        name="in_proj",
    )(h, w_all)


def _out_proj_kernel(a1_ref, a2_ref, w1_ref, w2_ref, x_ref, gate_ref, o_ref, wb1, wb2, *, tm, n_lat):
    i = pl.program_id(1)

    @pl.when(i == 0)
    def _():
        wb1[...] = w1_ref[...].astype(BF16)
        wb2[...] = w2_ref[...].astype(BF16)

    y = jnp.dot(a1_ref[...], wb1[...], preferred_element_type=F32)
    y = y + jnp.dot(a2_ref[...], wb2[...], preferred_element_type=F32)
    rows = i * tm + lax.broadcasted_iota(jnp.int32, (tm, 1), 0)
    gate = jnp.where(rows >= n_lat, gate_ref[1:2, :], gate_ref[0:1, :])
    o_ref[...] = x_ref[...] + gate * y


def _out_proj(a1, a1_blk, a2, a2_blk, w_all, layer_idx, x_all, mods_l, n_lat, latent_only):
    d = x_all.shape[1]
    t = n_lat if latent_only else x_all.shape[0]
    kh = w_all.shape[1] // 2
    tm = _largest_tile(t, 768, 256)
    tn = _largest_tile(d, 1024, 256)
    gate_blk0 = 2 * d // tn
    once = pl.Buffered(1)
    return pl.pallas_call(
        functools.partial(_out_proj_kernel, tm=tm, n_lat=n_lat),
        grid=(d // tn, t // tm),
        in_specs=[pl.BlockSpec((tm, kh), lambda j, i: (i, a1_blk)),
                  pl.BlockSpec((tm, kh), lambda j, i: (i, a2_blk)),
                  pl.BlockSpec((None, kh, tn), lambda j, i: (layer_idx, 0, j), pipeline_mode=once),
                  pl.BlockSpec((None, kh, tn), lambda j, i: (layer_idx, 1, j), pipeline_mode=once),
                  pl.BlockSpec((tm, tn), lambda j, i: (i, j)),
                  pl.BlockSpec((8, tn), lambda j, i: (0, gate_blk0 + j))],
        out_specs=pl.BlockSpec((tm, tn), lambda j, i: (i, j)),
        out_shape=jax.ShapeDtypeStruct((t, d), F32),
        scratch_shapes=[pltpu.VMEM((kh, tn), BF16), pltpu.VMEM((kh, tn), BF16)],
        compiler_params=_params(("arbitrary", "arbitrary"), 56),
        name="out_proj",
    )(a1, a2, w_all, w_all, x_all, mods_l)


def _rope_tables(n_lat, n_ctx):
    quarter = HEAD_DIM // 4
    inv = ROPE_THETA ** (-np.arange(quarter, dtype=np.float64) / quarter)
    tok = np.arange(n_lat)
    rows = (tok // GRID_W)[:, None] * inv[None]
    cols = (tok % GRID_W)[:, None] * inv[None]
    cos = np.concatenate([np.cos(rows), np.cos(rows), np.cos(cols), np.cos(cols)], axis=-1)
    sin = np.concatenate([-np.sin(rows), np.sin(rows), -np.sin(cols), np.sin(cols)], axis=-1)
    cos = np.concatenate([cos, np.ones((n_ctx, HEAD_DIM))], axis=0)
    sin = np.concatenate([sin, np.zeros((n_ctx, HEAD_DIM))], axis=0)
    return jnp.asarray(cos, F32), jnp.asarray(sin, F32)


GQA_ONES_ROWS = 16


def _attn_prep_kernel(q_ref, k_ref, v_ref, g_ref, cos_ref, sin_ref, qt_ref, ko_ref, vt_ref,
                      *, n_q, n_kv, rope):
    if rope:
        cos = cos_ref[...]
        sin = sin_ref[...]
        lane = lax.broadcasted_iota(jnp.int32, cos.shape, 1)
        first_quarter = (lane % 64) < 32

    def norm_rope(x, g):
        y = x * lax.rsqrt(jnp.mean(x * x, axis=-1, keepdims=True) + EPS) * g
        if not rope:
            return y
        partner = jnp.where(first_quarter, pltpu.roll(y, HEAD_DIM - 32, 1), pltpu.roll(y, 32, 1))
        return y * cos + partner * sin

    for h in range(n_q):
        sl = slice(h * HEAD_DIM, (h + 1) * HEAD_DIM)
        qt_ref[sl, :] = norm_rope(q_ref[:, sl].astype(F32), g_ref[:, sl]).T.astype(qt_ref.dtype)
    for h in range(n_kv):
        sl = slice(h * HEAD_DIM, (h + 1) * HEAD_DIM)
        gk = g_ref[:, (n_q + h) * HEAD_DIM:(n_q + h + 1) * HEAD_DIM]
        ko_ref[:, sl] = norm_rope(k_ref[:, sl].astype(F32), gk).astype(ko_ref.dtype)
        r0 = h * (HEAD_DIM + GQA_ONES_ROWS)
        vt_ref[r0:r0 + HEAD_DIM, :] = v_ref[:, sl].astype(F32).T.astype(vt_ref.dtype)
        vt_ref[r0 + HEAD_DIM:r0 + HEAD_DIM + GQA_ONES_ROWS, :] = jnp.ones(
            (GQA_ONES_ROWS, vt_ref.shape[1]), vt_ref.dtype)


def _attn_prep(u, off_q, off_k, off_v, n_q, n_kv, n_groups, q_gain, k_gain, cos_t, sin_t, rope):
    t = u.shape[0]
    qw, kw = n_q * HEAD_DIM, n_kv * HEAD_DIM
    tr = _largest_tile(t, 768, 256)
    vrows = n_kv * (HEAD_DIM + GQA_ONES_ROWS)
    assert off_q % qw == 0 and off_k % kw == 0 and off_v % kw == 0 and t % tr == 0
    q_blk, k_blk, v_blk = off_q // qw, off_k // kw, off_v // kw
    gvec = jnp.concatenate([jnp.tile(q_gain, n_q), jnp.tile(k_gain, n_kv)])[None]
    return pl.pallas_call(
        functools.partial(_attn_prep_kernel, n_q=n_q, n_kv=n_kv, rope=rope),
        grid=(t // tr, n_groups),
        in_specs=[pl.BlockSpec((tr, qw), lambda i, j: (i, q_blk + j)),
                  pl.BlockSpec((tr, kw), lambda i, j: (i, k_blk + j)),
                  pl.BlockSpec((tr, kw), lambda i, j: (i, v_blk + j)),
                  pl.BlockSpec((1, qw + kw), lambda i, j: (0, 0)),
                  pl.BlockSpec((tr, HEAD_DIM), lambda i, j: (i, 0)),
                  pl.BlockSpec((tr, HEAD_DIM), lambda i, j: (i, 0))],
        out_specs=[pl.BlockSpec((qw, tr), lambda i, j: (j, i)),
                   pl.BlockSpec((tr, kw), lambda i, j: (i, j)),
                   pl.BlockSpec((vrows, tr), lambda i, j: (j, i))],
        out_shape=[jax.ShapeDtypeStruct((n_groups * qw, t), BF16),
                   jax.ShapeDtypeStruct((t, n_groups * kw), BF16),
                   jax.ShapeDtypeStruct((n_groups * vrows, t), BF16)],
        compiler_params=_params(("arbitrary", "arbitrary"), 48),
        name="attn_prep",
    )(u, u, u, gvec, cos_t, sin_t)


def _gqa_kernel(bounded_ref, qt_ref, k_ref, vt_ref, gate_ref, o_ref, m_s, acc_s,
                *, group, tq, tk, n_lat, n_ctx):
    acc_s[...] = jnp.zeros(acc_s.shape, F32)

    def chunk_bounded(kc, vtc):
        ps = []
        for j in range(group):
            qt = qt_ref[j * HEAD_DIM:(j + 1) * HEAD_DIM, :]
            s = jnp.dot(kc, qt, preferred_element_type=F32)
            ps.append(jnp.exp2(s).astype(BF16))
        for j in range(group):
            acc_s[j] += jnp.dot(vtc, ps[j], preferred_element_type=F32)

    def chunk_online(kc, vtc):
        for j in range(group):
            qt = qt_ref[j * HEAD_DIM:(j + 1) * HEAD_DIM, :]
            s = jnp.dot(kc, qt, preferred_element_type=F32)
            m = m_s[j]
            m_new = jnp.maximum(m, jnp.max(s, axis=0, keepdims=True))
            p = jnp.exp2(s - m_new).astype(BF16)
            acc_s[j] = jnp.exp2(m - m_new) * acc_s[j] + jnp.dot(vtc, p, preferred_element_type=F32)
            m_s[j] = m_new

    def sweep(chunk):
        is_latent = pl.program_id(1) * tq < n_lat

        @pl.when(is_latent)
        def _():
            def body(c, carry):
                r0 = pl.multiple_of(c * tk, LANE)
                chunk(k_ref[pl.ds(r0, tk), :], vt_ref[:, pl.ds(r0, tk)])
                return carry

            lax.fori_loop(0, (n_lat + n_ctx) // tk, body, 0)

        @pl.when(jnp.logical_not(is_latent))
        def _():
            chunk(k_ref[n_lat:n_lat + n_ctx, :], vt_ref[:, n_lat:n_lat + n_ctx])

    @pl.when(bounded_ref[0] != 0)
    def _():
        sweep(chunk_bounded)

    @pl.when(bounded_ref[0] == 0)
    def _():
        m_s[...] = jnp.full(m_s.shape, NEG_INF, F32)
        sweep(chunk_online)

    for j in range(group):
        sl = slice(j * HEAD_DIM, (j + 1) * HEAD_DIM)
        acc = acc_s[j]
        o = (acc[:HEAD_DIM] / acc[HEAD_DIM:HEAD_DIM + 1]).T
        o_ref[:, sl] = (o * _silu(gate_ref[:, sl].astype(F32))).astype(o_ref.dtype)


SCORE_BOUND_LOG2 = 60.0


def _scores_bounded(q_gain, k_gain, extra=0.0):
    bound = HEAD_DIM * jnp.max(jnp.abs(q_gain)) * jnp.max(jnp.abs(k_gain)) * 1.02 + extra
    return (bound <= SCORE_BOUND_LOG2).astype(jnp.int32).reshape(1)


def _gqa_attention(bounded, qt, k, vt, u, n_q_heads, n_kv_heads, off_gate, n_lat):
    t = u.shape[0]
    group = n_q_heads // n_kv_heads
    gw = group * HEAD_DIM
    tq = 256
    tk = _largest_tile(t, 4608, LANE)
    vrows = HEAD_DIM + GQA_ONES_ROWS
    assert t % tq == 0 and n_lat % tq == 0 and off_gate % gw == 0
    g_blk0 = off_gate // gw
    grid_spec = pltpu.PrefetchScalarGridSpec(
        num_scalar_prefetch=1,
        grid=(n_kv_heads, t // tq),
        in_specs=[pl.BlockSpec((gw, tq), lambda g, i, b: (g, i)),
                  pl.BlockSpec((t, HEAD_DIM), lambda g, i, b: (0, g)),
                  pl.BlockSpec((vrows, t), lambda g, i, b: (g, 0)),
                  pl.BlockSpec((tq, gw), lambda g, i, b: (i, g_blk0 + g))],
        out_specs=pl.BlockSpec((tq, gw), lambda g, i, b: (i, g)),
        scratch_shapes=[pltpu.VMEM((group, 1, tq), F32),
                        pltpu.VMEM((group, vrows, tq), F32)])
    return pl.pallas_call(
        functools.partial(_gqa_kernel, group=group, tq=tq, tk=tk, n_lat=n_lat, n_ctx=t - n_lat),
        grid_spec=grid_spec,
        out_shape=jax.ShapeDtypeStruct((t, n_q_heads * HEAD_DIM), BF16),
        compiler_params=_params(("arbitrary", "arbitrary"), 48),
        name="gqa_attention",
    )(bounded, qt, k, vt, u)


def _na_kernel(bounded_ref, qt_ref, k0_ref, k1_ref, k2_ref, kc_ref, v0_ref, v1_ref, v2_ref, vc_ref,
               bias_ref, gate_ref, o_ref, *, n_lat_blocks):
    is_latent = pl.program_id(1) < n_lat_blocks
    is_bounded = bounded_ref[0] != 0
    vrows = HEAD_DIM + GQA_ONES_ROWS

    def probabilities(j, latent, bounded):
        hs = slice(j * HEAD_DIM, (j + 1) * HEAD_DIM)
        qt = qt_ref[hs, :]
        if not latent:
            s = jnp.dot(kc_ref[:, hs], qt, preferred_element_type=F32)
        else:
            n_ctx = kc_ref.shape[0]
            k = jnp.concatenate([kc_ref[:, hs], k0_ref[:, hs], k1_ref[:, hs], k2_ref[:, hs]], axis=0)
            s = jnp.dot(k, qt, preferred_element_type=F32)
            s = jnp.concatenate([s[:n_ctx], s[n_ctx:] + bias_ref[j].astype(F32)], axis=0)
        if not bounded:
            s = s - jnp.max(s, axis=0, keepdims=True)
        return jnp.exp2(s).astype(BF16)

    def output(j, probs, latent):
        hs = slice(j * HEAD_DIM, (j + 1) * HEAD_DIM)
        vs = slice(j * vrows, (j + 1) * vrows)
        vt = vc_ref[vs, :]
        if latent:
            vt = jnp.concatenate([vt, v0_ref[vs, :], v1_ref[vs, :], v2_ref[vs, :]], axis=1)
        acc = jnp.dot(vt, probs, preferred_element_type=F32)
        o = (acc[:HEAD_DIM] / acc[HEAD_DIM:HEAD_DIM + 1]).T
        o_ref[:, hs] = (o * _silu(gate_ref[:, hs].astype(F32))).astype(o_ref.dtype)

    heads = range(qt_ref.shape[0] // HEAD_DIM)
    for latent in (True, False):
        for bounded in (True, False):
            @pl.when(jnp.logical_and(is_latent == latent, is_bounded == bounded))
            def _():
                probs = [probabilities(j, latent, bounded) for j in heads]
                for j in heads:
                    output(j, probs[j], latent)


def _na_bias_table(rel_bias, n_rows):
    c = np.arange(GRID_W)[None, :]
    kj = np.arange(GRID_W)[:, None]
    cs = np.clip(c - NA_KW // 2, 0, GRID_W - NA_KW)
    col_ok = (kj >= cs) & (kj < cs + NA_KW)
    sel_c = (np.arange(2 * NA_KW - 1)[:, None, None] == (kj - c + NA_KW - 1)[None]) & col_ok[None]
    cols = jnp.einsum('hde,ekc->hdkc', rel_bias * math.log2(math.e), jnp.asarray(sel_c, F32),
                      precision=lax.Precision.HIGHEST)
    cols = jnp.where(jnp.asarray(col_ok)[None, None], cols, NEG_INF).astype(BF16)
    neg = jnp.full((rel_bias.shape[0], GRID_W, GRID_W), NEG_INF, BF16)
    tables = []
    for r0, ws in ((0, 0), (NA_Q_ROWS, 0), (n_rows - NA_Q_ROWS, n_rows - NA_WIN_ROWS)):
        key_rows = []
        for i in range(NA_WIN_ROWS):
            blocks = []
            for a in range(NA_Q_ROWS):
                r, kr = r0 + a, ws + i
                rs = min(max(r - NA_KH // 2, 0), n_rows - NA_KH)
                blocks.append(cols[:, kr - r + NA_KH - 1] if rs <= kr < rs + NA_KH else neg)
            key_rows.append(jnp.concatenate(blocks, axis=-1))
        tables.append(jnp.concatenate(key_rows, axis=-2))
    return jnp.stack(tables, axis=0)


def _na_attention(bounded, qt, kn, vt, u, n_heads, off_gate, bias_tbl, n_lat):
    t = u.shape[0]
    tb = NA_Q_ROWS * GRID_W
    n_lat_blocks = n_lat // tb
    ctx_blk = t // tb - 1
    hp = NA_HEADS_PER_STEP
    hw = hp * HEAD_DIM
    vrows = hp * (HEAD_DIM + GQA_ONES_ROWS)
    assert t - n_lat == tb and n_lat_blocks >= 3 and off_gate % hw == 0 and n_heads % hp == 0
    g_blk0 = off_gate // hw

    def win(i, j):
        return jnp.clip(i - 1, 0, n_lat_blocks - 3) + j

    def k_spec(j):
        return pl.BlockSpec((tb, hw), lambda h, i, b: (win(i, j), h))

    def v_spec(j):
        return pl.BlockSpec((vrows, tb), lambda h, i, b: (h, win(i, j)))

    def bias_cls(h, i, b):
        return (jnp.where(i == 0, 0, jnp.where(i >= n_lat_blocks - 1, 2, 1)), h, 0, 0)

    grid_spec = pltpu.PrefetchScalarGridSpec(
        num_scalar_prefetch=1,
        grid=(n_heads // hp, t // tb),
        in_specs=[pl.BlockSpec((hw, tb), lambda h, i, b: (h, i)),
                  k_spec(0), k_spec(1), k_spec(2),
                  pl.BlockSpec((tb, hw), lambda h, i, b: (ctx_blk, h)),
                  v_spec(0), v_spec(1), v_spec(2),
                  pl.BlockSpec((vrows, tb), lambda h, i, b: (h, ctx_blk)),
                  pl.BlockSpec((None, hp, NA_WIN_ROWS * GRID_W, tb), bias_cls),
                  pl.BlockSpec((tb, hw), lambda h, i, b: (i, g_blk0 + h))],
        out_specs=pl.BlockSpec((tb, hw), lambda h, i, b: (i, h)))
    return pl.pallas_call(
        functools.partial(_na_kernel, n_lat_blocks=n_lat_blocks),
        grid_spec=grid_spec,
        out_shape=jax.ShapeDtypeStruct((t, n_heads * HEAD_DIM), BF16),
        compiler_params=_params(("arbitrary", "arbitrary"), 32),
        name="na_attention",
    )(bounded, qt, kn, kn, kn, kn, vt, vt, vt, vt, bias_tbl, u)


def _store_slabs(o_ref, y):
    q, ns, _, ct = o_ref.shape
    for c in range(q):
        o_ref[c] = y[:, c * ct:(c + 1) * ct].reshape(ns, FFT_N2, ct)


def _load_slabs(ref):
    q, ns, _, ct = ref.shape
    return jnp.concatenate([ref[c].reshape(ns * FFT_N2, ct) for c in range(q)], axis=1)


def _short_conv_kernel(x_ref, prev_ref, next_ref, w_ref, o_ref, *, tr):
    i = pl.program_id(0)
    n = pl.num_programs(0)
    x = x_ref[...].astype(F32)
    halo = prev_ref.shape[0]
    prev_row = jnp.where(i == 0, 0.0, prev_ref[halo - 1:halo, :].astype(F32))
    next_row = jnp.where(i == n - 1, 0.0, next_ref[0:1, :].astype(F32))
    row = lax.broadcasted_iota(jnp.int32, x.shape, 0)
    up = jnp.where(row == 0, prev_row, pltpu.roll(x, 1, 0))
    down = jnp.where(row == tr - 1, next_row, pltpu.roll(x, tr - 1, 0))
    _store_slabs(o_ref, (up * w_ref[0:1, :] + x * w_ref[1:2, :] + down * w_ref[2:3, :]).astype(o_ref.dtype))


def _short_conv(u, width, conv_w, n_lat):
    tr = _largest_tile(n_lat, 1024, 256)
    tc = _largest_tile(width, 1024, 256)
    halo = 16
    hb = tr // halo
    n_tiles = n_lat // tr
    wpad = jnp.zeros((8, width), F32).at[:HY_SHORT].set(conv_w.astype(F32))
    return pl.pallas_call(
        functools.partial(_short_conv_kernel, tr=tr),
        grid=(n_tiles, width // tc),
        in_specs=[pl.BlockSpec((tr, tc), lambda i, j: (i, j)),
                  pl.BlockSpec((halo, tc), lambda i, j: (jnp.maximum(i * hb - 1, 0), j)),
                  pl.BlockSpec((halo, tc), lambda i, j: (jnp.minimum((i + 1) * hb, n_tiles * hb - 1), j)),
                  pl.BlockSpec((8, tc), lambda i, j: (0, j))],
        out_specs=pl.BlockSpec((tc // FFT_CT, tr // FFT_N2, FFT_N2, FFT_CT), lambda i, j: (j, i, 0, 0)),
        out_shape=jax.ShapeDtypeStruct((width // FFT_CT, n_lat // FFT_N2, FFT_N2, FFT_CT), BF16),
        compiler_params=_params(("arbitrary", "arbitrary"), 32),
        name="short_conv",
    )(u, u, u, wpad)


def _hy_gate2_kernel(x2_ref, c_ref, y1_ref, d_ref, g_ref, ctx_ref, o_ref, *, n_lat_tiles):
    i = pl.program_id(0)

    @pl.when(i < n_lat_tiles)
    def _():
        rows = lambda ref: _load_slabs(ref).astype(F32)
        y1 = rows(y1_ref)
        z = rows(x2_ref) * (rows(c_ref) + y1 * d_ref[1:2, :])
        o_ref[...] = (z * _silu(g_ref[...].astype(F32))).astype(o_ref.dtype)

    @pl.when(i >= n_lat_tiles)
    def _():
        o_ref[...] = ctx_ref[...]


def _hy_gate2(vx4, c2, y1, dskip8, u, off_gate, hy_ctx, hy):
    t = u.shape[0]
    n_lat = vx4.shape[1] * FFT_N2
    tr = t - n_lat
    tc = _largest_tile(hy, 2048, 256)
    nb = hy // tc
    n_lat_tiles = n_lat // tr
    assert n_lat % tr == 0 and off_gate % tc == 0 and tr % FFT_N2 == 0
    g_blk0 = off_gate // tc
    lat = lambda i: jnp.minimum(i, n_lat_tiles - 1)
    slab = lambda c0: pl.BlockSpec((tc // FFT_CT, tr // FFT_N2, FFT_N2, FFT_CT),
                                   lambda i, j: (c0 + j, lat(i), 0, 0))
    return pl.pallas_call(
        functools.partial(_hy_gate2_kernel, n_lat_tiles=n_lat_tiles),
        grid=(t // tr, nb),
        in_specs=[slab(2 * nb), slab(0), slab(0),
                  pl.BlockSpec((8, tc), lambda i, j: (0, j)),
                  pl.BlockSpec((tr, tc), lambda i, j: (i, g_blk0 + j)),
                  pl.BlockSpec((tr, tc), lambda i, j: (0, j))],
        out_specs=pl.BlockSpec((tr, tc), lambda i, j: (i, j)),
        out_shape=jax.ShapeDtypeStruct((t, hy), BF16),
        compiler_params=_params(("arbitrary", "arbitrary"), 32),
        name="hy_gate2",
    )(vx4, c2, y1, dskip8, u, hy_ctx)


def _filter_hidden_kernel(z_ref, w1_ref, b1_ref, w2_ref, b2_ref, fr_ref, o_ref):
    hp = lax.Precision.HIGHEST
    fr = fr_ref[...]
    h = jnp.sin(fr * (jnp.dot(z_ref[...], w1_ref[...], precision=hp,
                              preferred_element_type=F32) + b1_ref[...]))
    o_ref[...] = jnp.sin(fr * (jnp.dot(h, w2_ref[...], precision=hp,
                                       preferred_element_type=F32) + b2_ref[...]))


def _pad2(a, rows, cols):
    return jnp.zeros((rows, cols), F32).at[:a.shape[0], :a.shape[1]].set(a.astype(F32))


def _filter_hidden(feat, w1, b1, w2, b2, freq):
    r = feat.shape[0]
    tr = _largest_tile(r, 1024, 256)
    full = lambda i: (0, 0)
    return pl.pallas_call(
        _filter_hidden_kernel,
        grid=(r // tr,),
        in_specs=[pl.BlockSpec((tr, LANE), lambda i: (i, 0)),
                  pl.BlockSpec((LANE, LANE), full), pl.BlockSpec((1, LANE), full),
                  pl.BlockSpec((LANE, LANE), full), pl.BlockSpec((1, LANE), full),
                  pl.BlockSpec((1, LANE), full)],
        out_specs=pl.BlockSpec((tr, LANE), lambda i: (i, 0)),
        out_shape=jax.ShapeDtypeStruct((r, LANE), F32),
        compiler_params=_params(("arbitrary",), 32),
        name="filter_hidden",
    )(feat, _pad2(w1, LANE, LANE), _pad2(b1[None], 1, LANE), _pad2(w2, LANE, LANE),
      _pad2(b2[None], 1, LANE), _pad2(freq[None], 1, LANE))


def _two_sided_positions(n):
    idx = np.arange(2 * n)
    return np.where(idx < n, idx, 2 * n - idx) % n


def _filter_features(n, n_bands):
    pos = _two_sided_positions(n).astype(np.float64)[:, None]
    t = pos / (n - 1)
    w = (2.0 * math.pi / n) * pos
    bands = np.linspace(1e-4, n_bands - 1, n_bands)[None, :]
    z = np.zeros((2 * n, LANE))
    z[:, 0:1] = t
    z[:, 1:1 + n_bands] = np.cos(bands * w)
    z[:, 1 + n_bands:1 + 2 * n_bands] = -np.sin(bands * w)
    return jnp.asarray(z, F32), jnp.asarray(t, F32)


def _filter_gen_kernel(h_ref, w_ref, t_ref, dec_ref, o_ref, *, tr, n):
    i = pl.program_id(0)
    f = jnp.dot(h_ref[...].astype(BF16), w_ref[...], preferred_element_type=F32)
    f = f * jnp.exp(-t_ref[...] * jnp.abs(dec_ref[0:1, :]))
    row = i * tr + lax.broadcasted_iota(jnp.int32, (tr, 1), 0)
    _store_slabs(o_ref, jnp.where(row == n, 0.0, f).astype(o_ref.dtype))


def _filter_gen(hidden, row0, w3p, tcol, decay8, n, order, hy):
    tr = _largest_tile(n, 1024, 256)
    tc = _largest_tile(hy, 1024, 256)
    nb = hy // tc
    n_half = n // tr
    hb0 = row0 // tr
    assert row0 % tr == 0
    return pl.pallas_call(
        functools.partial(_filter_gen_kernel, tr=tr, n=n),
        grid=(2 * n // tr, order * nb),
        in_specs=[pl.BlockSpec((tr, LANE), lambda i, j: (hb0 + i, 0)),
                  pl.BlockSpec((LANE, tc), lambda i, j: (0, (j // nb) * 2 * nb + (i // n_half) * nb + j % nb)),
                  pl.BlockSpec((tr, 1), lambda i, j: (i, 0)),
                  pl.BlockSpec((None, 8, tc), lambda i, j: (j // nb, 0, j % nb))],
        out_specs=pl.BlockSpec((tc // FFT_CT, tr // FFT_N2, FFT_N2, FFT_CT), lambda i, j: (j, i, 0, 0)),
        out_shape=jax.ShapeDtypeStruct((order * hy // FFT_CT, 2 * n // FFT_N2, FFT_N2, FFT_CT), BF16),
        compiler_params=_params(("arbitrary", "arbitrary"), 32),
        name="filter_gen",
    )(hidden, w3p, tcol, decay8)


def _dft_tables(n1):
    n2 = FFT_N2
    n = n1 * n2
    half = n1 // 2
    nz = half
    k1 = np.concatenate([np.arange(0, half + 1, 2), np.arange(1, half + 1, 2)])

    def cs(idx, period):
        ang = (idx % period) * (2.0 * math.pi / period)
        return np.cos(ang), -np.sin(ang)

    eye8 = np.eye(8)
    fr, fi = cs(k1[:, None] * np.arange(nz)[None, :], n1)
    kf = np.stack([fr, fi], axis=1)
    kf = np.einsum('kpn,jl->kpjnl', kf, eye8).reshape((half + 1) * 16, nz * 8)
    gr, gi = cs(np.arange(half)[:, None] * k1[None, :], n1)
    weight = np.where((k1 == 0) | (k1 == half), 1.0, 2.0) / n
    ki = np.stack([gr, gi], axis=2) * weight[None, :, None]
    ki = np.einsum('nkp,jl->njkpl', ki, eye8).reshape(half * 8, (half + 1) * 16)
    r = np.arange(2 * n2)
    part = (r % 16) // 8
    n2_of_r = 8 * (r // 16) + r % 8
    k2 = np.arange(n2)
    idx = k2[None, :, None] * n2_of_r[None, None, :] * n1 + k1[:, None, None] * n2_of_r[None, None, :]
    cr, ci = cs(idx, n)
    top = np.where(part[None, None, :] == 0, cr, -ci)
    bot = np.where(part[None, None, :] == 0, ci, cr)
    m = np.concatenate([top, bot], axis=1)
    return tuple(jnp.asarray(a, F32).astype(BF16) for a in (kf, ki, m))


def _cross_slab_forward(z_ref, kf_ref, a_scr, nz, nh, ct):
    def body(b, carry):
        z = z_ref[:, pl.ds(pl.multiple_of(16 * b, 16), 16), :].astype(F32)
        for half in range(2):
            g = z[:, 8 * half:8 * half + 8, :].reshape(nz * 8, ct).astype(BF16)
            p = jnp.dot(kf_ref[...], g, preferred_element_type=F32)
            start = pl.multiple_of(32 * b + 16 * half, 16)
            a_scr[:, pl.ds(start, 16), :] = p.astype(BF16).reshape(nh, 16, ct)
        return carry

    lax.fori_loop(0, FFT_N2 // 16, body, 0)


def _slab_block(nh):
    return max(k for k in range(1, 17) if nh % k == 0)


def _cross_slab_forward_full(z_ref, kf_ref, a_scr, nh, ct):
    half = nh - 1
    n_even = half // 2 + 1

    def body(b, carry):
        rows = pl.ds(pl.multiple_of(16 * b, 16), 16)
        lo = z_ref[0:half, rows, :].astype(F32)
        hi = z_ref[half:2 * half, rows, :].astype(F32)
        for sub in range(2):
            sl = slice(8 * sub, 8 * sub + 8)
            g_even = (lo[:, sl, :] + hi[:, sl, :]).reshape(half * 8, ct).astype(BF16)
            g_odd = (lo[:, sl, :] - hi[:, sl, :]).reshape(half * 8, ct).astype(BF16)
            p_even = jnp.dot(kf_ref[0:n_even * 16, :], g_even, preferred_element_type=F32)
            p_odd = jnp.dot(kf_ref[n_even * 16:nh * 16, :], g_odd, preferred_element_type=F32)
            start = pl.ds(pl.multiple_of(32 * b + 16 * sub, 16), 16)
            a_scr[0:n_even, start, :] = p_even.astype(BF16).reshape(n_even, 16, ct)
            a_scr[n_even:nh, start, :] = p_odd.astype(BF16).reshape(nh - n_even, 16, ct)
        return carry

    lax.fori_loop(0, FFT_N2 // 16, body, 0)


def _fft_fwd_kernel(z_ref, kf_ref, m_ref, o_ref, a_scr, *, nh, kb, ct):
    kblk = pl.program_id(1)

    @pl.when(kblk == 0)
    def _():
        _cross_slab_forward_full(z_ref, kf_ref, a_scr, nh, ct)

    for k in range(kb):
        x = jnp.dot(m_ref[k], a_scr[kblk * kb + k], preferred_element_type=F32)
        o_ref[k] = x.astype(o_ref.dtype)


def _fft_forward(z4, kf, m):
    n_tiles, nz, _, ct = z4.shape
    nh = m.shape[0]
    kb = _slab_block(nh)
    assert nz == 2 * (nh - 1) and ct == FFT_CT
    return pl.pallas_call(
        functools.partial(_fft_fwd_kernel, nh=nh, kb=kb, ct=ct),
        grid=(n_tiles, nh // kb),
        in_specs=[pl.BlockSpec((None, nz, FFT_N2, ct), lambda j, k: (j, 0, 0, 0)),
                  pl.BlockSpec(kf.shape, lambda j, k: (0, 0)),
                  pl.BlockSpec((kb, 2 * FFT_N2, 2 * FFT_N2), lambda j, k: (k, 0, 0))],
        out_specs=pl.BlockSpec((None, kb, 2 * FFT_N2, ct), lambda j, k: (j, k, 0, 0)),
        out_shape=jax.ShapeDtypeStruct((n_tiles, nh, 2 * FFT_N2, ct), BF16),
        scratch_shapes=[pltpu.VMEM((nh, 2 * FFT_N2, ct), BF16)],
        compiler_params=_params(("arbitrary", "arbitrary"), 56),
        name="fft_forward",
    )(z4, kf, m)


def _fft_conv_kernel(z_ref, kf_ref, ki_ref, m_ref, h_ref, *rest, nz, nh, kb, ct, skip_row):
    gated = len(rest) == 4
    o_ref, a_scr = rest[-2:]
    kblk = pl.program_id(1)
    n2 = FFT_N2

    @pl.when(kblk == 0)
    def _():
        _cross_slab_forward(z_ref, kf_ref, a_scr, nz, nh, ct)

    for k in range(kb):
        mk = m_ref[k]
        x = jnp.dot(mk, a_scr[kblk * kb + k], preferred_element_type=F32)
        h = h_ref[k].astype(F32)
        xr, xi, hr, hi = x[:n2], x[n2:], h[:n2], h[n2:]
        y = jnp.concatenate([xr * hr - xi * hi, xr * hi + xi * hr], axis=0).astype(BF16)
        b = lax.dot_general(mk, y, (((0,), (0,)), ((), ())), preferred_element_type=F32)
        a_scr[kblk * kb + k] = b.astype(BF16)

    @pl.when(kblk == pl.num_programs(1) - 1)
    def _():
        def body(b, carry):
            halves = []
            for half in range(2):
                start = pl.multiple_of(32 * b + 16 * half, 16)
                r = a_scr[:, pl.ds(start, 16), :].reshape(nh * 16, ct)
                p = jnp.dot(ki_ref[...], r, preferred_element_type=F32)
                halves.append(p.reshape(nz, 8, ct))
            out = jnp.concatenate(halves, axis=1)
            rows = pl.ds(pl.multiple_of(16 * b, 16), 16)
            if gated:
                x_ref, d_ref = rest[:2]
                z = z_ref[:, rows, :].astype(F32)
                out = x_ref[:, rows, :].astype(F32) * (out + z * d_ref[skip_row:skip_row + 1, :])
            o_ref[:, rows, :] = out.astype(o_ref.dtype)
            return carry

        lax.fori_loop(0, n2 // 16, body, 0)


def _fft_conv(z4, z_blk0, spec, spec_blk0, kf, ki, m, c, gate=None):
    nz, ct = z4.shape[1], z4.shape[3]
    nh = m.shape[0]
    kb = _slab_block(nh)
    assert nh == nz + 1 and ct == FFT_CT
    in_specs = [pl.BlockSpec((None, nz, FFT_N2, ct), lambda j, k: (z_blk0 + j, 0, 0, 0)),
                pl.BlockSpec(kf.shape, lambda j, k: (0, 0)),
                pl.BlockSpec(ki.shape, lambda j, k: (0, 0)),
                pl.BlockSpec((kb, 2 * FFT_N2, 2 * FFT_N2), lambda j, k: (k, 0, 0)),
                pl.BlockSpec((None, kb, 2 * FFT_N2, ct), lambda j, k: (spec_blk0 + j, k, 0, 0))]
    operands = [z4, kf, ki, m, spec]
    skip_row = 0
    if gate is not None:
        x4, x_blk0, dskip8, skip_row = gate
        in_specs += [pl.BlockSpec((None, nz, FFT_N2, ct), lambda j, k: (x_blk0 + j, 0, 0, 0)),
                     pl.BlockSpec((8, ct), lambda j, k: (0, j))]
        operands += [x4, dskip8]
    return pl.pallas_call(
        functools.partial(_fft_conv_kernel, nz=nz, nh=nh, kb=kb, ct=ct, skip_row=skip_row),
        grid=(c // ct, nh // kb),
        in_specs=in_specs,
        out_specs=pl.BlockSpec((None, nz, FFT_N2, ct), lambda j, k: (j, 0, 0, 0)),
        out_shape=jax.ShapeDtypeStruct((c // ct, nz, FFT_N2, ct), BF16),
        scratch_shapes=[pltpu.VMEM((nh, 2 * FFT_N2, ct), BF16)],
        compiler_params=_params(("arbitrary", "arbitrary"), 56),
        name="fft_conv",
    )(*operands)


def _ctx_dft_tables(n):
    nn = 2 * n
    k = np.arange(nn)
    ang = ((k[:, None] * k[None, :]) % nn) * (2.0 * math.pi / nn)
    cr, ci = np.cos(ang), -np.sin(ang)
    full = np.concatenate([cr, ci], axis=0)
    inv = np.concatenate([cr[:n], ci[:n]], axis=1) / nn
    return tuple(jnp.asarray(a, F32).astype(BF16) for a in (full[:, :n], full, inv))


def _ctx_hyena_kernel(v_ref, x1_ref, x2_ref, g_ref, cw0_ref, cw1_ref, cw2_ref, hid_ref,
                      w10_ref, w11_ref, w20_ref, w21_ref, t_ref, dec_ref, dsk_ref,
                      ff_ref, ffull_ref, finv_ref, o_ref, *, n):
    row = lax.broadcasted_iota(jnp.int32, (n, 1), 0)

    def sconv(x_ref, w_ref):
        x = x_ref[...].astype(F32)
        up = jnp.where(row == 0, 0.0, pltpu.roll(x, 1, 0))
        down = jnp.where(row == n - 1, 0.0, pltpu.roll(x, n - 1, 0))
        y = up * w_ref[0:1, :] + x * w_ref[1:2, :] + down * w_ref[2:3, :]
        return y.astype(BF16).astype(F32)

    hid = hid_ref[...].astype(BF16)
    tcol = t_ref[...]
    row2 = lax.broadcasted_iota(jnp.int32, (2 * n, 1), 0)

    def spectrum(wf_ref, wb_ref, o):
        fwd = jnp.dot(hid[:n], wf_ref[...], preferred_element_type=F32)
        bwd = jnp.dot(hid[n:], wb_ref[...], preferred_element_type=F32)
        f = jnp.concatenate([fwd, bwd], axis=0) * jnp.exp(-tcol * jnp.abs(dec_ref[o:o + 1, :]))
        f = jnp.where(row2 == n, 0.0, f).astype(BF16)
        return jnp.dot(ffull_ref[...], f, preferred_element_type=F32)

    def conv(z, h):
        x = jnp.dot(ff_ref[...], z.astype(BF16), preferred_element_type=F32)
        h = h.astype(BF16).astype(F32)
        nn = 2 * n
        xr, xi, hr, hi = x[:nn], x[nn:], h[:nn], h[nn:]
        y = jnp.concatenate([xr * hr - xi * hi, xr * hi + xi * hr], axis=0).astype(BF16)
        return jnp.dot(finv_ref[...], y, preferred_element_type=F32)

    v = sconv(v_ref, cw0_ref)
    x1 = sconv(x1_ref, cw1_ref)
    x2 = sconv(x2_ref, cw2_ref)
    c1 = conv(v, spectrum(w10_ref, w11_ref, 0)).astype(BF16).astype(F32)
    y1 = (x1 * (c1 + v * dsk_ref[0:1, :])).astype(BF16).astype(F32)
    c2 = conv(y1, spectrum(w20_ref, w21_ref, 1)).astype(BF16).astype(F32)
    z = x2 * (c2 + y1 * dsk_ref[1:2, :])
    o_ref[...] = (z * _silu(g_ref[...].astype(F32))).astype(o_ref.dtype)


def _ctx_hyena(u, n_lat, hy, order, off_gate, conv8, hidden, hid_row0, w3p, tcol, decay8, dskip8, tables):
    t = u.shape[0]
    n = t - n_lat
    assert order == 2 and n_lat % n == 0 and hid_row0 % (2 * n) == 0
    ct = FFT_CT
    nb = hy // ct
    rb = n_lat // n
    ff, ffull, finv = tables
    const = lambda j: (0, 0)
    ublk = lambda c0: pl.BlockSpec((n, ct), lambda j: (rb, c0 + j))
    wblk = lambda c0: pl.BlockSpec((LANE, ct), lambda j: (0, c0 + j))
    cwblk = lambda c0: pl.BlockSpec((8, ct), lambda j: (0, c0 + j))
    return pl.pallas_call(
        functools.partial(_ctx_hyena_kernel, n=n),
        grid=(nb,),
        in_specs=[ublk(0), ublk(nb), ublk(2 * nb), ublk(off_gate // ct),
                  cwblk(0), cwblk(nb), cwblk(2 * nb),
                  pl.BlockSpec((2 * n, LANE), lambda j: (hid_row0 // (2 * n), 0)),
                  wblk(0), wblk(nb), wblk(2 * nb), wblk(3 * nb),
                  pl.BlockSpec((2 * n, 1), const),
                  pl.BlockSpec((8, ct), lambda j: (0, j)),
                  pl.BlockSpec((8, ct), lambda j: (0, j)),
                  pl.BlockSpec(ff.shape, const), pl.BlockSpec(ffull.shape, const),
                  pl.BlockSpec(finv.shape, const)],
        out_specs=pl.BlockSpec((n, ct), lambda j: (0, j)),
        out_shape=jax.ShapeDtypeStruct((n, hy), BF16),
        compiler_params=_params(("arbitrary",), 40),
        name="ctx_hyena",
    )(u, u, u, u, conv8, conv8, conv8, hidden, w3p, w3p, w3p, w3p, tcol, decay8, dskip8,
      ff, ffull, finv)


def _even_layer(x_all, h, mods_l, n_lat, w_in, w_out, conv_w, w1, b1, w2, b2, w3, freq, decay, dskip,
                q_norm, k_norm, rope, dft, ctx_dft, last):
    t, d = x_all.shape
    n_ctx = t - n_lat
    order, hy = decay.shape
    gqa_w = d - hy
    n_q = gqa_w // HEAD_DIM
    n_kv = n_q // GQA_GROUP
    kv_w = n_kv * HEAD_DIM
    off_hy_gate = (order + 1) * hy
    off_q = off_hy_gate + hy
    off_v = off_q + gqa_w + kv_w
    off_at_gate = off_v + kv_w
    scale = HEAD_DIM ** -0.5

    u = _in_proj(h, *w_in)

    q_gain = q_norm.astype(F32) * (scale * math.log2(math.e))
    k_gain = k_norm.astype(F32)
    qt, kn, vt = _attn_prep(u, off_q, off_q + gqa_w, off_v, n_q, n_kv, 1, q_gain, k_gain,
                            rope[0], rope[1], True)
    att = _gqa_attention(_scores_bounded(q_gain, k_gain), qt, kn, vt, u, n_q, n_kv, off_at_gate, n_lat)

    n_bands = (w1.shape[0] - 1) // 2
    feat_l, t_l = _filter_features(n_lat, n_bands)
    feat_c, t_c = _filter_features(n_ctx, n_bands)
    hidden = _filter_hidden(jnp.concatenate([feat_l, feat_c], axis=0), w1, b1, w2, b2, freq)
    w3p = jnp.zeros((LANE, w3.shape[1]), BF16).at[:w3.shape[0]].set(w3.astype(BF16))
    decay8 = jnp.zeros((order, 8, hy), F32).at[:, 0].set(decay.astype(F32))
    dskip8 = jnp.zeros((8, hy), F32).at[:order].set(dskip.astype(F32))
    conv8 = jnp.zeros((8, conv_w.shape[1]), F32).at[:HY_SHORT].set(conv_w.astype(F32))
    kf_pad, ki, m = dft
    filt = _filter_gen(hidden, 0, w3p, t_l, decay8, n_lat, order, hy)
    spec = _fft_forward(filt, kf_pad, m)

    vx4 = _short_conv(u, (order + 1) * hy, conv_w, n_lat)
    ct = FFT_CT
    y1 = _fft_conv(vx4, 0, spec, 0, kf_pad, ki, m, hy, gate=(vx4, hy // ct, dskip8, 0))
    c2 = _fft_conv(y1, 0, spec, hy // ct, kf_pad, ki, m, hy)

    decay_c8 = jnp.zeros((8, hy), F32).at[:order].set(decay.astype(F32))
    hy_ctx = _ctx_hyena(u, n_lat, hy, order, off_hy_gate, conv8, hidden, 2 * n_lat, w3p, t_c,
                        decay_c8, dskip8, ctx_dft)
    hy_mix = _hy_gate2(vx4, c2, y1, dskip8, u, off_hy_gate, hy_ctx, hy)

    return _out_proj(hy_mix, 0, att, 0, *w_out, x_all, mods_l, n_lat, last)


def _odd_layer(x_all, h, mods_l, n_lat, w_in, w_out, q_norm, k_norm, rel_bias, rope, last):
    n_heads = rel_bias.shape[0]
    w = n_heads * HEAD_DIM
    scale = HEAD_DIM ** -0.5
    u = _in_proj(h, *w_in)
    log2e = math.log2(math.e)
    q_gain = q_norm.astype(F32) * (scale * log2e)
    k_gain = k_norm.astype(F32)
    hp = NA_HEADS_PER_STEP
    qt, kn, vt = _attn_prep(u, 0, w, 2 * w, hp, hp, n_heads // hp, q_gain, k_gain, rope[0], rope[1], False)
    bias_tbl = _na_bias_table(rel_bias.astype(F32), n_lat // GRID_W)
    bounded = _scores_bounded(q_gain, k_gain, jnp.max(jnp.abs(rel_bias)).astype(F32) * (log2e * 1.02))
    att = _na_attention(bounded, qt, kn, vt, u, n_heads, 3 * w, bias_tbl, n_lat)
    return _out_proj(att, 0, att, 1, *w_out, x_all, mods_l, n_lat, last)


def kernel(x, c, ctx, c_ctx, norm_g, ada_w, ada_b, e_w_in, e_w_out, hy_conv, hy_w1, hy_b1, hy_w2, hy_b2,
           hy_w3, hy_freq, hy_decay, hy_dskip, gqa_q_norm, gqa_k_norm, o_w_in, o_w_out, na_q_norm,
           na_k_norm, na_rel_bias):
    batch, n_lat, d = x.shape
    n_ctx = ctx.shape[1]
    depth = norm_g.shape[0]
    assert batch == 1 and n_lat % (2 * FFT_N2) == 0 and n_ctx == NA_Q_ROWS * GRID_W

    cvecs = jnp.zeros((8, d), F32).at[0].set(c[0]).at[1].set(c_ctx)
    mods = _ada_mods(cvecs, ada_w, ada_b)

    rope = _rope_tables(n_lat, n_ctx)
    n1 = 2 * n_lat // FFT_N2
    dft = _dft_tables(n1)
    ctx_dft = _ctx_dft_tables(n_ctx)

    for layer in range(depth):
        i = layer // 2
        last = layer == depth - 1
        if layer == 0:
            h, x_all = _norm_mod_join(x[0].astype(F32), ctx[0].astype(F32), norm_g[0], mods[0])
        else:
            h = _norm_mod(x_all, norm_g[layer], mods[layer], n_lat)
        if layer % 2 == 0:
            x_all = _even_layer(x_all, h, mods[layer], n_lat,
                                (e_w_in, i), (e_w_out, i), hy_conv[i],
                                hy_w1[i], hy_b1[i], hy_w2[i], hy_b2[i], hy_w3[i], hy_freq[i],
                                hy_decay[i], hy_dskip[i], gqa_q_norm[i], gqa_k_norm[i],
                                rope, dft, ctx_dft, last)
        else:
            x_all = _odd_layer(x_all, h, mods[layer], n_lat,
                               (o_w_in, i), (o_w_out, i),
                               na_q_norm[i], na_k_norm[i], na_rel_bias[i], rope, last)
    return x_all[None]
```

```python
import functools
import math

import numpy as np
import jax
import jax.numpy as jnp
from jax import lax
from jax.experimental import pallas as pl
from jax.experimental.pallas import tpu as pltpu

F32 = jnp.float32
BF16 = jnp.bfloat16

HEAD_DIM = 128
GRID_W = 64
EPS = 1e-6
ROPE_THETA = 10000.0
NA_KH = 8
NA_KW = 16
NA_Q_ROWS = 4
NA_WIN_ROWS = 12
NA_HEADS_PER_STEP = 8
GQA_GROUP = 4
HY_SHORT = 3
FFT_N2 = 128
FFT_CT = 256
NEG_INF = -1e30

V7X_VMEM_BYTES = 64 * 1024 * 1024
LANE = 128


def _params(semantics, vmem_mb):
    assert vmem_mb * 1024 * 1024 < V7X_VMEM_BYTES
    return pltpu.CompilerParams(dimension_semantics=semantics,
                                vmem_limit_bytes=vmem_mb * 1024 * 1024)


def _largest_tile(n, target, quantum):
    best = None
    t = quantum
    while t <= min(n, target):
        if n % t == 0:
            best = t
        t += quantum
    assert best is not None, (n, target, quantum)
    return best


def _silu(x):
    return x * jax.nn.sigmoid(x)


def _mods_kernel(c_ref, w_ref, b_ref, o_ref):
    s = _silu(c_ref[...]).astype(BF16)
    o_ref[...] = jnp.dot(s, w_ref[...].astype(BF16), preferred_element_type=F32) + b_ref[...]


def _ada_mods(cvecs, ada_w, ada_b):
    depth, d, n = ada_w.shape
    tn = _largest_tile(n, 512, LANE)
    return pl.pallas_call(
        _mods_kernel,
        grid=(depth, n // tn),
        in_specs=[pl.BlockSpec((8, d), lambda l, j: (0, 0)),
                  pl.BlockSpec((None, d, tn), lambda l, j: (l, 0, j)),
                  pl.BlockSpec((None, 1, tn), lambda l, j: (l, 0, j))],
        out_specs=pl.BlockSpec((None, 8, tn), lambda l, j: (l, 0, j)),
        out_shape=jax.ShapeDtypeStruct((depth, 8, n), F32),
        compiler_params=_params(("arbitrary", "arbitrary"), 40),
        name="ada_mods",
    )(cvecs, ada_w, ada_b.reshape(depth, 1, n))


def _norm_mod_kernel(x_ref, g_ref, sh_ref, sc_ref, o_ref, *, n_lat_tiles):
    x = x_ref[...]
    y = x * lax.rsqrt(jnp.mean(x * x, axis=-1, keepdims=True) + EPS) * g_ref[...]
    is_ctx = pl.program_id(0) >= n_lat_tiles
    sh = jnp.where(is_ctx, sh_ref[1:2, :], sh_ref[0:1, :])
    sc = jnp.where(is_ctx, sc_ref[1:2, :], sc_ref[0:1, :])
    o_ref[...] = (y * (1.0 + sc) + sh).astype(o_ref.dtype)


def _norm_mod(x_all, g, mods_l, n_lat):
    t, d = x_all.shape
    tr = 256
    assert t % tr == 0 and n_lat % tr == 0
    return pl.pallas_call(
        functools.partial(_norm_mod_kernel, n_lat_tiles=n_lat // tr),
        grid=(t // tr,),
        in_specs=[pl.BlockSpec((tr, d), lambda i: (i, 0)),
                  pl.BlockSpec((1, d), lambda i: (0, 0)),
                  pl.BlockSpec((8, d), lambda i: (0, 0)),
                  pl.BlockSpec((8, d), lambda i: (0, 1))],
        out_specs=pl.BlockSpec((tr, d), lambda i: (i, 0)),
        out_shape=jax.ShapeDtypeStruct((t, d), BF16),
        compiler_params=_params(("arbitrary",), 40),
        name="norm_mod",
    )(x_all, g.reshape(1, d), mods_l, mods_l)


def _norm_mod_join_kernel(x_ref, c_ref, g_ref, sh_ref, sc_ref, o_ref, xo_ref, *, n_lat_tiles):
    is_ctx = pl.program_id(0) >= n_lat_tiles
    x = jnp.where(is_ctx, c_ref[...], x_ref[...])
    y = x * lax.rsqrt(jnp.mean(x * x, axis=-1, keepdims=True) + EPS) * g_ref[...]
    sh = jnp.where(is_ctx, sh_ref[1:2, :], sh_ref[0:1, :])
    sc = jnp.where(is_ctx, sc_ref[1:2, :], sc_ref[0:1, :])
    o_ref[...] = (y * (1.0 + sc) + sh).astype(o_ref.dtype)
    xo_ref[...] = x


def _norm_mod_join(x_lat, x_ctx, g, mods_l):
    n_lat, d = x_lat.shape
    n_ctx = x_ctx.shape[0]
    tr = n_ctx
    assert n_lat % tr == 0
    n_lat_tiles = n_lat // tr
    t = n_lat + n_ctx
    return pl.pallas_call(
        functools.partial(_norm_mod_join_kernel, n_lat_tiles=n_lat_tiles),
        grid=(t // tr,),
        in_specs=[pl.BlockSpec((tr, d), lambda i: (jnp.minimum(i, n_lat_tiles - 1), 0)),
                  pl.BlockSpec((tr, d), lambda i: (0, 0)),
                  pl.BlockSpec((1, d), lambda i: (0, 0)),
                  pl.BlockSpec((8, d), lambda i: (0, 0)),
                  pl.BlockSpec((8, d), lambda i: (0, 1))],
        out_specs=[pl.BlockSpec((tr, d), lambda i: (i, 0)),
                   pl.BlockSpec((tr, d), lambda i: (i, 0))],
        out_shape=[jax.ShapeDtypeStruct((t, d), BF16), jax.ShapeDtypeStruct((t, d), F32)],
        compiler_params=_params(("arbitrary",), 48),
        name="norm_mod_join",
    )(x_lat, x_ctx, g.reshape(1, d), mods_l, mods_l)


def _in_proj_kernel(a_ref, w_ref, o_ref):
    w = w_ref[...].astype(BF16)
    o_ref[...] = jnp.dot(a_ref[...], w, preferred_element_type=F32).astype(o_ref.dtype)


def _in_proj(h, w_all, layer_idx):
    m, k = h.shape
    n = w_all.shape[2]
    tm = _largest_tile(m, 1408, 128)
    tn = _largest_tile(n, 512, 256)
    return pl.pallas_call(
        _in_proj_kernel,
        grid=(m // tm, n // tn),
        in_specs=[pl.BlockSpec((tm, k), lambda i, j: (i, 0)),
                  pl.BlockSpec((None, k, tn), lambda i, j: (layer_idx, 0, j))],
        out_specs=pl.BlockSpec((tm, tn), lambda i, j: (i, j)),
        out_shape=jax.ShapeDtypeStruct((m, n), BF16),
        compiler_params=_params(("arbitrary", "arbitrary"), 56),
        name="in_proj",
    )(h, w_all)


def _out_proj_kernel(a1_ref, a2_ref, w1_ref, w2_ref, x_ref, gate_ref, o_ref, wb1, wb2, *, tm, n_lat):
    i = pl.program_id(1)

    @pl.when(i == 0)
    def _():
        wb1[...] = w1_ref[...].astype(BF16)
        wb2[...] = w2_ref[...].astype(BF16)

    y = jnp.dot(a1_ref[...], wb1[...], preferred_element_type=F32)
    y = y + jnp.dot(a2_ref[...], wb2[...], preferred_element_type=F32)
    rows = i * tm + lax.broadcasted_iota(jnp.int32, (tm, 1), 0)
    gate = jnp.where(rows >= n_lat, gate_ref[1:2, :], gate_ref[0:1, :])
    o_ref[...] = x_ref[...] + gate * y


def _out_proj(a1, a1_blk, a2, a2_blk, w_all, layer_idx, x_all, mods_l, n_lat, latent_only):
    d = x_all.shape[1]
    t = n_lat if latent_only else x_all.shape[0]
    kh = w_all.shape[1] // 2
    tm = _largest_tile(t, 768, 256)
    tn = _largest_tile(d, 1024, 256)
    gate_blk0 = 2 * d // tn
    once = pl.Buffered(1)
    return pl.pallas_call(
        functools.partial(_out_proj_kernel, tm=tm, n_lat=n_lat),
        grid=(d // tn, t // tm),
        in_specs=[pl.BlockSpec((tm, kh), lambda j, i: (i, a1_blk)),
                  pl.BlockSpec((tm, kh), lambda j, i: (i, a2_blk)),
                  pl.BlockSpec((None, kh, tn), lambda j, i: (layer_idx, 0, j), pipeline_mode=once),
                  pl.BlockSpec((None, kh, tn), lambda j, i: (layer_idx, 1, j), pipeline_mode=once),
                  pl.BlockSpec((tm, tn), lambda j, i: (i, j)),
                  pl.BlockSpec((8, tn), lambda j, i: (0, gate_blk0 + j))],
        out_specs=pl.BlockSpec((tm, tn), lambda j, i: (i, j)),
        out_shape=jax.ShapeDtypeStruct((t, d), F32),
        scratch_shapes=[pltpu.VMEM((kh, tn), BF16), pltpu.VMEM((kh, tn), BF16)],
        compiler_params=_params(("arbitrary", "arbitrary"), 56),
        name="out_proj",
    )(a1, a2, w_all, w_all, x_all, mods_l)


def _rope_tables(n_lat, n_ctx):
    quarter = HEAD_DIM // 4
    inv = ROPE_THETA ** (-np.arange(quarter, dtype=np.float64) / quarter)
    tok = np.arange(n_lat)
    rows = (tok // GRID_W)[:, None] * inv[None]
    cols = (tok % GRID_W)[:, None] * inv[None]
    cos = np.concatenate([np.cos(rows), np.cos(rows), np.cos(cols), np.cos(cols)], axis=-1)
    sin = np.concatenate([-np.sin(rows), np.sin(rows), -np.sin(cols), np.sin(cols)], axis=-1)
    cos = np.concatenate([cos, np.ones((n_ctx, HEAD_DIM))], axis=0)
    sin = np.concatenate([sin, np.zeros((n_ctx, HEAD_DIM))], axis=0)
    return jnp.asarray(cos, F32), jnp.asarray(sin, F32)


GQA_ONES_ROWS = 16


def _attn_prep_kernel(q_ref, k_ref, v_ref, g_ref, cos_ref, sin_ref, qt_ref, ko_ref, vt_ref,
                      *, n_q, n_kv, rope):
    if rope:
        cos = cos_ref[...]
        sin = sin_ref[...]
        lane = lax.broadcasted_iota(jnp.int32, cos.shape, 1)
        first_quarter = (lane % 64) < 32

    def norm_rope(x, g):
        y = x * lax.rsqrt(jnp.mean(x * x, axis=-1, keepdims=True) + EPS) * g
        if not rope:
            return y
        partner = jnp.where(first_quarter, pltpu.roll(y, HEAD_DIM - 32, 1), pltpu.roll(y, 32, 1))
        return y * cos + partner * sin

    for h in range(n_q):
        sl = slice(h * HEAD_DIM, (h + 1) * HEAD_DIM)
        qt_ref[sl, :] = norm_rope(q_ref[:, sl].astype(F32), g_ref[:, sl]).T.astype(qt_ref.dtype)
    for h in range(n_kv):
        sl = slice(h * HEAD_DIM, (h + 1) * HEAD_DIM)
        gk = g_ref[:, (n_q + h) * HEAD_DIM:(n_q + h + 1) * HEAD_DIM]
        ko_ref[:, sl] = norm_rope(k_ref[:, sl].astype(F32), gk).astype(ko_ref.dtype)
        r0 = h * (HEAD_DIM + GQA_ONES_ROWS)
        vt_ref[r0:r0 + HEAD_DIM, :] = v_ref[:, sl].astype(F32).T.astype(vt_ref.dtype)
        vt_ref[r0 + HEAD_DIM:r0 + HEAD_DIM + GQA_ONES_ROWS, :] = jnp.ones(
            (GQA_ONES_ROWS, vt_ref.shape[1]), vt_ref.dtype)


def _attn_prep(u, off_q, off_k, off_v, n_q, n_kv, n_groups, q_gain, k_gain, cos_t, sin_t, rope):
    t = u.shape[0]
    qw, kw = n_q * HEAD_DIM, n_kv * HEAD_DIM
    tr = _largest_tile(t, 768, 256)
    vrows = n_kv * (HEAD_DIM + GQA_ONES_ROWS)
    assert off_q % qw == 0 and off_k % kw == 0 and off_v % kw == 0 and t % tr == 0
    q_blk, k_blk, v_blk = off_q // qw, off_k // kw, off_v // kw
    gvec = jnp.concatenate([jnp.tile(q_gain, n_q), jnp.tile(k_gain, n_kv)])[None]
    return pl.pallas_call(
        functools.partial(_attn_prep_kernel, n_q=n_q, n_kv=n_kv, rope=rope),
        grid=(t // tr, n_groups),
        in_specs=[pl.BlockSpec((tr, qw), lambda i, j: (i, q_blk + j)),
                  pl.BlockSpec((tr, kw), lambda i, j: (i, k_blk + j)),
                  pl.BlockSpec((tr, kw), lambda i, j: (i, v_blk + j)),
                  pl.BlockSpec((1, qw + kw), lambda i, j: (0, 0)),
                  pl.BlockSpec((tr, HEAD_DIM), lambda i, j: (i, 0)),
                  pl.BlockSpec((tr, HEAD_DIM), lambda i, j: (i, 0))],
        out_specs=[pl.BlockSpec((qw, tr), lambda i, j: (j, i)),
                   pl.BlockSpec((tr, kw), lambda i, j: (i, j)),
                   pl.BlockSpec((vrows, tr), lambda i, j: (j, i))],
        out_shape=[jax.ShapeDtypeStruct((n_groups * qw, t), BF16),
                   jax.ShapeDtypeStruct((t, n_groups * kw), BF16),
                   jax.ShapeDtypeStruct((n_groups * vrows, t), BF16)],
        compiler_params=_params(("arbitrary", "arbitrary"), 48),
        name="attn_prep",
    )(u, u, u, gvec, cos_t, sin_t)


def _gqa_kernel(bounded_ref, qt_ref, k_ref, vt_ref, gate_ref, o_ref, m_s, acc_s,
                *, group, tq, tk, n_lat, n_ctx):
    acc_s[...] = jnp.zeros(acc_s.shape, F32)

    def chunk_bounded(kc, vtc):
        ps = []
        for j in range(group):
            qt = qt_ref[j * HEAD_DIM:(j + 1) * HEAD_DIM, :]
            s = jnp.dot(kc, qt, preferred_element_type=F32)
            ps.append(jnp.exp2(s).astype(BF16))
        for j in range(group):
            acc_s[j] += jnp.dot(vtc, ps[j], preferred_element_type=F32)

    def chunk_online(kc, vtc):
        for j in range(group):
            qt = qt_ref[j * HEAD_DIM:(j + 1) * HEAD_DIM, :]
            s = jnp.dot(kc, qt, preferred_element_type=F32)
            m = m_s[j]
            m_new = jnp.maximum(m, jnp.max(s, axis=0, keepdims=True))
            p = jnp.exp2(s - m_new).astype(BF16)
            acc_s[j] = jnp.exp2(m - m_new) * acc_s[j] + jnp.dot(vtc, p, preferred_element_type=F32)
            m_s[j] = m_new

    def sweep(chunk):
        is_latent = pl.program_id(1) * tq < n_lat

        @pl.when(is_latent)
        def _():
            def body(c, carry):
                r0 = pl.multiple_of(c * tk, LANE)
                chunk(k_ref[pl.ds(r0, tk), :], vt_ref[:, pl.ds(r0, tk)])
                return carry

            lax.fori_loop(0, (n_lat + n_ctx) // tk, body, 0)

        @pl.when(jnp.logical_not(is_latent))
        def _():
            chunk(k_ref[n_lat:n_lat + n_ctx, :], vt_ref[:, n_lat:n_lat + n_ctx])

    @pl.when(bounded_ref[0] != 0)
    def _():
        sweep(chunk_bounded)

    @pl.when(bounded_ref[0] == 0)
    def _():
        m_s[...] = jnp.full(m_s.shape, NEG_INF, F32)
        sweep(chunk_online)

    for j in range(group):
        sl = slice(j * HEAD_DIM, (j + 1) * HEAD_DIM)
        acc = acc_s[j]
        o = (acc[:HEAD_DIM] / acc[HEAD_DIM:HEAD_DIM + 1]).T
        o_ref[:, sl] = (o * _silu(gate_ref[:, sl].astype(F32))).astype(o_ref.dtype)


SCORE_BOUND_LOG2 = 60.0


def _scores_bounded(q_gain, k_gain, extra=0.0):
    bound = HEAD_DIM * jnp.max(jnp.abs(q_gain)) * jnp.max(jnp.abs(k_gain)) * 1.02 + extra
    return (bound <= SCORE_BOUND_LOG2).astype(jnp.int32).reshape(1)


def _gqa_attention(bounded, qt, k, vt, u, n_q_heads, n_kv_heads, off_gate, n_lat):
    t = u.shape[0]
    group = n_q_heads // n_kv_heads
    gw = group * HEAD_DIM
    tq = 256
    tk = _largest_tile(t, 4608, LANE)
    vrows = HEAD_DIM + GQA_ONES_ROWS
    assert t % tq == 0 and n_lat % tq == 0 and off_gate % gw == 0
    g_blk0 = off_gate // gw
    grid_spec = pltpu.PrefetchScalarGridSpec(
        num_scalar_prefetch=1,
        grid=(n_kv_heads, t // tq),
        in_specs=[pl.BlockSpec((gw, tq), lambda g, i, b: (g, i)),
                  pl.BlockSpec((t, HEAD_DIM), lambda g, i, b: (0, g)),
                  pl.BlockSpec((vrows, t), lambda g, i, b: (g, 0)),
                  pl.BlockSpec((tq, gw), lambda g, i, b: (i, g_blk0 + g))],
        out_specs=pl.BlockSpec((tq, gw), lambda g, i, b: (i, g)),
        scratch_shapes=[pltpu.VMEM((group, 1, tq), F32),
                        pltpu.VMEM((group, vrows, tq), F32)])
    return pl.pallas_call(
        functools.partial(_gqa_kernel, group=group, tq=tq, tk=tk, n_lat=n_lat, n_ctx=t - n_lat),
        grid_spec=grid_spec,
        out_shape=jax.ShapeDtypeStruct((t, n_q_heads * HEAD_DIM), BF16),
        compiler_params=_params(("arbitrary", "arbitrary"), 48),
        name="gqa_attention",
    )(bounded, qt, k, vt, u)


def _na_kernel(bounded_ref, qt_ref, k0_ref, k1_ref, k2_ref, kc_ref, v0_ref, v1_ref, v2_ref, vc_ref,
               bias_ref, gate_ref, o_ref, *, n_lat_blocks):
    is_latent = pl.program_id(1) < n_lat_blocks
    is_bounded = bounded_ref[0] != 0
    vrows = HEAD_DIM + GQA_ONES_ROWS

    def probabilities(j, latent, bounded):
        hs = slice(j * HEAD_DIM, (j + 1) * HEAD_DIM)
        qt = qt_ref[hs, :]
        if not latent:
            s = jnp.dot(kc_ref[:, hs], qt, preferred_element_type=F32)
        else:
            n_ctx = kc_ref.shape[0]
            k = jnp.concatenate([kc_ref[:, hs], k0_ref[:, hs], k1_ref[:, hs], k2_ref[:, hs]], axis=0)
            s = jnp.dot(k, qt, preferred_element_type=F32)
            s = jnp.concatenate([s[:n_ctx], s[n_ctx:] + bias_ref[j].astype(F32)], axis=0)
        if not bounded:
            s = s - jnp.max(s, axis=0, keepdims=True)
        return jnp.exp2(s).astype(BF16)

    def output(j, probs, latent):
        hs = slice(j * HEAD_DIM, (j + 1) * HEAD_DIM)
        vs = slice(j * vrows, (j + 1) * vrows)
        vt = vc_ref[vs, :]
        if latent:
            vt = jnp.concatenate([vt, v0_ref[vs, :], v1_ref[vs, :], v2_ref[vs, :]], axis=1)
        acc = jnp.dot(vt, probs, preferred_element_type=F32)
        o = (acc[:HEAD_DIM] / acc[HEAD_DIM:HEAD_DIM + 1]).T
        o_ref[:, hs] = (o * _silu(gate_ref[:, hs].astype(F32))).astype(o_ref.dtype)

    heads = range(qt_ref.shape[0] // HEAD_DIM)
    for latent in (True, False):
        for bounded in (True, False):
            @pl.when(jnp.logical_and(is_latent == latent, is_bounded == bounded))
            def _():
                probs = [probabilities(j, latent, bounded) for j in heads]
                for j in heads:
                    output(j, probs[j], latent)


def _na_bias_table(rel_bias, n_rows):
    c = np.arange(GRID_W)[None, :]
    kj = np.arange(GRID_W)[:, None]
    cs = np.clip(c - NA_KW // 2, 0, GRID_W - NA_KW)
    col_ok = (kj >= cs) & (kj < cs + NA_KW)
    sel_c = (np.arange(2 * NA_KW - 1)[:, None, None] == (kj - c + NA_KW - 1)[None]) & col_ok[None]
    cols = jnp.einsum('hde,ekc->hdkc', rel_bias * math.log2(math.e), jnp.asarray(sel_c, F32),
                      precision=lax.Precision.HIGHEST)
    cols = jnp.where(jnp.asarray(col_ok)[None, None], cols, NEG_INF).astype(BF16)
    neg = jnp.full((rel_bias.shape[0], GRID_W, GRID_W), NEG_INF, BF16)
    tables = []
    for r0, ws in ((0, 0), (NA_Q_ROWS, 0), (n_rows - NA_Q_ROWS, n_rows - NA_WIN_ROWS)):
        key_rows = []
        for i in range(NA_WIN_ROWS):
            blocks = []
            for a in range(NA_Q_ROWS):
                r, kr = r0 + a, ws + i
                rs = min(max(r - NA_KH // 2, 0), n_rows - NA_KH)
                blocks.append(cols[:, kr - r + NA_KH - 1] if rs <= kr < rs + NA_KH else neg)
            key_rows.append(jnp.concatenate(blocks, axis=-1))
        tables.append(jnp.concatenate(key_rows, axis=-2))
    return jnp.stack(tables, axis=0)


def _na_attention(bounded, qt, kn, vt, u, n_heads, off_gate, bias_tbl, n_lat):
    t = u.shape[0]
    tb = NA_Q_ROWS * GRID_W
    n_lat_blocks = n_lat // tb
    ctx_blk = t // tb - 1
    hp = NA_HEADS_PER_STEP
    hw = hp * HEAD_DIM
    vrows = hp * (HEAD_DIM + GQA_ONES_ROWS)
    assert t - n_lat == tb and n_lat_blocks >= 3 and off_gate % hw == 0 and n_heads % hp == 0
    g_blk0 = off_gate // hw

    def win(i, j):
        return jnp.clip(i - 1, 0, n_lat_blocks - 3) + j

    def k_spec(j):
        return pl.BlockSpec((tb, hw), lambda h, i, b: (win(i, j), h))

    def v_spec(j):
        return pl.BlockSpec((vrows, tb), lambda h, i, b: (h, win(i, j)))

    def bias_cls(h, i, b):
        return (jnp.where(i == 0, 0, jnp.where(i >= n_lat_blocks - 1, 2, 1)), h, 0, 0)

    grid_spec = pltpu.PrefetchScalarGridSpec(
        num_scalar_prefetch=1,
        grid=(n_heads // hp, t // tb),
        in_specs=[pl.BlockSpec((hw, tb), lambda h, i, b: (h, i)),
                  k_spec(0), k_spec(1), k_spec(2),
                  pl.BlockSpec((tb, hw), lambda h, i, b: (ctx_blk, h)),
                  v_spec(0), v_spec(1), v_spec(2),
                  pl.BlockSpec((vrows, tb), lambda h, i, b: (h, ctx_blk)),
                  pl.BlockSpec((None, hp, NA_WIN_ROWS * GRID_W, tb), bias_cls),
                  pl.BlockSpec((tb, hw), lambda h, i, b: (i, g_blk0 + h))],
        out_specs=pl.BlockSpec((tb, hw), lambda h, i, b: (i, h)))
    return pl.pallas_call(
        functools.partial(_na_kernel, n_lat_blocks=n_lat_blocks),
        grid_spec=grid_spec,
        out_shape=jax.ShapeDtypeStruct((t, n_heads * HEAD_DIM), BF16),
        compiler_params=_params(("arbitrary", "arbitrary"), 32),
        name="na_attention",
    )(bounded, qt, kn, kn, kn, kn, vt, vt, vt, vt, bias_tbl, u)


def _store_slabs(o_ref, y):
    q, ns, _, ct = o_ref.shape
    for c in range(q):
        o_ref[c] = y[:, c * ct:(c + 1) * ct].reshape(ns, FFT_N2, ct)


def _load_slabs(ref):
    q, ns, _, ct = ref.shape
    return jnp.concatenate([ref[c].reshape(ns * FFT_N2, ct) for c in range(q)], axis=1)


def _short_conv_kernel(x_ref, prev_ref, next_ref, w_ref, o_ref, *, tr):
    i = pl.program_id(0)
    n = pl.num_programs(0)
    x = x_ref[...].astype(F32)
    halo = prev_ref.shape[0]
    prev_row = jnp.where(i == 0, 0.0, prev_ref[halo - 1:halo, :].astype(F32))
    next_row = jnp.where(i == n - 1, 0.0, next_ref[0:1, :].astype(F32))
    row = lax.broadcasted_iota(jnp.int32, x.shape, 0)
    up = jnp.where(row == 0, prev_row, pltpu.roll(x, 1, 0))
    down = jnp.where(row == tr - 1, next_row, pltpu.roll(x, tr - 1, 0))
    _store_slabs(o_ref, (up * w_ref[0:1, :] + x * w_ref[1:2, :] + down * w_ref[2:3, :]).astype(o_ref.dtype))


def _short_conv(u, width, conv_w, n_lat):
    tr = _largest_tile(n_lat, 1024, 256)
    tc = _largest_tile(width, 1024, 256)
    halo = 16
    hb = tr // halo
    n_tiles = n_lat // tr
    wpad = jnp.zeros((8, width), F32).at[:HY_SHORT].set(conv_w.astype(F32))
    return pl.pallas_call(
        functools.partial(_short_conv_kernel, tr=tr),
        grid=(n_tiles, width // tc),
        in_specs=[pl.BlockSpec((tr, tc), lambda i, j: (i, j)),
                  pl.BlockSpec((halo, tc), lambda i, j: (jnp.maximum(i * hb - 1, 0), j)),
                  pl.BlockSpec((halo, tc), lambda i, j: (jnp.minimum((i + 1) * hb, n_tiles * hb - 1), j)),
                  pl.BlockSpec((8, tc), lambda i, j: (0, j))],
        out_specs=pl.BlockSpec((tc // FFT_CT, tr // FFT_N2, FFT_N2, FFT_CT), lambda i, j: (j, i, 0, 0)),
        out_shape=jax.ShapeDtypeStruct((width // FFT_CT, n_lat // FFT_N2, FFT_N2, FFT_CT), BF16),
        compiler_params=_params(("arbitrary", "arbitrary"), 32),
        name="short_conv",
    )(u, u, u, wpad)


def _hy_gate2_kernel(x2_ref, c_ref, y1_ref, d_ref, g_ref, ctx_ref, o_ref, *, n_lat_tiles):
    i = pl.program_id(0)

    @pl.when(i < n_lat_tiles)
    def _():
        rows = lambda ref: _load_slabs(ref).astype(F32)
        y1 = rows(y1_ref)
        z = rows(x2_ref) * (rows(c_ref) + y1 * d_ref[1:2, :])
        o_ref[...] = (z * _silu(g_ref[...].astype(F32))).astype(o_ref.dtype)

    @pl.when(i >= n_lat_tiles)
    def _():
        o_ref[...] = ctx_ref[...]


def _hy_gate2(vx4, c2, y1, dskip8, u, off_gate, hy_ctx, hy):
    t = u.shape[0]
    n_lat = vx4.shape[1] * FFT_N2
    tr = t - n_lat
    tc = _largest_tile(hy, 2048, 256)
    nb = hy // tc
    n_lat_tiles = n_lat // tr
    assert n_lat % tr == 0 and off_gate % tc == 0 and tr % FFT_N2 == 0
    g_blk0 = off_gate // tc
    lat = lambda i: jnp.minimum(i, n_lat_tiles - 1)
    slab = lambda c0: pl.BlockSpec((tc // FFT_CT, tr // FFT_N2, FFT_N2, FFT_CT),
                                   lambda i, j: (c0 + j, lat(i), 0, 0))
    return pl.pallas_call(
        functools.partial(_hy_gate2_kernel, n_lat_tiles=n_lat_tiles),
        grid=(t // tr, nb),
        in_specs=[slab(2 * nb), slab(0), slab(0),
                  pl.BlockSpec((8, tc), lambda i, j: (0, j)),
                  pl.BlockSpec((tr, tc), lambda i, j: (i, g_blk0 + j)),
                  pl.BlockSpec((tr, tc), lambda i, j: (0, j))],
        out_specs=pl.BlockSpec((tr, tc), lambda i, j: (i, j)),
        out_shape=jax.ShapeDtypeStruct((t, hy), BF16),
        compiler_params=_params(("arbitrary", "arbitrary"), 32),
        name="hy_gate2",
    )(vx4, c2, y1, dskip8, u, hy_ctx)


def _filter_hidden_kernel(z_ref, w1_ref, b1_ref, w2_ref, b2_ref, fr_ref, o_ref):
    hp = lax.Precision.HIGHEST
    fr = fr_ref[...]
    h = jnp.sin(fr * (jnp.dot(z_ref[...], w1_ref[...], precision=hp,
                              preferred_element_type=F32) + b1_ref[...]))
    o_ref[...] = jnp.sin(fr * (jnp.dot(h, w2_ref[...], precision=hp,
                                       preferred_element_type=F32) + b2_ref[...]))


def _pad2(a, rows, cols):
    return jnp.zeros((rows, cols), F32).at[:a.shape[0], :a.shape[1]].set(a.astype(F32))


def _filter_hidden(feat, w1, b1, w2, b2, freq):
    r = feat.shape[0]
    tr = _largest_tile(r, 1024, 256)
    full = lambda i: (0, 0)
    return pl.pallas_call(
        _filter_hidden_kernel,
        grid=(r // tr,),
        in_specs=[pl.BlockSpec((tr, LANE), lambda i: (i, 0)),
                  pl.BlockSpec((LANE, LANE), full), pl.BlockSpec((1, LANE), full),
                  pl.BlockSpec((LANE, LANE), full), pl.BlockSpec((1, LANE), full),
                  pl.BlockSpec((1, LANE), full)],
        out_specs=pl.BlockSpec((tr, LANE), lambda i: (i, 0)),
        out_shape=jax.ShapeDtypeStruct((r, LANE), F32),
        compiler_params=_params(("arbitrary",), 32),
        name="filter_hidden",
    )(feat, _pad2(w1, LANE, LANE), _pad2(b1[None], 1, LANE), _pad2(w2, LANE, LANE),
      _pad2(b2[None], 1, LANE), _pad2(freq[None], 1, LANE))


def _two_sided_positions(n):
    idx = np.arange(2 * n)
    return np.where(idx < n, idx, 2 * n - idx) % n


def _filter_features(n, n_bands):
    pos = _two_sided_positions(n).astype(np.float64)[:, None]
    t = pos / (n - 1)
    w = (2.0 * math.pi / n) * pos
    bands = np.linspace(1e-4, n_bands - 1, n_bands)[None, :]
    z = np.zeros((2 * n, LANE))
    z[:, 0:1] = t
    z[:, 1:1 + n_bands] = np.cos(bands * w)
    z[:, 1 + n_bands:1 + 2 * n_bands] = -np.sin(bands * w)
    return jnp.asarray(z, F32), jnp.asarray(t, F32)


def _filter_gen_kernel(h_ref, w_ref, t_ref, dec_ref, o_ref, *, tr, n):
    i = pl.program_id(0)
    f = jnp.dot(h_ref[...].astype(BF16), w_ref[...], preferred_element_type=F32)
    f = f * jnp.exp(-t_ref[...] * jnp.abs(dec_ref[0:1, :]))
    row = i * tr + lax.broadcasted_iota(jnp.int32, (tr, 1), 0)
    _store_slabs(o_ref, jnp.where(row == n, 0.0, f).astype(o_ref.dtype))


def _filter_gen(hidden, row0, w3p, tcol, decay8, n, order, hy):
    tr = _largest_tile(n, 1024, 256)
    tc = _largest_tile(hy, 1024, 256)
    nb = hy // tc
    n_half = n // tr
    hb0 = row0 // tr
    assert row0 % tr == 0
    return pl.pallas_call(
        functools.partial(_filter_gen_kernel, tr=tr, n=n),
        grid=(2 * n // tr, order * nb),
        in_specs=[pl.BlockSpec((tr, LANE), lambda i, j: (hb0 + i, 0)),
                  pl.BlockSpec((LANE, tc), lambda i, j: (0, (j // nb) * 2 * nb + (i // n_half) * nb + j % nb)),
                  pl.BlockSpec((tr, 1), lambda i, j: (i, 0)),
                  pl.BlockSpec((None, 8, tc), lambda i, j: (j // nb, 0, j % nb))],
        out_specs=pl.BlockSpec((tc // FFT_CT, tr // FFT_N2, FFT_N2, FFT_CT), lambda i, j: (j, i, 0, 0)),
        out_shape=jax.ShapeDtypeStruct((order * hy // FFT_CT, 2 * n // FFT_N2, FFT_N2, FFT_CT), BF16),
        compiler_params=_params(("arbitrary", "arbitrary"), 32),
        name="filter_gen",
    )(hidden, w3p, tcol, decay8)


def _dft_tables(n1):
    n2 = FFT_N2
    n = n1 * n2
    half = n1 // 2
    nz = half
    k1 = np.concatenate([np.arange(0, half + 1, 2), np.arange(1, half + 1, 2)])

    def cs(idx, period):
        ang = (idx % period) * (2.0 * math.pi / period)
        return np.cos(ang), -np.sin(ang)

    eye8 = np.eye(8)
    fr, fi = cs(k1[:, None] * np.arange(nz)[None, :], n1)
    kf = np.stack([fr, fi], axis=1)
    kf = np.einsum('kpn,jl->kpjnl', kf, eye8).reshape((half + 1) * 16, nz * 8)
    gr, gi = cs(np.arange(half)[:, None] * k1[None, :], n1)
    weight = np.where((k1 == 0) | (k1 == half), 1.0, 2.0) / n
    ki = np.stack([gr, gi], axis=2) * weight[None, :, None]
    ki = np.einsum('nkp,jl->njkpl', ki, eye8).reshape(half * 8, (half + 1) * 16)
    r = np.arange(2 * n2)
    part = (r % 16) // 8
    n2_of_r = 8 * (r // 16) + r % 8
    k2 = np.arange(n2)
    idx = k2[None, :, None] * n2_of_r[None, None, :] * n1 + k1[:, None, None] * n2_of_r[None, None, :]
    cr, ci = cs(idx, n)
    top = np.where(part[None, None, :] == 0, cr, -ci)
    bot = np.where(part[None, None, :] == 0, ci, cr)
    m = np.concatenate([top, bot], axis=1)
    return tuple(jnp.asarray(a, F32).astype(BF16) for a in (kf, ki, m))


def _cross_slab_forward(z_ref, kf_ref, a_scr, nz, nh, ct):
    def body(b, carry):
        z = z_ref[:, pl.ds(pl.multiple_of(16 * b, 16), 16), :].astype(F32)
        for half in range(2):
            g = z[:, 8 * half:8 * half + 8, :].reshape(nz * 8, ct).astype(BF16)
            p = jnp.dot(kf_ref[...], g, preferred_element_type=F32)
            start = pl.multiple_of(32 * b + 16 * half, 16)
            a_scr[:, pl.ds(start, 16), :] = p.astype(BF16).reshape(nh, 16, ct)
        return carry

    lax.fori_loop(0, FFT_N2 // 16, body, 0)


def _slab_block(nh):
    return max(k for k in range(1, 17) if nh % k == 0)


def _cross_slab_forward_full(z_ref, kf_ref, a_scr, nh, ct):
    half = nh - 1
    n_even = half // 2 + 1

    def body(b, carry):
        rows = pl.ds(pl.multiple_of(16 * b, 16), 16)
        lo = z_ref[0:half, rows, :].astype(F32)
        hi = z_ref[half:2 * half, rows, :].astype(F32)
        for sub in range(2):
            sl = slice(8 * sub, 8 * sub + 8)
            g_even = (lo[:, sl, :] + hi[:, sl, :]).reshape(half * 8, ct).astype(BF16)
            g_odd = (lo[:, sl, :] - hi[:, sl, :]).reshape(half * 8, ct).astype(BF16)
            p_even = jnp.dot(kf_ref[0:n_even * 16, :], g_even, preferred_element_type=F32)
            p_odd = jnp.dot(kf_ref[n_even * 16:nh * 16, :], g_odd, preferred_element_type=F32)
            start = pl.ds(pl.multiple_of(32 * b + 16 * sub, 16), 16)
            a_scr[0:n_even, start, :] = p_even.astype(BF16).reshape(n_even, 16, ct)
            a_scr[n_even:nh, start, :] = p_odd.astype(BF16).reshape(nh - n_even, 16, ct)
        return carry

    lax.fori_loop(0, FFT_N2 // 16, body, 0)


def _fft_fwd_kernel(z_ref, kf_ref, m_ref, o_ref, a_scr, *, nh, kb, ct):
    kblk = pl.program_id(1)

    @pl.when(kblk == 0)
    def _():
        _cross_slab_forward_full(z_ref, kf_ref, a_scr, nh, ct)

    for k in range(kb):
        x = jnp.dot(m_ref[k], a_scr[kblk * kb + k], preferred_element_type=F32)
        o_ref[k] = x.astype(o_ref.dtype)


def _fft_forward(z4, kf, m):
    n_tiles, nz, _, ct = z4.shape
    nh = m.shape[0]
    kb = nh
    once = pl.Buffered(1)
    assert nz == 2 * (nh - 1) and ct == FFT_CT
    return pl.pallas_call(
        functools.partial(_fft_fwd_kernel, nh=nh, kb=kb, ct=ct),
        grid=(n_tiles, nh // kb),
        in_specs=[pl.BlockSpec((None, nz, FFT_N2, ct), lambda j, k: (j, 0, 0, 0)),
                  pl.BlockSpec(kf.shape, lambda j, k: (0, 0), pipeline_mode=once),
                  pl.BlockSpec((kb, 2 * FFT_N2, 2 * FFT_N2), lambda j, k: (k, 0, 0), pipeline_mode=once)],
        out_specs=pl.BlockSpec((None, kb, 2 * FFT_N2, ct), lambda j, k: (j, k, 0, 0)),
        out_shape=jax.ShapeDtypeStruct((n_tiles, nh, 2 * FFT_N2, ct), BF16),
        scratch_shapes=[pltpu.VMEM((nh, 2 * FFT_N2, ct), BF16)],
        compiler_params=_params(("arbitrary", "arbitrary"), 56),
        name="fft_forward",
    )(z4, kf, m)


def _fft_conv_kernel(z_ref, kf_ref, ki_ref, m_ref, h_ref, *rest, nz, nh, kb, ct, skip_row):
    gated = len(rest) == 4
    o_ref, a_scr = rest[-2:]
    kblk = pl.program_id(1)
    n2 = FFT_N2

    @pl.when(kblk == 0)
    def _():
        _cross_slab_forward(z_ref, kf_ref, a_scr, nz, nh, ct)

    for k in range(kb):
        mk = m_ref[k]
        x = jnp.dot(mk, a_scr[kblk * kb + k], preferred_element_type=F32)
        h = h_ref[k].astype(F32)
        xr, xi, hr, hi = x[:n2], x[n2:], h[:n2], h[n2:]
        y = jnp.concatenate([xr * hr - xi * hi, xr * hi + xi * hr], axis=0).astype(BF16)
        b = lax.dot_general(mk, y, (((0,), (0,)), ((), ())), preferred_element_type=F32)
        a_scr[kblk * kb + k] = b.astype(BF16)

    @pl.when(kblk == pl.num_programs(1) - 1)
    def _():
        def body(b, carry):
            halves = []
            for half in range(2):
                start = pl.multiple_of(32 * b + 16 * half, 16)
                r = a_scr[:, pl.ds(start, 16), :].reshape(nh * 16, ct)
                p = jnp.dot(ki_ref[...], r, preferred_element_type=F32)
                halves.append(p.reshape(nz, 8, ct))
            out = jnp.concatenate(halves, axis=1)
            rows = pl.ds(pl.multiple_of(16 * b, 16), 16)
            if gated:
                x_ref, d_ref = rest[:2]
                z = z_ref[:, rows, :].astype(F32)
                out = x_ref[:, rows, :].astype(F32) * (out + z * d_ref[skip_row:skip_row + 1, :])
            o_ref[:, rows, :] = out.astype(o_ref.dtype)
            return carry

        lax.fori_loop(0, n2 // 16, body, 0)


def _fft_conv(z4, z_blk0, spec, spec_blk0, kf, ki, m, c, gate=None):
    nz, ct = z4.shape[1], z4.shape[3]
    nh = m.shape[0]
    kb = _slab_block(nh)
    assert nh == nz + 1 and ct == FFT_CT
    in_specs = [pl.BlockSpec((None, nz, FFT_N2, ct), lambda j, k: (z_blk0 + j, 0, 0, 0)),
                pl.BlockSpec(kf.shape, lambda j, k: (0, 0)),
                pl.BlockSpec(ki.shape, lambda j, k: (0, 0)),
                pl.BlockSpec((kb, 2 * FFT_N2, 2 * FFT_N2), lambda j, k: (k, 0, 0)),
                pl.BlockSpec((None, kb, 2 * FFT_N2, ct), lambda j, k: (spec_blk0 + j, k, 0, 0))]
    operands = [z4, kf, ki, m, spec]
    skip_row = 0
    if gate is not None:
        x4, x_blk0, dskip8, skip_row = gate
        in_specs += [pl.BlockSpec((None, nz, FFT_N2, ct), lambda j, k: (x_blk0 + j, 0, 0, 0)),
                     pl.BlockSpec((8, ct), lambda j, k: (0, j))]
        operands += [x4, dskip8]
    return pl.pallas_call(
        functools.partial(_fft_conv_kernel, nz=nz, nh=nh, kb=kb, ct=ct, skip_row=skip_row),
        grid=(c // ct, nh // kb),
        in_specs=in_specs,
        out_specs=pl.BlockSpec((None, nz, FFT_N2, ct), lambda j, k: (j, 0, 0, 0)),
        out_shape=jax.ShapeDtypeStruct((c // ct, nz, FFT_N2, ct), BF16),
        scratch_shapes=[pltpu.VMEM((nh, 2 * FFT_N2, ct), BF16)],
        compiler_params=_params(("arbitrary", "arbitrary"), 56),
        name="fft_conv",
    )(*operands)


def _ctx_dft_tables(n):
    nn = 2 * n
    k = np.arange(nn)
    ang = ((k[:, None] * k[None, :]) % nn) * (2.0 * math.pi / nn)
    cr, ci = np.cos(ang), -np.sin(ang)
    full = np.concatenate([cr, ci], axis=0)
    inv = np.concatenate([cr[:n], ci[:n]], axis=1) / nn
    return tuple(jnp.asarray(a, F32).astype(BF16) for a in (full[:, :n], full, inv))


def _ctx_hyena_kernel(v_ref, x1_ref, x2_ref, g_ref, cw0_ref, cw1_ref, cw2_ref, hid_ref,
                      w10_ref, w11_ref, w20_ref, w21_ref, t_ref, dec_ref, dsk_ref,
                      ff_ref, ffull_ref, finv_ref, o_ref, *, n):
    row = lax.broadcasted_iota(jnp.int32, (n, 1), 0)

    def sconv(x_ref, w_ref):
        x = x_ref[...].astype(F32)
        up = jnp.where(row == 0, 0.0, pltpu.roll(x, 1, 0))
        down = jnp.where(row == n - 1, 0.0, pltpu.roll(x, n - 1, 0))
        y = up * w_ref[0:1, :] + x * w_ref[1:2, :] + down * w_ref[2:3, :]
        return y.astype(BF16).astype(F32)

    hid = hid_ref[...].astype(BF16)
    tcol = t_ref[...]
    row2 = lax.broadcasted_iota(jnp.int32, (2 * n, 1), 0)

    def spectrum(wf_ref, wb_ref, o):
        fwd = jnp.dot(hid[:n], wf_ref[...], preferred_element_type=F32)
        bwd = jnp.dot(hid[n:], wb_ref[...], preferred_element_type=F32)
        f = jnp.concatenate([fwd, bwd], axis=0) * jnp.exp(-tcol * jnp.abs(dec_ref[o:o + 1, :]))
        f = jnp.where(row2 == n, 0.0, f).astype(BF16)
        return jnp.dot(ffull_ref[...], f, preferred_element_type=F32)

    def conv(z, h):
        x = jnp.dot(ff_ref[...], z.astype(BF16), preferred_element_type=F32)
        h = h.astype(BF16).astype(F32)
        nn = 2 * n
        xr, xi, hr, hi = x[:nn], x[nn:], h[:nn], h[nn:]
        y = jnp.concatenate([xr * hr - xi * hi, xr * hi + xi * hr], axis=0).astype(BF16)
        return jnp.dot(finv_ref[...], y, preferred_element_type=F32)

    v = sconv(v_ref, cw0_ref)
    x1 = sconv(x1_ref, cw1_ref)
    x2 = sconv(x2_ref, cw2_ref)
    c1 = conv(v, spectrum(w10_ref, w11_ref, 0)).astype(BF16).astype(F32)
    y1 = (x1 * (c1 + v * dsk_ref[0:1, :])).astype(BF16).astype(F32)
    c2 = conv(y1, spectrum(w20_ref, w21_ref, 1)).astype(BF16).astype(F32)
    z = x2 * (c2 + y1 * dsk_ref[1:2, :])
    o_ref[...] = (z * _silu(g_ref[...].astype(F32))).astype(o_ref.dtype)


def _ctx_hyena(u, n_lat, hy, order, off_gate, conv8, hidden, hid_row0, w3p, tcol, decay8, dskip8, tables):
    t = u.shape[0]
    n = t - n_lat
    assert order == 2 and n_lat % n == 0 and hid_row0 % (2 * n) == 0
    ct = FFT_CT
    nb = hy // ct
    rb = n_lat // n
    ff, ffull, finv = tables
    const = lambda j: (0, 0)
    ublk = lambda c0: pl.BlockSpec((n, ct), lambda j: (rb, c0 + j))
    wblk = lambda c0: pl.BlockSpec((LANE, ct), lambda j: (0, c0 + j))
    cwblk = lambda c0: pl.BlockSpec((8, ct), lambda j: (0, c0 + j))
    return pl.pallas_call(
        functools.partial(_ctx_hyena_kernel, n=n),
        grid=(nb,),
        in_specs=[ublk(0), ublk(nb), ublk(2 * nb), ublk(off_gate // ct),
                  cwblk(0), cwblk(nb), cwblk(2 * nb),
                  pl.BlockSpec((2 * n, LANE), lambda j: (hid_row0 // (2 * n), 0)),
                  wblk(0), wblk(nb), wblk(2 * nb), wblk(3 * nb),
                  pl.BlockSpec((2 * n, 1), const),
                  pl.BlockSpec((8, ct), lambda j: (0, j)),
                  pl.BlockSpec((8, ct), lambda j: (0, j)),
                  pl.BlockSpec(ff.shape, const), pl.BlockSpec(ffull.shape, const),
                  pl.BlockSpec(finv.shape, const)],
        out_specs=pl.BlockSpec((n, ct), lambda j: (0, j)),
        out_shape=jax.ShapeDtypeStruct((n, hy), BF16),
        compiler_params=_params(("arbitrary",), 40),
        name="ctx_hyena",
    )(u, u, u, u, conv8, conv8, conv8, hidden, w3p, w3p, w3p, w3p, tcol, decay8, dskip8,
      ff, ffull, finv)


def _even_layer(x_all, h, mods_l, n_lat, w_in, w_out, conv_w, w1, b1, w2, b2, w3, freq, decay, dskip,
                q_norm, k_norm, rope, dft, ctx_dft, last):
    t, d = x_all.shape
    n_ctx = t - n_lat
    order, hy = decay.shape
    gqa_w = d - hy
    n_q = gqa_w // HEAD_DIM
    n_kv = n_q // GQA_GROUP
    kv_w = n_kv * HEAD_DIM
    off_hy_gate = (order + 1) * hy
    off_q = off_hy_gate + hy
    off_v = off_q + gqa_w + kv_w
    off_at_gate = off_v + kv_w
    scale = HEAD_DIM ** -0.5

    u = _in_proj(h, *w_in)

    q_gain = q_norm.astype(F32) * (scale * math.log2(math.e))
    k_gain = k_norm.astype(F32)
    qt, kn, vt = _attn_prep(u, off_q, off_q + gqa_w, off_v, n_q, n_kv, 1, q_gain, k_gain,
                            rope[0], rope[1], True)
    att = _gqa_attention(_scores_bounded(q_gain, k_gain), qt, kn, vt, u, n_q, n_kv, off_at_gate, n_lat)

    n_bands = (w1.shape[0] - 1) // 2
    feat_l, t_l = _filter_features(n_lat, n_bands)
    feat_c, t_c = _filter_features(n_ctx, n_bands)
    hidden = _filter_hidden(jnp.concatenate([feat_l, feat_c], axis=0), w1, b1, w2, b2, freq)
    w3p = jnp.zeros((LANE, w3.shape[1]), BF16).at[:w3.shape[0]].set(w3.astype(BF16))
    decay8 = jnp.zeros((order, 8, hy), F32).at[:, 0].set(decay.astype(F32))
    dskip8 = jnp.zeros((8, hy), F32).at[:order].set(dskip.astype(F32))
    conv8 = jnp.zeros((8, conv_w.shape[1]), F32).at[:HY_SHORT].set(conv_w.astype(F32))
    kf_pad, ki, m = dft
    filt = _filter_gen(hidden, 0, w3p, t_l, decay8, n_lat, order, hy)
    spec = _fft_forward(filt, kf_pad, m)

    vx4 = _short_conv(u, (order + 1) * hy, conv_w, n_lat)
    ct = FFT_CT
    y1 = _fft_conv(vx4, 0, spec, 0, kf_pad, ki, m, hy, gate=(vx4, hy // ct, dskip8, 0))
    c2 = _fft_conv(y1, 0, spec, hy // ct, kf_pad, ki, m, hy)

    decay_c8 = jnp.zeros((8, hy), F32).at[:order].set(decay.astype(F32))
    hy_ctx = _ctx_hyena(u, n_lat, hy, order, off_hy_gate, conv8, hidden, 2 * n_lat, w3p, t_c,
                        decay_c8, dskip8, ctx_dft)
    hy_mix = _hy_gate2(vx4, c2, y1, dskip8, u, off_hy_gate, hy_ctx, hy)

    return _out_proj(hy_mix, 0, att, 0, *w_out, x_all, mods_l, n_lat, last)


def _odd_layer(x_all, h, mods_l, n_lat, w_in, w_out, q_norm, k_norm, rel_bias, rope, last):
    n_heads = rel_bias.shape[0]
    w = n_heads * HEAD_DIM
    scale = HEAD_DIM ** -0.5
    u = _in_proj(h, *w_in)
    log2e = math.log2(math.e)
    q_gain = q_norm.astype(F32) * (scale * log2e)
    k_gain = k_norm.astype(F32)
    hp = NA_HEADS_PER_STEP
    qt, kn, vt = _attn_prep(u, 0, w, 2 * w, hp, hp, n_heads // hp, q_gain, k_gain, rope[0], rope[1], False)
    bias_tbl = _na_bias_table(rel_bias.astype(F32), n_lat // GRID_W)
    bounded = _scores_bounded(q_gain, k_gain, jnp.max(jnp.abs(rel_bias)).astype(F32) * (log2e * 1.02))
    att = _na_attention(bounded, qt, kn, vt, u, n_heads, 3 * w, bias_tbl, n_lat)
    return _out_proj(att, 0, att, 1, *w_out, x_all, mods_l, n_lat, last)


def kernel(x, c, ctx, c_ctx, norm_g, ada_w, ada_b, e_w_in, e_w_out, hy_conv, hy_w1, hy_b1, hy_w2, hy_b2,
           hy_w3, hy_freq, hy_decay, hy_dskip, gqa_q_norm, gqa_k_norm, o_w_in, o_w_out, na_q_norm,
           na_k_norm, na_rel_bias):
    batch, n_lat, d = x.shape
    n_ctx = ctx.shape[1]
    depth = norm_g.shape[0]
    assert batch == 1 and n_lat % (2 * FFT_N2) == 0 and n_ctx == NA_Q_ROWS * GRID_W

    cvecs = jnp.zeros((8, d), F32).at[0].set(c[0]).at[1].set(c_ctx)
    mods = _ada_mods(cvecs, ada_w, ada_b)

    rope = _rope_tables(n_lat, n_ctx)
    n1 = 2 * n_lat // FFT_N2
    dft = _dft_tables(n1)
    ctx_dft = _ctx_dft_tables(n_ctx)

    for layer in range(depth):
        i = layer // 2
        last = layer == depth - 1
        if layer == 0:
            h, x_all = _norm_mod_join(x[0].astype(F32), ctx[0].astype(F32), norm_g[0], mods[0])
        else:
            h = _norm_mod(x_all, norm_g[layer], mods[layer], n_lat)
        if layer % 2 == 0:
            x_all = _even_layer(x_all, h, mods[layer], n_lat,
                                (e_w_in, i), (e_w_out, i), hy_conv[i],
                                hy_w1[i], hy_b1[i], hy_w2[i], hy_b2[i], hy_w3[i], hy_freq[i],
                                hy_decay[i], hy_dskip[i], gqa_q_norm[i], gqa_k_norm[i],
                                rope, dft, ctx_dft, last)
        else:
            x_all = _odd_layer(x_all, h, mods[layer], n_lat,
                               (o_w_in, i), (o_w_out, i),
                               na_q_norm[i], na_k_norm[i], na_rel_bias[i], rope, last)
    return x_all[None]
```

```python
import functools
import math

import numpy as np
import jax
import jax.numpy as jnp
from jax import lax
from jax.experimental import pallas as pl
from jax.experimental.pallas import tpu as pltpu

F32 = jnp.float32
BF16 = jnp.bfloat16

HEAD_DIM = 128
GRID_W = 64
EPS = 1e-6
ROPE_THETA = 10000.0
NA_KH = 8
NA_KW = 16
NA_Q_ROWS = 4
NA_WIN_ROWS = 12
NA_HEADS_PER_STEP = 8
GQA_GROUP = 4
HY_SHORT = 3
FFT_N2 = 128
FFT_CT = 256
NEG_INF = -1e30

V7X_VMEM_BYTES = 64 * 1024 * 1024
LANE = 128


def _params(semantics, vmem_mb):
    assert vmem_mb * 1024 * 1024 < V7X_VMEM_BYTES
    return pltpu.CompilerParams(dimension_semantics=semantics,
                                vmem_limit_bytes=vmem_mb * 1024 * 1024)


def _largest_tile(n, target, quantum):
    best = None
    t = quantum
    while t <= min(n, target):
        if n % t == 0:
            best = t
        t += quantum
    assert best is not None, (n, target, quantum)
    return best


def _silu(x):
    return x * jax.nn.sigmoid(x)


def _mods_kernel(c_ref, w_ref, b_ref, o_ref):
    s = _silu(c_ref[...]).astype(BF16)
    o_ref[...] = jnp.dot(s, w_ref[...].astype(BF16), preferred_element_type=F32) + b_ref[...]


def _ada_mods(cvecs, ada_w, ada_b):
    depth, d, n = ada_w.shape
    tn = _largest_tile(n, 512, LANE)
    return pl.pallas_call(
        _mods_kernel,
        grid=(depth, n // tn),
        in_specs=[pl.BlockSpec((8, d), lambda l, j: (0, 0)),
                  pl.BlockSpec((None, d, tn), lambda l, j: (l, 0, j)),
                  pl.BlockSpec((None, 1, tn), lambda l, j: (l, 0, j))],
        out_specs=pl.BlockSpec((None, 8, tn), lambda l, j: (l, 0, j)),
        out_shape=jax.ShapeDtypeStruct((depth, 8, n), F32),
        compiler_params=_params(("arbitrary", "arbitrary"), 40),
        name="ada_mods",
    )(cvecs, ada_w, ada_b.reshape(depth, 1, n))


def _norm_mod_kernel(x_ref, g_ref, sh_ref, sc_ref, o_ref, *, n_lat_tiles):
    x = x_ref[...]
    y = x * lax.rsqrt(jnp.mean(x * x, axis=-1, keepdims=True) + EPS) * g_ref[...]
    is_ctx = pl.program_id(0) >= n_lat_tiles
    sh = jnp.where(is_ctx, sh_ref[1:2, :], sh_ref[0:1, :])
    sc = jnp.where(is_ctx, sc_ref[1:2, :], sc_ref[0:1, :])
    o_ref[...] = (y * (1.0 + sc) + sh).astype(o_ref.dtype)


def _norm_mod_stream_kernel(x_hbm, g_ref, mods_hbm, o_hbm, *, tr, d, n_tiles, n_lat_tiles):
    def body(x_ref, sh_ref, sc_ref, o_ref):
        x = x_ref[...]
        y = x * lax.rsqrt(jnp.mean(x * x, axis=-1, keepdims=True) + EPS) * g_ref[...]
        o_ref[...] = (y * (1.0 + sc_ref[0]) + sh_ref[0]).astype(o_ref.dtype)

    cls = lambda i: jnp.where(i >= n_lat_tiles, 1, 0)
    pltpu.emit_pipeline(
        body,
        grid=(n_tiles,),
        in_specs=[pl.BlockSpec((tr, d), lambda i: (i, 0), pipeline_mode=pl.Buffered(3)),
                  pl.BlockSpec((1, 1, d), lambda i: (cls(i), 0, 0)),
                  pl.BlockSpec((1, 1, d), lambda i: (cls(i), 0, 1))],
        out_specs=[pl.BlockSpec((tr, d), lambda i: (i, 0))],
    )(x_hbm, mods_hbm, mods_hbm, o_hbm)


def _norm_mod(x_all, g, mods_l, n_lat):
    t, d = x_all.shape
    tr = 256
    assert t % tr == 0 and n_lat % tr == 0
    return pl.pallas_call(
        functools.partial(_norm_mod_stream_kernel, tr=tr, d=d, n_tiles=t // tr, n_lat_tiles=n_lat // tr),
        in_specs=[pl.BlockSpec(memory_space=pl.ANY),
                  pl.BlockSpec(memory_space=pltpu.VMEM),
                  pl.BlockSpec(memory_space=pl.ANY)],
        out_specs=pl.BlockSpec(memory_space=pl.ANY),
        out_shape=jax.ShapeDtypeStruct((t, d), BF16),
        compiler_params=pltpu.CompilerParams(vmem_limit_bytes=40 * 1024 * 1024),
        name="norm_mod",
    )(x_all, g.reshape(1, d), mods_l.reshape(mods_l.shape[0], 1, mods_l.shape[1]))


def _norm_mod_join_kernel(x_ref, c_ref, g_ref, sh_ref, sc_ref, o_ref, xo_ref, *, n_lat_tiles):
    is_ctx = pl.program_id(0) >= n_lat_tiles
    x = jnp.where(is_ctx, c_ref[...], x_ref[...])
    y = x * lax.rsqrt(jnp.mean(x * x, axis=-1, keepdims=True) + EPS) * g_ref[...]
    sh = jnp.where(is_ctx, sh_ref[1:2, :], sh_ref[0:1, :])
    sc = jnp.where(is_ctx, sc_ref[1:2, :], sc_ref[0:1, :])
    o_ref[...] = (y * (1.0 + sc) + sh).astype(o_ref.dtype)
    xo_ref[...] = x


def _norm_mod_join(x_lat, x_ctx, g, mods_l):
    n_lat, d = x_lat.shape
    n_ctx = x_ctx.shape[0]
    tr = n_ctx
    assert n_lat % tr == 0
    n_lat_tiles = n_lat // tr
    t = n_lat + n_ctx
    return pl.pallas_call(
        functools.partial(_norm_mod_join_kernel, n_lat_tiles=n_lat_tiles),
        grid=(t // tr,),
        in_specs=[pl.BlockSpec((tr, d), lambda i: (jnp.minimum(i, n_lat_tiles - 1), 0)),
                  pl.BlockSpec((tr, d), lambda i: (0, 0)),
                  pl.BlockSpec((1, d), lambda i: (0, 0)),
                  pl.BlockSpec((8, d), lambda i: (0, 0)),
                  pl.BlockSpec((8, d), lambda i: (0, 1))],
        out_specs=[pl.BlockSpec((tr, d), lambda i: (i, 0)),
                   pl.BlockSpec((tr, d), lambda i: (i, 0))],
        out_shape=[jax.ShapeDtypeStruct((t, d), BF16), jax.ShapeDtypeStruct((t, d), F32)],
        compiler_params=_params(("arbitrary",), 48),
        name="norm_mod_join",
    )(x_lat, x_ctx, g.reshape(1, d), mods_l, mods_l)


def _in_proj_kernel(a_ref, w_ref, o_ref):
    w = w_ref[...].astype(BF16)
    o_ref[...] = jnp.dot(a_ref[...], w, preferred_element_type=F32).astype(o_ref.dtype)


def _in_proj(h, w_all, layer_idx):
    m, k = h.shape
    n = w_all.shape[2]
    tm = _largest_tile(m, 1408, 128)
    tn = _largest_tile(n, 512, 256)
    return pl.pallas_call(
        _in_proj_kernel,
        grid=(m // tm, n // tn),
        in_specs=[pl.BlockSpec((tm, k), lambda i, j: (i, 0)),
                  pl.BlockSpec((None, k, tn), lambda i, j: (layer_idx, 0, j))],
        out_specs=pl.BlockSpec((tm, tn), lambda i, j: (i, j)),
        out_shape=jax.ShapeDtypeStruct((m, n), BF16),
        compiler_params=_params(("arbitrary", "arbitrary"), 56),
        name="in_proj",
    )(h, w_all)


def _out_proj_kernel(a1_ref, a2_ref, w1_ref, w2_ref, x_ref, gate_ref, o_ref, wb1, wb2, *, tm, n_lat):
    i = pl.program_id(1)

    @pl.when(i == 0)
    def _():
        wb1[...] = w1_ref[...].astype(BF16)
        wb2[...] = w2_ref[...].astype(BF16)

    y = jnp.dot(a1_ref[...], wb1[...], preferred_element_type=F32)
    y = y + jnp.dot(a2_ref[...], wb2[...], preferred_element_type=F32)
    rows = i * tm + lax.broadcasted_iota(jnp.int32, (tm, 1), 0)
    gate = jnp.where(rows >= n_lat, gate_ref[1:2, :], gate_ref[0:1, :])
    o_ref[...] = x_ref[...] + gate * y


def _out_proj(a1, a1_blk, a2, a2_blk, w_all, layer_idx, x_all, mods_l, n_lat, latent_only):
    d = x_all.shape[1]
    t = n_lat if latent_only else x_all.shape[0]
    kh = w_all.shape[1] // 2
    tm = _largest_tile(t, 768, 256)
    tn = _largest_tile(d, 1024, 256)
    gate_blk0 = 2 * d // tn
    once = pl.Buffered(1)
    return pl.pallas_call(
        functools.partial(_out_proj_kernel, tm=tm, n_lat=n_lat),
        grid=(d // tn, t // tm),
        in_specs=[pl.BlockSpec((tm, kh), lambda j, i: (i, a1_blk)),
                  pl.BlockSpec((tm, kh), lambda j, i: (i, a2_blk)),
                  pl.BlockSpec((None, kh, tn), lambda j, i: (layer_idx, 0, j), pipeline_mode=once),
                  pl.BlockSpec((None, kh, tn), lambda j, i: (layer_idx, 1, j), pipeline_mode=once),
                  pl.BlockSpec((tm, tn), lambda j, i: (i, j)),
                  pl.BlockSpec((8, tn), lambda j, i: (0, gate_blk0 + j))],
        out_specs=pl.BlockSpec((tm, tn), lambda j, i: (i, j)),
        out_shape=jax.ShapeDtypeStruct((t, d), F32),
        scratch_shapes=[pltpu.VMEM((kh, tn), BF16), pltpu.VMEM((kh, tn), BF16)],
        compiler_params=_params(("arbitrary", "arbitrary"), 56),
        name="out_proj",
    )(a1, a2, w_all, w_all, x_all, mods_l)


def _rope_tables(n_lat, n_ctx):
    quarter = HEAD_DIM // 4
    inv = ROPE_THETA ** (-np.arange(quarter, dtype=np.float64) / quarter)
    tok = np.arange(n_lat)
    rows = (tok // GRID_W)[:, None] * inv[None]
    cols = (tok % GRID_W)[:, None] * inv[None]
    cos = np.concatenate([np.cos(rows), np.cos(rows), np.cos(cols), np.cos(cols)], axis=-1)
    sin = np.concatenate([-np.sin(rows), np.sin(rows), -np.sin(cols), np.sin(cols)], axis=-1)
    cos = np.concatenate([cos, np.ones((n_ctx, HEAD_DIM))], axis=0)
    sin = np.concatenate([sin, np.zeros((n_ctx, HEAD_DIM))], axis=0)
    return jnp.asarray(cos, F32), jnp.asarray(sin, F32)


GQA_ONES_ROWS = 16


def _attn_prep_kernel(q_ref, k_ref, v_ref, g_ref, cos_ref, sin_ref, qt_ref, ko_ref, vt_ref,
                      *, n_q, n_kv, rope):
    if rope:
        cos = cos_ref[...]
        sin = sin_ref[...]
        lane = lax.broadcasted_iota(jnp.int32, cos.shape, 1)
        first_quarter = (lane % 64) < 32

    def norm_rope(x, g):
        y = x * lax.rsqrt(jnp.mean(x * x, axis=-1, keepdims=True) + EPS) * g
        if not rope:
            return y
        partner = jnp.where(first_quarter, pltpu.roll(y, HEAD_DIM - 32, 1), pltpu.roll(y, 32, 1))
        return y * cos + partner * sin

    for h in range(n_q):
        sl = slice(h * HEAD_DIM, (h + 1) * HEAD_DIM)
        qt_ref[sl, :] = norm_rope(q_ref[:, sl].astype(F32), g_ref[:, sl]).T.astype(qt_ref.dtype)
    for h in range(n_kv):
        sl = slice(h * HEAD_DIM, (h + 1) * HEAD_DIM)
        gk = g_ref[:, (n_q + h) * HEAD_DIM:(n_q + h + 1) * HEAD_DIM]
        ko_ref[:, sl] = norm_rope(k_ref[:, sl].astype(F32), gk).astype(ko_ref.dtype)
        r0 = h * (HEAD_DIM + GQA_ONES_ROWS)
        vt_ref[r0:r0 + HEAD_DIM, :] = v_ref[:, sl].astype(F32).T.astype(vt_ref.dtype)
        vt_ref[r0 + HEAD_DIM:r0 + HEAD_DIM + GQA_ONES_ROWS, :] = jnp.ones(
            (GQA_ONES_ROWS, vt_ref.shape[1]), vt_ref.dtype)


def _attn_prep(u, off_q, off_k, off_v, n_q, n_kv, n_groups, q_gain, k_gain, cos_t, sin_t, rope):
    t = u.shape[0]
    qw, kw = n_q * HEAD_DIM, n_kv * HEAD_DIM
    tr = _largest_tile(t, 768, 256)
    vrows = n_kv * (HEAD_DIM + GQA_ONES_ROWS)
    assert off_q % qw == 0 and off_k % kw == 0 and off_v % kw == 0 and t % tr == 0
    q_blk, k_blk, v_blk = off_q // qw, off_k // kw, off_v // kw
    gvec = jnp.concatenate([jnp.tile(q_gain, n_q), jnp.tile(k_gain, n_kv)])[None]
    return pl.pallas_call(
        functools.partial(_attn_prep_kernel, n_q=n_q, n_kv=n_kv, rope=rope),
        grid=(t // tr, n_groups),
        in_specs=[pl.BlockSpec((tr, qw), lambda i, j: (i, q_blk + j)),
                  pl.BlockSpec((tr, kw), lambda i, j: (i, k_blk + j)),
                  pl.BlockSpec((tr, kw), lambda i, j: (i, v_blk + j)),
                  pl.BlockSpec((1, qw + kw), lambda i, j: (0, 0)),
                  pl.BlockSpec((tr, HEAD_DIM), lambda i, j: (i, 0)),
                  pl.BlockSpec((tr, HEAD_DIM), lambda i, j: (i, 0))],
        out_specs=[pl.BlockSpec((qw, tr), lambda i, j: (j, i)),
                   pl.BlockSpec((tr, kw), lambda i, j: (i, j)),
                   pl.BlockSpec((vrows, tr), lambda i, j: (j, i))],
        out_shape=[jax.ShapeDtypeStruct((n_groups * qw, t), BF16),
                   jax.ShapeDtypeStruct((t, n_groups * kw), BF16),
                   jax.ShapeDtypeStruct((n_groups * vrows, t), BF16)],
        compiler_params=_params(("arbitrary", "arbitrary"), 48),
        name="attn_prep",
    )(u, u, u, gvec, cos_t, sin_t)


def _gqa_kernel(bounded_ref, qt_ref, k_ref, vt_ref, gate_ref, o_ref, m_s, acc_s,
                *, group, tq, tk, n_lat, n_ctx):
    acc_s[...] = jnp.zeros(acc_s.shape, F32)

    def chunk_bounded(kc, vtc):
        ps = []
        for j in range(group):
            qt = qt_ref[j * HEAD_DIM:(j + 1) * HEAD_DIM, :]
            s = jnp.dot(kc, qt, preferred_element_type=F32)
            ps.append(jnp.exp2(s).astype(BF16))
        for j in range(group):
            acc_s[j] += jnp.dot(vtc, ps[j], preferred_element_type=F32)

    def chunk_online(kc, vtc):
        for j in range(group):
            qt = qt_ref[j * HEAD_DIM:(j + 1) * HEAD_DIM, :]
            s = jnp.dot(kc, qt, preferred_element_type=F32)
            m = m_s[j]
            m_new = jnp.maximum(m, jnp.max(s, axis=0, keepdims=True))
            p = jnp.exp2(s - m_new).astype(BF16)
            acc_s[j] = jnp.exp2(m - m_new) * acc_s[j] + jnp.dot(vtc, p, preferred_element_type=F32)
            m_s[j] = m_new

    def sweep(chunk):
        is_latent = pl.program_id(1) * tq < n_lat

        @pl.when(is_latent)
        def _():
            def body(c, carry):
                r0 = pl.multiple_of(c * tk, LANE)
                chunk(k_ref[pl.ds(r0, tk), :], vt_ref[:, pl.ds(r0, tk)])
                return carry

            lax.fori_loop(0, (n_lat + n_ctx) // tk, body, 0)

        @pl.when(jnp.logical_not(is_latent))
        def _():
            chunk(k_ref[n_lat:n_lat + n_ctx, :], vt_ref[:, n_lat:n_lat + n_ctx])

    @pl.when(bounded_ref[0] != 0)
    def _():
        sweep(chunk_bounded)

    @pl.when(bounded_ref[0] == 0)
    def _():
        m_s[...] = jnp.full(m_s.shape, NEG_INF, F32)
        sweep(chunk_online)

    for j in range(group):
        sl = slice(j * HEAD_DIM, (j + 1) * HEAD_DIM)
        acc = acc_s[j]
        o = (acc[:HEAD_DIM] / acc[HEAD_DIM:HEAD_DIM + 1]).T
        o_ref[:, sl] = (o * _silu(gate_ref[:, sl].astype(F32))).astype(o_ref.dtype)


SCORE_BOUND_LOG2 = 60.0


def _scores_bounded(q_gain, k_gain, extra=0.0):
    bound = HEAD_DIM * jnp.max(jnp.abs(q_gain)) * jnp.max(jnp.abs(k_gain)) * 1.02 + extra
    return (bound <= SCORE_BOUND_LOG2).astype(jnp.int32).reshape(1)


def _gqa_attention(bounded, qt, k, vt, u, n_q_heads, n_kv_heads, off_gate, n_lat):
    t = u.shape[0]
    group = n_q_heads // n_kv_heads
    gw = group * HEAD_DIM
    tq = 256
    tk = _largest_tile(t, 4608, LANE)
    vrows = HEAD_DIM + GQA_ONES_ROWS
    assert t % tq == 0 and n_lat % tq == 0 and off_gate % gw == 0
    g_blk0 = off_gate // gw
    grid_spec = pltpu.PrefetchScalarGridSpec(
        num_scalar_prefetch=1,
        grid=(n_kv_heads, t // tq),
        in_specs=[pl.BlockSpec((gw, tq), lambda g, i, b: (g, i)),
                  pl.BlockSpec((t, HEAD_DIM), lambda g, i, b: (0, g)),
                  pl.BlockSpec((vrows, t), lambda g, i, b: (g, 0)),
                  pl.BlockSpec((tq, gw), lambda g, i, b: (i, g_blk0 + g))],
        out_specs=pl.BlockSpec((tq, gw), lambda g, i, b: (i, g)),
        scratch_shapes=[pltpu.VMEM((group, 1, tq), F32),
                        pltpu.VMEM((group, vrows, tq), F32)])
    return pl.pallas_call(
        functools.partial(_gqa_kernel, group=group, tq=tq, tk=tk, n_lat=n_lat, n_ctx=t - n_lat),
        grid_spec=grid_spec,
        out_shape=jax.ShapeDtypeStruct((t, n_q_heads * HEAD_DIM), BF16),
        compiler_params=_params(("arbitrary", "arbitrary"), 48),
        name="gqa_attention",
    )(bounded, qt, k, vt, u)


def _na_kernel(bounded_ref, qt_ref, k0_ref, k1_ref, k2_ref, kc_ref, v0_ref, v1_ref, v2_ref, vc_ref,
               bias_ref, gate_ref, o_ref, *, n_lat_blocks):
    is_latent = pl.program_id(1) < n_lat_blocks
    is_bounded = bounded_ref[0] != 0
    vrows = HEAD_DIM + GQA_ONES_ROWS

    def probabilities(j, latent, bounded):
        hs = slice(j * HEAD_DIM, (j + 1) * HEAD_DIM)
        qt = qt_ref[hs, :]
        if not latent:
            s = jnp.dot(kc_ref[:, hs], qt, preferred_element_type=F32)
        else:
            n_ctx = kc_ref.shape[0]
            k = jnp.concatenate([kc_ref[:, hs], k0_ref[:, hs], k1_ref[:, hs], k2_ref[:, hs]], axis=0)
            s = jnp.dot(k, qt, preferred_element_type=F32)
            s = jnp.concatenate([s[:n_ctx], s[n_ctx:] + bias_ref[j].astype(F32)], axis=0)
        if not bounded:
            s = s - jnp.max(s, axis=0, keepdims=True)
        return jnp.exp2(s).astype(BF16)

    def output(j, probs, latent):
        hs = slice(j * HEAD_DIM, (j + 1) * HEAD_DIM)
        vs = slice(j * vrows, (j + 1) * vrows)
        vt = vc_ref[vs, :]
        if latent:
            vt = jnp.concatenate([vt, v0_ref[vs, :], v1_ref[vs, :], v2_ref[vs, :]], axis=1)
        acc = jnp.dot(vt, probs, preferred_element_type=F32)
        o = (acc[:HEAD_DIM] / acc[HEAD_DIM:HEAD_DIM + 1]).T
        o_ref[:, hs] = (o * _silu(gate_ref[:, hs].astype(F32))).astype(o_ref.dtype)

    heads = range(qt_ref.shape[0] // HEAD_DIM)
    for latent in (True, False):
        for bounded in (True, False):
            @pl.when(jnp.logical_and(is_latent == latent, is_bounded == bounded))
            def _():
                probs = [probabilities(j, latent, bounded) for j in heads]
                for j in heads:
                    output(j, probs[j], latent)


def _na_bias_table(rel_bias, n_rows):
    c = np.arange(GRID_W)[None, :]
    kj = np.arange(GRID_W)[:, None]
    cs = np.clip(c - NA_KW // 2, 0, GRID_W - NA_KW)
    col_ok = (kj >= cs) & (kj < cs + NA_KW)
    sel_c = (np.arange(2 * NA_KW - 1)[:, None, None] == (kj - c + NA_KW - 1)[None]) & col_ok[None]
    cols = jnp.einsum('hde,ekc->hdkc', rel_bias * math.log2(math.e), jnp.asarray(sel_c, F32),
                      precision=lax.Precision.HIGHEST)
    cols = jnp.where(jnp.asarray(col_ok)[None, None], cols, NEG_INF).astype(BF16)
    neg = jnp.full((rel_bias.shape[0], GRID_W, GRID_W), NEG_INF, BF16)
    tables = []
    for r0, ws in ((0, 0), (NA_Q_ROWS, 0), (n_rows - NA_Q_ROWS, n_rows - NA_WIN_ROWS)):
        key_rows = []
        for i in range(NA_WIN_ROWS):
            blocks = []
            for a in range(NA_Q_ROWS):
                r, kr = r0 + a, ws + i
                rs = min(max(r - NA_KH // 2, 0), n_rows - NA_KH)
                blocks.append(cols[:, kr - r + NA_KH - 1] if rs <= kr < rs + NA_KH else neg)
            key_rows.append(jnp.concatenate(blocks, axis=-1))
        tables.append(jnp.concatenate(key_rows, axis=-2))
    return jnp.stack(tables, axis=0)


def _na_attention(bounded, qt, kn, vt, u, n_heads, off_gate, bias_tbl, n_lat):
    t = u.shape[0]
    tb = NA_Q_ROWS * GRID_W
    n_lat_blocks = n_lat // tb
    ctx_blk = t // tb - 1
    hp = NA_HEADS_PER_STEP
    hw = hp * HEAD_DIM
    vrows = hp * (HEAD_DIM + GQA_ONES_ROWS)
    assert t - n_lat == tb and n_lat_blocks >= 3 and off_gate % hw == 0 and n_heads % hp == 0
    g_blk0 = off_gate // hw

    def win(i, j):
        return jnp.clip(i - 1, 0, n_lat_blocks - 3) + j

    def k_spec(j):
        return pl.BlockSpec((tb, hw), lambda h, i, b: (win(i, j), h))

    def v_spec(j):
        return pl.BlockSpec((vrows, tb), lambda h, i, b: (h, win(i, j)))

    def bias_cls(h, i, b):
        return (jnp.where(i == 0, 0, jnp.where(i >= n_lat_blocks - 1, 2, 1)), h, 0, 0)

    grid_spec = pltpu.PrefetchScalarGridSpec(
        num_scalar_prefetch=1,
        grid=(n_heads // hp, t // tb),
        in_specs=[pl.BlockSpec((hw, tb), lambda h, i, b: (h, i)),
                  k_spec(0), k_spec(1), k_spec(2),
                  pl.BlockSpec((tb, hw), lambda h, i, b: (ctx_blk, h)),
                  v_spec(0), v_spec(1), v_spec(2),
                  pl.BlockSpec((vrows, tb), lambda h, i, b: (h, ctx_blk)),
                  pl.BlockSpec((None, hp, NA_WIN_ROWS * GRID_W, tb), bias_cls),
                  pl.BlockSpec((tb, hw), lambda h, i, b: (i, g_blk0 + h))],
        out_specs=pl.BlockSpec((tb, hw), lambda h, i, b: (i, h)))
    return pl.pallas_call(
        functools.partial(_na_kernel, n_lat_blocks=n_lat_blocks),
        grid_spec=grid_spec,
        out_shape=jax.ShapeDtypeStruct((t, n_heads * HEAD_DIM), BF16),
        compiler_params=_params(("arbitrary", "arbitrary"), 32),
        name="na_attention",
    )(bounded, qt, kn, kn, kn, kn, vt, vt, vt, vt, bias_tbl, u)


def _store_slabs(o_ref, y):
    q, ns, _, ct = o_ref.shape
    for c in range(q):
        o_ref[c] = y[:, c * ct:(c + 1) * ct].reshape(ns, FFT_N2, ct)


def _load_slabs(ref):
    q, ns, _, ct = ref.shape
    return jnp.concatenate([ref[c].reshape(ns * FFT_N2, ct) for c in range(q)], axis=1)


def _short_conv_kernel(x_ref, prev_ref, next_ref, w_ref, o_ref, *, tr):
    i = pl.program_id(0)
    n = pl.num_programs(0)
    x = x_ref[...].astype(F32)
    halo = prev_ref.shape[0]
    prev_row = jnp.where(i == 0, 0.0, prev_ref[halo - 1:halo, :].astype(F32))
    next_row = jnp.where(i == n - 1, 0.0, next_ref[0:1, :].astype(F32))
    row = lax.broadcasted_iota(jnp.int32, x.shape, 0)
    up = jnp.where(row == 0, prev_row, pltpu.roll(x, 1, 0))
    down = jnp.where(row == tr - 1, next_row, pltpu.roll(x, tr - 1, 0))
    _store_slabs(o_ref, (up * w_ref[0:1, :] + x * w_ref[1:2, :] + down * w_ref[2:3, :]).astype(o_ref.dtype))


def _short_conv(u, width, conv_w, n_lat):
    tr = _largest_tile(n_lat, 1024, 256)
    tc = _largest_tile(width, 1024, 256)
    halo = 16
    hb = tr // halo
    n_tiles = n_lat // tr
    wpad = jnp.zeros((8, width), F32).at[:HY_SHORT].set(conv_w.astype(F32))
    return pl.pallas_call(
        functools.partial(_short_conv_kernel, tr=tr),
        grid=(n_tiles, width // tc),
        in_specs=[pl.BlockSpec((tr, tc), lambda i, j: (i, j)),
                  pl.BlockSpec((halo, tc), lambda i, j: (jnp.maximum(i * hb - 1, 0), j)),
                  pl.BlockSpec((halo, tc), lambda i, j: (jnp.minimum((i + 1) * hb, n_tiles * hb - 1), j)),
                  pl.BlockSpec((8, tc), lambda i, j: (0, j))],
        out_specs=pl.BlockSpec((tc // FFT_CT, tr // FFT_N2, FFT_N2, FFT_CT), lambda i, j: (j, i, 0, 0)),
        out_shape=jax.ShapeDtypeStruct((width // FFT_CT, n_lat // FFT_N2, FFT_N2, FFT_CT), BF16),
        compiler_params=_params(("arbitrary", "arbitrary"), 32),
        name="short_conv",
    )(u, u, u, wpad)


def _hy_gate2_kernel(x2_ref, c_ref, y1_ref, d_ref, g_ref, ctx_ref, o_ref, *, n_lat_tiles):
    i = pl.program_id(0)

    @pl.when(i < n_lat_tiles)
    def _():
        rows = lambda ref: _load_slabs(ref).astype(F32)
        y1 = rows(y1_ref)
        z = rows(x2_ref) * (rows(c_ref) + y1 * d_ref[1:2, :])
        o_ref[...] = (z * _silu(g_ref[...].astype(F32))).astype(o_ref.dtype)

    @pl.when(i >= n_lat_tiles)
    def _():
        o_ref[...] = ctx_ref[...]


def _hy_gate2(vx4, c2, y1, dskip8, u, off_gate, hy_ctx, hy):
    t = u.shape[0]
    n_lat = vx4.shape[1] * FFT_N2
    tr = t - n_lat
    tc = _largest_tile(hy, 2048, 256)
    nb = hy // tc
    n_lat_tiles = n_lat // tr
    assert n_lat % tr == 0 and off_gate % tc == 0 and tr % FFT_N2 == 0
    g_blk0 = off_gate // tc
    lat = lambda i: jnp.minimum(i, n_lat_tiles - 1)
    slab = lambda c0: pl.BlockSpec((tc // FFT_CT, tr // FFT_N2, FFT_N2, FFT_CT),
                                   lambda i, j: (c0 + j, lat(i), 0, 0))
    return pl.pallas_call(
        functools.partial(_hy_gate2_kernel, n_lat_tiles=n_lat_tiles),
        grid=(t // tr, nb),
        in_specs=[slab(2 * nb), slab(0), slab(0),
                  pl.BlockSpec((8, tc), lambda i, j: (0, j)),
                  pl.BlockSpec((tr, tc), lambda i, j: (i, g_blk0 + j)),
                  pl.BlockSpec((tr, tc), lambda i, j: (0, j))],
        out_specs=pl.BlockSpec((tr, tc), lambda i, j: (i, j)),
        out_shape=jax.ShapeDtypeStruct((t, hy), BF16),
        compiler_params=_params(("arbitrary", "arbitrary"), 32),
        name="hy_gate2",
    )(vx4, c2, y1, dskip8, u, hy_ctx)


def _filter_hidden_kernel(z_ref, w1_ref, b1_ref, w2_ref, b2_ref, fr_ref, o_ref):
    hp = lax.Precision.HIGHEST
    fr = fr_ref[...]
    h = jnp.sin(fr * (jnp.dot(z_ref[...], w1_ref[...], precision=hp,
                              preferred_element_type=F32) + b1_ref[...]))
    o_ref[...] = jnp.sin(fr * (jnp.dot(h, w2_ref[...], precision=hp,
                                       preferred_element_type=F32) + b2_ref[...]))


def _pad2(a, rows, cols):
    return jnp.zeros((rows, cols), F32).at[:a.shape[0], :a.shape[1]].set(a.astype(F32))


def _filter_hidden(feat, w1, b1, w2, b2, freq):
    r = feat.shape[0]
    tr = _largest_tile(r, 1024, 256)
    full = lambda i: (0, 0)
    return pl.pallas_call(
        _filter_hidden_kernel,
        grid=(r // tr,),
        in_specs=[pl.BlockSpec((tr, LANE), lambda i: (i, 0)),
                  pl.BlockSpec((LANE, LANE), full), pl.BlockSpec((1, LANE), full),
                  pl.BlockSpec((LANE, LANE), full), pl.BlockSpec((1, LANE), full),
                  pl.BlockSpec((1, LANE), full)],
        out_specs=pl.BlockSpec((tr, LANE), lambda i: (i, 0)),
        out_shape=jax.ShapeDtypeStruct((r, LANE), F32),
        compiler_params=_params(("arbitrary",), 32),
        name="filter_hidden",
    )(feat, _pad2(w1, LANE, LANE), _pad2(b1[None], 1, LANE), _pad2(w2, LANE, LANE),
      _pad2(b2[None], 1, LANE), _pad2(freq[None], 1, LANE))


def _two_sided_positions(n):
    idx = np.arange(2 * n)
    return np.where(idx < n, idx, 2 * n - idx) % n


def _filter_features(n, n_bands):
    pos = _two_sided_positions(n).astype(np.float64)[:, None]
    t = pos / (n - 1)
    w = (2.0 * math.pi / n) * pos
    bands = np.linspace(1e-4, n_bands - 1, n_bands)[None, :]
    z = np.zeros((2 * n, LANE))
    z[:, 0:1] = t
    z[:, 1:1 + n_bands] = np.cos(bands * w)
    z[:, 1 + n_bands:1 + 2 * n_bands] = -np.sin(bands * w)
    return jnp.asarray(z, F32), jnp.asarray(t, F32)


def _filter_gen_kernel(h_ref, w_ref, t_ref, dec_ref, o_ref, *, tr, n):
    i = pl.program_id(0)
    f = jnp.dot(h_ref[...].astype(BF16), w_ref[...], preferred_element_type=F32)
    f = f * jnp.exp(-t_ref[...] * jnp.abs(dec_ref[0:1, :]))
    row = i * tr + lax.broadcasted_iota(jnp.int32, (tr, 1), 0)
    _store_slabs(o_ref, jnp.where(row == n, 0.0, f).astype(o_ref.dtype))


def _filter_gen(hidden, row0, w3p, tcol, decay8, n, order, hy):
    tr = _largest_tile(n, 1024, 256)
    tc = _largest_tile(hy, 1024, 256)
    nb = hy // tc
    n_half = n // tr
    hb0 = row0 // tr
    assert row0 % tr == 0
    return pl.pallas_call(
        functools.partial(_filter_gen_kernel, tr=tr, n=n),
        grid=(2 * n // tr, order * nb),
        in_specs=[pl.BlockSpec((tr, LANE), lambda i, j: (hb0 + i, 0)),
                  pl.BlockSpec((LANE, tc), lambda i, j: (0, (j // nb) * 2 * nb + (i // n_half) * nb + j % nb)),
                  pl.BlockSpec((tr, 1), lambda i, j: (i, 0)),
                  pl.BlockSpec((None, 8, tc), lambda i, j: (j // nb, 0, j % nb))],
        out_specs=pl.BlockSpec((tc // FFT_CT, tr // FFT_N2, FFT_N2, FFT_CT), lambda i, j: (j, i, 0, 0)),
        out_shape=jax.ShapeDtypeStruct((order * hy // FFT_CT, 2 * n // FFT_N2, FFT_N2, FFT_CT), BF16),
        compiler_params=_params(("arbitrary", "arbitrary"), 32),
        name="filter_gen",
    )(hidden, w3p, tcol, decay8)


def _dft_tables(n1):
    n2 = FFT_N2
    n = n1 * n2
    half = n1 // 2
    nz = half
    k1 = np.concatenate([np.arange(0, half + 1, 2), np.arange(1, half + 1, 2)])

    def cs(idx, period):
        ang = (idx % period) * (2.0 * math.pi / period)
        return np.cos(ang), -np.sin(ang)

    eye8 = np.eye(8)
    fr, fi = cs(k1[:, None] * np.arange(nz)[None, :], n1)
    kf = np.stack([fr, fi], axis=1)
    kf = np.einsum('kpn,jl->kpjnl', kf, eye8).reshape((half + 1) * 16, nz * 8)
    gr, gi = cs(np.arange(half)[:, None] * k1[None, :], n1)
    weight = np.where((k1 == 0) | (k1 == half), 1.0, 2.0) / n
    ki = np.stack([gr, gi], axis=2) * weight[None, :, None]
    ki = np.einsum('nkp,jl->njkpl', ki, eye8).reshape(half * 8, (half + 1) * 16)
    r = np.arange(2 * n2)
    part = (r % 16) // 8
    n2_of_r = 8 * (r // 16) + r % 8
    k2 = np.arange(n2)
    idx = k2[None, :, None] * n2_of_r[None, None, :] * n1 + k1[:, None, None] * n2_of_r[None, None, :]
    cr, ci = cs(idx, n)
    top = np.where(part[None, None, :] == 0, cr, -ci)
    bot = np.where(part[None, None, :] == 0, ci, cr)
    m = np.concatenate([top, bot], axis=1)
    return tuple(jnp.asarray(a, F32).astype(BF16) for a in (kf, ki, m))


def _cross_slab_forward(z_ref, kf_ref, a_scr, nz, nh, ct):
    def body(b, carry):
        z = z_ref[:, pl.ds(pl.multiple_of(16 * b, 16), 16), :].astype(F32)
        for half in range(2):
            g = z[:, 8 * half:8 * half + 8, :].reshape(nz * 8, ct).astype(BF16)
            p = jnp.dot(kf_ref[...], g, preferred_element_type=F32)
            start = pl.multiple_of(32 * b + 16 * half, 16)
            a_scr[:, pl.ds(start, 16), :] = p.astype(BF16).reshape(nh, 16, ct)
        return carry

    lax.fori_loop(0, FFT_N2 // 16, body, 0)


def _slab_block(nh):
    return max(k for k in range(1, 17) if nh % k == 0)


def _cross_slab_forward_full(z_ref, kf_ref, a_scr, nh, ct):
    half = nh - 1
    n_even = half // 2 + 1

    def body(b, carry):
        rows = pl.ds(pl.multiple_of(16 * b, 16), 16)
        lo = z_ref[0:half, rows, :].astype(F32)
        hi = z_ref[half:2 * half, rows, :].astype(F32)
        for sub in range(2):
            sl = slice(8 * sub, 8 * sub + 8)
            g_even = (lo[:, sl, :] + hi[:, sl, :]).reshape(half * 8, ct).astype(BF16)
            g_odd = (lo[:, sl, :] - hi[:, sl, :]).reshape(half * 8, ct).astype(BF16)
            p_even = jnp.dot(kf_ref[0:n_even * 16, :], g_even, preferred_element_type=F32)
            p_odd = jnp.dot(kf_ref[n_even * 16:nh * 16, :], g_odd, preferred_element_type=F32)
            start = pl.ds(pl.multiple_of(32 * b + 16 * sub, 16), 16)
            a_scr[0:n_even, start, :] = p_even.astype(BF16).reshape(n_even, 16, ct)
            a_scr[n_even:nh, start, :] = p_odd.astype(BF16).reshape(nh - n_even, 16, ct)
        return carry

    lax.fori_loop(0, FFT_N2 // 16, body, 0)


def _fft_fwd_kernel(z_ref, kf_ref, m_ref, o_ref, a_scr, *, nh, kb, ct):
    kblk = pl.program_id(1)

    @pl.when(kblk == 0)
    def _():
        _cross_slab_forward_full(z_ref, kf_ref, a_scr, nh, ct)

    for k in range(kb):
        x = jnp.dot(m_ref[k], a_scr[kblk * kb + k], preferred_element_type=F32)
        o_ref[k] = x.astype(o_ref.dtype)


def _fft_forward(z4, kf, m):
    n_tiles, nz, _, ct = z4.shape
    nh = m.shape[0]
    kb = nh
    once = pl.Buffered(1)
    assert nz == 2 * (nh - 1) and ct == FFT_CT
    return pl.pallas_call(
        functools.partial(_fft_fwd_kernel, nh=nh, kb=kb, ct=ct),
        grid=(n_tiles, nh // kb),
        in_specs=[pl.BlockSpec((None, nz, FFT_N2, ct), lambda j, k: (j, 0, 0, 0)),
                  pl.BlockSpec(kf.shape, lambda j, k: (0, 0), pipeline_mode=once),
                  pl.BlockSpec((kb, 2 * FFT_N2, 2 * FFT_N2), lambda j, k: (k, 0, 0), pipeline_mode=once)],
        out_specs=pl.BlockSpec((None, kb, 2 * FFT_N2, ct), lambda j, k: (j, k, 0, 0)),
        out_shape=jax.ShapeDtypeStruct((n_tiles, nh, 2 * FFT_N2, ct), BF16),
        scratch_shapes=[pltpu.VMEM((nh, 2 * FFT_N2, ct), BF16)],
        compiler_params=_params(("arbitrary", "arbitrary"), 56),
        name="fft_forward",
    )(z4, kf, m)


def _fft_conv_kernel(z_ref, kf_ref, ki_ref, m_ref, h_ref, *rest, nz, nh, kb, ct, skip_row):
    gated = len(rest) == 4
    o_ref, a_scr = rest[-2:]
    kblk = pl.program_id(1)
    n2 = FFT_N2

    @pl.when(kblk == 0)
    def _():
        _cross_slab_forward(z_ref, kf_ref, a_scr, nz, nh, ct)

    for k in range(kb):
        mk = m_ref[k]
        x = jnp.dot(mk, a_scr[kblk * kb + k], preferred_element_type=F32)
        h = h_ref[k].astype(F32)
        xr, xi, hr, hi = x[:n2], x[n2:], h[:n2], h[n2:]
        y = jnp.concatenate([xr * hr - xi * hi, xr * hi + xi * hr], axis=0).astype(BF16)
        b = lax.dot_general(mk, y, (((0,), (0,)), ((), ())), preferred_element_type=F32)
        a_scr[kblk * kb + k] = b.astype(BF16)

    @pl.when(kblk == pl.num_programs(1) - 1)
    def _():
        def body(b, carry):
            halves = []
            for half in range(2):
                start = pl.multiple_of(32 * b + 16 * half, 16)
                r = a_scr[:, pl.ds(start, 16), :].reshape(nh * 16, ct)
                p = jnp.dot(ki_ref[...], r, preferred_element_type=F32)
                halves.append(p.reshape(nz, 8, ct))
            out = jnp.concatenate(halves, axis=1)
            rows = pl.ds(pl.multiple_of(16 * b, 16), 16)
            if gated:
                x_ref, d_ref = rest[:2]
                z = z_ref[:, rows, :].astype(F32)
                out = x_ref[:, rows, :].astype(F32) * (out + z * d_ref[skip_row:skip_row + 1, :])
            o_ref[:, rows, :] = out.astype(o_ref.dtype)
            return carry

        lax.fori_loop(0, n2 // 16, body, 0)


def _fft_conv(z4, z_blk0, spec, spec_blk0, kf, ki, m, c, gate=None):
    nz, ct = z4.shape[1], z4.shape[3]
    nh = m.shape[0]
    kb = _slab_block(nh)
    assert nh == nz + 1 and ct == FFT_CT
    in_specs = [pl.BlockSpec((None, nz, FFT_N2, ct), lambda j, k: (z_blk0 + j, 0, 0, 0)),
                pl.BlockSpec(kf.shape, lambda j, k: (0, 0)),
                pl.BlockSpec(ki.shape, lambda j, k: (0, 0)),
                pl.BlockSpec((kb, 2 * FFT_N2, 2 * FFT_N2), lambda j, k: (k, 0, 0)),
                pl.BlockSpec((None, kb, 2 * FFT_N2, ct), lambda j, k: (spec_blk0 + j, k, 0, 0))]
    operands = [z4, kf, ki, m, spec]
    skip_row = 0
    if gate is not None:
        x4, x_blk0, dskip8, skip_row = gate
        in_specs += [pl.BlockSpec((None, nz, FFT_N2, ct), lambda j, k: (x_blk0 + j, 0, 0, 0)),
                     pl.BlockSpec((8, ct), lambda j, k: (0, j))]
        operands += [x4, dskip8]
    return pl.pallas_call(
        functools.partial(_fft_conv_kernel, nz=nz, nh=nh, kb=kb, ct=ct, skip_row=skip_row),
        grid=(c // ct, nh // kb),
        in_specs=in_specs,
        out_specs=pl.BlockSpec((None, nz, FFT_N2, ct), lambda j, k: (j, 0, 0, 0)),
        out_shape=jax.ShapeDtypeStruct((c // ct, nz, FFT_N2, ct), BF16),
        scratch_shapes=[pltpu.VMEM((nh, 2 * FFT_N2, ct), BF16)],
        compiler_params=_params(("arbitrary", "arbitrary"), 56),
        name="fft_conv",
    )(*operands)


def _ctx_dft_tables(n):
    nn = 2 * n
    k = np.arange(nn)
    ang = ((k[:, None] * k[None, :]) % nn) * (2.0 * math.pi / nn)
    cr, ci = np.cos(ang), -np.sin(ang)
    full = np.concatenate([cr, ci], axis=0)
    inv = np.concatenate([cr[:n], ci[:n]], axis=1) / nn
    return tuple(jnp.asarray(a, F32).astype(BF16) for a in (full[:, :n], full, inv))


def _ctx_hyena_kernel(v_ref, x1_ref, x2_ref, g_ref, cw0_ref, cw1_ref, cw2_ref, hid_ref,
                      w10_ref, w11_ref, w20_ref, w21_ref, t_ref, dec_ref, dsk_ref,
                      ff_ref, ffull_ref, finv_ref, o_ref, *, n):
    row = lax.broadcasted_iota(jnp.int32, (n, 1), 0)

    def sconv(x_ref, w_ref):
        x = x_ref[...].astype(F32)
        up = jnp.where(row == 0, 0.0, pltpu.roll(x, 1, 0))
        down = jnp.where(row == n - 1, 0.0, pltpu.roll(x, n - 1, 0))
        y = up * w_ref[0:1, :] + x * w_ref[1:2, :] + down * w_ref[2:3, :]
        return y.astype(BF16).astype(F32)

    hid = hid_ref[...].astype(BF16)
    tcol = t_ref[...]
    row2 = lax.broadcasted_iota(jnp.int32, (2 * n, 1), 0)

    def spectrum(wf_ref, wb_ref, o):
        fwd = jnp.dot(hid[:n], wf_ref[...], preferred_element_type=F32)
        bwd = jnp.dot(hid[n:], wb_ref[...], preferred_element_type=F32)
        f = jnp.concatenate([fwd, bwd], axis=0) * jnp.exp(-tcol * jnp.abs(dec_ref[o:o + 1, :]))
        f = jnp.where(row2 == n, 0.0, f).astype(BF16)
        return jnp.dot(ffull_ref[...], f, preferred_element_type=F32)

    def conv(z, h):
        x = jnp.dot(ff_ref[...], z.astype(BF16), preferred_element_type=F32)
        h = h.astype(BF16).astype(F32)
        nn = 2 * n
        xr, xi, hr, hi = x[:nn], x[nn:], h[:nn], h[nn:]
        y = jnp.concatenate([xr * hr - xi * hi, xr * hi + xi * hr], axis=0).astype(BF16)
        return jnp.dot(finv_ref[...], y, preferred_element_type=F32)

    v = sconv(v_ref, cw0_ref)
    x1 = sconv(x1_ref, cw1_ref)
    x2 = sconv(x2_ref, cw2_ref)
    c1 = conv(v, spectrum(w10_ref, w11_ref, 0)).astype(BF16).astype(F32)
    y1 = (x1 * (c1 + v * dsk_ref[0:1, :])).astype(BF16).astype(F32)
    c2 = conv(y1, spectrum(w20_ref, w21_ref, 1)).astype(BF16).astype(F32)
    z = x2 * (c2 + y1 * dsk_ref[1:2, :])
    o_ref[...] = (z * _silu(g_ref[...].astype(F32))).astype(o_ref.dtype)


def _ctx_hyena(u, n_lat, hy, order, off_gate, conv8, hidden, hid_row0, w3p, tcol, decay8, dskip8, tables):
    t = u.shape[0]
    n = t - n_lat
    assert order == 2 and n_lat % n == 0 and hid_row0 % (2 * n) == 0
    ct = FFT_CT
    nb = hy // ct
    rb = n_lat // n
    ff, ffull, finv = tables
    const = lambda j: (0, 0)
    ublk = lambda c0: pl.BlockSpec((n, ct), lambda j: (rb, c0 + j))
    wblk = lambda c0: pl.BlockSpec((LANE, ct), lambda j: (0, c0 + j))
    cwblk = lambda c0: pl.BlockSpec((8, ct), lambda j: (0, c0 + j))
    return pl.pallas_call(
        functools.partial(_ctx_hyena_kernel, n=n),
        grid=(nb,),
        in_specs=[ublk(0), ublk(nb), ublk(2 * nb), ublk(off_gate // ct),
                  cwblk(0), cwblk(nb), cwblk(2 * nb),
                  pl.BlockSpec((2 * n, LANE), lambda j: (hid_row0 // (2 * n), 0)),
                  wblk(0), wblk(nb), wblk(2 * nb), wblk(3 * nb),
                  pl.BlockSpec((2 * n, 1), const),
                  pl.BlockSpec((8, ct), lambda j: (0, j)),
                  pl.BlockSpec((8, ct), lambda j: (0, j)),
                  pl.BlockSpec(ff.shape, const), pl.BlockSpec(ffull.shape, const),
                  pl.BlockSpec(finv.shape, const)],
        out_specs=pl.BlockSpec((n, ct), lambda j: (0, j)),
        out_shape=jax.ShapeDtypeStruct((n, hy), BF16),
        compiler_params=_params(("arbitrary",), 40),
        name="ctx_hyena",
    )(u, u, u, u, conv8, conv8, conv8, hidden, w3p, w3p, w3p, w3p, tcol, decay8, dskip8,
      ff, ffull, finv)


def _even_layer(x_all, h, mods_l, n_lat, w_in, w_out, conv_w, w1, b1, w2, b2, w3, freq, decay, dskip,
                q_norm, k_norm, rope, dft, ctx_dft, last):
    t, d = x_all.shape
    n_ctx = t - n_lat
    order, hy = decay.shape
    gqa_w = d - hy
    n_q = gqa_w // HEAD_DIM
    n_kv = n_q // GQA_GROUP
    kv_w = n_kv * HEAD_DIM
    off_hy_gate = (order + 1) * hy
    off_q = off_hy_gate + hy
    off_v = off_q + gqa_w + kv_w
    off_at_gate = off_v + kv_w
    scale = HEAD_DIM ** -0.5

    u = _in_proj(h, *w_in)

    q_gain = q_norm.astype(F32) * (scale * math.log2(math.e))
    k_gain = k_norm.astype(F32)
    qt, kn, vt = _attn_prep(u, off_q, off_q + gqa_w, off_v, n_q, n_kv, 1, q_gain, k_gain,
                            rope[0], rope[1], True)
    att = _gqa_attention(_scores_bounded(q_gain, k_gain), qt, kn, vt, u, n_q, n_kv, off_at_gate, n_lat)

    n_bands = (w1.shape[0] - 1) // 2
    feat_l, t_l = _filter_features(n_lat, n_bands)
    feat_c, t_c = _filter_features(n_ctx, n_bands)
    hidden = _filter_hidden(jnp.concatenate([feat_l, feat_c], axis=0), w1, b1, w2, b2, freq)
    w3p = jnp.zeros((LANE, w3.shape[1]), BF16).at[:w3.shape[0]].set(w3.astype(BF16))
    decay8 = jnp.zeros((order, 8, hy), F32).at[:, 0].set(decay.astype(F32))
    dskip8 = jnp.zeros((8, hy), F32).at[:order].set(dskip.astype(F32))
    conv8 = jnp.zeros((8, conv_w.shape[1]), F32).at[:HY_SHORT].set(conv_w.astype(F32))
    kf_pad, ki, m = dft
    filt = _filter_gen(hidden, 0, w3p, t_l, decay8, n_lat, order, hy)
    spec = _fft_forward(filt, kf_pad, m)

    vx4 = _short_conv(u, (order + 1) * hy, conv_w, n_lat)
    ct = FFT_CT
    y1 = _fft_conv(vx4, 0, spec, 0, kf_pad, ki, m, hy, gate=(vx4, hy // ct, dskip8, 0))
    c2 = _fft_conv(y1, 0, spec, hy // ct, kf_pad, ki, m, hy)

    decay_c8 = jnp.zeros((8, hy), F32).at[:order].set(decay.astype(F32))
    hy_ctx = _ctx_hyena(u, n_lat, hy, order, off_hy_gate, conv8, hidden, 2 * n_lat, w3p, t_c,
                        decay_c8, dskip8, ctx_dft)
    hy_mix = _hy_gate2(vx4, c2, y1, dskip8, u, off_hy_gate, hy_ctx, hy)

    return _out_proj(hy_mix, 0, att, 0, *w_out, x_all, mods_l, n_lat, last)


def _odd_layer(x_all, h, mods_l, n_lat, w_in, w_out, q_norm, k_norm, rel_bias, rope, last):
    n_heads = rel_bias.shape[0]
    w = n_heads * HEAD_DIM
    scale = HEAD_DIM ** -0.5
    u = _in_proj(h, *w_in)
    log2e = math.log2(math.e)
    q_gain = q_norm.astype(F32) * (scale * log2e)
    k_gain = k_norm.astype(F32)
    hp = NA_HEADS_PER_STEP
    qt, kn, vt = _attn_prep(u, 0, w, 2 * w, hp, hp, n_heads // hp, q_gain, k_gain, rope[0], rope[1], False)
    bias_tbl = _na_bias_table(rel_bias.astype(F32), n_lat // GRID_W)
    bounded = _scores_bounded(q_gain, k_gain, jnp.max(jnp.abs(rel_bias)).astype(F32) * (log2e * 1.02))
    att = _na_attention(bounded, qt, kn, vt, u, n_heads, 3 * w, bias_tbl, n_lat)
    return _out_proj(att, 0, att, 1, *w_out, x_all, mods_l, n_lat, last)


def kernel(x, c, ctx, c_ctx, norm_g, ada_w, ada_b, e_w_in, e_w_out, hy_conv, hy_w1, hy_b1, hy_w2, hy_b2,
           hy_w3, hy_freq, hy_decay, hy_dskip, gqa_q_norm, gqa_k_norm, o_w_in, o_w_out, na_q_norm,
           na_k_norm, na_rel_bias):
    batch, n_lat, d = x.shape
    n_ctx = ctx.shape[1]
    depth = norm_g.shape[0]
    assert batch == 1 and n_lat % (2 * FFT_N2) == 0 and n_ctx == NA_Q_ROWS * GRID_W

    cvecs = jnp.zeros((8, d), F32).at[0].set(c[0]).at[1].set(c_ctx)
    mods = _ada_mods(cvecs, ada_w, ada_b)

    rope = _rope_tables(n_lat, n_ctx)
    n1 = 2 * n_lat // FFT_N2
    dft = _dft_tables(n1)
    ctx_dft = _ctx_dft_tables(n_ctx)

    for layer in range(depth):
        i = layer // 2
        last = layer == depth - 1
        if layer == 0:
            h, x_all = _norm_mod_join(x[0].astype(F32), ctx[0].astype(F32), norm_g[0], mods[0])
        else:
            h = _norm_mod(x_all, norm_g[layer], mods[layer], n_lat)
        if layer % 2 == 0:
            x_all = _even_layer(x_all, h, mods[layer], n_lat,
                                (e_w_in, i), (e_w_out, i), hy_conv[i],
                                hy_w1[i], hy_b1[i], hy_w2[i], hy_b2[i], hy_w3[i], hy_freq[i],
                                hy_decay[i], hy_dskip[i], gqa_q_norm[i], gqa_k_norm[i],
                                rope, dft, ctx_dft, last)
        else:
            x_all = _odd_layer(x_all, h, mods[layer], n_lat,
                               (o_w_in, i), (o_w_out, i),
                               na_q_norm[i], na_k_norm[i], na_rel_bias[i], rope, last)
    return x_all[None]
```
